```python
import math, functools
import jax, jax.numpy as jnp
from jax import lax
import numpy as np

D_MODEL = 1024
BATCH = 4
SEQ = 8192
DEPTH = 1
DEC_BATCH = 128
DEC_SEQ = 1
PAST_LEN = 8192
PAGE_SIZE = 128

D_MIX = D_MODEL
N_HEADS = 8
HEAD_DIM = 64
N_KV_HEADS = 2
GQ = N_HEADS // N_KV_HEADS
Q_W = N_HEADS * HEAD_DIM
KV_W = N_KV_HEADS * HEAD_DIM
S5_W = D_MIX - Q_W
S5_GROUP = 16
S5_GROUPS = S5_W // S5_GROUP
S5_STATE = 64
GATE_OFF = Q_W + 6 * KV_W
U_OFF = GATE_OFF + 3 * N_HEADS
N_IN = U_OFF + S5_W
CMP_BLOCK = 32
SEL_BLOCK = 64
CMP_PER_SEL = SEL_BLOCK // CMP_BLOCK
N_SEL = 16
WINDOW = 512
Q_BLOCK = 128
FORCED_SCORE = 1e4
NUM_BUCKETS = 32
REL_MAX_DIST = 128
N_EXPERTS = 256
TOP_K = 8
N_EXPERT_GROUPS = 8
TOPK_GROUPS = 4
D_EXPERT = 256
D_SHARED = 256
ROUTED_SCALE = 2.5
EXPERT_BLOCK = 128
EPS = 1e-6
SCALE = HEAD_DIM ** -0.5
F32 = jnp.float32

kernel_name = 'nsa_s5_moe_hybrid_step'


def rmsnorm(x, g):
    xf = x.astype(F32)
    y = xf * lax.rsqrt(jnp.mean(xf * xf, axis=-1, keepdims=True) + EPS)
    return (y * g.astype(F32)).astype(x.dtype)


def adaln(c, w_ada, b_ada):
    m = jnp.einsum('bd,de->be', jax.nn.silu(c), w_ada) + b_ada
    return jnp.split(m[:, None, :], 6, axis=-1)


def masked_softmax(logits, mask):
    l = jnp.where(mask, logits, -jnp.inf)
    m = jnp.max(l, axis=-1, keepdims=True)
    e = jnp.exp(l - jnp.where(jnp.isfinite(m), m, 0.0))
    return e / jnp.maximum(jnp.sum(e, axis=-1, keepdims=True), 1e-30)


def rel_bucket(dist):
    n = jnp.maximum(dist, 0)
    exact = NUM_BUCKETS // 2
    nf = jnp.maximum(n, 1).astype(F32)
    large = exact + (jnp.log(nf / exact) / math.log(REL_MAX_DIST / exact) * (NUM_BUCKETS - exact)).astype(jnp.int32)
    return jnp.where(n < exact, n, jnp.minimum(large, NUM_BUCKETS - 1))


def compress(k, pe, w1, w2):
    B, L = k.shape[:2]
    nc = L // CMP_BLOCK
    kb = k[:, :nc * CMP_BLOCK].reshape(B, nc, CMP_BLOCK, N_KV_HEADS, HEAD_DIM) + pe[None, None, :, None, :]
    hid = jax.nn.silu(jnp.einsum('bnjgd,jde->bnge', kb, w1))
    return jnp.einsum('bnge,ef->bngf', hid, w2)


def pad_to_sel(k):
    L = k.shape[1]
    Lp = -(-L // SEL_BLOCK) * SEL_BLOCK
    return jnp.pad(k, ((0, 0), (0, Lp - L), (0, 0), (0, 0)))


def gather_pages(pool, page_table):
    g = pool[page_table]
    return g.reshape(g.shape[0], -1, *g.shape[3:])


def nsa_core(q, t_q, kcmp, vcmp, cmp_end, ks, vs, kw, vw, t_w, gates, rel_bias):
    B, Tq = q.shape[:2]
    qg = q.reshape(B, Tq, N_KV_HEADS, GQ, HEAD_DIM)
    tab = rel_bias.T.reshape(N_KV_HEADS, GQ, NUM_BUCKETS)
    d_c = t_q[:, None] - cmp_end[None, :]
    l_c = jnp.einsum('bqgrd,bngd->bgrqn', qg, kcmp, preferred_element_type=F32) * SCALE + tab[:, :, rel_bucket(d_c)]
    p_c = masked_softmax(l_c, d_c >= 0)
    o_c = jnp.einsum('bgrqn,bngd->bqgrd', p_c.astype(vcmp.dtype), vcmp)
    n_sel_blk = ks.shape[1] // SEL_BLOCK
    nc = kcmp.shape[1]
    imp = jnp.pad(p_c.sum(axis=2), ((0, 0), (0, 0), (0, 0), (0, n_sel_blk * CMP_PER_SEL - nc)))
    imp = imp.reshape(B, N_KV_HEADS, Tq, n_sel_blk, CMP_PER_SEL).sum(-1)
    blk = jnp.arange(n_sel_blk)[None, :]
    cur = t_q[:, None] // SEL_BLOCK
    valid = blk * SEL_BLOCK <= t_q[:, None]
    forced = (blk == 0) | (blk == cur) | (blk == cur - 1)
    score = jnp.where(valid, jnp.where(forced, FORCED_SCORE, imp), -jnp.inf)
    n_top = min(N_SEL, n_sel_blk)
    _, sel = lax.top_k(score, n_top)
    tok = (sel[..., None] * SEL_BLOCK + jnp.arange(SEL_BLOCK)).reshape(B, N_KV_HEADS, Tq, n_top * SEL_BLOCK)
    bi = jnp.arange(B)[:, None, None, None]
    gi = jnp.arange(N_KV_HEADS)[None, :, None, None]
    k_g = ks[bi, tok, gi]
    v_g = vs[bi, tok, gi]
    d_s = t_q[None, None, :, None] - tok
    b_s = tab[gi[..., None], jnp.arange(GQ)[None, None, :, None, None], rel_bucket(d_s)[:, :, None]]
    l_s = jnp.einsum('bqgrd,bgqjd->bgrqj', qg, k_g, preferred_element_type=F32) * SCALE + b_s
    p_s = masked_softmax(l_s, (d_s >= 0)[:, :, None])
    o_s = jnp.einsum('bgrqj,bgqjd->bqgrd', p_s.astype(v_g.dtype), v_g)
    d_w = t_q[:, None] - t_w[None, :]
    m_w = (d_w >= 0) & (d_w < WINDOW) & (t_w[None, :] >= 0)
    l_w = jnp.einsum('bqgrd,bkgd->bgrqk', qg, kw, preferred_element_type=F32) * SCALE + tab[:, :, rel_bucket(d_w)]
    p_w = masked_softmax(l_w, m_w)
    o_w = jnp.einsum('bgrqk,bkgd->bqgrd', p_w.astype(vw.dtype), vw)
    g = gates.reshape(B, Tq, N_KV_HEADS, GQ, 3)
    o = g[..., 0:1] * o_c + g[..., 1:2] * o_s + g[..., 2:3] * o_w
    return o.reshape(B, Tq, Q_W).astype(q.dtype)


def nsa_prompt(pe, wk1, wk2, wv1, wv2, rel_bias, q, kc, vc, ks, vs, kw, vw, gates):
    B, L = q.shape[:2]
    kcmp, vcmp = compress(kc, pe, wk1, wk2), compress(vc, pe, wv1, wv2)
    cmp_end = jnp.arange(kcmp.shape[1]) * CMP_BLOCK + (CMP_BLOCK - 1)
    ks_p, vs_p = pad_to_sel(ks), pad_to_sel(vs)
    pad_w = ((0, 0), (WINDOW, 0), (0, 0), (0, 0))
    kw_p, vw_p = jnp.pad(kw, pad_w), jnp.pad(vw, pad_w)
    nb = L // Q_BLOCK
    qb = q.reshape(B, nb, Q_BLOCK, N_HEADS, HEAD_DIM).swapaxes(0, 1)
    gb = gates.reshape(B, nb, Q_BLOCK, N_HEADS, 3).swapaxes(0, 1)

    def one_block(args):
        q_i, g_i, i = args
        s = i * Q_BLOCK
        t_q = s + jnp.arange(Q_BLOCK)
        t_w = s - WINDOW + jnp.arange(WINDOW + Q_BLOCK)
        kw_i = lax.dynamic_slice_in_dim(kw_p, s, WINDOW + Q_BLOCK, axis=1)
        vw_i = lax.dynamic_slice_in_dim(vw_p, s, WINDOW + Q_BLOCK, axis=1)
        return nsa_core(q_i, t_q, kcmp, vcmp, cmp_end, ks_p, vs_p, kw_i, vw_i, t_w, g_i, rel_bias)

    o = lax.map(one_block, (qb, gb, jnp.arange(nb)))
    o = o.swapaxes(0, 1).reshape(B, L, Q_W)
    n_win = min(WINDOW, L)
    return o, kw[:, L - n_win:], vw[:, L - n_win:]


def nsa_sample(page_table, ck_c, cv_c, ck_s, cv_s, cw_k, cw_v, pe, wk1, wk2, wv1, wv2, rel_bias, q, kc, vc, ks, vs, kw, vw, gates):
    past = page_table.shape[1] * PAGE_SIZE
    n_new = q.shape[1]
    kc_all = jnp.concatenate([gather_pages(ck_c, page_table), kc], axis=1)
    vc_all = jnp.concatenate([gather_pages(cv_c, page_table), vc], axis=1)
    kcmp, vcmp = compress(kc_all, pe, wk1, wk2), compress(vc_all, pe, wv1, wv2)
    cmp_end = jnp.arange(kcmp.shape[1]) * CMP_BLOCK + (CMP_BLOCK - 1)
    ks_all = pad_to_sel(jnp.concatenate([gather_pages(ck_s, page_table), ks], axis=1))
    vs_all = pad_to_sel(jnp.concatenate([gather_pages(cv_s, page_table), vs], axis=1))
    win = cw_k.shape[1]
    kw_all = jnp.concatenate([cw_k, kw], axis=1)
    vw_all = jnp.concatenate([cw_v, vw], axis=1)
    t_w = past - win + jnp.arange(win + n_new)
    t_q = past + jnp.arange(n_new)
    o = nsa_core(q, t_q, kcmp, vcmp, cmp_end, ks_all, vs_all, kw_all, vw_all, t_w, gates, rel_bias)
    return o, kw_all[:, n_new:], vw_all[:, n_new:]


def s5_params(lam_re, lam_im, log_dt, b_re, b_im, c_re, c_im):
    lam = lax.complex(lam_re.astype(F32), lam_im.astype(F32))
    dt = jnp.exp(log_dt.astype(F32))[:, None]
    lbar = jnp.exp(lam * dt)
    bbar = ((lbar - 1.0) / lam)[:, :, None] * lax.complex(b_re.astype(F32), b_im.astype(F32))
    c = lax.complex(c_re.astype(F32), c_im.astype(F32))
    return lbar, bbar, c


def _lin_combine(e1, e2):
    a1, b1 = e1
    a2, b2 = e2
    return a1 * a2, a2 * b1 + b2


def s5_scan(u, h0, lbar, bbar, c, d_skip):
    uf = u.astype(F32)
    bu = jnp.einsum('blgh,gph->blgp', uf.astype(jnp.complex64), bbar)
    bu = bu.at[:, 0].add(lbar * h0)
    a = jnp.broadcast_to(lbar, bu.shape)
    _, h = lax.associative_scan(_lin_combine, (a, bu), axis=1)
    y = jnp.einsum('blgp,ghp->blgh', h, c).real + d_skip.astype(F32) * uf
    return y, h[:, -1]


def mix_sublayer(x, shift, scale, gate, g_pre, g_post, w_in, attend, h0, s5p, d_skip, w_glu, w_out):
    B, L, _ = x.shape
    h = rmsnorm(x, g_pre) * (1 + scale) + shift
    z = jnp.einsum('bld,de->ble', h, w_in)
    q = z[..., :Q_W].reshape(B, L, N_HEADS, HEAD_DIM)
    kv = z[..., Q_W:GATE_OFF].reshape(B, L, 6, N_KV_HEADS, HEAD_DIM)
    kc, vc, ks, vs, kw, vw = (kv[:, :, i] for i in range(6))
    gates = jax.nn.sigmoid(z[..., GATE_OFF:U_OFF].astype(F32)).reshape(B, L, N_HEADS, 3)
    u = z[..., U_OFF:].reshape(B, L, S5_GROUPS, S5_GROUP)
    o_att, win_k, win_v = attend(q, kc, vc, ks, vs, kw, vw, gates)
    y5, h_last = s5_scan(u, h0, *s5p, d_skip)
    g5 = jax.nn.gelu(y5.reshape(B, L, S5_W)).astype(o_att.dtype)
    g5 = g5 * jax.nn.sigmoid(g5 @ w_glu)
    m = jnp.concatenate([o_att, g5], axis=-1) @ w_out
    x = x + gate * rmsnorm(m, g_post)
    return x, (kc, vc, ks, vs, win_k, win_v, h_last.real, h_last.imag)


def swiglu(x, w_g, w_u, w_d):
    return (jax.nn.silu(x @ w_g) * (x @ w_u)) @ w_d


def routed_experts(x, eidx, wts, w_g, w_u, w_d):
    T, K = eidx.shape
    n_assign = T * K
    flat_e = eidx.reshape(-1)
    order = jnp.argsort(flat_e)
    e_sorted = flat_e[order]
    counts = jnp.bincount(flat_e, length=N_EXPERTS)
    padded = (counts + EXPERT_BLOCK - 1) // EXPERT_BLOCK * EXPERT_BLOCK
    pad_end = jnp.cumsum(padded)
    pad_start = pad_end - padded
    start = jnp.cumsum(counts) - counts
    dest = pad_start[e_sorted] + jnp.arange(n_assign) - start[e_sorted]
    n_blk = -(-n_assign // EXPERT_BLOCK) + N_EXPERTS
    n_slot = n_blk * EXPERT_BLOCK
    rows = jnp.full((n_slot,), T, jnp.int32).at[dest].set((order // K).astype(jnp.int32))
    slot_w = jnp.zeros((n_slot,), F32).at[dest].set(wts.reshape(-1)[order].astype(F32))
    blk_e = jnp.minimum(jnp.searchsorted(pad_end, jnp.arange(n_blk) * EXPERT_BLOCK, side='right'), N_EXPERTS - 1)
    xz = jnp.concatenate([x, jnp.zeros((1, x.shape[-1]), x.dtype)], axis=0)

    def expert_block(args):
        r, wb, e = args
        xb = xz[r]
        yb = (jax.nn.silu(xb @ w_g[e]) * (xb @ w_u[e])) @ w_d[e]
        return yb.astype(F32) * wb[:, None]

    yb = lax.map(expert_block, (rows.reshape(n_blk, EXPERT_BLOCK), slot_w.reshape(n_blk, EXPERT_BLOCK), blk_e))
    out = jnp.zeros((T + 1, x.shape[-1]), F32).at[rows].add(yb.reshape(n_slot, -1))
    return out[:T]


def moe_ffn(h, w_router, b_router, w_eg, w_eu, w_ed, w_sg, w_su, w_sd):
    shp = h.shape
    x = h.reshape(-1, shp[-1])
    T = x.shape[0]
    s = jax.nn.sigmoid(jnp.einsum('td,de->te', x, w_router, preferred_element_type=F32))
    s_sel = s + b_router.astype(F32)
    grp_score = lax.top_k(s_sel.reshape(T, N_EXPERT_GROUPS, -1), 2)[0].sum(-1)
    _, gidx = lax.top_k(grp_score, TOPK_GROUPS)
    gmask = jax.nn.one_hot(gidx, N_EXPERT_GROUPS, dtype=F32).sum(-2) > 0
    emask = jnp.repeat(gmask, N_EXPERTS // N_EXPERT_GROUPS, axis=-1)
    _, eidx = lax.top_k(jnp.where(emask, s_sel, -jnp.inf), TOP_K)
    w = jnp.take_along_axis(s, eidx, axis=-1)
    w = w / jnp.sum(w, axis=-1, keepdims=True) * ROUTED_SCALE
    y = routed_experts(x, eidx, w, w_eg, w_eu, w_ed) + swiglu(x, w_sg, w_su, w_sd)
    return y.reshape(shp)


def ffn_sublayer(x, shift, scale, gate, g_pre, g_post, w_router, b_router, w_eg, w_eu, w_ed, w_sg, w_su, w_sd):
    h = rmsnorm(x, g_pre) * (1 + scale) + shift
    f = moe_ffn(h, w_router, b_router, w_eg, w_eu, w_ed, w_sg, w_su, w_sd).astype(x.dtype)
    return x + gate * rmsnorm(f, g_post)


def setup_inputs(seed: int = 0) -> dict:
    key = jax.random.key(seed)
    keys = list(jax.random.split(key, 64))

    def nrm(shape, scale):
        return jax.random.normal(keys.pop(), shape, F32) * scale

    n_pages = PAST_LEN // PAGE_SIZE
    n_used = DEC_BATCH * n_pages
    n_phys = n_used + (n_used + 3) // 4
    win_buf = min(WINDOW, PAST_LEN)
    page_table = jax.random.permutation(keys.pop(), n_phys)[:n_used].reshape(DEC_BATCH, n_pages).astype(jnp.int32)
    paged = (DEPTH, n_phys, PAGE_SIZE, N_KV_HEADS, HEAD_DIM)
    wshape = (DEPTH, DEC_BATCH, win_buf, N_KV_HEADS, HEAD_DIM)
    sshape = (DEPTH, DEC_BATCH, S5_GROUPS, S5_STATE)
    G, P, H = S5_GROUPS, S5_STATE, S5_GROUP
    return {
        'x_prompt': nrm((BATCH, SEQ, D_MODEL), 1.0),
        'x_sample': nrm((DEC_BATCH, DEC_SEQ, D_MODEL), 1.0),
        'c_prompt': nrm((BATCH, D_MODEL), 1.0),
        'c_sample': nrm((DEC_BATCH, D_MODEL), 1.0),
        'page_table': page_table,
        'cache_cmp_k': nrm(paged, 1.0),
        'cache_cmp_v': nrm(paged, 1.0),
        'cache_sel_k': nrm(paged, 1.0),
        'cache_sel_v': nrm(paged, 1.0),
        'cache_win_k': nrm(wshape, 1.0),
        'cache_win_v': nrm(wshape, 1.0),
        'state_s5_re': nrm(sshape, 0.5),
        'state_s5_im': nrm(sshape, 0.5),
        'w_ada': nrm((DEPTH, D_MODEL, 6 * D_MODEL), 0.5 * D_MODEL ** -0.5),
        'b_ada': nrm((DEPTH, 6 * D_MODEL), 0.02),
        'g_pre_mix': 1.0 + nrm((DEPTH, D_MODEL), 0.05),
        'g_post_mix': 1.0 + nrm((DEPTH, D_MODEL), 0.05),
        'g_pre_ffn': 1.0 + nrm((DEPTH, D_MODEL), 0.05),
        'g_post_ffn': 1.0 + nrm((DEPTH, D_MODEL), 0.05),
        'w_in': nrm((DEPTH, D_MODEL, N_IN), D_MODEL ** -0.5),
        'pe_cmp': nrm((DEPTH, CMP_BLOCK, HEAD_DIM), 0.1),
        'w_cmp_k1': nrm((DEPTH, CMP_BLOCK, HEAD_DIM, HEAD_DIM), (CMP_BLOCK * HEAD_DIM) ** -0.5),
        'w_cmp_k2': nrm((DEPTH, HEAD_DIM, HEAD_DIM), HEAD_DIM ** -0.5),
        'w_cmp_v1': nrm((DEPTH, CMP_BLOCK, HEAD_DIM, HEAD_DIM), (CMP_BLOCK * HEAD_DIM) ** -0.5),
        'w_cmp_v2': nrm((DEPTH, HEAD_DIM, HEAD_DIM), HEAD_DIM ** -0.5),
        'rel_bias': nrm((NUM_BUCKETS, N_HEADS), 0.5),
        'lam_re': -0.5 + nrm((DEPTH, G, P), 0.01),
        'lam_im': math.pi * jnp.arange(P, dtype=F32) + nrm((DEPTH, G, P), 0.01),
        'log_dt': jax.random.uniform(keys.pop(), (DEPTH, G), F32, math.log(1e-3), math.log(1e-1)),
        'b_re': nrm((DEPTH, G, P, H), (2 * H) ** -0.5),
        'b_im': nrm((DEPTH, G, P, H), (2 * H) ** -0.5),
        'c_re': nrm((DEPTH, G, H, P), (2 * P) ** -0.5),
        'c_im': nrm((DEPTH, G, H, P), (2 * P) ** -0.5),
        'd_skip': nrm((DEPTH, G, H), 1.0),
        'w_glu': nrm((DEPTH, S5_W, S5_W), S5_W ** -0.5),
        'w_out': nrm((DEPTH, D_MIX, D_MODEL), D_MIX ** -0.5),
        'w_router': nrm((DEPTH, D_MODEL, N_EXPERTS), D_MODEL ** -0.5),
        'b_router': nrm((DEPTH, N_EXPERTS), 0.01),
        'w_exp_gate': nrm((DEPTH, N_EXPERTS, D_MODEL, D_EXPERT), D_MODEL ** -0.5),
        'w_exp_up': nrm((DEPTH, N_EXPERTS, D_MODEL, D_EXPERT), D_MODEL ** -0.5),
        'w_exp_down': nrm((DEPTH, N_EXPERTS, D_EXPERT, D_MODEL), D_EXPERT ** -0.5),
        'w_sh_gate': nrm((DEPTH, D_MODEL, D_SHARED), D_MODEL ** -0.5),
        'w_sh_up': nrm((DEPTH, D_MODEL, D_SHARED), D_MODEL ** -0.5),
        'w_sh_down': nrm((DEPTH, D_SHARED, D_MODEL), D_SHARED ** -0.5),
    }


def reference(x_prompt, x_sample, c_prompt, c_sample, page_table,
              cache_cmp_k, cache_cmp_v, cache_sel_k, cache_sel_v, cache_win_k, cache_win_v,
              state_s5_re, state_s5_im,
              w_ada, b_ada, g_pre_mix, g_post_mix, g_pre_ffn, g_post_ffn,
              w_in, pe_cmp, w_cmp_k1, w_cmp_k2, w_cmp_v1, w_cmp_v2, rel_bias,
              lam_re, lam_im, log_dt, b_re, b_im, c_re, c_im, d_skip, w_glu, w_out,
              w_router, b_router, w_exp_gate, w_exp_up, w_exp_down, w_sh_gate, w_sh_up, w_sh_down):
    xp, xs = x_prompt, x_sample
    p_states, s_states = [], []
    for l in range(DEPTH):
        sh_mp, sc_mp, gt_mp, sh_fp, sc_fp, gt_fp = adaln(c_prompt, w_ada[l], b_ada[l])
        sh_ms, sc_ms, gt_ms, sh_fs, sc_fs, gt_fs = adaln(c_sample, w_ada[l], b_ada[l])
        s5p = s5_params(lam_re[l], lam_im[l], log_dt[l], b_re[l], b_im[l], c_re[l], c_im[l])
        cmp_w = (pe_cmp[l], w_cmp_k1[l], w_cmp_k2[l], w_cmp_v1[l], w_cmp_v2[l])
        attend_p = functools.partial(nsa_prompt, *cmp_w, rel_bias)
        attend_s = functools.partial(nsa_sample, page_table, cache_cmp_k[l], cache_cmp_v[l], cache_sel_k[l],
                                     cache_sel_v[l], cache_win_k[l], cache_win_v[l], *cmp_w, rel_bias)
        h0_p = jnp.zeros((xp.shape[0], S5_GROUPS, S5_STATE), jnp.complex64)
        h0_s = lax.complex(state_s5_re[l].astype(F32), state_s5_im[l].astype(F32))
        xp, st_p = mix_sublayer(xp, sh_mp, sc_mp, gt_mp, g_pre_mix[l], g_post_mix[l], w_in[l], attend_p,
                                h0_p, s5p, d_skip[l], w_glu[l], w_out[l])
        xs, st_s = mix_sublayer(xs, sh_ms, sc_ms, gt_ms, g_pre_mix[l], g_post_mix[l], w_in[l], attend_s,
                                h0_s, s5p, d_skip[l], w_glu[l], w_out[l])
        moe_w = (w_router[l], b_router[l], w_exp_gate[l], w_exp_up[l], w_exp_down[l], w_sh_gate[l], w_sh_up[l], w_sh_down[l])
        xp = ffn_sublayer(xp, sh_fp, sc_fp, gt_fp, g_pre_ffn[l], g_post_ffn[l], *moe_w)
        xs = ffn_sublayer(xs, sh_fs, sc_fs, gt_fs, g_pre_ffn[l], g_post_ffn[l], *moe_w)
        p_states.append(st_p)
        s_states.append(st_s)
    p_kc, p_vc, p_ks, p_vs, p_kw, p_vw, p_s5r, p_s5i = (jnp.stack(a) for a in zip(*p_states))
    s_kc, s_vc, s_ks, s_vs, s_kw, s_vw, s_s5r, s_s5i = (jnp.stack(a) for a in zip(*s_states))
    return (xp, xs, p_kc, p_vc, p_ks, p_vs, p_kw, p_vw, p_s5r, p_s5i,
            s_kc, s_vc, s_ks, s_vs, s_kw, s_vw, s_s5r, s_s5i)
```

```python
import functools
import math

import numpy as np
import jax
import jax.numpy as jnp
from jax import lax
from jax.experimental import pallas as pl
from jax.experimental.pallas import tpu as pltpu

F32 = jnp.float32
BF16 = jnp.bfloat16
I32 = jnp.int32

D_MODEL = 1024
N_HEADS = 8
HEAD_DIM = 64
N_KV = 2
GQ = N_HEADS // N_KV
Q_W = N_HEADS * HEAD_DIM
KV_W = N_KV * HEAD_DIM
S5_W = D_MODEL - Q_W
S5_H = 16
S5_G = S5_W // S5_H
S5_P = 64
GATE_OFF = Q_W + 6 * KV_W
U_OFF = GATE_OFF + 3 * N_HEADS
CMP_BLOCK = 32
SEL_BLOCK = 64
N_SEL = 16
WINDOW = 512
NUM_BUCKETS = 32
REL_MAX_DIST = 128
N_EXPERTS = 256
TOP_K = 8
N_EGROUPS = 8
TOPK_GROUPS = 4
EGROUP = N_EXPERTS // N_EGROUPS
D_EXPERT = 256
ROUTED_SCALE = 2.5
EPS = 1e-6
SCALE = HEAD_DIM ** -0.5
PAGE = 128

QT = 128
S5_CHUNK = 64
S5_ROWS = 8
MOE_BLK = 256
NEG = -2e30
M_INIT = -1e30
VMEM_LIMIT = 56 * 1024 * 1024
WT_ROWS = Q_W + 2 * KV_W + 32
HIGHEST = lax.Precision.HIGHEST


def _cp(*sem):
    return pltpu.CompilerParams(dimension_semantics=sem, vmem_limit_bytes=VMEM_LIMIT)


def _dot(a, b, precision=None):
    return jnp.dot(a, b, preferred_element_type=F32, precision=precision)


def _dot_nt(a, b):
    return lax.dot_general(a, b, (((1,), (1,)), ((), ())), preferred_element_type=F32)


def _sigmoid(x):
    return 1.0 / (1.0 + jnp.exp(-x))


def _silu(x):
    return x * _sigmoid(x)


def _rms(x, g):
    return x * lax.rsqrt(jnp.mean(x * x, axis=-1, keepdims=True) + EPS) * g


def _bucket_table(n):
    d = np.arange(n)
    exact = NUM_BUCKETS // 2
    nf = np.maximum(d, 1).astype(np.float32)
    large = exact + (np.log(nf / np.float32(exact)) / np.float32(math.log(REL_MAX_DIST / exact))
                     * np.float32(NUM_BUCKETS - exact)).astype(np.int32)
    return np.where(d < exact, d, np.minimum(large, NUM_BUCKETS - 1)).astype(np.int32)


def _ada_kernel(c_ref, w_ref, b_ref, o_ref):
    a = _silu(c_ref[...]).astype(BF16)
    o_ref[...] = _dot(a, w_ref[...].astype(BF16)) + b_ref[...]


def _adaln(c, w_ada, b_ada):
    n, d = c.shape
    n_out = w_ada.shape[1]
    tn = 1024
    return pl.pallas_call(
        _ada_kernel,
        grid=(n_out // tn,),
        in_specs=[pl.BlockSpec((n, d), lambda j: (0, 0)),
                  pl.BlockSpec((d, tn), lambda j: (0, j)),
                  pl.BlockSpec((1, tn), lambda j: (0, j))],
        out_specs=pl.BlockSpec((n, tn), lambda j: (0, j)),
        out_shape=jax.ShapeDtypeStruct((n, n_out), F32),
        compiler_params=_cp("arbitrary"),
        name="adaln",
    )(c, w_ada, b_ada.reshape(1, n_out))


def _pre_kernel(x_ref, sc_ref, sh_ref, g_ref, wn_ref, wt_ref,
                kc_ref, vc_ref, ks_ref, vs_ref, kw_ref, vw_ref, u_ref,
                ksb_ref, kwb_ref, qt_ref, vst_ref, vwt_ref, gt_ref):
    tm = x_ref.shape[0]
    h = _rms(x_ref[...], g_ref[...]) * (1.0 + sc_ref[0]) + sh_ref[0]
    hb = h.astype(BF16)
    zn = _dot(hb, wn_ref[...])
    zt = _dot_nt(wt_ref[...], hb)
    for j, ref in enumerate((kc_ref, vc_ref, ks_ref, vs_ref, kw_ref, vw_ref)):
        ref[...] = zn[:, j * KV_W:(j + 1) * KV_W]
    u_ref[...] = zn[:, 6 * KV_W:]
    for g in range(N_KV):
        ksb_ref[g] = zn[:, 2 * KV_W + g * HEAD_DIM:2 * KV_W + (g + 1) * HEAD_DIM].astype(BF16)
        kwb_ref[g] = zn[:, 4 * KV_W + g * HEAD_DIM:4 * KV_W + (g + 1) * HEAD_DIM].astype(BF16)
    qt_ref[...] = zt[:Q_W].astype(BF16)
    for g in range(N_KV):
        for c in range(tm // QT):
            r0 = Q_W + g * HEAD_DIM
            vst_ref[g, c] = zt[r0:r0 + HEAD_DIM, c * QT:(c + 1) * QT].astype(BF16)
            r1 = Q_W + KV_W + g * HEAD_DIM
            vwt_ref[g, c] = zt[r1:r1 + HEAD_DIM, c * QT:(c + 1) * QT].astype(BF16)
        r2 = Q_W + 2 * KV_W + g * 16
        gt_ref[g] = _sigmoid(zt[r2:r2 + 16])


def _pre_weights(w_in):
    wn = jnp.concatenate([w_in[:, Q_W:GATE_OFF], w_in[:, U_OFF:]], axis=1).astype(BF16)
    gate_cols = []
    for g in range(N_KV):
        for j in range(3):
            for hh in range(GQ):
                gate_cols.append(GATE_OFF + (g * GQ + hh) * 3 + j)
        gate_cols.extend([GATE_OFF] * 4)
    wt = jnp.concatenate([
        w_in[:, :Q_W] * SCALE,
        w_in[:, Q_W + 3 * KV_W:Q_W + 4 * KV_W],
        w_in[:, Q_W + 5 * KV_W:Q_W + 6 * KV_W],
        w_in[:, np.array(gate_cols)],
    ], axis=1).T.astype(BF16)
    return wn, wt


def _pre_mix(x, sc, sh, g_pre, wn, wt, tm, rows_per_mod):
    t = x.shape[0]
    r = sc.shape[1]
    nt = t // QT
    per = rows_per_mod // tm
    f = lambda shape: jax.ShapeDtypeStruct(shape, F32)
    b = lambda shape: jax.ShapeDtypeStruct(shape, BF16)
    kv_spec = pl.BlockSpec((tm, KV_W), lambda i: (i, 0))
    outs = pl.pallas_call(
        _pre_kernel,
        grid=(t // tm,),
        in_specs=[pl.BlockSpec((tm, D_MODEL), lambda i: (i, 0)),
                  pl.BlockSpec((1, r, D_MODEL), lambda i: (i // per, 0, 0)),
                  pl.BlockSpec((1, r, D_MODEL), lambda i: (i // per, 0, 0)),
                  pl.BlockSpec((1, D_MODEL), lambda i: (0, 0)),
                  pl.BlockSpec(wn.shape, lambda i: (0, 0)),
                  pl.BlockSpec(wt.shape, lambda i: (0, 0))],
        out_specs=[kv_spec] * 6 + [
            pl.BlockSpec((tm, S5_W), lambda i: (i, 0)),
            pl.BlockSpec((N_KV, tm, HEAD_DIM), lambda i: (0, i, 0)),
            pl.BlockSpec((N_KV, tm, HEAD_DIM), lambda i: (0, i, 0)),
            pl.BlockSpec((Q_W, tm), lambda i: (0, i)),
            pl.BlockSpec((N_KV, tm // QT, HEAD_DIM, QT), lambda i: (0, i, 0, 0)),
            pl.BlockSpec((N_KV, tm // QT, HEAD_DIM, QT), lambda i: (0, i, 0, 0)),
            pl.BlockSpec((N_KV, 16, tm), lambda i: (0, 0, i))],
        out_shape=[f((t, KV_W))] * 6 + [
            f((t, S5_W)), b((N_KV, t, HEAD_DIM)), b((N_KV, t, HEAD_DIM)), b((Q_W, t)),
            b((N_KV, nt, HEAD_DIM, QT)), b((N_KV, nt, HEAD_DIM, QT)), f((N_KV, 16, t))],
        compiler_params=_cp("arbitrary"),
        name="pre_mix",
    )(x, sc, sh, g_pre.reshape(1, D_MODEL), wn, wt)
    return outs


def _cmp_kernel(x_ref, pe_ref, w1_ref, w2_ref, o_ref):
    xb = (x_ref[...] + pe_ref[...]).astype(BF16)
    hid = _silu(_dot(xb, w1_ref[...]))
    o_ref[...] = _dot(hid.astype(BF16), w2_ref[...])


def _cmp_weights(pe, w1, w2):
    eye = jnp.eye(N_KV, dtype=F32)
    w1b = jnp.einsum('jde,gh->jgdhe', w1, eye).reshape(CMP_BLOCK * KV_W, KV_W).astype(BF16)
    w2b = jnp.einsum('ef,gh->gehf', w2, eye).reshape(KV_W, KV_W).astype(BF16)
    peb = jnp.broadcast_to(pe[:, None, :], (CMP_BLOCK, N_KV, HEAD_DIM)).reshape(1, CMP_BLOCK * KV_W)
    return peb, w1b, w2b


def _compress(x, peb, w1b, w2b):
    r, k = x.shape
    tr = min(256, r)
    return pl.pallas_call(
        _cmp_kernel,
        grid=(r // tr,),
        in_specs=[pl.BlockSpec((tr, k), lambda i: (i, 0)),
                  pl.BlockSpec((1, k), lambda i: (0, 0)),
                  pl.BlockSpec((k, KV_W), lambda i: (0, 0)),
                  pl.BlockSpec((KV_W, KV_W), lambda i: (0, 0))],
        out_specs=pl.BlockSpec((tr, KV_W), lambda i: (i, 0)),
        out_shape=jax.ShapeDtypeStruct((r, KV_W), F32),
        compiler_params=_cp("arbitrary"),
        name="compress",
    )(x, peb, w1b, w2b)


def _attn_kernel(q_ref, g_ref, kc_ref, vct_ref, ks_ref, vst_ref, kw_ref, vwt_ref,
                 ctab_ref, wtab_ref, o_ref, mb_ref, *, n_cmp):
    i = pl.program_id(2)
    lanes = GQ * QT
    qt = q_ref[...]
    q4 = jnp.concatenate([qt[hh * HEAD_DIM:(hh + 1) * HEAD_DIM] for hh in range(GQ)], axis=1)

    half = n_cmp // 2
    sc = _dot(kc_ref[0, 0], q4)
    rho = lax.broadcasted_iota(I32, (n_cmp, 16), 0)
    col = lax.broadcasted_iota(I32, (n_cmp, 16), 1)
    blk_n = 2 * jnp.where(rho >= half, rho - half, rho) + jnp.where(rho >= half, 1, 0)
    rel = blk_n - (4 * i - 4)
    want = jnp.where(rel < 0, 8, jnp.where(rel > 7, 9, rel))
    place = jnp.where(col == want, 1.0, 0.0).astype(F32)
    sc = sc + _dot(place, ctab_ref[0], precision=HIGHEST)
    mc = jnp.maximum(jnp.max(sc, axis=0, keepdims=True), M_INIT)
    ec = jnp.exp(sc - mc)
    pc = ec * (1.0 / jnp.maximum(jnp.sum(ec, axis=0, keepdims=True), 1e-30))
    o_c = _dot(vct_ref[0, 0], pc.astype(BF16))

    ps = pc[:half] + pc[half:]
    imp = ps[:, 0:QT]
    for hh in range(1, GQ):
        imp = imp + ps[:, hh * QT:(hh + 1) * QT]
    n_blk = half
    blk = lax.broadcasted_iota(I32, (n_blk, QT), 0)
    tq = i * QT + lax.broadcasted_iota(I32, (n_blk, QT), 1)
    cur = lax.shift_right_logical(tq, 6)
    forced = (blk == 0) | (blk == cur) | (blk == cur - 1)
    score = jnp.where(blk * SEL_BLOCK <= tq, jnp.where(forced, 1e4, imp), -jnp.inf)
    blkf = blk.astype(F32)

    def pick(_, carry):
        work, mb = carry
        mx = jnp.max(work, axis=0, keepdims=True)
        idx = jnp.min(jnp.where(work == mx, blkf, 1e9), axis=0, keepdims=True)
        hit = blkf == idx
        return jnp.where(hit, -jnp.inf, work), jnp.where(hit, 0.0, mb)

    _, mb = lax.fori_loop(0, min(N_SEL, n_blk), pick, (score, jnp.full((n_blk, QT), NEG, F32)))
    for hh in range(GQ):
        mb_ref[:, hh * QT:(hh + 1) * QT] = mb

    def flash(k_ref, vt_ref, lo, hi, masked):
        def body(kt, carry):
            m, l, acc = carry
            k_t = k_ref[0, pl.ds(pl.multiple_of(kt * QT, QT), QT), :]
            s = _dot(k_t, q4)
            delta = i - kt
            if masked:
                r0 = jnp.broadcast_to(mb_ref[pl.ds(2 * kt, 1), :], (SEL_BLOCK, lanes))
                r1 = jnp.broadcast_to(mb_ref[pl.ds(2 * kt + 1, 1), :], (SEL_BLOCK, lanes))
                s = s + jnp.concatenate([r0, r1], axis=0)
                delta = jnp.minimum(delta, 2)
            s = s + wtab_ref[0, delta]
            m_new = jnp.maximum(m, jnp.max(s, axis=0, keepdims=True))
            alpha = jnp.exp(m - m_new)
            p = jnp.exp(s - m_new)
            l = alpha * l + jnp.sum(p, axis=0, keepdims=True)
            acc = alpha * acc + _dot(vt_ref[0, kt], p.astype(BF16))
            return m_new, l, acc
        init = (jnp.full((1, lanes), M_INIT, F32), jnp.zeros((1, lanes), F32),
                jnp.zeros((HEAD_DIM, lanes), F32))
        _, l, acc = lax.fori_loop(lo, hi, body, init)
        return acc * (1.0 / jnp.maximum(l, 1e-30))

    o_s = flash(ks_ref, vst_ref, 0, i + 1, True)
    o_w = flash(kw_ref, vwt_ref, jnp.maximum(i - WINDOW // QT, 0), i + 1, False)

    gates = g_ref[0]
    grow = lambda j: jnp.concatenate([gates[j * GQ + hh:j * GQ + hh + 1] for hh in range(GQ)], axis=1)
    o_t = grow(0) * o_c + grow(1) * o_s + grow(2) * o_w
    o_hd = jnp.concatenate([o_t[:, hh * QT:(hh + 1) * QT] for hh in range(GQ)], axis=0)
    o_ref[...] = o_hd.T.astype(BF16)


def _bias_tables(rel_bias):
    bt = _bucket_table(WINDOW + 2 * QT)
    kl = np.arange(QT)[:, None]
    ql = np.arange(QT)[None, :]
    n_delta = WINDOW // QT + 1
    idx = np.zeros((n_delta, QT, QT), np.int32)
    ok = np.zeros((n_delta, QT, QT), bool)
    for dl in range(n_delta):
        d = dl * QT + ql - kl
        ok[dl] = (d >= 0) & (d < WINDOW)
        idx[dl] = bt[np.clip(d, 0, len(bt) - 1)]
    tab = jnp.where(ok[..., None], rel_bias[idx], NEG)
    wtab = tab.reshape(n_delta, QT, QT, N_KV, GQ).transpose(3, 0, 1, 4, 2).reshape(
        N_KV, n_delta, QT, GQ * QT)
    r = np.arange(8)[:, None]
    d = ql + (4 * CMP_BLOCK - CMP_BLOCK + 1) - CMP_BLOCK * r
    near = jnp.where((d >= 0)[..., None], rel_bias[bt[np.clip(d, 0, len(bt) - 1)]], NEG)
    far = jnp.broadcast_to(rel_bias[NUM_BUCKETS - 1][None, None, :], (1, QT, N_HEADS))
    rows = jnp.concatenate([near, far, jnp.full((1, QT, N_HEADS), NEG, F32),
                            jnp.zeros((6, QT, N_HEADS), F32)], axis=0)
    ctab = rows.reshape(16, QT, N_KV, GQ).transpose(2, 0, 3, 1).reshape(N_KV, 16, GQ * QT)
    return wtab.astype(F32), ctab.astype(F32)


def _attn_prompt(qt, gt, kcmp, vcmpt, ksb, vst, kwb, vwt, wtab, ctab, batch, seq):
    nq = seq // QT
    n_cmp = kcmp.shape[2]
    n_delta = wtab.shape[1]
    lanes = GQ * QT
    t = batch * seq
    return pl.pallas_call(
        functools.partial(_attn_kernel, n_cmp=n_cmp),
        grid=(batch, N_KV, nq),
        in_specs=[
            pl.BlockSpec((GQ * HEAD_DIM, QT), lambda b, g, i: (g, b * nq + i)),
            pl.BlockSpec((1, 16, QT), lambda b, g, i: (g, 0, b * nq + i)),
            pl.BlockSpec((1, 1, n_cmp, HEAD_DIM), lambda b, g, i: (b, g, 0, 0)),
            pl.BlockSpec((1, 1, HEAD_DIM, n_cmp), lambda b, g, i: (b, g, 0, 0)),
            pl.BlockSpec((1, seq, HEAD_DIM), lambda b, g, i: (g, b, 0)),
            pl.BlockSpec((1, nq, HEAD_DIM, QT), lambda b, g, i: (g, b, 0, 0)),
            pl.BlockSpec((1, seq, HEAD_DIM), lambda b, g, i: (g, b, 0)),
            pl.BlockSpec((1, nq, HEAD_DIM, QT), lambda b, g, i: (g, b, 0, 0)),
            pl.BlockSpec((1, 16, lanes), lambda b, g, i: (g, 0, 0)),
            pl.BlockSpec((1, n_delta, QT, lanes), lambda b, g, i: (g, 0, 0, 0)),
        ],
        out_specs=pl.BlockSpec((QT, GQ * HEAD_DIM), lambda b, g, i: (b * nq + i, g)),
        out_shape=jax.ShapeDtypeStruct((t, Q_W), BF16),
        scratch_shapes=[pltpu.VMEM((n_cmp // 2, lanes), F32)],
        compiler_params=_cp("arbitrary", "arbitrary", "arbitrary"),
        name="nsa_prompt",
    )(qt, gt, kcmp, vcmpt, ksb, vst, kwb, vwt, ctab, wtab)


def _s5_kernel(u_ref, mt_ref, sbr_ref, sbi_ref, ccr_ref, cci_ref, lb_ref, h0r_ref, h0i_ref,
               y_ref, hr_ref, hi_ref, sr_sc, si_sc, pr_sc, pi_sc, *, n_chunks, rb):
    u = u_ref[0]
    sr_sc[...] = _dot(u, sbr_ref[0])
    si_sc[...] = _dot(u, sbi_ref[0])
    lr = lb_ref[0, 0:1, :]
    li = lb_ref[0, 1:2, :]

    def step(c, carry):
        hr, hi = carry
        rows = pl.ds(pl.multiple_of(c * rb, rb), rb)
        pr_sc[rows, :] = hr
        pi_sc[rows, :] = hi
        return (lr * hr - li * hi + sr_sc[rows, :], lr * hi + li * hr + si_sc[rows, :])

    hr, hi = lax.fori_loop(0, n_chunks, step, (h0r_ref[0], h0i_ref[0]))
    hr_ref[0] = hr
    hi_ref[0] = hi
    y_ref[0] = (_dot(u, mt_ref[0]) + _dot(pr_sc[...].astype(BF16), ccr_ref[0])
                + _dot(pi_sc[...].astype(BF16), cci_ref[0]))


def _s5_prep(lam_re, lam_im, log_dt, b_re, b_im, c_re, c_im, d_skip, chunk):
    lam = lax.complex(lam_re, lam_im)
    z = lam * jnp.exp(log_dt)[:, None]
    lbar = jnp.exp(z)
    bbar = ((lbar - 1.0) / lam)[:, :, None] * lax.complex(b_re, b_im)
    c = lax.complex(c_re, c_im)
    pw = jnp.exp(z[None] * jnp.arange(chunk + 1, dtype=F32)[:, None, None])
    kern = jnp.einsum('gap,jgp,gpb->jgab', c, pw[:chunk], bbar, precision=HIGHEST).real
    kern = kern.at[0].add(jax.vmap(jnp.diag)(d_skip))
    lag = np.arange(chunk)[None, :] - np.arange(chunk)[:, None]
    kt = kern.transpose(1, 0, 3, 2)
    m5 = jnp.where((lag >= 0)[None, :, :, None, None], kt[:, np.clip(lag, 0, None)], 0.0)
    mt = m5.transpose(0, 1, 3, 2, 4).reshape(S5_G, chunk * S5_H, chunk * S5_H)
    sb = jnp.einsum('lgp,gph->glhp', pw[chunk - 1 - np.arange(chunk)], bbar).reshape(
        S5_G, chunk * S5_H, S5_P)
    cc = jnp.einsum('ghp,lgp->gplh', c, pw[1:]).reshape(S5_G, S5_P, chunk * S5_H)
    lb = jnp.stack([pw[chunk].real, pw[chunk].imag], axis=1)
    return (mt.astype(BF16), sb.real.astype(BF16), sb.imag.astype(BF16),
            cc.real.astype(BF16), (-cc.imag).astype(BF16), lb.astype(F32))


def _s5_call(u, prep, h0r, h0i, n_chunks, rb):
    mt, sbr, sbi, ccr, cci, lb = prep
    g, rows, w = u.shape
    per_g = lambda *shape: pl.BlockSpec((1,) + shape, lambda i: (i,) + (0,) * len(shape))
    return pl.pallas_call(
        functools.partial(_s5_kernel, n_chunks=n_chunks, rb=rb),
        grid=(g,),
        in_specs=[per_g(rows, w), per_g(w, w), per_g(w, S5_P), per_g(w, S5_P),
                  per_g(S5_P, w), per_g(S5_P, w), per_g(2, S5_P), per_g(rb, S5_P), per_g(rb, S5_P)],
        out_specs=[per_g(rows, w), per_g(rb, S5_P), per_g(rb, S5_P)],
        out_shape=[jax.ShapeDtypeStruct((g, rows, w), F32),
                   jax.ShapeDtypeStruct((g, rb, S5_P), F32),
                   jax.ShapeDtypeStruct((g, rb, S5_P), F32)],
        scratch_shapes=[pltpu.VMEM((rows, S5_P), F32)] * 4,
        compiler_params=_cp("arbitrary"),
        name="s5_scan",
    )(u, mt, sbr, sbi, ccr, cci, lb, h0r, h0i)


def _s5_prompt(u, prep, batch, seq):
    nc = seq // S5_CHUNK
    ub = u.reshape(batch, nc, S5_CHUNK, S5_G, S5_H).transpose(3, 1, 0, 2, 4)
    ub = jnp.pad(ub, ((0, 0), (0, 0), (0, S5_ROWS - batch), (0, 0), (0, 0)))
    ub = ub.reshape(S5_G, nc * S5_ROWS, S5_CHUNK * S5_H).astype(BF16)
    zero = jnp.zeros((S5_G, S5_ROWS, S5_P), F32)
    y, hr, hi = _s5_call(ub, prep, zero, zero, nc, S5_ROWS)
    y = y.reshape(S5_G, nc, S5_ROWS, S5_CHUNK, S5_H)[:, :, :batch].transpose(2, 1, 3, 0, 4)
    return (y.reshape(batch * seq, S5_W),
            hr[:, :batch].transpose(1, 0, 2), hi[:, :batch].transpose(1, 0, 2))


def _s5_sample(u, prep, h0r, h0i):
    n = u.shape[0]
    ub = u.reshape(n, S5_G, S5_H).transpose(1, 0, 2).astype(BF16)
    y, hr, hi = _s5_call(ub, prep, h0r.transpose(1, 0, 2), h0i.transpose(1, 0, 2), 1, n)
    return y.transpose(1, 0, 2).reshape(n, S5_W), hr.transpose(1, 0, 2), hi.transpose(1, 0, 2)


def _softmax_lanes(s):
    m = jnp.maximum(jnp.max(s, axis=-1, keepdims=True), M_INIT)
    e = jnp.exp(s - m)
    return e * (1.0 / jnp.maximum(jnp.sum(e, axis=-1, keepdims=True), 1e-30))


def _bqk(q, k):
    return jnp.einsum('bhd,bnd->bhn', q, k, preferred_element_type=F32)


def _bpv(p, v):
    return jnp.einsum('bhn,bnd->bhd', p.astype(BF16), v, preferred_element_type=F32)


def _sattn1_kernel(q_ref, kc_ref, vc_ref, bias_ref, oc_ref, pick_ref, *, n_pick):
    q = q_ref[...]
    p = _softmax_lanes(_bqk(q, kc_ref[...]) + bias_ref[...])
    oc_ref[...] = _bpv(p, vc_ref[...])
    n_cmp = p.shape[-1]
    half = n_cmp // 2
    ps = p[:, 0]
    for hh in range(1, GQ):
        ps = ps + p[:, hh]
    imp = ps[:, :half] + ps[:, half:]
    lane = lax.broadcasted_iota(I32, imp.shape, 1).astype(F32)
    work = jnp.where((lane > 0) & (lane < half - 1), imp, -jnp.inf)
    picks = jnp.zeros(imp.shape, F32)
    for t in range(n_pick):
        mx = jnp.max(work, axis=-1, keepdims=True)
        idx = jnp.min(jnp.where(work == mx, lane, 1e9), axis=-1, keepdims=True)
        work = jnp.where(lane == idx, -jnp.inf, work)
        picks = jnp.where(lane == t, idx, picks)
    pick_ref[...] = picks.astype(I32)


def _sattn2_kernel(q_ref, ks_ref, vs_ref, sb_ref, kw_ref, vw_ref, wb_ref, oc_ref, g_ref, o_ref):
    q = q_ref[...]
    p_s = _softmax_lanes(_bqk(q, ks_ref[...]) + sb_ref[...])
    o_s = _bpv(p_s, vs_ref[...])
    p_w = _softmax_lanes(_bqk(q, kw_ref[...].astype(BF16)) + wb_ref[...])
    o_w = _bpv(p_w, vw_ref[...].astype(BF16))
    g = g_ref[...]
    o_ref[...] = g[:, :, 0:1] * oc_ref[...] + g[:, :, 1:2] * o_s + g[:, :, 2:3] * o_w


def _attn_sample(q, gates, kcmp, vcmp, page_table, cache_sk, cache_sv, ks_new, vs_new,
                 win_k, win_v, rel_bias):
    n = q.shape[0]
    n_cmp = kcmp.shape[2]
    past = page_table.shape[1] * PAGE
    n_blk = past // SEL_BLOCK
    n_pick = N_SEL - 3
    nbq = n * N_KV
    bt = _bucket_table(past + 1)
    qg = q.reshape(nbq, GQ, HEAD_DIM)
    hsel = lambda tab: tab.reshape(-1, N_KV, GQ).transpose(1, 2, 0)

    order = np.concatenate([np.arange(0, n_cmp, 2), np.arange(1, n_cmp, 2)])
    d_c = past - (order * CMP_BLOCK + CMP_BLOCK - 1)
    cb = jnp.where((d_c >= 0)[:, None], rel_bias[bt[np.clip(d_c, 0, None)]], NEG)
    cb = jnp.broadcast_to(hsel(cb)[None], (n, N_KV, GQ, n_cmp)).reshape(nbq, GQ, n_cmp)

    nb = 16
    blk3 = lambda *s: pl.BlockSpec((nb,) + s, lambda i: (i,) + (0,) * len(s))
    o_c, picks = pl.pallas_call(
        functools.partial(_sattn1_kernel, n_pick=n_pick),
        grid=(nbq // nb,),
        in_specs=[blk3(GQ, HEAD_DIM), blk3(n_cmp, HEAD_DIM), blk3(n_cmp, HEAD_DIM), blk3(GQ, n_cmp)],
        out_specs=[blk3(GQ, HEAD_DIM), pl.BlockSpec((nb, n_cmp // 2), lambda i: (i, 0))],
        out_shape=[jax.ShapeDtypeStruct((nbq, GQ, HEAD_DIM), F32),
                   jax.ShapeDtypeStruct((nbq, n_cmp // 2), I32)],
        compiler_params=_cp("arbitrary"),
        name="nsa_sample_cmp",
    )(qg, kcmp.reshape(nbq, n_cmp, HEAD_DIM), vcmp.reshape(nbq, n_cmp, HEAD_DIM), cb)

    picks = picks[:, :n_pick].reshape(n, N_KV, n_pick)
    forced = jnp.broadcast_to(jnp.array([n_blk - 1, 0], I32), (n, N_KV, 2))
    sel = jnp.concatenate([forced, picks], axis=-1)
    n_slot = sel.shape[-1]
    page = jnp.take_along_axis(page_table, (sel // 2).reshape(n, -1), axis=1).reshape(sel.shape)
    row0 = (sel % 2) * SEL_BLOCK
    rows = row0[..., None] + jnp.arange(SEL_BLOCK)
    gi = jnp.arange(N_KV)[None, :, None, None]

    def gather(cache, new):
        got = cache[page[..., None], rows, gi].astype(BF16)
        got = got.reshape(n, N_KV, n_slot * SEL_BLOCK, HEAD_DIM)
        tail = jnp.zeros((n, N_KV, SEL_BLOCK, HEAD_DIM), BF16).at[:, :, 0].set(
            new.reshape(n, N_KV, HEAD_DIM).astype(BF16))
        return jnp.concatenate([got, tail], axis=2).reshape(nbq, (n_slot + 1) * SEL_BLOCK, HEAD_DIM)

    k_sel = gather(cache_sk, ks_new)
    v_sel = gather(cache_sv, vs_new)
    tok = (sel[..., None] * SEL_BLOCK + jnp.arange(SEL_BLOCK)).reshape(n, N_KV, -1)
    d_s = jnp.concatenate([past - tok, jnp.broadcast_to(-jnp.arange(SEL_BLOCK), (n, N_KV, SEL_BLOCK))],
                          axis=-1)
    sb = jnp.where((d_s >= 0)[..., None], rel_bias[jnp.asarray(bt)[jnp.clip(d_s, 0, past)]], NEG)
    sb = sb.reshape(n, N_KV, -1, N_KV, GQ)
    sb = jnp.stack([sb[:, g, :, g] for g in range(N_KV)], axis=1)
    sb = sb.transpose(0, 1, 3, 2).reshape(nbq, GQ, -1)

    n_win = win_k.shape[1]
    d_w = n_win - 1 - np.arange(n_win)
    wb = jnp.where((d_w < WINDOW)[:, None], rel_bias[bt[d_w]], NEG)
    wb = jnp.broadcast_to(hsel(wb)[None], (n, N_KV, GQ, n_win)).reshape(nbq, GQ, n_win)
    wk = win_k.transpose(0, 2, 1, 3).reshape(nbq, n_win, HEAD_DIM)
    wv = win_v.transpose(0, 2, 1, 3).reshape(nbq, n_win, HEAD_DIM)
    n_key = k_sel.shape[1]
    o = pl.pallas_call(
        _sattn2_kernel,
        grid=(nbq // nb,),
        in_specs=[blk3(GQ, HEAD_DIM), blk3(n_key, HEAD_DIM), blk3(n_key, HEAD_DIM), blk3(GQ, n_key),
                  blk3(n_win, HEAD_DIM), blk3(n_win, HEAD_DIM), blk3(GQ, n_win),
                  blk3(GQ, HEAD_DIM), blk3(GQ, 3)],
        out_specs=blk3(GQ, HEAD_DIM),
        out_shape=jax.ShapeDtypeStruct((nbq, GQ, HEAD_DIM), F32),
        compiler_params=_cp("arbitrary"),
        name="nsa_sample_sel_win",
    )(qg, k_sel, v_sel, sb, wk, wv, wb, o_c, gates.reshape(nbq, GQ, 3))
    return o.reshape(n, Q_W).astype(BF16)


def _gelu_tanh(x):
    return 0.5 * x * (1.0 + jnp.tanh(math.sqrt(2.0 / math.pi) * (x + 0.044715 * (x * x * x))))


def _post_kernel(x_ref, o_ref, y5_ref, gm_ref, shf_ref, scf_ref, gpost_ref, gpre_ref,
                 wglu_ref, wout_ref, wr_ref, br_ref, tri_ref, cin_ref,
                 x1_ref, h2_ref, eidx_ref, wts_ref, pos_ref, cout_ref, cnt_sc):
    i = pl.program_id(0)
    tm = x_ref.shape[0]

    @pl.when(i == 0)
    def _():
        cnt_sc[...] = cin_ref[...]

    g5 = _gelu_tanh(y5_ref[...])
    g5 = g5 * _sigmoid(_dot(g5.astype(BF16), wglu_ref[...]))
    m = _dot(o_ref[...], wout_ref[:Q_W]) + _dot(g5.astype(BF16), wout_ref[Q_W:])
    x1 = x_ref[...] + gm_ref[0] * _rms(m, gpost_ref[...])
    h2 = _rms(x1, gpre_ref[...]) * (1.0 + scf_ref[0]) + shf_ref[0]
    x1_ref[...] = x1
    h2_ref[...] = h2.astype(BF16)

    s_t = _sigmoid(_dot(h2, wr_ref[...], precision=HIGHEST)).T
    s_sel = s_t + br_ref[...]
    eio = lax.broadcasted_iota(I32, (N_EXPERTS, tm), 0).astype(F32)
    first = lambda hit, ids: jnp.min(jnp.where(hit, ids, 1e9), axis=0, keepdims=True)
    gscore = []
    for g in range(N_EGROUPS):
        xg = s_sel[g * EGROUP:(g + 1) * EGROUP]
        ig = lax.broadcasted_iota(I32, (EGROUP, tm), 0).astype(F32) + float(g * EGROUP)
        m1 = jnp.max(xg, axis=0, keepdims=True)
        m2 = jnp.max(jnp.where(ig == first(xg == m1, ig), -jnp.inf, xg), axis=0, keepdims=True)
        gscore.append(m1 + m2)
    gsc = jnp.concatenate(gscore, axis=0)
    gio = lax.broadcasted_iota(I32, (N_EGROUPS, tm), 0).astype(F32)
    gmask = jnp.zeros((N_EGROUPS, tm), F32)
    for _ in range(TOPK_GROUPS):
        hit = gio == first(gsc == jnp.max(gsc, axis=0, keepdims=True), gio)
        gmask = jnp.where(hit, 1.0, gmask)
        gsc = jnp.where(hit, -jnp.inf, gsc)
    work = jnp.concatenate(
        [jnp.where(gmask[g:g + 1] > 0.0, s_sel[g * EGROUP:(g + 1) * EGROUP], -jnp.inf)
         for g in range(N_EGROUPS)], axis=0)
    chosen = jnp.zeros((N_EXPERTS, tm), F32)
    ids, wsel = [], []
    for _ in range(TOP_K):
        ik = first(work == jnp.max(work, axis=0, keepdims=True), eio)
        hit = eio == ik
        ids.append(ik)
        wsel.append(jnp.sum(jnp.where(hit, s_t, 0.0), axis=0, keepdims=True))
        work = jnp.where(hit, -jnp.inf, work)
        chosen = jnp.where(hit, 1.0, chosen)
    rank = cnt_sc[:, 0:1] + _dot(chosen.astype(BF16), tri_ref[...])
    pos = [jnp.sum(jnp.where(eio == ik, rank, 0.0), axis=0, keepdims=True) for ik in ids]
    cnt_sc[...] = cnt_sc[...] + jnp.sum(chosen, axis=1, keepdims=True)
    wsum = wsel[0]
    for k in range(1, TOP_K):
        wsum = wsum + wsel[k]
    eidx_ref[...] = jnp.concatenate(ids, axis=0).astype(I32)
    pos_ref[...] = jnp.concatenate(pos, axis=0).astype(I32)
    wts_ref[...] = jnp.concatenate([w / wsum * ROUTED_SCALE for w in wsel], axis=0)
    cout_ref[...] = cnt_sc[...]


def _post_mix(x, o_att, y5, gm, shf, scf, g_post, g_pre, wglu, wout, w_router, b_router,
              cnt_in, tm, rows_per_mod):
    t = x.shape[0]
    r = gm.shape[1]
    per = rows_per_mod // tm
    mod = pl.BlockSpec((1, r, D_MODEL), lambda i: (i // per, 0, 0))
    full = lambda a: pl.BlockSpec(a.shape, lambda i: (0,) * a.ndim)
    vec = pl.BlockSpec((1, D_MODEL), lambda i: (0, 0))
    tri = jnp.asarray(np.triu(np.ones((tm, tm), np.float32), 1), BF16)
    brb = jnp.broadcast_to(b_router[:, None], (N_EXPERTS, tm)).astype(F32)
    route = pl.BlockSpec((TOP_K, tm), lambda i: (0, i))
    return pl.pallas_call(
        _post_kernel,
        grid=(t // tm,),
        in_specs=[pl.BlockSpec((tm, D_MODEL), lambda i: (i, 0)),
                  pl.BlockSpec((tm, Q_W), lambda i: (i, 0)),
                  pl.BlockSpec((tm, S5_W), lambda i: (i, 0)),
                  mod, mod, mod, vec, vec, full(wglu), full(wout), full(w_router), full(brb),
                  full(tri), full(cnt_in)],
        out_specs=[pl.BlockSpec((tm, D_MODEL), lambda i: (i, 0)),
                   pl.BlockSpec((tm, D_MODEL), lambda i: (i, 0)),
                   route, route, route, full(cnt_in)],
        out_shape=[jax.ShapeDtypeStruct((t, D_MODEL), F32),
                   jax.ShapeDtypeStruct((t, D_MODEL), BF16),
                   jax.ShapeDtypeStruct((TOP_K, t), I32),
                   jax.ShapeDtypeStruct((TOP_K, t), F32),
                   jax.ShapeDtypeStruct((TOP_K, t), I32),
                   jax.ShapeDtypeStruct(cnt_in.shape, F32)],
        scratch_shapes=[pltpu.VMEM(cnt_in.shape, F32)],
        compiler_params=_cp("arbitrary"),
        name="post_mix_router",
    )(x, o_att, y5, gm, shf, scf, g_post.reshape(1, D_MODEL), g_pre.reshape(1, D_MODEL),
      wglu, wout, w_router, brb, tri, cnt_in)


def _moe_kernel(be_ref, nb_ref, x_ref, wg_ref, wu_ref, wd_ref, y_ref):
    @pl.when(pl.program_id(0) < nb_ref[0])
    def _():
        x = x_ref[...]
        a = _dot(x, wg_ref[0].astype(BF16))
        b = _dot(x, wu_ref[0].astype(BF16))
        y_ref[...] = _dot((_silu(a) * b).astype(BF16), wd_ref[0].astype(BF16))


def _moe(xs, blk_e, nb_used, w_g, w_u, w_d):
    n_slot = xs.shape[0]
    n_blk = n_slot // MOE_BLK
    row = lambda j, be, nb: (jnp.minimum(j, nb[0] - 1), 0)
    wsel = lambda j, be, nb: (be[jnp.minimum(j, nb[0] - 1)], 0, 0)
    return pl.pallas_call(
        _moe_kernel,
        grid_spec=pltpu.PrefetchScalarGridSpec(
            num_scalar_prefetch=2,
            grid=(n_blk,),
            in_specs=[pl.BlockSpec((MOE_BLK, D_MODEL), row),
                      pl.BlockSpec((1, D_MODEL, D_EXPERT), wsel),
                      pl.BlockSpec((1, D_MODEL, D_EXPERT), wsel),
                      pl.BlockSpec((1, D_EXPERT, D_MODEL), wsel)],
            out_specs=pl.BlockSpec((MOE_BLK, D_MODEL), row)),
        out_shape=jax.ShapeDtypeStruct((n_slot, D_MODEL), F32),
        compiler_params=_cp("arbitrary"),
        name="moe_experts",
    )(blk_e, nb_used, xs, w_g, w_u, w_d)


def _fin_kernel(x1_ref, h2_ref, yg_ref, w_ref, gf_ref, gpost_ref, wsg_ref, wsu_ref, wsd_ref, o_ref):
    w = w_ref[...]
    f = w[:, 0:1] * yg_ref[0]
    for k in range(1, TOP_K):
        f = f + w[:, k:k + 1] * yg_ref[k]
    hb = h2_ref[...]
    sh = (_silu(_dot(hb, wsg_ref[...])) * _dot(hb, wsu_ref[...])).astype(BF16)
    f = f + _dot(sh, wsd_ref[...])
    o_ref[...] = x1_ref[...] + gf_ref[0] * _rms(f, gpost_ref[...])


def _final(x1, h2, yg, wts, gf, g_post, wsg, wsu, wsd, tm, rows_per_mod):
    t = x1.shape[0]
    r = gf.shape[1]
    per = rows_per_mod // tm
    full = lambda a: pl.BlockSpec(a.shape, lambda i: (0,) * a.ndim)
    return pl.pallas_call(
        _fin_kernel,
        grid=(t // tm,),
        in_specs=[pl.BlockSpec((tm, D_MODEL), lambda i: (i, 0)),
                  pl.BlockSpec((tm, D_MODEL), lambda i: (i, 0)),
                  pl.BlockSpec((TOP_K, tm, D_MODEL), lambda i: (0, i, 0)),
                  pl.BlockSpec((tm, TOP_K), lambda i: (i, 0)),
                  pl.BlockSpec((1, r, D_MODEL), lambda i: (i // per, 0, 0)),
                  pl.BlockSpec((1, D_MODEL), lambda i: (0, 0)),
                  full(wsg), full(wsu), full(wsd)],
        out_specs=pl.BlockSpec((tm, D_MODEL), lambda i: (i, 0)),
        out_shape=jax.ShapeDtypeStruct((t, D_MODEL), F32),
        compiler_params=_cp("arbitrary"),
        name="moe_combine_final",
    )(x1, h2, yg, wts, gf, g_post.reshape(1, D_MODEL), wsg, wsu, wsd)


def _even_odd(a, axis):
    n = a.shape[axis]
    order = np.concatenate([np.arange(0, n, 2), np.arange(1, n, 2)])
    return jnp.take(a, order, axis=axis)


def _layer(xp, xs, c_prompt, c_sample, page_table, ck_c, cv_c, ck_s, cv_s, cw_k, cw_v, st_re, st_im,
           w_ada, b_ada, g_pre_mix, g_post_mix, g_pre_ffn, g_post_ffn, w_in, pe, wk1, wk2, wv1, wv2,
           rel_bias, lam_re, lam_im, log_dt, b_re, b_im, c_re, c_im, d_skip, w_glu, w_out,
           w_router, b_router, w_eg, w_eu, w_ed, w_sg, w_su, w_sd):
    batch, seq, _ = xp.shape
    n_dec = xs.shape[0]
    tp = batch * seq
    x_p = xp.reshape(tp, D_MODEL)
    x_s = xs.reshape(n_dec, D_MODEL)

    n_c = batch + n_dec
    n_pad = -(-n_c // 8) * 8
    c_all = jnp.pad(jnp.concatenate([c_prompt, c_sample], axis=0), ((0, n_pad - n_c), (0, 0)))
    mod = _adaln(c_all, w_ada, b_ada)
    mod_p = [m.reshape(batch, 1, D_MODEL) for m in jnp.split(mod[:batch], 6, axis=-1)]
    mod_s = [m.reshape(1, n_dec, D_MODEL) for m in jnp.split(mod[batch:n_c], 6, axis=-1)]

    wn, wt = _pre_weights(w_in)
    tm_p = 512
    (kc_p, vc_p, ks_p, vs_p, kw_p, vw_p, u_p, ksb, kwb, qt_p, vst, vwt, gt_p) = _pre_mix(
        x_p, mod_p[1], mod_p[0], g_pre_mix, wn, wt, tm_p, seq)
    (kc_s, vc_s, ks_s, vs_s, kw_s, vw_s, u_s, _, _, qt_s, _, _, gt_s) = _pre_mix(
        x_s, mod_s[1], mod_s[0], g_pre_mix, wn, wt, n_dec, n_dec)

    cw_kk = _cmp_weights(pe, wk1, wk2)
    cw_vv = _cmp_weights(pe, wv1, wv2)
    n_cmp_p = seq // CMP_BLOCK
    kcmp_p = _compress(kc_p.reshape(batch * n_cmp_p, CMP_BLOCK * KV_W), *cw_kk)
    vcmp_p = _compress(vc_p.reshape(batch * n_cmp_p, CMP_BLOCK * KV_W), *cw_vv)
    kcmp_p = _even_odd(kcmp_p.reshape(batch, n_cmp_p, N_KV, HEAD_DIM), 1).transpose(0, 2, 1, 3)
    vcmp_p = _even_odd(vcmp_p.reshape(batch, n_cmp_p, N_KV, HEAD_DIM), 1).transpose(0, 2, 3, 1)
    wtab, ctab = _bias_tables(rel_bias)
    o_p = _attn_prompt(qt_p, gt_p, kcmp_p.astype(BF16), vcmp_p.astype(BF16), ksb, vst, kwb, vwt,
                       wtab, ctab, batch, seq)

    n_pages = page_table.shape[1]
    past = n_pages * PAGE
    n_cmp_s = (past + 1) // CMP_BLOCK
    kcmp_s = _compress(ck_c[page_table].reshape(n_dec * n_cmp_s, CMP_BLOCK * KV_W), *cw_kk)
    vcmp_s = _compress(cv_c[page_table].reshape(n_dec * n_cmp_s, CMP_BLOCK * KV_W), *cw_vv)
    kcmp_s = _even_odd(kcmp_s.reshape(n_dec, n_cmp_s, N_KV, HEAD_DIM), 1).transpose(0, 2, 1, 3)
    vcmp_s = _even_odd(vcmp_s.reshape(n_dec, n_cmp_s, N_KV, HEAD_DIM), 1).transpose(0, 2, 1, 3)
    win_k = jnp.concatenate([cw_k[:, 1:], kw_s.reshape(n_dec, 1, N_KV, HEAD_DIM)], axis=1)
    win_v = jnp.concatenate([cw_v[:, 1:], vw_s.reshape(n_dec, 1, N_KV, HEAD_DIM)], axis=1)
    q_s = qt_s.T.reshape(n_dec, N_HEADS, HEAD_DIM)
    gates_s = gt_s[:, :3 * GQ].reshape(N_KV, 3, GQ, n_dec).transpose(3, 0, 2, 1).reshape(n_dec, N_HEADS, 3)
    o_s = _attn_sample(q_s, gates_s, kcmp_s.astype(BF16), vcmp_s.astype(BF16), page_table,
                       ck_s, cv_s, ks_s, vs_s, win_k, win_v, rel_bias)

    y5_p, s5r_p, s5i_p = _s5_prompt(
        u_p, _s5_prep(lam_re, lam_im, log_dt, b_re, b_im, c_re, c_im, d_skip, S5_CHUNK), batch, seq)
    y5_s, s5r_s, s5i_s = _s5_sample(
        u_s, _s5_prep(lam_re, lam_im, log_dt, b_re, b_im, c_re, c_im, d_skip, 1), st_re, st_im)

    wglu_b, wout_b = w_glu.astype(BF16), w_out.astype(BF16)
    cnt0 = jnp.zeros((N_EXPERTS, 128), F32)
    tm_q = 256
    x1_p, h2_p, e_p, wt_p, pos_p, cnt1 = _post_mix(
        x_p, o_p, y5_p, mod_p[2], mod_p[3], mod_p[4], g_post_mix, g_pre_ffn, wglu_b, wout_b,
        w_router, b_router, cnt0, tm_q, seq)
    x1_s, h2_s, e_s, wt_s, pos_s, cnt2 = _post_mix(
        x_s, o_s, y5_s, mod_s[2], mod_s[3], mod_s[4], g_post_mix, g_pre_ffn, wglu_b, wout_b,
        w_router, b_router, cnt1, n_dec, n_dec)

    t_all = tp + n_dec
    eidx = jnp.concatenate([e_p, e_s], axis=1)
    pos = jnp.concatenate([pos_p, pos_s], axis=1)
    counts = cnt2[:, 0].astype(I32)
    padded = (counts + MOE_BLK - 1) // MOE_BLK * MOE_BLK
    pad_end = jnp.cumsum(padded)
    dest = (pad_end - padded)[eidx] + pos
    n_blk = -(-(t_all * TOP_K) // MOE_BLK) + N_EXPERTS
    n_slot = n_blk * MOE_BLK
    tok = jnp.broadcast_to(jnp.arange(t_all, dtype=I32)[None], (TOP_K, t_all))
    rows = jnp.full((n_slot,), t_all, I32).at[dest.reshape(-1)].set(tok.reshape(-1))
    h2z = jnp.concatenate([h2_p, h2_s, jnp.zeros((1, D_MODEL), BF16)], axis=0)
    xs_rows = h2z[rows]
    blk_e = jnp.minimum(jnp.searchsorted(pad_end, jnp.arange(n_blk, dtype=I32) * MOE_BLK, side='right'),
                        N_EXPERTS - 1).astype(I32)
    nb_used = (pad_end[-1:] // MOE_BLK).astype(I32)
    ys = _moe(xs_rows, blk_e, nb_used, w_eg, w_eu, w_ed)
    yg_p = ys[dest[:, :tp]]
    yg_s = ys[dest[:, tp:]]

    wsg, wsu, wsd = w_sg.astype(BF16), w_su.astype(BF16), w_sd.astype(BF16)
    out_p = _final(x1_p, h2_p, yg_p, wt_p.T, mod_p[5], g_post_ffn, wsg, wsu, wsd, tm_q, seq)
    out_s = _final(x1_s, h2_s, yg_s, wt_s.T, mod_s[5], g_post_ffn, wsg, wsu, wsd, n_dec, n_dec)

    kv5 = lambda a, n, l: a.reshape(n, l, N_KV, HEAD_DIM)
    n_win = min(WINDOW, seq)
    st_p = (kv5(kc_p, batch, seq), kv5(vc_p, batch, seq), kv5(ks_p, batch, seq), kv5(vs_p, batch, seq),
            kv5(kw_p, batch, seq)[:, seq - n_win:], kv5(vw_p, batch, seq)[:, seq - n_win:], s5r_p, s5i_p)
    st_s = (kv5(kc_s, n_dec, 1), kv5(vc_s, n_dec, 1), kv5(ks_s, n_dec, 1), kv5(vs_s, n_dec, 1),
            win_k, win_v, s5r_s, s5i_s)
    return out_p.reshape(batch, seq, D_MODEL), out_s.reshape(n_dec, 1, D_MODEL), st_p, st_s


def kernel(x_prompt, x_sample, c_prompt, c_sample, page_table, cache_cmp_k, cache_cmp_v, cache_sel_k, cache_sel_v, cache_win_k, cache_win_v, state_s5_re, state_s5_im, w_ada, b_ada, g_pre_mix, g_post_mix, g_pre_ffn, g_post_ffn, w_in, pe_cmp, w_cmp_k1, w_cmp_k2, w_cmp_v1, w_cmp_v2, rel_bias, lam_re, lam_im, log_dt, b_re, b_im, c_re, c_im, d_skip, w_glu, w_out, w_router, b_router, w_exp_gate, w_exp_up, w_exp_down, w_sh_gate, w_sh_up, w_sh_down):
    depth = w_in.shape[0]
    xp, xs = x_prompt, x_sample
    p_states, s_states = [], []
    for l in range(depth):
        xp, xs, st_p, st_s = _layer(
            xp, xs, c_prompt, c_sample, page_table, cache_cmp_k[l], cache_cmp_v[l], cache_sel_k[l],
            cache_sel_v[l], cache_win_k[l], cache_win_v[l], state_s5_re[l], state_s5_im[l],
            w_ada[l], b_ada[l], g_pre_mix[l], g_post_mix[l], g_pre_ffn[l], g_post_ffn[l], w_in[l],
            pe_cmp[l], w_cmp_k1[l], w_cmp_k2[l], w_cmp_v1[l], w_cmp_v2[l], rel_bias,
            lam_re[l], lam_im[l], log_dt[l], b_re[l], b_im[l], c_re[l], c_im[l], d_skip[l],
            w_glu[l], w_out[l], w_router[l], b_router[l], w_exp_gate[l], w_exp_up[l], w_exp_down[l],
            w_sh_gate[l], w_sh_up[l], w_sh_down[l])
        p_states.append(st_p)
        s_states.append(st_s)
    p_st = tuple(jnp.stack(a) for a in zip(*p_states))
    s_st = tuple(jnp.stack(a) for a in zip(*s_states))
    return (xp, xs) + p_st + s_st
```

```python
import functools
import math

import numpy as np
import jax
import jax.numpy as jnp
from jax import lax
from jax.experimental import pallas as pl
from jax.experimental.pallas import tpu as pltpu

F32 = jnp.float32
BF16 = jnp.bfloat16
I32 = jnp.int32
U32 = jnp.uint32

D_MODEL = 1024
N_HEADS = 8
HEAD_DIM = 64
N_KV = 2
GQ = N_HEADS // N_KV
Q_W = N_HEADS * HEAD_DIM
KV_W = N_KV * HEAD_DIM
S5_W = D_MODEL - Q_W
S5_H = 16
S5_G = S5_W // S5_H
S5_P = 64
GATE_OFF = Q_W + 6 * KV_W
U_OFF = GATE_OFF + 3 * N_HEADS
CMP_BLOCK = 32
SEL_BLOCK = 64
N_SEL = 16
WINDOW = 512
NUM_BUCKETS = 32
REL_MAX_DIST = 128
N_EXPERTS = 256
TOP_K = 8
N_EGROUPS = 8
TOPK_GROUPS = 4
EGROUP = N_EXPERTS // N_EGROUPS
D_EXPERT = 256
ROUTED_SCALE = 2.5
EPS = 1e-6
SCALE = HEAD_DIM ** -0.5
PAGE = 128
CMP_PER_PAGE = PAGE // CMP_BLOCK

QT = 128
VT = 2 * QT
N_DELTA = WINDOW // QT + 1
S5_CHUNK = 64
S5_ROWS = 8
MOE_BLK = 256
NEG = -2e30
M_INIT = -1e30
VMEM_LIMIT = 56 * 1024 * 1024
WT_ROWS = Q_W + 6 * KV_W + 32
HIGHEST = lax.Precision.HIGHEST


def _cp(*sem):
    return pltpu.CompilerParams(dimension_semantics=sem, vmem_limit_bytes=VMEM_LIMIT)


def _dot(a, b, precision=None):
    return jnp.dot(a, b, preferred_element_type=F32, precision=precision)


def _dot_nt(a, b):
    return lax.dot_general(a, b, (((1,), (1,)), ((), ())), preferred_element_type=F32)


def _sigmoid(x):
    return 1.0 / (1.0 + jnp.exp(-x))


def _silu(x):
    return x * _sigmoid(x)


def _rms(x, g):
    return x * lax.rsqrt(jnp.mean(x * x, axis=-1, keepdims=True) + EPS) * g


def _bucket_table(n):
    d = np.arange(n)
    exact = NUM_BUCKETS // 2
    nf = np.maximum(d, 1).astype(np.float32)
    large = exact + (np.log(nf / np.float32(exact)) / np.float32(math.log(REL_MAX_DIST / exact))
                     * np.float32(NUM_BUCKETS - exact)).astype(np.int32)
    return np.where(d < exact, d, np.minimum(large, NUM_BUCKETS - 1)).astype(np.int32)


def _ada_kernel(c_ref, w_ref, b_ref, o_ref):
    a = _silu(c_ref[...]).astype(BF16)
    o_ref[...] = _dot(a, w_ref[...].astype(BF16)) + b_ref[...]


def _adaln(c, w_ada, b_ada):
    n, d = c.shape
    n_out = w_ada.shape[1]
    tn = 1024
    return pl.pallas_call(
        _ada_kernel,
        grid=(n_out // tn,),
        in_specs=[pl.BlockSpec((n, d), lambda j: (0, 0)),
                  pl.BlockSpec((d, tn), lambda j: (0, j)),
                  pl.BlockSpec((1, tn), lambda j: (0, j))],
        out_specs=pl.BlockSpec((n, tn), lambda j: (0, j)),
        out_shape=jax.ShapeDtypeStruct((n, n_out), F32),
        compiler_params=_cp("arbitrary"),
        name="adaln",
    )(c, w_ada, b_ada.reshape(1, n_out))


def _pre_project(x_ref, sc_ref, sh_ref, g_ref, wn_ref, wt_ref):
    h = _rms(x_ref[...], g_ref[...]) * (1.0 + sc_ref[0]) + sh_ref[0]
    hb = h.astype(BF16)
    return _dot(hb, wn_ref[...]), _dot_nt(wt_ref[...], hb)


def _pre_prompt_kernel(x_ref, sc_ref, sh_ref, g_ref, wn_ref, wt_ref,
                       kct_ref, vct_ref, kst_ref, vst32_ref, kwt_ref, vwt32_ref, kc_ref, vc_ref, u_ref,
                       ksb_ref, kwb_ref, qt_ref, vst_ref, vwt_ref, gt_ref):
    tm = x_ref.shape[0]
    zn, zt = _pre_project(x_ref, sc_ref, sh_ref, g_ref, wn_ref, wt_ref)
    trans = lambda j, g: zt[Q_W + j * KV_W + g * HEAD_DIM:Q_W + j * KV_W + (g + 1) * HEAD_DIM]
    for j, ref in enumerate((kct_ref, vct_ref, kst_ref, vst32_ref, kwt_ref, vwt32_ref)):
        for g in range(N_KV):
            ref[0, g] = trans(j, g)
    kc_ref[...] = zn[:, 0:KV_W]
    vc_ref[...] = zn[:, KV_W:2 * KV_W]
    u_ref[...] = zn[:, 6 * KV_W:]
    for g in range(N_KV):
        ksb_ref[g] = zn[:, 2 * KV_W + g * HEAD_DIM:2 * KV_W + (g + 1) * HEAD_DIM].astype(BF16)
        kwb_ref[g] = zn[:, 4 * KV_W + g * HEAD_DIM:4 * KV_W + (g + 1) * HEAD_DIM].astype(BF16)
        vs_t = trans(3, g).astype(BF16)
        vw_t = trans(5, g).astype(BF16)
        for c in range(tm // VT):
            vst_ref[g, c] = vs_t[:, c * VT:(c + 1) * VT]
        for c in range(tm // QT):
            vwt_ref[g, c] = vw_t[:, c * QT:(c + 1) * QT]
        r2 = Q_W + 6 * KV_W + g * 16
        gt_ref[g] = _sigmoid(zt[r2:r2 + 16])
    qt_ref[...] = zt[:Q_W].astype(BF16)


def _pre_sample_kernel(x_ref, sc_ref, sh_ref, g_ref, wn_ref, wt_ref,
                       kc_ref, vc_ref, ks_ref, vs_ref, kw_ref, vw_ref, u_ref, qt_ref, gt_ref):
    zn, zt = _pre_project(x_ref, sc_ref, sh_ref, g_ref, wn_ref, wt_ref)
    for j, ref in enumerate((kc_ref, vc_ref, ks_ref, vs_ref, kw_ref, vw_ref)):
        ref[...] = zn[:, j * KV_W:(j + 1) * KV_W]
    u_ref[...] = zn[:, 6 * KV_W:]
    qt_ref[...] = zt[:Q_W].astype(BF16)
    for g in range(N_KV):
        r2 = Q_W + 6 * KV_W + g * 16
        gt_ref[g] = _sigmoid(zt[r2:r2 + 16])


def _pre_weights(w_in):
    wn = jnp.concatenate([w_in[:, Q_W:GATE_OFF], w_in[:, U_OFF:]], axis=1).astype(BF16)
    gate_cols = []
    for g in range(N_KV):
        for j in range(3):
            for hh in range(GQ):
                gate_cols.append(GATE_OFF + (g * GQ + hh) * 3 + j)
        gate_cols.extend([GATE_OFF] * 4)
    wt = jnp.concatenate([
        w_in[:, :Q_W] * SCALE,
        w_in[:, Q_W:GATE_OFF],
        w_in[:, np.array(gate_cols)],
    ], axis=1).T.astype(BF16)
    return wn, wt


def _pre_in_specs(tm, r, per, wn, wt):
    return [pl.BlockSpec((tm, D_MODEL), lambda i: (i, 0)),
            pl.BlockSpec((1, r, D_MODEL), lambda i: (i // per, 0, 0)),
            pl.BlockSpec((1, r, D_MODEL), lambda i: (i // per, 0, 0)),
            pl.BlockSpec((1, D_MODEL), lambda i: (0, 0)),
            pl.BlockSpec(wn.shape, lambda i: (0, 0)),
            pl.BlockSpec(wt.shape, lambda i: (0, 0))]


def _pre_mix_prompt(x, sc, sh, g_pre, wn, wt, tm, batch, seq):
    t = x.shape[0]
    per = seq // tm
    f = lambda shape: jax.ShapeDtypeStruct(shape, F32)
    b = lambda shape: jax.ShapeDtypeStruct(shape, BF16)
    tr_spec = pl.BlockSpec((1, N_KV, HEAD_DIM, tm), lambda i: (i // per, 0, 0, i % per))
    kv_spec = pl.BlockSpec((tm, KV_W), lambda i: (i, 0))
    return pl.pallas_call(
        _pre_prompt_kernel,
        grid=(t // tm,),
        in_specs=_pre_in_specs(tm, 1, per, wn, wt),
        out_specs=[tr_spec] * 6 + [
            kv_spec, kv_spec,
            pl.BlockSpec((tm, S5_W), lambda i: (i, 0)),
            pl.BlockSpec((N_KV, tm, HEAD_DIM), lambda i: (0, i, 0)),
            pl.BlockSpec((N_KV, tm, HEAD_DIM), lambda i: (0, i, 0)),
            pl.BlockSpec((Q_W, tm), lambda i: (0, i)),
            pl.BlockSpec((N_KV, tm // VT, HEAD_DIM, VT), lambda i: (0, i, 0, 0)),
            pl.BlockSpec((N_KV, tm // QT, HEAD_DIM, QT), lambda i: (0, i, 0, 0)),
            pl.BlockSpec((N_KV, 16, tm), lambda i: (0, 0, i))],
        out_shape=[f((batch, N_KV, HEAD_DIM, seq))] * 6 + [
            f((t, KV_W)), f((t, KV_W)),
            f((t, S5_W)), b((N_KV, t, HEAD_DIM)), b((N_KV, t, HEAD_DIM)), b((Q_W, t)),
            b((N_KV, t // VT, HEAD_DIM, VT)), b((N_KV, t // QT, HEAD_DIM, QT)), f((N_KV, 16, t))],
        compiler_params=_cp("arbitrary"),
        name="pre_mix_prompt",
    )(x, sc, sh, g_pre.reshape(1, D_MODEL), wn, wt)


def _pre_mix_sample(x, sc, sh, g_pre, wn, wt):
    t = x.shape[0]
    f = lambda shape: jax.ShapeDtypeStruct(shape, F32)
    kv_spec = pl.BlockSpec((t, KV_W), lambda i: (i, 0))
    return pl.pallas_call(
        _pre_sample_kernel,
        grid=(1,),
        in_specs=_pre_in_specs(t, t, 1, wn, wt),
        out_specs=[kv_spec] * 6 + [pl.BlockSpec((t, S5_W), lambda i: (i, 0)),
                                   pl.BlockSpec((Q_W, t), lambda i: (0, i)),
                                   pl.BlockSpec((N_KV, 16, t), lambda i: (0, 0, i))],
        out_shape=[f((t, KV_W))] * 6 + [f((t, S5_W)), jax.ShapeDtypeStruct((Q_W, t), BF16),
                                        f((N_KV, 16, t))],
        compiler_params=_cp("arbitrary"),
        name="pre_mix_sample",
    )(x, sc, sh, g_pre.reshape(1, D_MODEL), wn, wt)


def _cmp_kernel(x_ref, pe_ref, w1_ref, w2_ref, o_ref):
    xb = (x_ref[...] + pe_ref[...]).astype(BF16)
    hid = _silu(_dot(xb, w1_ref[...]))
    o_ref[...] = _dot(hid.astype(BF16), w2_ref[...])


def _cmp_weights(pe, w1, w2):
    eye = jnp.eye(N_KV, dtype=F32)
    w1b = jnp.einsum('jde,gh->jgdhe', w1, eye).reshape(CMP_BLOCK * KV_W, KV_W).astype(BF16)
    w2b = jnp.einsum('ef,gh->gehf', w2, eye).reshape(KV_W, KV_W).astype(BF16)
    peb = jnp.broadcast_to(pe[:, None, :], (CMP_BLOCK, N_KV, HEAD_DIM)).reshape(1, CMP_BLOCK * KV_W)
    return peb, w1b, w2b


def _compress(x, peb, w1b, w2b):
    r, k = x.shape
    tr = min(256, r)
    return pl.pallas_call(
        _cmp_kernel,
        grid=(r // tr,),
        in_specs=[pl.BlockSpec((tr, k), lambda i: (i, 0)),
                  pl.BlockSpec((1, k), lambda i: (0, 0)),
                  pl.BlockSpec((k, KV_W), lambda i: (0, 0)),
                  pl.BlockSpec((KV_W, KV_W), lambda i: (0, 0))],
        out_specs=pl.BlockSpec((tr, KV_W), lambda i: (i, 0)),
        out_shape=jax.ShapeDtypeStruct((r, KV_W), F32),
        compiler_params=_cp("arbitrary"),
        name="compress",
    )(x, peb, w1b, w2b)


def _cmp_pages_kernel(xk_ref, xv_ref, w1k_ref, w1v_ref, bk_ref, bv_ref, w2k_ref, w2v_ref,
                      ok_ref, ov_ref):
    n_rows = ok_ref.shape[0]
    for x_ref, w1_ref, b_ref, w2_ref, o_ref in ((xk_ref, w1k_ref, bk_ref, w2k_ref, ok_ref),
                                               (xv_ref, w1v_ref, bv_ref, w2v_ref, ov_ref)):
        acc = jnp.zeros((n_rows, CMP_PER_PAGE * HEAD_DIM), F32)
        for dd in range(HEAD_DIM // 2):
            r0 = x_ref[pl.ds(2 * dd, n_rows, stride=HEAD_DIM), :]
            r1 = x_ref[pl.ds(2 * dd + 1, n_rows, stride=HEAD_DIM), :]
            acc = acc + _dot(jnp.concatenate([r0, r1], axis=1).astype(BF16), w1_ref[dd])
        hid = _silu(acc + b_ref[...])
        o_ref[...] = _dot(hid.astype(BF16), w2_ref[...])


def _cmp_pages_weights(pe, w1, w2):
    eye = jnp.eye(CMP_PER_PAGE, dtype=F32)
    w1t = jnp.einsum('jde,mn->dmjne', w1, eye).reshape(HEAD_DIM // 2, 2 * PAGE, CMP_PER_PAGE * HEAD_DIM)
    bias = jnp.einsum('jd,jde->e', pe, w1, precision=HIGHEST)
    bias = jnp.tile(bias, CMP_PER_PAGE).reshape(1, CMP_PER_PAGE * HEAD_DIM)
    w2t = jnp.einsum('ef,mn->menf', w2, eye).reshape(CMP_PER_PAGE * HEAD_DIM, CMP_PER_PAGE * HEAD_DIM)
    return w1t.astype(BF16), bias, w2t.astype(BF16)


def _compress_pages(xk, xv, wk, wv, n_seq, n_pages):
    rows_in = n_pages * N_KV * HEAD_DIM
    rows_out = n_pages * N_KV
    wcols = CMP_PER_PAGE * HEAD_DIM
    full = lambda a: pl.BlockSpec(a.shape, lambda i: (0,) * a.ndim)
    x_spec = pl.BlockSpec((rows_in, PAGE), lambda i: (i, 0))
    o_spec = pl.BlockSpec((rows_out, wcols), lambda i: (i, 0))
    o_shape = jax.ShapeDtypeStruct((n_seq * rows_out, wcols), F32)
    return pl.pallas_call(
        _cmp_pages_kernel,
        grid=(n_seq,),
        in_specs=[x_spec, x_spec, full(wk[0]), full(wv[0]), full(wk[1]), full(wv[1]),
                  full(wk[2]), full(wv[2])],
        out_specs=[o_spec, o_spec],
        out_shape=[o_shape, o_shape],
        compiler_params=_cp("arbitrary"),
        name="compress_pages",
    )(xk, xv, wk[0], wv[0], wk[1], wv[1], wk[2], wv[2])


def _attn_kernel(q_ref, g_ref, kc_ref, vct_ref, ks_ref, vst_ref, kw_ref, vwt_ref,
                 ctab_ref, wtab_ref, o_ref, mb_ref, *, n_cmp):
    i = pl.program_id(2)
    lanes = GQ * QT
    qt = q_ref[...]
    q4 = jnp.concatenate([qt[hh * HEAD_DIM:(hh + 1) * HEAD_DIM] for hh in range(GQ)], axis=1)

    half = n_cmp // 2
    sc = _dot(kc_ref[0, 0], q4)
    rho = lax.broadcasted_iota(I32, (n_cmp, 16), 0)
    col = lax.broadcasted_iota(I32, (n_cmp, 16), 1)
    blk_n = 2 * jnp.where(rho >= half, rho - half, rho) + jnp.where(rho >= half, 1, 0)
    rel = blk_n - (4 * i - 4)
    want = jnp.where(rel < 0, 8, jnp.where(rel > 7, 9, rel))
    place = jnp.where(col == want, 1.0, 0.0).astype(F32)
    sc = sc + _dot(place, ctab_ref[0], precision=HIGHEST)
    mc = jnp.maximum(jnp.max(sc, axis=0, keepdims=True), M_INIT)
    ec = jnp.exp(sc - mc)
    pc = ec * (1.0 / jnp.maximum(jnp.sum(ec, axis=0, keepdims=True), 1e-30))
    o_c = _dot(vct_ref[0, 0], pc.astype(BF16))

    ps = pc[:half] + pc[half:]
    imp = ps[:, 0:QT]
    for hh in range(1, GQ):
        imp = imp + ps[:, hh * QT:(hh + 1) * QT]
    n_blk = half
    blk = lax.broadcasted_iota(I32, (n_blk, QT), 0)
    tq = i * QT + lax.broadcasted_iota(I32, (n_blk, QT), 1)
    cur = lax.shift_right_logical(tq, 6)
    forced = (blk == 0) | (blk == cur) | (blk == cur - 1)
    score = jnp.where(blk * SEL_BLOCK <= tq, jnp.where(forced, 1e4, imp), -jnp.inf)
    blkf = blk.astype(F32)

    def pick(_, carry):
        work, mb = carry
        mx = jnp.max(work, axis=0, keepdims=True)
        idx = jnp.min(jnp.where(work == mx, blkf, 1e9), axis=0, keepdims=True)
        hit = blkf == idx
        return jnp.where(hit, -jnp.inf, work), jnp.where(hit, 0.0, mb)

    _, mb = lax.fori_loop(0, min(N_SEL, n_blk), pick, (score, jnp.full((n_blk, QT), NEG, F32)))
    for hh in range(GQ):
        mb_ref[:, hh * QT:(hh + 1) * QT] = mb

    def online(carry, s, v_t):
        m, l, acc = carry
        m_new = jnp.maximum(m, jnp.max(s, axis=0, keepdims=True))
        alpha = jnp.exp(m - m_new)
        p = jnp.exp(s - m_new)
        l = alpha * l + jnp.sum(p, axis=0, keepdims=True)
        return m_new, l, alpha * acc + _dot(v_t, p.astype(BF16))

    init = (jnp.full((1, lanes), M_INIT, F32), jnp.zeros((1, lanes), F32),
            jnp.zeros((HEAD_DIM, lanes), F32))
    finish = lambda carry: carry[2] * (1.0 / jnp.maximum(carry[1], 1e-30))

    def sel_body(p, carry):
        k_t = ks_ref[0, pl.ds(pl.multiple_of(p * VT, VT), VT), :]
        rows = [jnp.broadcast_to(mb_ref[pl.ds((VT // SEL_BLOCK) * p + c, 1), :], (SEL_BLOCK, lanes))
                for c in range(VT // SEL_BLOCK)]
        d0 = i - 2 * p
        t0 = wtab_ref[0, jnp.minimum(d0, 2)]
        t1 = wtab_ref[0, jnp.where(d0 < 1, N_DELTA, jnp.minimum(d0 - 1, 2))]
        s = _dot(k_t, q4) + jnp.concatenate(rows, axis=0) + jnp.concatenate([t0, t1], axis=0)
        return online(carry, s, vst_ref[0, p])

    o_s = finish(lax.fori_loop(0, i // 2 + 1, sel_body, init))

    deltas = list(range(N_DELTA - 1, -1, -1))
    carry = init
    for grp in (deltas[a:a + 2] for a in range(0, N_DELTA, 2)):
        ks, vs, bs = [], [], []
        for dl in grp:
            kt = i - dl
            ktc = jnp.maximum(kt, 0)
            ks.append(kw_ref[0, pl.ds(pl.multiple_of(ktc * QT, QT), QT), :])
            vs.append(vwt_ref[0, ktc])
            bs.append(wtab_ref[0, dl] + jnp.where(kt >= 0, 0.0, NEG))
        s = _dot(jnp.concatenate(ks, axis=0), q4) + jnp.concatenate(bs, axis=0)
        carry = online(carry, s, jnp.concatenate(vs, axis=1))
    o_w = finish(carry)

    gates = g_ref[0]
    grow = lambda j: jnp.concatenate([gates[j * GQ + hh:j * GQ + hh + 1] for hh in range(GQ)], axis=1)
    o_t = grow(0) * o_c + grow(1) * o_s + grow(2) * o_w
    o_hd = jnp.concatenate([o_t[:, hh * QT:(hh + 1) * QT] for hh in range(GQ)], axis=0)
    o_ref[...] = o_hd.T.astype(BF16)


def _bias_tables(rel_bias):
    span = N_DELTA * QT
    bt = _bucket_table(WINDOW)
    period = span + QT
    by_dist = jnp.concatenate([rel_bias[bt], jnp.full((period - WINDOW, N_HEADS), NEG, F32)], axis=0)
    toep = jnp.tile(by_dist.T, (1, QT))[:, :QT * (period - 1)].reshape(N_HEADS, QT, period - 1)
    tab = toep[:, :, :span].reshape(N_KV, GQ, QT, N_DELTA, QT)
    wtab = tab.transpose(0, 3, 2, 1, 4).reshape(N_KV, N_DELTA, QT, GQ * QT)
    wtab = jnp.concatenate([wtab, jnp.full((N_KV, 1, QT, GQ * QT), NEG, F32)], axis=1)
    ql = np.arange(QT)[None, :]
    r = np.arange(8)[:, None]
    d = ql + (4 * CMP_BLOCK - CMP_BLOCK + 1) - CMP_BLOCK * r
    near = jnp.where((d >= 0)[..., None], rel_bias[bt[np.clip(d, 0, len(bt) - 1)]], NEG)
    far = jnp.broadcast_to(rel_bias[NUM_BUCKETS - 1][None, None, :], (1, QT, N_HEADS))
    rows = jnp.concatenate([near, far, jnp.full((1, QT, N_HEADS), NEG, F32),
                            jnp.zeros((6, QT, N_HEADS), F32)], axis=0)
    ctab = rows.reshape(16, QT, N_KV, GQ).transpose(2, 0, 3, 1).reshape(N_KV, 16, GQ * QT)
    return wtab.astype(F32), ctab.astype(F32)


def _attn_prompt(qt, gt, kcmp, vcmpt, ksb, vst, kwb, vwt, wtab, ctab, batch, seq):
    nq = seq // QT
    n_cmp = kcmp.shape[2]
    lanes = GQ * QT
    t = batch * seq
    return pl.pallas_call(
        functools.partial(_attn_kernel, n_cmp=n_cmp),
        grid=(batch, N_KV, nq),
        in_specs=[
            pl.BlockSpec((GQ * HEAD_DIM, QT), lambda b, g, i: (g, b * nq + i)),
            pl.BlockSpec((1, 16, QT), lambda b, g, i: (g, 0, b * nq + i)),
            pl.BlockSpec((1, 1, n_cmp, HEAD_DIM), lambda b, g, i: (b, g, 0, 0)),
            pl.BlockSpec((1, 1, HEAD_DIM, n_cmp), lambda b, g, i: (b, g, 0, 0)),
            pl.BlockSpec((1, seq, HEAD_DIM), lambda b, g, i: (g, b, 0)),
            pl.BlockSpec((1, seq // VT, HEAD_DIM, VT), lambda b, g, i: (g, b, 0, 0)),
            pl.BlockSpec((1, seq, HEAD_DIM), lambda b, g, i: (g, b, 0)),
            pl.BlockSpec((1, nq, HEAD_DIM, QT), lambda b, g, i: (g, b, 0, 0)),
            pl.BlockSpec((1, 16, lanes), lambda b, g, i: (g, 0, 0)),
            pl.BlockSpec((1, N_DELTA + 1, QT, lanes), lambda b, g, i: (g, 0, 0, 0)),
        ],
        out_specs=pl.BlockSpec((QT, GQ * HEAD_DIM), lambda b, g, i: (b * nq + i, g)),
        out_shape=jax.ShapeDtypeStruct((t, Q_W), BF16),
        scratch_shapes=[pltpu.VMEM((n_cmp // 2, lanes), F32)],
        compiler_params=_cp("arbitrary", "arbitrary", "arbitrary"),
        name="nsa_prompt",
    )(qt, gt, kcmp, vcmpt, ksb, vst, kwb, vwt, ctab, wtab)


def _s5_kernel(u_ref, mt_ref, sbr_ref, sbi_ref, ccr_ref, cci_ref, lb_ref, h0r_ref, h0i_ref,
               y_ref, hr_ref, hi_ref, sr_sc, si_sc, pr_sc, pi_sc, *, n_chunks, rb):
    u = u_ref[0]
    sr_sc[...] = _dot(u, sbr_ref[0])
    si_sc[...] = _dot(u, sbi_ref[0])
    lr = lb_ref[0, 0:1, :]
    li = lb_ref[0, 1:2, :]

    def step(c, carry):
        hr, hi = carry
        rows = pl.ds(pl.multiple_of(c * rb, rb), rb)
        pr_sc[rows, :] = hr
        pi_sc[rows, :] = hi
        return (lr * hr - li * hi + sr_sc[rows, :], lr * hi + li * hr + si_sc[rows, :])

    hr, hi = lax.fori_loop(0, n_chunks, step, (h0r_ref[0], h0i_ref[0]))
    hr_ref[0] = hr
    hi_ref[0] = hi
    y_ref[0] = (_dot(u, mt_ref[0]) + _dot(pr_sc[...].astype(BF16), ccr_ref[0])
                + _dot(pi_sc[...].astype(BF16), cci_ref[0]))


def _s5_prep(lam_re, lam_im, log_dt, b_re, b_im, c_re, c_im, d_skip, chunk):
    lam = lax.complex(lam_re, lam_im)
    z = lam * jnp.exp(log_dt)[:, None]
    lbar = jnp.exp(z)
    bbar = ((lbar - 1.0) / lam)[:, :, None] * lax.complex(b_re, b_im)
    c = lax.complex(c_re, c_im)
    pw = jnp.exp(z[None] * jnp.arange(chunk + 1, dtype=F32)[:, None, None])
    kern = jnp.einsum('gap,jgp,gpb->jgab', c, pw[:chunk], bbar, precision=HIGHEST).real
    kern = kern.at[0].add(jax.vmap(jnp.diag)(d_skip))
    lag = np.arange(chunk)[None, :] - np.arange(chunk)[:, None]
    kt = kern.transpose(1, 0, 3, 2)
    m5 = jnp.where((lag >= 0)[None, :, :, None, None], kt[:, np.clip(lag, 0, None)], 0.0)
    mt = m5.transpose(0, 1, 3, 2, 4).reshape(S5_G, chunk * S5_H, chunk * S5_H)
    sb = jnp.einsum('lgp,gph->glhp', pw[chunk - 1 - np.arange(chunk)], bbar).reshape(
        S5_G, chunk * S5_H, S5_P)
    cc = jnp.einsum('ghp,lgp->gplh', c, pw[1:]).reshape(S5_G, S5_P, chunk * S5_H)
    lb = jnp.stack([pw[chunk].real, pw[chunk].imag], axis=1)
    return (mt.astype(BF16), sb.real.astype(BF16), sb.imag.astype(BF16),
            cc.real.astype(BF16), (-cc.imag).astype(BF16), lb.astype(F32))


def _s5_call(u, prep, h0r, h0i, n_chunks, rb):
    mt, sbr, sbi, ccr, cci, lb = prep
    g, rows, w = u.shape
    per_g = lambda *shape: pl.BlockSpec((1,) + shape, lambda i: (i,) + (0,) * len(shape))
    return pl.pallas_call(
        functools.partial(_s5_kernel, n_chunks=n_chunks, rb=rb),
        grid=(g,),
        in_specs=[per_g(rows, w), per_g(w, w), per_g(w, S5_P), per_g(w, S5_P),
                  per_g(S5_P, w), per_g(S5_P, w), per_g(2, S5_P), per_g(rb, S5_P), per_g(rb, S5_P)],
        out_specs=[per_g(rows, w), per_g(rb, S5_P), per_g(rb, S5_P)],
        out_shape=[jax.ShapeDtypeStruct((g, rows, w), F32),
                   jax.ShapeDtypeStruct((g, rb, S5_P), F32),
                   jax.ShapeDtypeStruct((g, rb, S5_P), F32)],
        scratch_shapes=[pltpu.VMEM((rows, S5_P), F32)] * 4,
        compiler_params=_cp("arbitrary"),
        name="s5_scan",
    )(u, mt, sbr, sbi, ccr, cci, lb, h0r, h0i)


def _s5_prompt(u, prep, batch, seq):
    nc = seq // S5_CHUNK
    ub = u.reshape(batch, nc, S5_CHUNK, S5_G, S5_H).transpose(3, 1, 0, 2, 4)
    ub = jnp.pad(ub, ((0, 0), (0, 0), (0, S5_ROWS - batch), (0, 0), (0, 0)))
    ub = ub.reshape(S5_G, nc * S5_ROWS, S5_CHUNK * S5_H).astype(BF16)
    zero = jnp.zeros((S5_G, S5_ROWS, S5_P), F32)
    y, hr, hi = _s5_call(ub, prep, zero, zero, nc, S5_ROWS)
    y = y.reshape(S5_G, nc, S5_ROWS, S5_CHUNK, S5_H)[:, :, :batch].transpose(2, 1, 3, 0, 4)
    return (y.reshape(batch * seq, S5_W),
            hr[:, :batch].transpose(1, 0, 2), hi[:, :batch].transpose(1, 0, 2))


def _s5_sample(u, prep, h0r, h0i):
    n = u.shape[0]
    ub = u.reshape(n, S5_G, S5_H).transpose(1, 0, 2).astype(BF16)
    y, hr, hi = _s5_call(ub, prep, h0r.transpose(1, 0, 2), h0i.transpose(1, 0, 2), 1, n)
    return y.transpose(1, 0, 2).reshape(n, S5_W), hr.transpose(1, 0, 2), hi.transpose(1, 0, 2)


def _softmax_lanes(s):
    m = jnp.maximum(jnp.max(s, axis=-1, keepdims=True), M_INIT)
    e = jnp.exp(s - m)
    return e * (1.0 / jnp.maximum(jnp.sum(e, axis=-1, keepdims=True), 1e-30))


def _bqk(q, k):
    return jnp.einsum('bhd,bnd->bhn', q, k, preferred_element_type=F32)


def _bpv(p, v):
    return jnp.einsum('bhn,bnd->bhd', p.astype(BF16), v, preferred_element_type=F32)


def _sattn1_kernel(q_ref, kc_ref, vc_ref, bias_ref, oc_ref, pick_ref, *, n_pick):
    q = q_ref[...]
    p = _softmax_lanes(_bqk(q, kc_ref[...]) + bias_ref[...])
    oc_ref[...] = _bpv(p, vc_ref[...])
    n_cmp = p.shape[-1]
    half = n_cmp // 2
    ps = p[:, 0]
    for hh in range(1, GQ):
        ps = ps + p[:, hh]
    imp = ps[:, :half] + ps[:, half:]
    lane = lax.broadcasted_iota(I32, imp.shape, 1).astype(F32)
    work = jnp.where((lane > 0) & (lane < half - 1), imp, -jnp.inf)
    picks = jnp.zeros(imp.shape, F32)
    for t in range(n_pick):
        mx = jnp.max(work, axis=-1, keepdims=True)
        idx = jnp.min(jnp.where(work == mx, lane, 1e9), axis=-1, keepdims=True)
        work = jnp.where(lane == idx, -jnp.inf, work)
        picks = jnp.where(lane == t, idx, picks)
    pick_ref[...] = picks.astype(I32)


def _sattn2_kernel(q_ref, ks_ref, vs_ref, base_ref, corr_ref, flag_ref, ex_ref,
                   kw_ref, vw_ref, wb_ref, oc_ref, g_ref, o_ref):
    q = q_ref[...]
    near = _dot(flag_ref[...], ex_ref[...])
    bias = base_ref[...] + near[:, None, :] * corr_ref[...]
    p_s = _softmax_lanes(_bqk(q, ks_ref[...].astype(BF16)) + bias)
    o_s = _bpv(p_s, vs_ref[...].astype(BF16))
    p_w = _softmax_lanes(_bqk(q, kw_ref[...].astype(BF16)) + wb_ref[...])
    o_w = _bpv(p_w, vw_ref[...].astype(BF16))
    g = g_ref[...]
    o_ref[...] = g[:, :, 0:1] * oc_ref[...] + g[:, :, 1:2] * o_s + g[:, :, 2:3] * o_w


def _attn_sample(q, gates, kcmp, vcmp, page_table, cache_sk, cache_sv, ks_new, vs_new,
                 win_k, win_v, rel_bias):
    n = q.shape[0]
    n_cmp = kcmp.shape[2]
    past = page_table.shape[1] * PAGE
    n_blk = past // SEL_BLOCK
    n_pick = N_SEL - 3
    nbq = n * N_KV
    nb = 16
    bt = _bucket_table(past + 1)
    qg = q.reshape(nbq, GQ, HEAD_DIM)
    per_row = lambda tab: jnp.tile(tab.reshape(-1, N_KV, GQ).transpose(1, 2, 0), (nb // N_KV, 1, 1))
    blk3 = lambda *s: pl.BlockSpec((nb,) + s, lambda i: (i,) + (0,) * len(s))
    const3 = lambda *s: pl.BlockSpec((nb,) + s, lambda i: (0,) * (len(s) + 1))

    order = np.concatenate([np.arange(0, n_cmp, 2), np.arange(1, n_cmp, 2)])
    d_c = past - (order * CMP_BLOCK + CMP_BLOCK - 1)
    cb = per_row(jnp.where((d_c >= 0)[:, None], rel_bias[bt[np.clip(d_c, 0, None)]], NEG))

    o_c, picks = pl.pallas_call(
        functools.partial(_sattn1_kernel, n_pick=n_pick),
        grid=(nbq // nb,),
        in_specs=[blk3(GQ, HEAD_DIM), blk3(n_cmp, HEAD_DIM), blk3(n_cmp, HEAD_DIM), const3(GQ, n_cmp)],
        out_specs=[blk3(GQ, HEAD_DIM), pl.BlockSpec((nb, n_cmp // 2), lambda i: (i, 0))],
        out_shape=[jax.ShapeDtypeStruct((nbq, GQ, HEAD_DIM), F32),
                   jax.ShapeDtypeStruct((nbq, n_cmp // 2), I32)],
        compiler_params=_cp("arbitrary"),
        name="nsa_sample_cmp",
    )(qg, kcmp.reshape(nbq, n_cmp, HEAD_DIM), vcmp.reshape(nbq, n_cmp, HEAD_DIM), cb)

    picks = picks[:, :n_pick].reshape(n, N_KV, n_pick)
    forced = jnp.broadcast_to(jnp.array([n_blk - 1, 0], I32), (n, N_KV, 2))
    sel = jnp.concatenate([forced, picks], axis=-1)
    n_slot = sel.shape[-1]
    page = jnp.take_along_axis(page_table, (sel // 2).reshape(n, -1), axis=1).reshape(sel.shape)
    rows = ((sel % 2) * SEL_BLOCK)[..., None] + jnp.arange(SEL_BLOCK)
    gi = jnp.arange(N_KV)[None, :, None, None]

    def gather(cache, new):
        got = cache[page[..., None], rows, gi].reshape(n, N_KV, n_slot * SEL_BLOCK, HEAD_DIM)
        tail = jnp.zeros((n, N_KV, SEL_BLOCK, HEAD_DIM), F32).at[:, :, 0].set(
            new.reshape(n, N_KV, HEAD_DIM))
        return jnp.concatenate([got, tail], axis=2).reshape(nbq, (n_slot + 1) * SEL_BLOCK, HEAD_DIM)

    k_sel = gather(cache_sk, ks_new)
    v_sel = gather(cache_sv, vs_new)
    n_key = k_sel.shape[1]
    r = np.arange(SEL_BLOCK)
    d_base = np.concatenate([SEL_BLOCK - r, past - r] + [np.full(SEL_BLOCK, past)] * n_pick + [-r])
    base = per_row(jnp.where((d_base >= 0)[:, None], rel_bias[bt[np.clip(d_base, 0, past)]], NEG))
    d_near = np.clip(2 * SEL_BLOCK - r, 0, past)
    delta = rel_bias[bt[d_near]] - rel_bias[bt[past]][None, :]
    in_pick = np.zeros((n_slot + 1, 1, 1), np.float32)
    in_pick[2:n_slot] = 1.0
    corr = per_row((in_pick * delta[None]).reshape(n_key, N_HEADS))
    flag = jnp.pad((sel == n_blk - 2).astype(F32).reshape(nbq, n_slot), ((0, 0), (0, 1)))
    expand = jnp.asarray(np.kron(np.eye(n_slot + 1, dtype=np.float32), np.ones((1, SEL_BLOCK), np.float32)))

    n_win = win_k.shape[1]
    d_w = n_win - 1 - np.arange(n_win)
    wb = per_row(jnp.where((d_w < WINDOW)[:, None], rel_bias[bt[d_w]], NEG))
    wk = win_k.transpose(0, 2, 1, 3).reshape(nbq, n_win, HEAD_DIM)
    wv = win_v.transpose(0, 2, 1, 3).reshape(nbq, n_win, HEAD_DIM)
    o = pl.pallas_call(
        _sattn2_kernel,
        grid=(nbq // nb,),
        in_specs=[blk3(GQ, HEAD_DIM), blk3(n_key, HEAD_DIM), blk3(n_key, HEAD_DIM),
                  const3(GQ, n_key), const3(GQ, n_key),
                  pl.BlockSpec((nb, n_slot + 1), lambda i: (i, 0)),
                  pl.BlockSpec((n_slot + 1, n_key), lambda i: (0, 0)),
                  blk3(n_win, HEAD_DIM), blk3(n_win, HEAD_DIM), const3(GQ, n_win),
                  blk3(GQ, HEAD_DIM), blk3(GQ, 3)],
        out_specs=blk3(GQ, HEAD_DIM),
        out_shape=jax.ShapeDtypeStruct((nbq, GQ, HEAD_DIM), F32),
        compiler_params=_cp("arbitrary"),
        name="nsa_sample_sel_win",
    )(qg, k_sel, v_sel, base, corr, flag, expand, wk, wv, wb, o_c, gates.reshape(nbq, GQ, 3))
    return o.reshape(n, Q_W).astype(BF16)


def _gelu_tanh(x):
    return 0.5 * x * (1.0 + jnp.tanh(math.sqrt(2.0 / math.pi) * (x + 0.044715 * (x * x * x))))


def _post_kernel(x_ref, o_ref, y5_ref, gm_ref, shf_ref, scf_ref, gpost_ref, gpre_ref,
                 wglu_ref, wout_ref, wr_ref, br_ref, tri_ref, cin_ref,
                 x1_ref, h2_ref, h2p_ref, eidx_ref, wts_ref, pos_ref, cout_ref, cnt_sc):
    i = pl.program_id(0)
    tm = x_ref.shape[0]

    @pl.when(i == 0)
    def _():
        cnt_sc[...] = cin_ref[...]

    g5 = _gelu_tanh(y5_ref[...])
    g5 = g5 * _sigmoid(_dot(g5.astype(BF16), wglu_ref[...]))
    m = _dot(o_ref[...], wout_ref[:Q_W]) + _dot(g5.astype(BF16), wout_ref[Q_W:])
    x1 = x_ref[...] + gm_ref[0] * _rms(m, gpost_ref[...])
    h2 = _rms(x1, gpre_ref[...]) * (1.0 + scf_ref[0]) + shf_ref[0]
    x1_ref[...] = x1
    h2b = h2.astype(BF16)
    h2_ref[...] = h2b
    bits = pltpu.bitcast(h2b.astype(F32), U32)
    h2p_ref[...] = (bits[:, :D_MODEL // 2] >> 16) | bits[:, D_MODEL // 2:]

    s_t = _sigmoid(_dot(h2, wr_ref[...], precision=HIGHEST)).T
    s_sel = s_t + br_ref[...]
    eio = lax.broadcasted_iota(I32, (N_EXPERTS, tm), 0).astype(F32)
    first = lambda hit, ids: jnp.min(jnp.where(hit, ids, 1e9), axis=0, keepdims=True)
    gscore = []
    for g in range(N_EGROUPS):
        xg = s_sel[g * EGROUP:(g + 1) * EGROUP]
        ig = lax.broadcasted_iota(I32, (EGROUP, tm), 0).astype(F32) + float(g * EGROUP)
        m1 = jnp.max(xg, axis=0, keepdims=True)
        m2 = jnp.max(jnp.where(ig == first(xg == m1, ig), -jnp.inf, xg), axis=0, keepdims=True)
        gscore.append(m1 + m2)
    gsc = jnp.concatenate(gscore, axis=0)
    gio = lax.broadcasted_iota(I32, (N_EGROUPS, tm), 0).astype(F32)
    gmask = jnp.zeros((N_EGROUPS, tm), F32)
    for _ in range(TOPK_GROUPS):
        hit = gio == first(gsc == jnp.max(gsc, axis=0, keepdims=True), gio)
        gmask = jnp.where(hit, 1.0, gmask)
        gsc = jnp.where(hit, -jnp.inf, gsc)
    work = jnp.concatenate(
        [jnp.where(gmask[g:g + 1] > 0.0, s_sel[g * EGROUP:(g + 1) * EGROUP], -jnp.inf)
         for g in range(N_EGROUPS)], axis=0)
    chosen = jnp.zeros((N_EXPERTS, tm), F32)
    ids, wsel = [], []
    for _ in range(TOP_K):
        ik = first(work == jnp.max(work, axis=0, keepdims=True), eio)
        hit = eio == ik
        ids.append(ik)
        wsel.append(jnp.sum(jnp.where(hit, s_t, 0.0), axis=0, keepdims=True))
        work = jnp.where(hit, -jnp.inf, work)
        chosen = jnp.where(hit, 1.0, chosen)
    rank = cnt_sc[:, 0:1] + _dot(chosen.astype(BF16), tri_ref[...])
    pos = [jnp.sum(jnp.where(eio == ik, rank, 0.0), axis=0, keepdims=True) for ik in ids]
    cnt_sc[...] = cnt_sc[...] + jnp.sum(chosen, axis=1, keepdims=True)
    wsum = wsel[0]
    for k in range(1, TOP_K):
        wsum = wsum + wsel[k]
    eidx_ref[...] = jnp.concatenate(ids, axis=0).astype(I32)
    pos_ref[...] = jnp.concatenate(pos, axis=0).astype(I32)
    wts_ref[...] = jnp.concatenate([w / wsum * ROUTED_SCALE for w in wsel], axis=0)
    cout_ref[...] = cnt_sc[...]


def _post_mix(x, o_att, y5, gm, shf, scf, g_post, g_pre, wglu, wout, w_router, b_router,
              cnt_in, tm, rows_per_mod):
    t = x.shape[0]
    r = gm.shape[1]
    per = rows_per_mod // tm
    mod = pl.BlockSpec((1, r, D_MODEL), lambda i: (i // per, 0, 0))
    full = lambda a: pl.BlockSpec(a.shape, lambda i: (0,) * a.ndim)
    vec = pl.BlockSpec((1, D_MODEL), lambda i: (0, 0))
    tri = jnp.asarray(np.triu(np.ones((tm, tm), np.float32), 1), BF16)
    brb = jnp.broadcast_to(b_router[:, None], (N_EXPERTS, tm)).astype(F32)
    route = pl.BlockSpec((TOP_K, tm), lambda i: (0, i))
    return pl.pallas_call(
        _post_kernel,
        grid=(t // tm,),
        in_specs=[pl.BlockSpec((tm, D_MODEL), lambda i: (i, 0)),
                  pl.BlockSpec((tm, Q_W), lambda i: (i, 0)),
                  pl.BlockSpec((tm, S5_W), lambda i: (i, 0)),
                  mod, mod, mod, vec, vec, full(wglu), full(wout), full(w_router), full(brb),
                  full(tri), full(cnt_in)],
        out_specs=[pl.BlockSpec((tm, D_MODEL), lambda i: (i, 0)),
                   pl.BlockSpec((tm, D_MODEL), lambda i: (i, 0)),
                   pl.BlockSpec((tm, D_MODEL // 2), lambda i: (i, 0)),
                   route, route, route, full(cnt_in)],
        out_shape=[jax.ShapeDtypeStruct((t, D_MODEL), F32),
                   jax.ShapeDtypeStruct((t, D_MODEL), BF16),
                   jax.ShapeDtypeStruct((t, D_MODEL // 2), U32),
                   jax.ShapeDtypeStruct((TOP_K, t), I32),
                   jax.ShapeDtypeStruct((TOP_K, t), F32),
                   jax.ShapeDtypeStruct((TOP_K, t), I32),
                   jax.ShapeDtypeStruct(cnt_in.shape, F32)],
        scratch_shapes=[pltpu.VMEM(cnt_in.shape, F32)],
        compiler_params=_cp("arbitrary"),
        name="post_mix_router",
    )(x, o_att, y5, gm, shf, scf, g_post.reshape(1, D_MODEL), g_pre.reshape(1, D_MODEL),
      wglu, wout, w_router, brb, tri, cnt_in)


def _dest_kernel(e_ref, p_ref, ps_ref, o_ref):
    tm = e_ref.shape[1]
    eio = lax.broadcasted_iota(I32, (N_EXPERTS, tm), 0)
    e = e_ref[...]
    start = ps_ref[...]
    rows = [jnp.sum(jnp.where(eio == e[k:k + 1], start, 0.0), axis=0, keepdims=True)
            for k in range(TOP_K)]
    o_ref[...] = jnp.concatenate(rows, axis=0).astype(I32) + p_ref[...]


def _dest(eidx, pos, pad_start, tm):
    t = eidx.shape[1]
    route = pl.BlockSpec((TOP_K, tm), lambda i: (0, i))
    start = jnp.broadcast_to(pad_start.astype(F32)[:, None], (N_EXPERTS, tm))
    return pl.pallas_call(
        _dest_kernel,
        grid=(t // tm,),
        in_specs=[route, route, pl.BlockSpec((N_EXPERTS, tm), lambda i: (0, 0))],
        out_specs=route,
        out_shape=jax.ShapeDtypeStruct((TOP_K, t), I32),
        compiler_params=_cp("arbitrary"),
        name="moe_dest",
    )(eidx, pos, start)


def _moe_kernel(be_ref, nb_ref, x_ref, wg_ref, wu_ref, wd_ref, y_ref):
    @pl.when(pl.program_id(0) < nb_ref[0])
    def _():
        xp = x_ref[...]
        lo = pltpu.bitcast(xp << 16, F32)
        hi = pltpu.bitcast(xp & jnp.uint32(0xFFFF0000), F32)
        x = jnp.concatenate([lo, hi], axis=1).astype(BF16)
        a = _dot(x, wg_ref[0].astype(BF16))
        b = _dot(x, wu_ref[0].astype(BF16))
        y_ref[...] = _dot((_silu(a) * b).astype(BF16), wd_ref[0].astype(BF16))


def _moe(xs, blk_e, nb_used, w_g, w_u, w_d):
    n_slot = xs.shape[0]
    n_blk = n_slot // MOE_BLK
    row = lambda j, be, nb: (jnp.minimum(j, nb[0] - 1), 0)
    wsel = lambda j, be, nb: (be[jnp.minimum(j, nb[0] - 1)], 0, 0)
    return pl.pallas_call(
        _moe_kernel,
        grid_spec=pltpu.PrefetchScalarGridSpec(
            num_scalar_prefetch=2,
            grid=(n_blk,),
            in_specs=[pl.BlockSpec((MOE_BLK, D_MODEL // 2), row),
                      pl.BlockSpec((1, D_MODEL, D_EXPERT), wsel),
                      pl.BlockSpec((1, D_MODEL, D_EXPERT), wsel),
                      pl.BlockSpec((1, D_EXPERT, D_MODEL), wsel)],
            out_specs=pl.BlockSpec((MOE_BLK, D_MODEL), row)),
        out_shape=jax.ShapeDtypeStruct((n_slot, D_MODEL), F32),
        compiler_params=_cp("arbitrary"),
        name="moe_experts",
    )(blk_e, nb_used, xs, w_g, w_u, w_d)


def _fin_kernel(x1_ref, h2_ref, yg_ref, w_ref, gf_ref, gpost_ref, wsg_ref, wsu_ref, wsd_ref, o_ref):
    w = w_ref[...]
    f = w[:, 0:1] * yg_ref[0]
    for k in range(1, TOP_K):
        f = f + w[:, k:k + 1] * yg_ref[k]
    hb = h2_ref[...]
    sh = (_silu(_dot(hb, wsg_ref[...])) * _dot(hb, wsu_ref[...])).astype(BF16)
    f = f + _dot(sh, wsd_ref[...])
    o_ref[...] = x1_ref[...] + gf_ref[0] * _rms(f, gpost_ref[...])


def _final(x1, h2, yg, wts, gf, g_post, wsg, wsu, wsd, tm, rows_per_mod):
    t = x1.shape[0]
    r = gf.shape[1]
    per = rows_per_mod // tm
    full = lambda a: pl.BlockSpec(a.shape, lambda i: (0,) * a.ndim)
    return pl.pallas_call(
        _fin_kernel,
        grid=(t // tm,),
        in_specs=[pl.BlockSpec((tm, D_MODEL), lambda i: (i, 0)),
                  pl.BlockSpec((tm, D_MODEL), lambda i: (i, 0)),
                  pl.BlockSpec((TOP_K, tm, D_MODEL), lambda i: (0, i, 0)),
                  pl.BlockSpec((tm, TOP_K), lambda i: (i, 0)),
                  pl.BlockSpec((1, r, D_MODEL), lambda i: (i // per, 0, 0)),
                  pl.BlockSpec((1, D_MODEL), lambda i: (0, 0)),
                  full(wsg), full(wsu), full(wsd)],
        out_specs=pl.BlockSpec((tm, D_MODEL), lambda i: (i, 0)),
        out_shape=jax.ShapeDtypeStruct((t, D_MODEL), F32),
        compiler_params=_cp("arbitrary"),
        name="moe_combine_final",
    )(x1, h2, yg, wts, gf, g_post.reshape(1, D_MODEL), wsg, wsu, wsd)


def _even_odd(a, axis):
    n = a.shape[axis]
    order = np.concatenate([np.arange(0, n, 2), np.arange(1, n, 2)])
    return jnp.take(a, order, axis=axis)


def _layer(xp, xs, c_prompt, c_sample, page_table, ck_c, cv_c, ck_s, cv_s, cw_k, cw_v, st_re, st_im,
           w_ada, b_ada, g_pre_mix, g_post_mix, g_pre_ffn, g_post_ffn, w_in, pe, wk1, wk2, wv1, wv2,
           rel_bias, lam_re, lam_im, log_dt, b_re, b_im, c_re, c_im, d_skip, w_glu, w_out,
           w_router, b_router, w_eg, w_eu, w_ed, w_sg, w_su, w_sd):
    batch, seq, _ = xp.shape
    n_dec = xs.shape[0]
    tp = batch * seq
    x_p = xp.reshape(tp, D_MODEL)
    x_s = xs.reshape(n_dec, D_MODEL)

    n_c = batch + n_dec
    n_pad = -(-n_c // 8) * 8
    c_all = jnp.pad(jnp.concatenate([c_prompt, c_sample], axis=0), ((0, n_pad - n_c), (0, 0)))
    mod = _adaln(c_all, w_ada, b_ada)
    mod_p = [m.reshape(batch, 1, D_MODEL) for m in jnp.split(mod[:batch], 6, axis=-1)]
    mod_s = [m.reshape(1, n_dec, D_MODEL) for m in jnp.split(mod[batch:n_c], 6, axis=-1)]

    wn, wt = _pre_weights(w_in)
    (kct_p, vct_p, kst_p, vst_p, kwt_p, vwt_p, kc_p, vc_p, u_p, ksb, kwb, qt_p, vst, vwt, gt_p) = \
        _pre_mix_prompt(x_p, mod_p[1], mod_p[0], g_pre_mix, wn, wt, 512, batch, seq)
    (kc_s, vc_s, ks_s, vs_s, kw_s, vw_s, u_s, qt_s, gt_s) = _pre_mix_sample(
        x_s, mod_s[1], mod_s[0], g_pre_mix, wn, wt)

    n_cmp_p = seq // CMP_BLOCK
    kcmp_p = _compress(kc_p.reshape(batch * n_cmp_p, CMP_BLOCK * KV_W), *_cmp_weights(pe, wk1, wk2))
    vcmp_p = _compress(vc_p.reshape(batch * n_cmp_p, CMP_BLOCK * KV_W), *_cmp_weights(pe, wv1, wv2))
    kcmp_p = _even_odd(kcmp_p.reshape(batch, n_cmp_p, N_KV, HEAD_DIM), 1).transpose(0, 2, 1, 3)
    vcmp_p = _even_odd(vcmp_p.reshape(batch, n_cmp_p, N_KV, HEAD_DIM), 1).transpose(0, 2, 3, 1)
    wtab, ctab = _bias_tables(rel_bias)
    o_p = _attn_prompt(qt_p, gt_p, kcmp_p.astype(BF16), vcmp_p.astype(BF16), ksb, vst, kwb, vwt,
                       wtab, ctab, batch, seq)

    n_pages = page_table.shape[1]
    n_cmp_s = n_pages * CMP_PER_PAGE
    pages = lambda cache: cache.transpose(0, 2, 3, 1)[page_table].reshape(
        n_dec * n_pages * N_KV * HEAD_DIM, PAGE)
    kcmp_s, vcmp_s = _compress_pages(pages(ck_c), pages(cv_c), _cmp_pages_weights(pe, wk1, wk2),
                                     _cmp_pages_weights(pe, wv1, wv2), n_dec, n_pages)
    by_head = lambda a: _even_odd(
        a.reshape(n_dec, n_pages, N_KV, CMP_PER_PAGE, HEAD_DIM).transpose(0, 2, 1, 3, 4).reshape(
            n_dec, N_KV, n_cmp_s, HEAD_DIM), 2)
    win_k = jnp.concatenate([cw_k[:, 1:], kw_s.reshape(n_dec, 1, N_KV, HEAD_DIM)], axis=1)
    win_v = jnp.concatenate([cw_v[:, 1:], vw_s.reshape(n_dec, 1, N_KV, HEAD_DIM)], axis=1)
    q_s = qt_s.T.reshape(n_dec, N_HEADS, HEAD_DIM)
    gates_s = gt_s[:, :3 * GQ].reshape(N_KV, 3, GQ, n_dec).transpose(3, 0, 2, 1).reshape(n_dec, N_HEADS, 3)
    o_s = _attn_sample(q_s, gates_s, by_head(kcmp_s).astype(BF16), by_head(vcmp_s).astype(BF16),
                       page_table, ck_s, cv_s, ks_s, vs_s, win_k, win_v, rel_bias)

    y5_p, s5r_p, s5i_p = _s5_prompt(
        u_p, _s5_prep(lam_re, lam_im, log_dt, b_re, b_im, c_re, c_im, d_skip, S5_CHUNK), batch, seq)
    y5_s, s5r_s, s5i_s = _s5_sample(
        u_s, _s5_prep(lam_re, lam_im, log_dt, b_re, b_im, c_re, c_im, d_skip, 1), st_re, st_im)

    wglu_b, wout_b = w_glu.astype(BF16), w_out.astype(BF16)
    cnt0 = jnp.zeros((N_EXPERTS, 128), F32)
    tm_q = 256
    x1_p, h2_p, h2p_p, e_p, wt_p, pos_p, cnt1 = _post_mix(
        x_p, o_p, y5_p, mod_p[2], mod_p[3], mod_p[4], g_post_mix, g_pre_ffn, wglu_b, wout_b,
        w_router, b_router, cnt0, tm_q, seq)
    x1_s, h2_s, h2p_s, e_s, wt_s, pos_s, cnt2 = _post_mix(
        x_s, o_s, y5_s, mod_s[2], mod_s[3], mod_s[4], g_post_mix, g_pre_ffn, wglu_b, wout_b,
        w_router, b_router, cnt1, n_dec, n_dec)

    t_all = tp + n_dec
    counts = cnt2[:, 0].astype(I32)
    padded = (counts + MOE_BLK - 1) // MOE_BLK * MOE_BLK
    pad_end = jnp.cumsum(padded)
    dest_p = _dest(e_p, pos_p, pad_end - padded, 1024)
    dest_s = _dest(e_s, pos_s, pad_end - padded, n_dec)
    dest = jnp.concatenate([dest_p, dest_s], axis=1)
    n_blk = -(-(t_all * TOP_K) // MOE_BLK) + N_EXPERTS
    n_slot = n_blk * MOE_BLK
    tok = jnp.broadcast_to(jnp.arange(t_all, dtype=I32)[None], (TOP_K, t_all))
    rows = jnp.full((n_slot,), t_all, I32).at[dest.reshape(-1)].set(tok.reshape(-1))
    h2z = jnp.concatenate([h2p_p, h2p_s, jnp.zeros((1, D_MODEL // 2), U32)], axis=0)
    xs_rows = h2z[rows]
    blk_e = jnp.minimum(jnp.searchsorted(pad_end, jnp.arange(n_blk, dtype=I32) * MOE_BLK, side='right'),
                        N_EXPERTS - 1).astype(I32)
    nb_used = (pad_end[-1:] // MOE_BLK).astype(I32)
    ys = _moe(xs_rows, blk_e, nb_used, w_eg, w_eu, w_ed)
    yg_p = ys[dest_p]
    yg_s = ys[dest_s]

    wsg, wsu, wsd = w_sg.astype(BF16), w_su.astype(BF16), w_sd.astype(BF16)
    out_p = _final(x1_p, h2_p, yg_p, wt_p.T, mod_p[5], g_post_ffn, wsg, wsu, wsd, tm_q, seq)
    out_s = _final(x1_s, h2_s, yg_s, wt_s.T, mod_s[5], g_post_ffn, wsg, wsu, wsd, n_dec, n_dec)

    n_win = min(WINDOW, seq)
    rows_p = lambda a: a.transpose(0, 3, 1, 2)
    kv5 = lambda a: a.reshape(n_dec, 1, N_KV, HEAD_DIM)
    st_p = (rows_p(kct_p), rows_p(vct_p), rows_p(kst_p), rows_p(vst_p),
            rows_p(kwt_p)[:, seq - n_win:], rows_p(vwt_p)[:, seq - n_win:], s5r_p, s5i_p)
    st_s = (kv5(kc_s), kv5(vc_s), kv5(ks_s), kv5(vs_s), win_k, win_v, s5r_s, s5i_s)
    return out_p.reshape(batch, seq, D_MODEL), out_s.reshape(n_dec, 1, D_MODEL), st_p, st_s


def kernel(x_prompt, x_sample, c_prompt, c_sample, page_table, cache_cmp_k, cache_cmp_v, cache_sel_k, cache_sel_v, cache_win_k, cache_win_v, state_s5_re, state_s5_im, w_ada, b_ada, g_pre_mix, g_post_mix, g_pre_ffn, g_post_ffn, w_in, pe_cmp, w_cmp_k1, w_cmp_k2, w_cmp_v1, w_cmp_v2, rel_bias, lam_re, lam_im, log_dt, b_re, b_im, c_re, c_im, d_skip, w_glu, w_out, w_router, b_router, w_exp_gate, w_exp_up, w_exp_down, w_sh_gate, w_sh_up, w_sh_down):
    depth = w_in.shape[0]
    xp, xs = x_prompt, x_sample
    p_states, s_states = [], []
    for l in range(depth):
        xp, xs, st_p, st_s = _layer(
            xp, xs, c_prompt, c_sample, page_table, cache_cmp_k[l], cache_cmp_v[l], cache_sel_k[l],
            cache_sel_v[l], cache_win_k[l], cache_win_v[l], state_s5_re[l], state_s5_im[l],
            w_ada[l], b_ada[l], g_pre_mix[l], g_post_mix[l], g_pre_ffn[l], g_post_ffn[l], w_in[l],
            pe_cmp[l], w_cmp_k1[l], w_cmp_k2[l], w_cmp_v1[l], w_cmp_v2[l], rel_bias,
            lam_re[l], lam_im[l], log_dt[l], b_re[l], b_im[l], c_re[l], c_im[l], d_skip[l],
            w_glu[l], w_out[l], w_router[l], b_router[l], w_exp_gate[l], w_exp_up[l], w_exp_down[l],
            w_sh_gate[l], w_sh_up[l], w_sh_down[l])
        p_states.append(st_p)
        s_states.append(st_s)
    p_st = tuple(jnp.stack(a) for a in zip(*p_states))
    s_st = tuple(jnp.stack(a) for a in zip(*s_states))
    return (xp, xs) + p_st + s_st
```

```python
import functools
import math

import numpy as np
import jax
import jax.numpy as jnp
from jax import lax
from jax.experimental import pallas as pl
from jax.experimental.pallas import tpu as pltpu

F32 = jnp.float32
BF16 = jnp.bfloat16
I32 = jnp.int32
U32 = jnp.uint32

D_MODEL = 1024
N_HEADS = 8
HEAD_DIM = 64
N_KV = 2
GQ = N_HEADS // N_KV
Q_W = N_HEADS * HEAD_DIM
KV_W = N_KV * HEAD_DIM
S5_W = D_MODEL - Q_W
S5_H = 16
S5_G = S5_W // S5_H
S5_P = 64
GATE_OFF = Q_W + 6 * KV_W
U_OFF = GATE_OFF + 3 * N_HEADS
CMP_BLOCK = 32
SEL_BLOCK = 64
N_SEL = 16
WINDOW = 512
NUM_BUCKETS = 32
REL_MAX_DIST = 128
N_EXPERTS = 256
TOP_K = 8
N_EGROUPS = 8
TOPK_GROUPS = 4
EGROUP = N_EXPERTS // N_EGROUPS
D_EXPERT = 256
ROUTED_SCALE = 2.5
EPS = 1e-6
SCALE = HEAD_DIM ** -0.5
PAGE = 128
CMP_PER_PAGE = PAGE // CMP_BLOCK

QT = 128
VT = 2 * QT
N_DELTA = WINDOW // QT + 1
S5_CHUNK = 64
S5_ROWS = 8
MOE_BLK = 256
NEG = -2e30
M_INIT = -1e30
VMEM_LIMIT = 56 * 1024 * 1024
WT_ROWS = Q_W + 6 * KV_W + 32
HIGHEST = lax.Precision.HIGHEST


def _cp(*sem):
    return pltpu.CompilerParams(dimension_semantics=sem, vmem_limit_bytes=VMEM_LIMIT)


def _dot(a, b, precision=None):
    return jnp.dot(a, b, preferred_element_type=F32, precision=precision)


def _dot_nt(a, b):
    return lax.dot_general(a, b, (((1,), (1,)), ((), ())), preferred_element_type=F32)


def _sigmoid(x):
    return 1.0 / (1.0 + jnp.exp(-x))


def _silu(x):
    return x * _sigmoid(x)


def _rms(x, g):
    return x * lax.rsqrt(jnp.mean(x * x, axis=-1, keepdims=True) + EPS) * g


def _bucket_table(n):
    d = np.arange(n)
    exact = NUM_BUCKETS // 2
    nf = np.maximum(d, 1).astype(np.float32)
    large = exact + (np.log(nf / np.float32(exact)) / np.float32(math.log(REL_MAX_DIST / exact))
                     * np.float32(NUM_BUCKETS - exact)).astype(np.int32)
    return np.where(d < exact, d, np.minimum(large, NUM_BUCKETS - 1)).astype(np.int32)


def _ada_kernel(c_ref, w_ref, b_ref, o_ref):
    a = _silu(c_ref[...]).astype(BF16)
    o_ref[...] = _dot(a, w_ref[...].astype(BF16)) + b_ref[...]


def _adaln(c, w_ada, b_ada):
    n, d = c.shape
    n_out = w_ada.shape[1]
    tn = 1024
    return pl.pallas_call(
        _ada_kernel,
        grid=(n_out // tn,),
        in_specs=[pl.BlockSpec((n, d), lambda j: (0, 0)),
                  pl.BlockSpec((d, tn), lambda j: (0, j)),
                  pl.BlockSpec((1, tn), lambda j: (0, j))],
        out_specs=pl.BlockSpec((n, tn), lambda j: (0, j)),
        out_shape=jax.ShapeDtypeStruct((n, n_out), F32),
        compiler_params=_cp("arbitrary"),
        name="adaln",
    )(c, w_ada, b_ada.reshape(1, n_out))


def _pre_project(x_ref, sc_ref, sh_ref, g_ref, wn_ref, wt_ref):
    h = _rms(x_ref[...], g_ref[...]) * (1.0 + sc_ref[0]) + sh_ref[0]
    hb = h.astype(BF16)
    return _dot(hb, wn_ref[...]), _dot_nt(wt_ref[...], hb)


def _pre_prompt_kernel(x_ref, sc_ref, sh_ref, g_ref, wn_ref, wt_ref,
                       kct_ref, vct_ref, kst_ref, vst32_ref, kwt_ref, vwt32_ref, kc_ref, vc_ref, u_ref,
                       ksb_ref, kwb_ref, qt_ref, vst_ref, vsn_ref, vwt_ref, gt_ref):
    tm = x_ref.shape[0]
    zn, zt = _pre_project(x_ref, sc_ref, sh_ref, g_ref, wn_ref, wt_ref)
    trans = lambda j, g: zt[Q_W + j * KV_W + g * HEAD_DIM:Q_W + j * KV_W + (g + 1) * HEAD_DIM]
    for j, ref in enumerate((kct_ref, vct_ref, kst_ref, vst32_ref, kwt_ref, vwt32_ref)):
        for g in range(N_KV):
            ref[0, g] = trans(j, g)
    kc_ref[...] = zn[:, 0:KV_W]
    vc_ref[...] = zn[:, KV_W:2 * KV_W]
    u_ref[...] = zn[:, 6 * KV_W:]
    for g in range(N_KV):
        ksb_ref[g] = zn[:, 2 * KV_W + g * HEAD_DIM:2 * KV_W + (g + 1) * HEAD_DIM].astype(BF16)
        kwb_ref[g] = zn[:, 4 * KV_W + g * HEAD_DIM:4 * KV_W + (g + 1) * HEAD_DIM].astype(BF16)
        vs_t = trans(3, g).astype(BF16)
        vw_t = trans(5, g).astype(BF16)
        for c in range(tm // VT):
            vst_ref[g, c] = vs_t[:, c * VT:(c + 1) * VT]
        for c in range(tm // QT):
            vsn_ref[g, c] = vs_t[:, c * QT:(c + 1) * QT]
            vwt_ref[g, c] = vw_t[:, c * QT:(c + 1) * QT]
        r2 = Q_W + 6 * KV_W + g * 16
        gt_ref[g] = _sigmoid(zt[r2:r2 + 16])
    qt_ref[...] = zt[:Q_W].astype(BF16)


def _pre_sample_kernel(x_ref, sc_ref, sh_ref, g_ref, wn_ref, wt_ref,
                       kc_ref, vc_ref, ks_ref, vs_ref, kw_ref, vw_ref, u_ref, qt_ref, gt_ref):
    zn, zt = _pre_project(x_ref, sc_ref, sh_ref, g_ref, wn_ref, wt_ref)
    for j, ref in enumerate((kc_ref, vc_ref, ks_ref, vs_ref, kw_ref, vw_ref)):
        ref[...] = zn[:, j * KV_W:(j + 1) * KV_W]
    u_ref[...] = zn[:, 6 * KV_W:]
    qt_ref[...] = zt[:Q_W].astype(BF16)
    for g in range(N_KV):
        r2 = Q_W + 6 * KV_W + g * 16
        gt_ref[g] = _sigmoid(zt[r2:r2 + 16])


def _pre_weights(w_in):
    wn = jnp.concatenate([w_in[:, Q_W:GATE_OFF], w_in[:, U_OFF:]], axis=1).astype(BF16)
    gate_cols = []
    for g in range(N_KV):
        for j in range(3):
            for hh in range(GQ):
                gate_cols.append(GATE_OFF + (g * GQ + hh) * 3 + j)
        gate_cols.extend([GATE_OFF] * 4)
    wt = jnp.concatenate([
        w_in[:, :Q_W] * SCALE,
        w_in[:, Q_W:GATE_OFF],
        w_in[:, np.array(gate_cols)],
    ], axis=1).T.astype(BF16)
    return wn, wt


def _pre_in_specs(tm, r, per, wn, wt):
    return [pl.BlockSpec((tm, D_MODEL), lambda i: (i, 0)),
            pl.BlockSpec((1, r, D_MODEL), lambda i: (i // per, 0, 0)),
            pl.BlockSpec((1, r, D_MODEL), lambda i: (i // per, 0, 0)),
            pl.BlockSpec((1, D_MODEL), lambda i: (0, 0)),
            pl.BlockSpec(wn.shape, lambda i: (0, 0)),
            pl.BlockSpec(wt.shape, lambda i: (0, 0))]


def _pre_mix_prompt(x, sc, sh, g_pre, wn, wt, tm, batch, seq):
    t = x.shape[0]
    per = seq // tm
    f = lambda shape: jax.ShapeDtypeStruct(shape, F32)
    b = lambda shape: jax.ShapeDtypeStruct(shape, BF16)
    tr_spec = pl.BlockSpec((1, N_KV, HEAD_DIM, tm), lambda i: (i // per, 0, 0, i % per))
    kv_spec = pl.BlockSpec((tm, KV_W), lambda i: (i, 0))
    return pl.pallas_call(
        _pre_prompt_kernel,
        grid=(t // tm,),
        in_specs=_pre_in_specs(tm, 1, per, wn, wt),
        out_specs=[tr_spec] * 6 + [
            kv_spec, kv_spec,
            pl.BlockSpec((tm, S5_W), lambda i: (i, 0)),
            pl.BlockSpec((N_KV, tm, HEAD_DIM), lambda i: (0, i, 0)),
            pl.BlockSpec((N_KV, tm, HEAD_DIM), lambda i: (0, i, 0)),
            pl.BlockSpec((Q_W, tm), lambda i: (0, i)),
            pl.BlockSpec((N_KV, tm // VT, HEAD_DIM, VT), lambda i: (0, i, 0, 0)),
            pl.BlockSpec((N_KV, tm // QT, HEAD_DIM, QT), lambda i: (0, i, 0, 0)),
            pl.BlockSpec((N_KV, tm // QT, HEAD_DIM, QT), lambda i: (0, i, 0, 0)),
            pl.BlockSpec((N_KV, 16, tm), lambda i: (0, 0, i))],
        out_shape=[f((batch, N_KV, HEAD_DIM, seq))] * 6 + [
            f((t, KV_W)), f((t, KV_W)),
            f((t, S5_W)), b((N_KV, t, HEAD_DIM)), b((N_KV, t, HEAD_DIM)), b((Q_W, t)),
            b((N_KV, t // VT, HEAD_DIM, VT)), b((N_KV, t // QT, HEAD_DIM, QT)),
            b((N_KV, t // QT, HEAD_DIM, QT)), f((N_KV, 16, t))],
        compiler_params=_cp("arbitrary"),
        name="pre_mix_prompt",
    )(x, sc, sh, g_pre.reshape(1, D_MODEL), wn, wt)


def _pre_mix_sample(x, sc, sh, g_pre, wn, wt):
    t = x.shape[0]
    f = lambda shape: jax.ShapeDtypeStruct(shape, F32)
    kv_spec = pl.BlockSpec((t, KV_W), lambda i: (i, 0))
    return pl.pallas_call(
        _pre_sample_kernel,
        grid=(1,),
        in_specs=_pre_in_specs(t, t, 1, wn, wt),
        out_specs=[kv_spec] * 6 + [pl.BlockSpec((t, S5_W), lambda i: (i, 0)),
                                   pl.BlockSpec((Q_W, t), lambda i: (0, i)),
                                   pl.BlockSpec((N_KV, 16, t), lambda i: (0, 0, i))],
        out_shape=[f((t, KV_W))] * 6 + [f((t, S5_W)), jax.ShapeDtypeStruct((Q_W, t), BF16),
                                        f((N_KV, 16, t))],
        compiler_params=_cp("arbitrary"),
        name="pre_mix_sample",
    )(x, sc, sh, g_pre.reshape(1, D_MODEL), wn, wt)


def _cmp_kernel(x_ref, pe_ref, w1_ref, w2_ref, o_ref):
    xb = (x_ref[...] + pe_ref[...]).astype(BF16)
    hid = _silu(_dot(xb, w1_ref[...]))
    o_ref[...] = _dot(hid.astype(BF16), w2_ref[...])


def _cmp_weights(pe, w1, w2):
    eye = jnp.eye(N_KV, dtype=F32)
    w1b = jnp.einsum('jde,gh->jgdhe', w1, eye).reshape(CMP_BLOCK * KV_W, KV_W).astype(BF16)
    w2b = jnp.einsum('ef,gh->gehf', w2, eye).reshape(KV_W, KV_W).astype(BF16)
    peb = jnp.broadcast_to(pe[:, None, :], (CMP_BLOCK, N_KV, HEAD_DIM)).reshape(1, CMP_BLOCK * KV_W)
    return peb, w1b, w2b


def _compress(x, peb, w1b, w2b):
    r, k = x.shape
    tr = min(256, r)
    return pl.pallas_call(
        _cmp_kernel,
        grid=(r // tr,),
        in_specs=[pl.BlockSpec((tr, k), lambda i: (i, 0)),
                  pl.BlockSpec((1, k), lambda i: (0, 0)),
                  pl.BlockSpec((k, KV_W), lambda i: (0, 0)),
                  pl.BlockSpec((KV_W, KV_W), lambda i: (0, 0))],
        out_specs=pl.BlockSpec((tr, KV_W), lambda i: (i, 0)),
        out_shape=jax.ShapeDtypeStruct((r, KV_W), F32),
        compiler_params=_cp("arbitrary"),
        name="compress",
    )(x, peb, w1b, w2b)


def _cmp_pages_kernel(xk_ref, xv_ref, w1k_ref, w1v_ref, bk_ref, bv_ref, w2k_ref, w2v_ref,
                      ok_ref, ov_ref):
    n_rows = ok_ref.shape[0]
    for x_ref, w1_ref, b_ref, w2_ref, o_ref in ((xk_ref, w1k_ref, bk_ref, w2k_ref, ok_ref),
                                               (xv_ref, w1v_ref, bv_ref, w2v_ref, ov_ref)):
        acc = jnp.zeros((n_rows, CMP_PER_PAGE * HEAD_DIM), F32)
        for dd in range(HEAD_DIM // 2):
            r0 = x_ref[pl.ds(2 * dd, n_rows, stride=HEAD_DIM), :]
            r1 = x_ref[pl.ds(2 * dd + 1, n_rows, stride=HEAD_DIM), :]
            acc = acc + _dot(jnp.concatenate([r0, r1], axis=1).astype(BF16), w1_ref[dd])
        hid = _silu(acc + b_ref[...])
        o_ref[...] = _dot(hid.astype(BF16), w2_ref[...])


def _cmp_pages_weights(pe, w1, w2):
    eye = jnp.eye(CMP_PER_PAGE, dtype=F32)
    w1t = jnp.einsum('jde,mn->dmjne', w1, eye).reshape(HEAD_DIM // 2, 2 * PAGE, CMP_PER_PAGE * HEAD_DIM)
    bias = jnp.einsum('jd,jde->e', pe, w1, precision=HIGHEST)
    bias = jnp.tile(bias, CMP_PER_PAGE).reshape(1, CMP_PER_PAGE * HEAD_DIM)
    w2t = jnp.einsum('ef,mn->menf', w2, eye).reshape(CMP_PER_PAGE * HEAD_DIM, CMP_PER_PAGE * HEAD_DIM)
    return w1t.astype(BF16), bias, w2t.astype(BF16)


def _compress_pages(xk, xv, wk, wv, n_seq, n_pages):
    rows_in = n_pages * N_KV * HEAD_DIM
    rows_out = n_pages * N_KV
    wcols = CMP_PER_PAGE * HEAD_DIM
    full = lambda a: pl.BlockSpec(a.shape, lambda i: (0,) * a.ndim)
    x_spec = pl.BlockSpec((rows_in, PAGE), lambda i: (i, 0))
    o_spec = pl.BlockSpec((rows_out, wcols), lambda i: (i, 0))
    o_shape = jax.ShapeDtypeStruct((n_seq * rows_out, wcols), F32)
    return pl.pallas_call(
        _cmp_pages_kernel,
        grid=(n_seq,),
        in_specs=[x_spec, x_spec, full(wk[0]), full(wv[0]), full(wk[1]), full(wv[1]),
                  full(wk[2]), full(wv[2])],
        out_specs=[o_spec, o_spec],
        out_shape=[o_shape, o_shape],
        compiler_params=_cp("arbitrary"),
        name="compress_pages",
    )(xk, xv, wk[0], wv[0], wk[1], wv[1], wk[2], wv[2])


def _attn_kernel(q_ref, g_ref, kc_ref, vct_ref, ks_ref, vst_ref, vsn_ref, kw_ref, vwt_ref,
                 ctab_ref, wtab_ref, o_ref, mb_ref, mbf_ref, s_sc, p_sc, *, n_cmp):
    i = pl.program_id(2)
    lanes = GQ * QT
    qt = q_ref[...]
    q4 = jnp.concatenate([qt[hh * HEAD_DIM:(hh + 1) * HEAD_DIM] for hh in range(GQ)], axis=1)
    gates = g_ref[0]
    grow = lambda j: jnp.concatenate([gates[j * GQ + hh:j * GQ + hh + 1] for hh in range(GQ)], axis=1)

    def online(carry, s, v_t):
        m, l, acc = carry
        m_new = jnp.maximum(m, jnp.max(s, axis=0, keepdims=True))
        alpha = jnp.exp(m - m_new)
        p = jnp.exp(s - m_new)
        l = alpha * l + jnp.sum(p, axis=0, keepdims=True)
        return m_new, l, alpha * acc + _dot(v_t, p.astype(BF16))

    init = (jnp.full((1, lanes), M_INIT, F32), jnp.zeros((1, lanes), F32),
            jnp.zeros((HEAD_DIM, lanes), F32))
    finish = lambda carry: carry[2] * (1.0 / jnp.maximum(carry[1], 1e-30))
    block_rows = lambda ref, first, n, extra=0.0: jnp.concatenate(
        [jnp.broadcast_to(ref[pl.ds(first + c, 1), :] + extra, (SEL_BLOCK, lanes)) for c in range(n)],
        axis=0)

    def near_tiles(carry, k_ref, vt_ref, deltas, live, masked):
        ks, vs, bs = [], [], []
        for dl in deltas:
            kt = i - dl
            ktc = jnp.maximum(kt, 0)
            ks.append(k_ref[0, pl.ds(pl.multiple_of(ktc * QT, QT), QT), :])
            vs.append(vt_ref[0, ktc])
            dead = jnp.where(live(dl, kt), 0.0, NEG)
            if masked:
                bs.append(wtab_ref[0, dl] + block_rows(mb_ref, (QT // SEL_BLOCK) * ktc, QT // SEL_BLOCK, dead))
            else:
                bs.append(wtab_ref[0, dl] + dead)
        s = _dot(jnp.concatenate(ks, axis=0), q4) + jnp.concatenate(bs, axis=0)
        return online(carry, s, jnp.concatenate(vs, axis=1))

    o_w = finish(near_tiles(init, kw_ref, vwt_ref, list(range(N_DELTA - 1, -1, -1)),
                            lambda dl, kt: kt >= 0, False))

    half = n_cmp // 2
    sc = _dot(kc_ref[0, 0], q4)
    rho = lax.broadcasted_iota(I32, (n_cmp, 16), 0)
    col = lax.broadcasted_iota(I32, (n_cmp, 16), 1)
    blk_n = 2 * jnp.where(rho >= half, rho - half, rho) + jnp.where(rho >= half, 1, 0)
    rel = blk_n - (4 * i - 4)
    want = jnp.where(rel < 0, 8, jnp.where(rel > 7, 9, rel))
    place = jnp.where(col == want, 1.0, 0.0).astype(BF16)
    sc = sc + _dot(place, ctab_ref[0, 0]) + _dot(place, ctab_ref[0, 1])
    mc = jnp.maximum(jnp.max(sc, axis=0, keepdims=True), M_INIT)
    ec = jnp.exp(sc - mc)
    pc = ec * (1.0 / jnp.maximum(jnp.sum(ec, axis=0, keepdims=True), 1e-30))
    o_cw = grow(0) * _dot(vct_ref[0, 0], pc.astype(BF16)) + grow(2) * o_w

    ps = pc[:half] + pc[half:]
    imp = ps[:, 0:QT]
    for hh in range(1, GQ):
        imp = imp + ps[:, hh * QT:(hh + 1) * QT]
    n_blk = half
    blk = lax.broadcasted_iota(I32, (n_blk, QT), 0)
    tq = i * QT + lax.broadcasted_iota(I32, (n_blk, QT), 1)
    cur = lax.shift_right_logical(tq, 6)
    forced = (blk == 0) | (blk == cur) | (blk == cur - 1)
    score = jnp.where(blk * SEL_BLOCK <= tq, jnp.where(forced, 1e4, imp), -jnp.inf)
    blkf = blk.astype(F32)

    def pick(_, carry):
        work, mb = carry
        mx = jnp.max(work, axis=0, keepdims=True)
        idx = jnp.min(jnp.where(work == mx, blkf, 1e9), axis=0, keepdims=True)
        hit = blkf == idx
        return jnp.where(hit, -jnp.inf, work), jnp.where(hit, 0.0, mb)

    _, mb = lax.fori_loop(0, min(N_SEL, n_blk), pick, (score, jnp.full((n_blk, QT), NEG, F32)))
    for hh in range(GQ):
        mb_ref[:, hh * QT:(hh + 1) * QT] = mb
    mbf_ref[...] = mb_ref[...] + wtab_ref[0, 2, 0:1, :]

    n_far = jnp.maximum(i - 1, 0) // 2
    last_pair = ks_ref.shape[1] // VT - 1
    s_sc[...] = jnp.full(s_sc.shape, NEG, F32)
    p_sc[...] = jnp.zeros(p_sc.shape, BF16)

    def far_step(t, carry):
        m, l, acc, alpha_prev = carry
        pv = _dot(vst_ref[0, jnp.clip(t - 2, 0, last_pair)], p_sc[...])
        s = s_sc[...]
        m_new = jnp.maximum(m, jnp.max(s, axis=0, keepdims=True))
        alpha = jnp.exp(m - m_new)
        p = jnp.exp(s - m_new)
        l = alpha * l + jnp.sum(p, axis=0, keepdims=True)
        ta = jnp.minimum(t, last_pair)
        k_t = ks_ref[0, pl.ds(pl.multiple_of(ta * VT, VT), VT), :]
        s_next = _dot(k_t, q4) + block_rows(mbf_ref, (VT // SEL_BLOCK) * ta, VT // SEL_BLOCK,
                                            jnp.where(t < n_far, 0.0, NEG))
        p_sc[...] = p.astype(BF16)
        s_sc[...] = s_next
        return m_new, l, alpha_prev * acc + pv, alpha

    m, l, acc, _ = lax.fori_loop(0, n_far + 2, far_step, init + (jnp.ones((1, lanes), F32),))

    sel_live = lambda dl, kt: (kt >= 0) & ((dl < 2) | (i % 2 == 0))
    o_s = finish(near_tiles((m, l, acc), ks_ref, vsn_ref, [2, 1, 0], sel_live, True))
    o_t = o_cw + grow(1) * o_s
    o_hd = jnp.concatenate([o_t[:, hh * QT:(hh + 1) * QT] for hh in range(GQ)], axis=0)
    o_ref[...] = o_hd.T.astype(BF16)


def _bias_tables(rel_bias):
    span = N_DELTA * QT
    bt = _bucket_table(WINDOW)
    period = span + QT
    by_dist = jnp.concatenate([rel_bias[bt], jnp.full((period - WINDOW, N_HEADS), NEG, F32)], axis=0)
    toep = jnp.tile(by_dist.T, (1, QT))[:, :QT * (period - 1)].reshape(N_HEADS, QT, period - 1)
    tab = toep[:, :, :span].reshape(N_KV, GQ, QT, N_DELTA, QT)
    wtab = tab.transpose(0, 3, 2, 1, 4).reshape(N_KV, N_DELTA, QT, GQ * QT)
    wtab = jnp.concatenate([wtab, jnp.full((N_KV, 1, QT, GQ * QT), NEG, F32)], axis=1)
    ql = np.arange(QT)[None, :]
    r = np.arange(8)[:, None]
    d = ql + (4 * CMP_BLOCK - CMP_BLOCK + 1) - CMP_BLOCK * r
    near = jnp.where((d >= 0)[..., None], rel_bias[bt[np.clip(d, 0, len(bt) - 1)]], NEG)
    far = jnp.broadcast_to(rel_bias[NUM_BUCKETS - 1][None, None, :], (1, QT, N_HEADS))
    rows = jnp.concatenate([near, far, jnp.full((1, QT, N_HEADS), NEG, F32),
                            jnp.zeros((6, QT, N_HEADS), F32)], axis=0)
    ctab = rows.reshape(16, QT, N_KV, GQ).transpose(2, 0, 3, 1).reshape(N_KV, 16, GQ * QT)
    hi = ctab.astype(BF16)
    lo = (ctab - hi.astype(F32)).astype(BF16)
    return wtab.astype(F32), jnp.stack([hi, lo], axis=1)


def _attn_prompt(qt, gt, kcmp, vcmpt, ksb, vst, vsn, kwb, vwt, wtab, ctab, batch, seq):
    nq = seq // QT
    n_cmp = kcmp.shape[2]
    lanes = GQ * QT
    t = batch * seq
    return pl.pallas_call(
        functools.partial(_attn_kernel, n_cmp=n_cmp),
        grid=(batch, N_KV, nq),
        in_specs=[
            pl.BlockSpec((GQ * HEAD_DIM, QT), lambda b, g, i: (g, b * nq + i)),
            pl.BlockSpec((1, 16, QT), lambda b, g, i: (g, 0, b * nq + i)),
            pl.BlockSpec((1, 1, n_cmp, HEAD_DIM), lambda b, g, i: (b, g, 0, 0)),
            pl.BlockSpec((1, 1, HEAD_DIM, n_cmp), lambda b, g, i: (b, g, 0, 0)),
            pl.BlockSpec((1, seq, HEAD_DIM), lambda b, g, i: (g, b, 0)),
            pl.BlockSpec((1, seq // VT, HEAD_DIM, VT), lambda b, g, i: (g, b, 0, 0)),
            pl.BlockSpec((1, nq, HEAD_DIM, QT), lambda b, g, i: (g, b, 0, 0)),
            pl.BlockSpec((1, seq, HEAD_DIM), lambda b, g, i: (g, b, 0)),
            pl.BlockSpec((1, nq, HEAD_DIM, QT), lambda b, g, i: (g, b, 0, 0)),
            pl.BlockSpec((1, 2, 16, lanes), lambda b, g, i: (g, 0, 0, 0)),
            pl.BlockSpec((1, N_DELTA + 1, QT, lanes), lambda b, g, i: (g, 0, 0, 0)),
        ],
        out_specs=pl.BlockSpec((QT, GQ * HEAD_DIM), lambda b, g, i: (b * nq + i, g)),
        out_shape=jax.ShapeDtypeStruct((t, Q_W), BF16),
        scratch_shapes=[pltpu.VMEM((n_cmp // 2, lanes), F32), pltpu.VMEM((n_cmp // 2, lanes), F32),
                        pltpu.VMEM((VT, lanes), F32), pltpu.VMEM((VT, lanes), BF16)],
        compiler_params=_cp("arbitrary", "arbitrary", "arbitrary"),
        name="nsa_prompt",
    )(qt, gt, kcmp, vcmpt, ksb, vst, vsn, kwb, vwt, ctab, wtab)


def _s5_kernel(u_ref, mt_ref, sbr_ref, sbi_ref, ccr_ref, cci_ref, lb_ref, h0r_ref, h0i_ref,
               y_ref, hr_ref, hi_ref, sr_sc, si_sc, pr_sc, pi_sc, *, n_chunks, rb):
    u = u_ref[0]
    sr_sc[...] = _dot(u, sbr_ref[0])
    si_sc[...] = _dot(u, sbi_ref[0])
    lr = lb_ref[0, 0:1, :]
    li = lb_ref[0, 1:2, :]

    def step(c, carry):
        hr, hi = carry
        rows = pl.ds(pl.multiple_of(c * rb, rb), rb)
        pr_sc[rows, :] = hr
        pi_sc[rows, :] = hi
        return (lr * hr - li * hi + sr_sc[rows, :], lr * hi + li * hr + si_sc[rows, :])

    hr, hi = lax.fori_loop(0, n_chunks, step, (h0r_ref[0], h0i_ref[0]))
    hr_ref[0] = hr
    hi_ref[0] = hi
    y_ref[0] = (_dot(u, mt_ref[0]) + _dot(pr_sc[...].astype(BF16), ccr_ref[0])
                + _dot(pi_sc[...].astype(BF16), cci_ref[0]))


def _s5_prep(lam_re, lam_im, log_dt, b_re, b_im, c_re, c_im, d_skip, chunk):
    lam = lax.complex(lam_re, lam_im)
    z = lam * jnp.exp(log_dt)[:, None]
    lbar = jnp.exp(z)
    bbar = ((lbar - 1.0) / lam)[:, :, None] * lax.complex(b_re, b_im)
    c = lax.complex(c_re, c_im)
    pw = jnp.exp(z[None] * jnp.arange(chunk + 1, dtype=F32)[:, None, None])
    kern = jnp.einsum('gap,jgp,gpb->jgab', c, pw[:chunk], bbar, precision=HIGHEST).real
    kern = kern.at[0].add(jax.vmap(jnp.diag)(d_skip))
    lag = np.arange(chunk)[None, :] - np.arange(chunk)[:, None]
    kt = kern.transpose(1, 0, 3, 2)
    m5 = jnp.where((lag >= 0)[None, :, :, None, None], kt[:, np.clip(lag, 0, None)], 0.0)
    mt = m5.transpose(0, 1, 3, 2, 4).reshape(S5_G, chunk * S5_H, chunk * S5_H)
    sb = jnp.einsum('lgp,gph->glhp', pw[chunk - 1 - np.arange(chunk)], bbar).reshape(
        S5_G, chunk * S5_H, S5_P)
    cc = jnp.einsum('ghp,lgp->gplh', c, pw[1:]).reshape(S5_G, S5_P, chunk * S5_H)
    lb = jnp.stack([pw[chunk].real, pw[chunk].imag], axis=1)
    return (mt.astype(BF16), sb.real.astype(BF16), sb.imag.astype(BF16),
            cc.real.astype(BF16), (-cc.imag).astype(BF16), lb.astype(F32))


def _s5_call(u, prep, h0r, h0i, n_chunks, rb):
    mt, sbr, sbi, ccr, cci, lb = prep
    g, rows, w = u.shape
    per_g = lambda *shape: pl.BlockSpec((1,) + shape, lambda i: (i,) + (0,) * len(shape))
    return pl.pallas_call(
        functools.partial(_s5_kernel, n_chunks=n_chunks, rb=rb),
        grid=(g,),
        in_specs=[per_g(rows, w), per_g(w, w), per_g(w, S5_P), per_g(w, S5_P),
                  per_g(S5_P, w), per_g(S5_P, w), per_g(2, S5_P), per_g(rb, S5_P), per_g(rb, S5_P)],
        out_specs=[per_g(rows, w), per_g(rb, S5_P), per_g(rb, S5_P)],
        out_shape=[jax.ShapeDtypeStruct((g, rows, w), F32),
                   jax.ShapeDtypeStruct((g, rb, S5_P), F32),
                   jax.ShapeDtypeStruct((g, rb, S5_P), F32)],
        scratch_shapes=[pltpu.VMEM((rows, S5_P), F32)] * 4,
        compiler_params=_cp("arbitrary"),
        name="s5_scan",
    )(u, mt, sbr, sbi, ccr, cci, lb, h0r, h0i)


def _s5_prompt(u, prep, batch, seq):
    nc = seq // S5_CHUNK
    ub = u.reshape(batch, nc, S5_CHUNK, S5_G, S5_H).transpose(3, 1, 0, 2, 4)
    ub = jnp.pad(ub, ((0, 0), (0, 0), (0, S5_ROWS - batch), (0, 0), (0, 0)))
    ub = ub.reshape(S5_G, nc * S5_ROWS, S5_CHUNK * S5_H).astype(BF16)
    zero = jnp.zeros((S5_G, S5_ROWS, S5_P), F32)
    y, hr, hi = _s5_call(ub, prep, zero, zero, nc, S5_ROWS)
    y = y.reshape(S5_G, nc, S5_ROWS, S5_CHUNK, S5_H)[:, :, :batch].transpose(2, 1, 3, 0, 4)
    return (y.reshape(batch * seq, S5_W),
            hr[:, :batch].transpose(1, 0, 2), hi[:, :batch].transpose(1, 0, 2))


def _s5_sample(u, prep, h0r, h0i):
    n = u.shape[0]
    ub = u.reshape(n, S5_G, S5_H).transpose(1, 0, 2).astype(BF16)
    y, hr, hi = _s5_call(ub, prep, h0r.transpose(1, 0, 2), h0i.transpose(1, 0, 2), 1, n)
    return y.transpose(1, 0, 2).reshape(n, S5_W), hr.transpose(1, 0, 2), hi.transpose(1, 0, 2)


def _softmax_lanes(s):
    m = jnp.maximum(jnp.max(s, axis=-1, keepdims=True), M_INIT)
    e = jnp.exp(s - m)
    return e * (1.0 / jnp.maximum(jnp.sum(e, axis=-1, keepdims=True), 1e-30))


def _bqk(q, k):
    return jnp.einsum('bhd,bnd->bhn', q, k, preferred_element_type=F32)


def _bpv(p, v):
    return jnp.einsum('bhn,bnd->bhd', p.astype(BF16), v, preferred_element_type=F32)


def _sattn1_kernel(q_ref, kc_ref, vc_ref, bias_ref, oc_ref, pick_ref, *, n_pick):
    q = q_ref[...]
    p = _softmax_lanes(_bqk(q, kc_ref[...]) + bias_ref[...])
    oc_ref[...] = _bpv(p, vc_ref[...])
    n_cmp = p.shape[-1]
    half = n_cmp // 2
    ps = p[:, 0]
    for hh in range(1, GQ):
        ps = ps + p[:, hh]
    imp = ps[:, :half] + ps[:, half:]
    lane = lax.broadcasted_iota(I32, imp.shape, 1).astype(F32)
    work = jnp.where((lane > 0) & (lane < half - 1), imp, -jnp.inf)
    picks = jnp.zeros(imp.shape, F32)
    for t in range(n_pick):
        mx = jnp.max(work, axis=-1, keepdims=True)
        idx = jnp.min(jnp.where(work == mx, lane, 1e9), axis=-1, keepdims=True)
        work = jnp.where(lane == idx, -jnp.inf, work)
        picks = jnp.where(lane == t, idx, picks)
    pick_ref[...] = picks.astype(I32)


def _bqkt(q, kt):
    return jnp.einsum('bhd,bdn->bhn', q, kt.astype(BF16), preferred_element_type=F32)


def _bpvt(p, vt):
    return jnp.einsum('bhn,bdn->bhd', p.astype(BF16), vt.astype(BF16), preferred_element_type=F32)


def _sattn2_kernel(q_ref, ks_ref, vs_ref, base_ref, corr_ref, flag_ref, ex_ref,
                   kw_ref, vw_ref, wb_ref, oc_ref, g_ref, o_ref):
    q = q_ref[...]
    near = _dot(flag_ref[...], ex_ref[...])
    bias = base_ref[...] + near[:, None, :] * corr_ref[...]
    p_s = _softmax_lanes(_bqkt(q, ks_ref[...]) + bias)
    o_s = _bpvt(p_s, vs_ref[...])
    p_w = _softmax_lanes(_bqkt(q, kw_ref[...]) + wb_ref[...])
    o_w = _bpvt(p_w, vw_ref[...])
    g = g_ref[...]
    o_ref[...] = g[:, :, 0:1] * oc_ref[...] + g[:, :, 1:2] * o_s + g[:, :, 2:3] * o_w


def _attn_sample(q, gates, kcmp, vcmp, page_table, cache_sk, cache_sv, ks_new, vs_new,
                 win_k, win_v, rel_bias):
    n = q.shape[0]
    n_cmp = kcmp.shape[2]
    past = page_table.shape[1] * PAGE
    n_blk = past // SEL_BLOCK
    n_pick = N_SEL - 3
    nbq = n * N_KV
    nb = 16
    bt = _bucket_table(past + 1)
    qg = q.reshape(nbq, GQ, HEAD_DIM)
    per_row = lambda tab: jnp.tile(tab.reshape(-1, N_KV, GQ).transpose(1, 2, 0), (nb // N_KV, 1, 1))
    blk3 = lambda *s: pl.BlockSpec((nb,) + s, lambda i: (i,) + (0,) * len(s))
    const3 = lambda *s: pl.BlockSpec((nb,) + s, lambda i: (0,) * (len(s) + 1))

    order = np.concatenate([np.arange(0, n_cmp, 2), np.arange(1, n_cmp, 2)])
    d_c = past - (order * CMP_BLOCK + CMP_BLOCK - 1)
    cb = per_row(jnp.where((d_c >= 0)[:, None], rel_bias[bt[np.clip(d_c, 0, None)]], NEG))

    o_c, picks = pl.pallas_call(
        functools.partial(_sattn1_kernel, n_pick=n_pick),
        grid=(nbq // nb,),
        in_specs=[blk3(GQ, HEAD_DIM), blk3(n_cmp, HEAD_DIM), blk3(n_cmp, HEAD_DIM), const3(GQ, n_cmp)],
        out_specs=[blk3(GQ, HEAD_DIM), pl.BlockSpec((nb, n_cmp // 2), lambda i: (i, 0))],
        out_shape=[jax.ShapeDtypeStruct((nbq, GQ, HEAD_DIM), F32),
                   jax.ShapeDtypeStruct((nbq, n_cmp // 2), I32)],
        compiler_params=_cp("arbitrary"),
        name="nsa_sample_cmp",
    )(qg, kcmp.reshape(nbq, n_cmp, HEAD_DIM), vcmp.reshape(nbq, n_cmp, HEAD_DIM), cb)

    picks = picks[:, :n_pick].reshape(n, N_KV, n_pick)
    forced = jnp.broadcast_to(jnp.array([n_blk - 1, 0], I32), (n, N_KV, 2))
    sel = jnp.concatenate([forced, picks], axis=-1)
    n_slot = sel.shape[-1]
    page = jnp.take_along_axis(page_table, (sel // 2).reshape(n, -1), axis=1).reshape(sel.shape)
    head = jnp.broadcast_to(jnp.arange(N_KV, dtype=I32)[None, :, None], sel.shape)
    starts = jnp.stack([page, head, (sel % 2) * SEL_BLOCK], axis=-1)
    dnums = lax.GatherDimensionNumbers(offset_dims=(3, 4), collapsed_slice_dims=(0, 1),
                                       start_index_map=(0, 1, 3))
    n_key = (n_slot + 1) * SEL_BLOCK

    def gather(cache, new):
        got = lax.gather(cache.transpose(0, 2, 3, 1), starts, dnums,
                         slice_sizes=(1, 1, HEAD_DIM, SEL_BLOCK),
                         mode=lax.GatherScatterMode.PROMISE_IN_BOUNDS)
        got = got.transpose(0, 1, 3, 2, 4).reshape(n, N_KV, HEAD_DIM, n_slot * SEL_BLOCK)
        tail = jnp.zeros((n, N_KV, HEAD_DIM, SEL_BLOCK), F32).at[:, :, :, 0].set(
            new.reshape(n, N_KV, HEAD_DIM))
        return jnp.concatenate([got, tail], axis=3).reshape(nbq, HEAD_DIM, n_key)

    k_sel = gather(cache_sk, ks_new)
    v_sel = gather(cache_sv, vs_new)
    r = np.arange(SEL_BLOCK)
    d_base = np.concatenate([SEL_BLOCK - r, past - r] + [np.full(SEL_BLOCK, past)] * n_pick + [-r])
    base = per_row(jnp.where((d_base >= 0)[:, None], rel_bias[bt[np.clip(d_base, 0, past)]], NEG))
    d_near = np.clip(2 * SEL_BLOCK - r, 0, past)
    delta = rel_bias[bt[d_near]] - rel_bias[bt[past]][None, :]
    in_pick = np.zeros((n_slot + 1, 1, 1), np.float32)
    in_pick[2:n_slot] = 1.0
    corr = per_row((in_pick * delta[None]).reshape(n_key, N_HEADS))
    flag = jnp.pad((sel == n_blk - 2).astype(F32).reshape(nbq, n_slot), ((0, 0), (0, 1)))
    expand = jnp.asarray(np.kron(np.eye(n_slot + 1, dtype=np.float32), np.ones((1, SEL_BLOCK), np.float32)))

    n_win = win_k.shape[1]
    d_w = n_win - 1 - np.arange(n_win)
    wb = per_row(jnp.where((d_w < WINDOW)[:, None], rel_bias[bt[d_w]], NEG))
    wk = win_k.transpose(0, 2, 3, 1).reshape(nbq, HEAD_DIM, n_win)
    wv = win_v.transpose(0, 2, 3, 1).reshape(nbq, HEAD_DIM, n_win)
    o = pl.pallas_call(
        _sattn2_kernel,
        grid=(nbq // nb,),
        in_specs=[blk3(GQ, HEAD_DIM), blk3(HEAD_DIM, n_key), blk3(HEAD_DIM, n_key),
                  const3(GQ, n_key), const3(GQ, n_key),
                  pl.BlockSpec((nb, n_slot + 1), lambda i: (i, 0)),
                  pl.BlockSpec((n_slot + 1, n_key), lambda i: (0, 0)),
                  blk3(HEAD_DIM, n_win), blk3(HEAD_DIM, n_win), const3(GQ, n_win),
                  blk3(GQ, HEAD_DIM), blk3(GQ, 3)],
        out_specs=blk3(GQ, HEAD_DIM),
        out_shape=jax.ShapeDtypeStruct((nbq, GQ, HEAD_DIM), F32),
        compiler_params=_cp("arbitrary"),
        name="nsa_sample_sel_win",
    )(qg, k_sel, v_sel, base, corr, flag, expand, wk, wv, wb, o_c, gates.reshape(nbq, GQ, 3))
    return o.reshape(n, Q_W).astype(BF16)


def _gelu_tanh(x):
    return 0.5 * x * (1.0 + jnp.tanh(math.sqrt(2.0 / math.pi) * (x + 0.044715 * (x * x * x))))


def _post_kernel(x_ref, o_ref, y5_ref, gm_ref, shf_ref, scf_ref, gpost_ref, gpre_ref,
                 wglu_ref, wout_ref, wr_ref, br_ref, tri_ref, cin_ref,
                 x1_ref, h2_ref, h2p_ref, eidx_ref, wts_ref, pos_ref, cout_ref, cnt_sc):
    i = pl.program_id(0)
    tm = x_ref.shape[0]

    @pl.when(i == 0)
    def _():
        cnt_sc[...] = cin_ref[...]

    g5 = _gelu_tanh(y5_ref[...])
    g5 = g5 * _sigmoid(_dot(g5.astype(BF16), wglu_ref[...]))
    m = _dot(o_ref[...], wout_ref[:Q_W]) + _dot(g5.astype(BF16), wout_ref[Q_W:])
    x1 = x_ref[...] + gm_ref[0] * _rms(m, gpost_ref[...])
    h2 = _rms(x1, gpre_ref[...]) * (1.0 + scf_ref[0]) + shf_ref[0]
    x1_ref[...] = x1
    h2b = h2.astype(BF16)
    h2_ref[...] = h2b
    bits = pltpu.bitcast(h2b.astype(F32), U32)
    h2p_ref[...] = (bits[:, :D_MODEL // 2] >> 16) | bits[:, D_MODEL // 2:]

    s_t = _sigmoid(_dot(h2, wr_ref[...], precision=HIGHEST)).T
    s_sel = s_t + br_ref[...]
    eio = lax.broadcasted_iota(I32, (N_EXPERTS, tm), 0).astype(F32)
    first = lambda hit, ids: jnp.min(jnp.where(hit, ids, 1e9), axis=0, keepdims=True)
    gscore = []
    for g in range(N_EGROUPS):
        xg = s_sel[g * EGROUP:(g + 1) * EGROUP]
        ig = lax.broadcasted_iota(I32, (EGROUP, tm), 0).astype(F32) + float(g * EGROUP)
        m1 = jnp.max(xg, axis=0, keepdims=True)
        m2 = jnp.max(jnp.where(ig == first(xg == m1, ig), -jnp.inf, xg), axis=0, keepdims=True)
        gscore.append(m1 + m2)
    gsc = jnp.concatenate(gscore, axis=0)
    gio = lax.broadcasted_iota(I32, (N_EGROUPS, tm), 0).astype(F32)
    gmask = jnp.zeros((N_EGROUPS, tm), F32)
    for _ in range(TOPK_GROUPS):
        hit = gio == first(gsc == jnp.max(gsc, axis=0, keepdims=True), gio)
        gmask = jnp.where(hit, 1.0, gmask)
        gsc = jnp.where(hit, -jnp.inf, gsc)
    work = jnp.concatenate(
        [jnp.where(gmask[g:g + 1] > 0.0, s_sel[g * EGROUP:(g + 1) * EGROUP], -jnp.inf)
         for g in range(N_EGROUPS)], axis=0)
    chosen = jnp.zeros((N_EXPERTS, tm), F32)
    ids, wsel = [], []
    for _ in range(TOP_K):
        ik = first(work == jnp.max(work, axis=0, keepdims=True), eio)
        hit = eio == ik
        ids.append(ik)
        wsel.append(jnp.sum(jnp.where(hit, s_t, 0.0), axis=0, keepdims=True))
        work = jnp.where(hit, -jnp.inf, work)
        chosen = jnp.where(hit, 1.0, chosen)
    rank = cnt_sc[:, 0:1] + _dot(chosen.astype(BF16), tri_ref[...])
    pos = [jnp.sum(jnp.where(eio == ik, rank, 0.0), axis=0, keepdims=True) for ik in ids]
    cnt_sc[...] = cnt_sc[...] + jnp.sum(chosen, axis=1, keepdims=True)
    wsum = wsel[0]
    for k in range(1, TOP_K):
        wsum = wsum + wsel[k]
    eidx_ref[...] = jnp.concatenate(ids, axis=0).astype(I32)
    pos_ref[...] = jnp.concatenate(pos, axis=0).astype(I32)
    wts_ref[...] = jnp.concatenate([w / wsum * ROUTED_SCALE for w in wsel], axis=0)
    cout_ref[...] = cnt_sc[...]


def _post_mix(x, o_att, y5, gm, shf, scf, g_post, g_pre, wglu, wout, w_router, b_router,
              cnt_in, tm, rows_per_mod):
    t = x.shape[0]
    r = gm.shape[1]
    per = rows_per_mod // tm
    mod = pl.BlockSpec((1, r, D_MODEL), lambda i: (i // per, 0, 0))
    full = lambda a: pl.BlockSpec(a.shape, lambda i: (0,) * a.ndim)
    vec = pl.BlockSpec((1, D_MODEL), lambda i: (0, 0))
    tri = jnp.asarray(np.triu(np.ones((tm, tm), np.float32), 1), BF16)
    brb = jnp.broadcast_to(b_router[:, None], (N_EXPERTS, tm)).astype(F32)
    route = pl.BlockSpec((TOP_K, tm), lambda i: (0, i))
    return pl.pallas_call(
        _post_kernel,
        grid=(t // tm,),
        in_specs=[pl.BlockSpec((tm, D_MODEL), lambda i: (i, 0)),
                  pl.BlockSpec((tm, Q_W), lambda i: (i, 0)),
                  pl.BlockSpec((tm, S5_W), lambda i: (i, 0)),
                  mod, mod, mod, vec, vec, full(wglu), full(wout), full(w_router), full(brb),
                  full(tri), full(cnt_in)],
        out_specs=[pl.BlockSpec((tm, D_MODEL), lambda i: (i, 0)),
                   pl.BlockSpec((tm, D_MODEL), lambda i: (i, 0)),
                   pl.BlockSpec((tm, D_MODEL // 2), lambda i: (i, 0)),
                   route, route, route, full(cnt_in)],
        out_shape=[jax.ShapeDtypeStruct((t, D_MODEL), F32),
                   jax.ShapeDtypeStruct((t, D_MODEL), BF16),
                   jax.ShapeDtypeStruct((t, D_MODEL // 2), U32),
                   jax.ShapeDtypeStruct((TOP_K, t), I32),
                   jax.ShapeDtypeStruct((TOP_K, t), F32),
                   jax.ShapeDtypeStruct((TOP_K, t), I32),
                   jax.ShapeDtypeStruct(cnt_in.shape, F32)],
        scratch_shapes=[pltpu.VMEM(cnt_in.shape, F32)],
        compiler_params=_cp("arbitrary"),
        name="post_mix_router",
    )(x, o_att, y5, gm, shf, scf, g_post.reshape(1, D_MODEL), g_pre.reshape(1, D_MODEL),
      wglu, wout, w_router, brb, tri, cnt_in)


def _dest_kernel(e_ref, p_ref, ps_ref, o_ref):
    tm = e_ref.shape[1]
    eio = lax.broadcasted_iota(I32, (N_EXPERTS, tm), 0)
    e = e_ref[...]
    start = ps_ref[...]
    rows = [jnp.sum(jnp.where(eio == e[k:k + 1], start, 0.0), axis=0, keepdims=True)
            for k in range(TOP_K)]
    o_ref[...] = jnp.concatenate(rows, axis=0).astype(I32) + p_ref[...]


def _dest(eidx, pos, pad_start, tm):
    t = eidx.shape[1]
    route = pl.BlockSpec((TOP_K, tm), lambda i: (0, i))
    start = jnp.broadcast_to(pad_start.astype(F32)[:, None], (N_EXPERTS, tm))
    return pl.pallas_call(
        _dest_kernel,
        grid=(t // tm,),
        in_specs=[route, route, pl.BlockSpec((N_EXPERTS, tm), lambda i: (0, 0))],
        out_specs=route,
        out_shape=jax.ShapeDtypeStruct((TOP_K, t), I32),
        compiler_params=_cp("arbitrary"),
        name="moe_dest",
    )(eidx, pos, start)


def _moe_kernel(be_ref, nb_ref, x_ref, wg_ref, wu_ref, wd_ref, y_ref, wg_sc, wu_sc, wd_sc):
    j = pl.program_id(0)

    @pl.when(j < nb_ref[0])
    def _():
        @pl.when((j == 0) | (be_ref[j] != be_ref[jnp.maximum(j - 1, 0)]))
        def _():
            wg_sc[...] = wg_ref[0].astype(BF16)
            wu_sc[...] = wu_ref[0].astype(BF16)
            wd_sc[...] = wd_ref[0].astype(BF16)

        xp = x_ref[...]
        lo = pltpu.bitcast(xp << 16, F32)
        hi = pltpu.bitcast(xp & jnp.uint32(0xFFFF0000), F32)
        x = jnp.concatenate([lo, hi], axis=1).astype(BF16)
        a = _dot(x, wg_sc[...])
        b = _dot(x, wu_sc[...])
        y_ref[...] = _dot((_silu(a) * b).astype(BF16), wd_sc[...])


def _moe(xs, blk_e, nb_used, w_g, w_u, w_d):
    n_slot = xs.shape[0]
    n_blk = n_slot // MOE_BLK
    last = lambda j, nb: jnp.maximum(jnp.minimum(j, nb[0] - 1), 0)
    row = lambda j, be, nb: (last(j, nb), 0)
    wsel = lambda j, be, nb: (be[last(j, nb)], 0, 0)
    return pl.pallas_call(
        _moe_kernel,
        grid_spec=pltpu.PrefetchScalarGridSpec(
            num_scalar_prefetch=2,
            grid=(n_blk,),
            in_specs=[pl.BlockSpec((MOE_BLK, D_MODEL // 2), row),
                      pl.BlockSpec((1, D_MODEL, D_EXPERT), wsel),
                      pl.BlockSpec((1, D_MODEL, D_EXPERT), wsel),
                      pl.BlockSpec((1, D_EXPERT, D_MODEL), wsel)],
            out_specs=pl.BlockSpec((MOE_BLK, D_MODEL), row),
            scratch_shapes=[pltpu.VMEM((D_MODEL, D_EXPERT), BF16), pltpu.VMEM((D_MODEL, D_EXPERT), BF16),
                            pltpu.VMEM((D_EXPERT, D_MODEL), BF16)]),
        out_shape=jax.ShapeDtypeStruct((n_slot, D_MODEL), F32),
        compiler_params=_cp("arbitrary"),
        name="moe_experts",
    )(blk_e, nb_used, xs, w_g, w_u, w_d)


def _fin_kernel(x1_ref, h2_ref, yg_ref, w_ref, gf_ref, gpost_ref, wsg_ref, wsu_ref, wsd_ref, o_ref):
    w = w_ref[...]
    f = w[:, 0:1] * yg_ref[0]
    for k in range(1, TOP_K):
        f = f + w[:, k:k + 1] * yg_ref[k]
    hb = h2_ref[...]
    sh = (_silu(_dot(hb, wsg_ref[...])) * _dot(hb, wsu_ref[...])).astype(BF16)
    f = f + _dot(sh, wsd_ref[...])
    o_ref[...] = x1_ref[...] + gf_ref[0] * _rms(f, gpost_ref[...])


def _final(x1, h2, yg, wts, gf, g_post, wsg, wsu, wsd, tm, rows_per_mod):
    t = x1.shape[0]
    r = gf.shape[1]
    per = rows_per_mod // tm
    full = lambda a: pl.BlockSpec(a.shape, lambda i: (0,) * a.ndim)
    return pl.pallas_call(
        _fin_kernel,
        grid=(t // tm,),
        in_specs=[pl.BlockSpec((tm, D_MODEL), lambda i: (i, 0)),
                  pl.BlockSpec((tm, D_MODEL), lambda i: (i, 0)),
                  pl.BlockSpec((TOP_K, tm, D_MODEL), lambda i: (0, i, 0)),
                  pl.BlockSpec((tm, TOP_K), lambda i: (i, 0)),
                  pl.BlockSpec((1, r, D_MODEL), lambda i: (i // per, 0, 0)),
                  pl.BlockSpec((1, D_MODEL), lambda i: (0, 0)),
                  full(wsg), full(wsu), full(wsd)],
        out_specs=pl.BlockSpec((tm, D_MODEL), lambda i: (i, 0)),
        out_shape=jax.ShapeDtypeStruct((t, D_MODEL), F32),
        compiler_params=_cp("arbitrary"),
        name="moe_combine_final",
    )(x1, h2, yg, wts, gf, g_post.reshape(1, D_MODEL), wsg, wsu, wsd)


def _even_odd(a, axis):
    n = a.shape[axis]
    order = np.concatenate([np.arange(0, n, 2), np.arange(1, n, 2)])
    return jnp.take(a, order, axis=axis)


def _layer(xp, xs, c_prompt, c_sample, page_table, ck_c, cv_c, ck_s, cv_s, cw_k, cw_v, st_re, st_im,
           w_ada, b_ada, g_pre_mix, g_post_mix, g_pre_ffn, g_post_ffn, w_in, pe, wk1, wk2, wv1, wv2,
           rel_bias, lam_re, lam_im, log_dt, b_re, b_im, c_re, c_im, d_skip, w_glu, w_out,
           w_router, b_router, w_eg, w_eu, w_ed, w_sg, w_su, w_sd):
    batch, seq, _ = xp.shape
    n_dec = xs.shape[0]
    tp = batch * seq
    x_p = xp.reshape(tp, D_MODEL)
    x_s = xs.reshape(n_dec, D_MODEL)

    n_c = batch + n_dec
    n_pad = -(-n_c // 8) * 8
    c_all = jnp.pad(jnp.concatenate([c_prompt, c_sample], axis=0), ((0, n_pad - n_c), (0, 0)))
    mod = _adaln(c_all, w_ada, b_ada)
    mod_p = [m.reshape(batch, 1, D_MODEL) for m in jnp.split(mod[:batch], 6, axis=-1)]
    mod_s = [m.reshape(1, n_dec, D_MODEL) for m in jnp.split(mod[batch:n_c], 6, axis=-1)]

    wn, wt = _pre_weights(w_in)
    (kct_p, vct_p, kst_p, vst_p, kwt_p, vwt_p, kc_p, vc_p, u_p, ksb, kwb, qt_p, vst, vsn, vwt, gt_p) = \
        _pre_mix_prompt(x_p, mod_p[1], mod_p[0], g_pre_mix, wn, wt, 512, batch, seq)
    (kc_s, vc_s, ks_s, vs_s, kw_s, vw_s, u_s, qt_s, gt_s) = _pre_mix_sample(
        x_s, mod_s[1], mod_s[0], g_pre_mix, wn, wt)

    n_cmp_p = seq // CMP_BLOCK
    kcmp_p = _compress(kc_p.reshape(batch * n_cmp_p, CMP_BLOCK * KV_W), *_cmp_weights(pe, wk1, wk2))
    vcmp_p = _compress(vc_p.reshape(batch * n_cmp_p, CMP_BLOCK * KV_W), *_cmp_weights(pe, wv1, wv2))
    kcmp_p = _even_odd(kcmp_p.reshape(batch, n_cmp_p, N_KV, HEAD_DIM), 1).transpose(0, 2, 1, 3)
    vcmp_p = _even_odd(vcmp_p.reshape(batch, n_cmp_p, N_KV, HEAD_DIM), 1).transpose(0, 2, 3, 1)
    wtab, ctab = _bias_tables(rel_bias)
    o_p = _attn_prompt(qt_p, gt_p, kcmp_p.astype(BF16), vcmp_p.astype(BF16), ksb, vst, vsn, kwb, vwt,
                       wtab, ctab, batch, seq)

    n_pages = page_table.shape[1]
    n_cmp_s = n_pages * CMP_PER_PAGE
    pages = lambda cache: cache.transpose(0, 2, 3, 1)[page_table].reshape(
        n_dec * n_pages * N_KV * HEAD_DIM, PAGE)
    kcmp_s, vcmp_s = _compress_pages(pages(ck_c), pages(cv_c), _cmp_pages_weights(pe, wk1, wk2),
                                     _cmp_pages_weights(pe, wv1, wv2), n_dec, n_pages)
    by_head = lambda a: _even_odd(
        a.reshape(n_dec, n_pages, N_KV, CMP_PER_PAGE, HEAD_DIM).transpose(0, 2, 1, 3, 4).reshape(
            n_dec, N_KV, n_cmp_s, HEAD_DIM), 2)
    win_k = jnp.concatenate([cw_k[:, 1:], kw_s.reshape(n_dec, 1, N_KV, HEAD_DIM)], axis=1)
    win_v = jnp.concatenate([cw_v[:, 1:], vw_s.reshape(n_dec, 1, N_KV, HEAD_DIM)], axis=1)
    q_s = qt_s.T.reshape(n_dec, N_HEADS, HEAD_DIM)
    gates_s = gt_s[:, :3 * GQ].reshape(N_KV, 3, GQ, n_dec).transpose(3, 0, 2, 1).reshape(n_dec, N_HEADS, 3)
    o_s = _attn_sample(q_s, gates_s, by_head(kcmp_s).astype(BF16), by_head(vcmp_s).astype(BF16),
                       page_table, ck_s, cv_s, ks_s, vs_s, win_k, win_v, rel_bias)

    y5_p, s5r_p, s5i_p = _s5_prompt(
        u_p, _s5_prep(lam_re, lam_im, log_dt, b_re, b_im, c_re, c_im, d_skip, S5_CHUNK), batch, seq)
    y5_s, s5r_s, s5i_s = _s5_sample(
        u_s, _s5_prep(lam_re, lam_im, log_dt, b_re, b_im, c_re, c_im, d_skip, 1), st_re, st_im)

    wglu_b, wout_b = w_glu.astype(BF16), w_out.astype(BF16)
    cnt0 = jnp.zeros((N_EXPERTS, 128), F32)
    tm_q = 256
    x1_p, h2_p, h2p_p, e_p, wt_p, pos_p, cnt1 = _post_mix(
        x_p, o_p, y5_p, mod_p[2], mod_p[3], mod_p[4], g_post_mix, g_pre_ffn, wglu_b, wout_b,
        w_router, b_router, cnt0, tm_q, seq)
    x1_s, h2_s, h2p_s, e_s, wt_s, pos_s, cnt2 = _post_mix(
        x_s, o_s, y5_s, mod_s[2], mod_s[3], mod_s[4], g_post_mix, g_pre_ffn, wglu_b, wout_b,
        w_router, b_router, cnt1, n_dec, n_dec)

    t_all = tp + n_dec
    counts = cnt2[:, 0].astype(I32)
    padded = (counts + MOE_BLK - 1) // MOE_BLK * MOE_BLK
    pad_end = jnp.cumsum(padded)
    dest_p = _dest(e_p, pos_p, pad_end - padded, 1024)
    dest_s = _dest(e_s, pos_s, pad_end - padded, n_dec)
    dest = jnp.concatenate([dest_p, dest_s], axis=1)
    n_blk = -(-(t_all * TOP_K) // MOE_BLK) + N_EXPERTS
    n_slot = n_blk * MOE_BLK
    tok = jnp.broadcast_to(jnp.arange(t_all, dtype=I32)[None], (TOP_K, t_all))
    rows = (jnp.arange(n_slot, dtype=I32) % t_all).at[dest.reshape(-1)].set(
        tok.reshape(-1), unique_indices=True)
    xs_rows = jnp.concatenate([h2p_p, h2p_s], axis=0)[rows]
    blk_e = jnp.minimum(jnp.searchsorted(pad_end, jnp.arange(n_blk, dtype=I32) * MOE_BLK, side='right'),
                        N_EXPERTS - 1).astype(I32)
    nb_used = (pad_end[-1:] // MOE_BLK).astype(I32)
    ys = _moe(xs_rows, blk_e, nb_used, w_eg, w_eu, w_ed)
    yg_p = ys[dest_p]
    yg_s = ys[dest_s]

    wsg, wsu, wsd = w_sg.astype(BF16), w_su.astype(BF16), w_sd.astype(BF16)
    out_p = _final(x1_p, h2_p, yg_p, wt_p.T, mod_p[5], g_post_ffn, wsg, wsu, wsd, tm_q, seq)
    out_s = _final(x1_s, h2_s, yg_s, wt_s.T, mod_s[5], g_post_ffn, wsg, wsu, wsd, n_dec, n_dec)

    n_win = min(WINDOW, seq)
    rows_p = lambda a: a.transpose(0, 3, 1, 2)
    kv5 = lambda a: a.reshape(n_dec, 1, N_KV, HEAD_DIM)
    st_p = (rows_p(kct_p), rows_p(vct_p), rows_p(kst_p), rows_p(vst_p),
            rows_p(kwt_p)[:, seq - n_win:], rows_p(vwt_p)[:, seq - n_win:], s5r_p, s5i_p)
    st_s = (kv5(kc_s), kv5(vc_s), kv5(ks_s), kv5(vs_s), win_k, win_v, s5r_s, s5i_s)
    return out_p.reshape(batch, seq, D_MODEL), out_s.reshape(n_dec, 1, D_MODEL), st_p, st_s


def kernel(x_prompt, x_sample, c_prompt, c_sample, page_table, cache_cmp_k, cache_cmp_v, cache_sel_k, cache_sel_v, cache_win_k, cache_win_v, state_s5_re, state_s5_im, w_ada, b_ada, g_pre_mix, g_post_mix, g_pre_ffn, g_post_ffn, w_in, pe_cmp, w_cmp_k1, w_cmp_k2, w_cmp_v1, w_cmp_v2, rel_bias, lam_re, lam_im, log_dt, b_re, b_im, c_re, c_im, d_skip, w_glu, w_out, w_router, b_router, w_exp_gate, w_exp_up, w_exp_down, w_sh_gate, w_sh_up, w_sh_down):
    depth = w_in.shape[0]
    xp, xs = x_prompt, x_sample
    p_states, s_states = [], []
    for l in range(depth):
        xp, xs, st_p, st_s = _layer(
            xp, xs, c_prompt, c_sample, page_table, cache_cmp_k[l], cache_cmp_v[l], cache_sel_k[l],
            cache_sel_v[l], cache_win_k[l], cache_win_v[l], state_s5_re[l], state_s5_im[l],
            w_ada[l], b_ada[l], g_pre_mix[l], g_post_mix[l], g_pre_ffn[l], g_post_ffn[l], w_in[l],
            pe_cmp[l], w_cmp_k1[l], w_cmp_k2[l], w_cmp_v1[l], w_cmp_v2[l], rel_bias,
            lam_re[l], lam_im[l], log_dt[l], b_re[l], b_im[l], c_re[l], c_im[l], d_skip[l],
            w_glu[l], w_out[l], w_router[l], b_router[l], w_exp_gate[l], w_exp_up[l], w_exp_down[l],
            w_sh_gate[l], w_sh_up[l], w_sh_down[l])
        p_states.append(st_p)
        s_states.append(st_s)
    p_st = tuple(jnp.stack(a) for a in zip(*p_states))
    s_st = tuple(jnp.stack(a) for a in zip(*s_states))
    return (xp, xs) + p_st + s_st
```

```python
import functools
import math

import numpy as np
import jax
import jax.numpy as jnp
from jax import lax
from jax.experimental import pallas as pl
from jax.experimental.pallas import tpu as pltpu

F32 = jnp.float32
BF16 = jnp.bfloat16
I32 = jnp.int32
U32 = jnp.uint32

D_MODEL = 1024
N_HEADS = 8
HEAD_DIM = 64
N_KV = 2
GQ = N_HEADS // N_KV
Q_W = N_HEADS * HEAD_DIM
KV_W = N_KV * HEAD_DIM
S5_W = D_MODEL - Q_W
S5_H = 16
S5_G = S5_W // S5_H
S5_P = 64
GATE_OFF = Q_W + 6 * KV_W
U_OFF = GATE_OFF + 3 * N_HEADS
CMP_BLOCK = 32
SEL_BLOCK = 64
N_SEL = 16
WINDOW = 512
NUM_BUCKETS = 32
REL_MAX_DIST = 128
N_EXPERTS = 256
TOP_K = 8
N_EGROUPS = 8
TOPK_GROUPS = 4
EGROUP = N_EXPERTS // N_EGROUPS
D_EXPERT = 256
ROUTED_SCALE = 2.5
EPS = 1e-6
SCALE = HEAD_DIM ** -0.5
PAGE = 128
CMP_PER_PAGE = PAGE // CMP_BLOCK

QT = 128
VT = 2 * QT
N_DELTA = WINDOW // QT + 1
S5_CHUNK = 32
S5_ROWS = 8
MOE_BLK = 256
NEG = -2e30
M_INIT = -1e30
VMEM_LIMIT = 56 * 1024 * 1024
WT_ROWS = Q_W + 6 * KV_W + 32
HIGHEST = lax.Precision.HIGHEST


def _cp(*sem):
    return pltpu.CompilerParams(dimension_semantics=sem, vmem_limit_bytes=VMEM_LIMIT)


def _dot(a, b, precision=None):
    return jnp.dot(a, b, preferred_element_type=F32, precision=precision)


def _dot_nt(a, b):
    return lax.dot_general(a, b, (((1,), (1,)), ((), ())), preferred_element_type=F32)


def _sigmoid(x):
    return 1.0 / (1.0 + jnp.exp(-x))


def _silu(x):
    return x * _sigmoid(x)


def _rms(x, g):
    return x * lax.rsqrt(jnp.mean(x * x, axis=-1, keepdims=True) + EPS) * g


def _pack_bf16_pairs(x):
    c = x.shape[1] // 2
    bits = pltpu.bitcast(x.astype(BF16).astype(F32), U32)
    return (bits[:, :c] >> 16) | bits[:, c:]


def _unpack_bf16_pairs(xp):
    return pltpu.bitcast(xp << 16, F32), pltpu.bitcast(xp & jnp.uint32(0xFFFF0000), F32)


def _bucket_table(n):
    d = np.arange(n)
    exact = NUM_BUCKETS // 2
    nf = np.maximum(d, 1).astype(np.float32)
    large = exact + (np.log(nf / np.float32(exact)) / np.float32(math.log(REL_MAX_DIST / exact))
                     * np.float32(NUM_BUCKETS - exact)).astype(np.int32)
    return np.where(d < exact, d, np.minimum(large, NUM_BUCKETS - 1)).astype(np.int32)


def _ada_kernel(c_ref, w_ref, b_ref, o_ref):
    a = _silu(c_ref[...]).astype(BF16)
    o_ref[...] = _dot(a, w_ref[...].astype(BF16)) + b_ref[...]


def _adaln(c, w_ada, b_ada):
    n, d = c.shape
    n_out = w_ada.shape[1]
    tn = 1024
    return pl.pallas_call(
        _ada_kernel,
        grid=(n_out // tn,),
        in_specs=[pl.BlockSpec((n, d), lambda j: (0, 0)),
                  pl.BlockSpec((d, tn), lambda j: (0, j)),
                  pl.BlockSpec((1, tn), lambda j: (0, j))],
        out_specs=pl.BlockSpec((n, tn), lambda j: (0, j)),
        out_shape=jax.ShapeDtypeStruct((n, n_out), F32),
        compiler_params=_cp("arbitrary"),
        name="adaln",
    )(c, w_ada, b_ada.reshape(1, n_out))


def _pre_project(x_ref, sc_ref, sh_ref, g_ref, wn_ref, wt_ref):
    h = _rms(x_ref[...], g_ref[...]) * (1.0 + sc_ref[0]) + sh_ref[0]
    hb = h.astype(BF16)
    return _dot(hb, wn_ref[...]), _dot_nt(wt_ref[...], hb)


def _pre_prompt_kernel(x_ref, sc_ref, sh_ref, g_ref, wn_ref, wt_ref,
                       kct_ref, vct_ref, kst_ref, vst32_ref, kwt_ref, vwt32_ref, kc_ref, vc_ref, u_ref,
                       ksb_ref, kwb_ref, qt_ref, vst_ref, vsn_ref, vwt_ref, gt_ref):
    tm = x_ref.shape[0]
    zn, zt = _pre_project(x_ref, sc_ref, sh_ref, g_ref, wn_ref, wt_ref)
    trans = lambda j, g: zt[Q_W + j * KV_W + g * HEAD_DIM:Q_W + j * KV_W + (g + 1) * HEAD_DIM]
    for j, ref in enumerate((kct_ref, vct_ref, kst_ref, vst32_ref, kwt_ref, vwt32_ref)):
        for g in range(N_KV):
            ref[0, g] = trans(j, g)
    kc_ref[...] = zn[:, 0:KV_W]
    vc_ref[...] = zn[:, KV_W:2 * KV_W]
    u_ref[...] = zn[:, 6 * KV_W:]
    for g in range(N_KV):
        ksb_ref[g] = zn[:, 2 * KV_W + g * HEAD_DIM:2 * KV_W + (g + 1) * HEAD_DIM].astype(BF16)
        kwb_ref[g] = zn[:, 4 * KV_W + g * HEAD_DIM:4 * KV_W + (g + 1) * HEAD_DIM].astype(BF16)
        vs_t = trans(3, g).astype(BF16)
        vw_t = trans(5, g).astype(BF16)
        for c in range(tm // VT):
            vst_ref[g, c] = vs_t[:, c * VT:(c + 1) * VT]
        for c in range(tm // QT):
            vsn_ref[g, c] = vs_t[:, c * QT:(c + 1) * QT]
            vwt_ref[g, c] = vw_t[:, c * QT:(c + 1) * QT]
        r2 = Q_W + 6 * KV_W + g * 16
        gt_ref[g] = _sigmoid(zt[r2:r2 + 16])
    qt_ref[...] = zt[:Q_W].astype(BF16)


def _pre_sample_kernel(x_ref, sc_ref, sh_ref, g_ref, wn_ref, wt_ref,
                       kc_ref, vc_ref, ks_ref, vs_ref, kw_ref, vw_ref, u_ref, qt_ref, gt_ref):
    zn, zt = _pre_project(x_ref, sc_ref, sh_ref, g_ref, wn_ref, wt_ref)
    for j, ref in enumerate((kc_ref, vc_ref, ks_ref, vs_ref, kw_ref, vw_ref)):
        ref[...] = zn[:, j * KV_W:(j + 1) * KV_W]
    u_ref[...] = zn[:, 6 * KV_W:]
    qt_ref[...] = zt[:Q_W].astype(BF16)
    for g in range(N_KV):
        r2 = Q_W + 6 * KV_W + g * 16
        gt_ref[g] = _sigmoid(zt[r2:r2 + 16])


def _pre_weights(w_in):
    wn = jnp.concatenate([w_in[:, Q_W:GATE_OFF], w_in[:, U_OFF:]], axis=1).astype(BF16)
    gate_cols = []
    for g in range(N_KV):
        for j in range(3):
            for hh in range(GQ):
                gate_cols.append(GATE_OFF + (g * GQ + hh) * 3 + j)
        gate_cols.extend([GATE_OFF] * 4)
    wt = jnp.concatenate([
        w_in[:, :Q_W] * SCALE,
        w_in[:, Q_W:GATE_OFF],
        w_in[:, np.array(gate_cols)],
    ], axis=1).T.astype(BF16)
    return wn, wt


def _pre_in_specs(tm, r, per, wn, wt):
    return [pl.BlockSpec((tm, D_MODEL), lambda i: (i, 0)),
            pl.BlockSpec((1, r, D_MODEL), lambda i: (i // per, 0, 0)),
            pl.BlockSpec((1, r, D_MODEL), lambda i: (i // per, 0, 0)),
            pl.BlockSpec((1, D_MODEL), lambda i: (0, 0)),
            pl.BlockSpec(wn.shape, lambda i: (0, 0)),
            pl.BlockSpec(wt.shape, lambda i: (0, 0))]


def _pre_mix_prompt(x, sc, sh, g_pre, wn, wt, tm, batch, seq):
    t = x.shape[0]
    per = seq // tm
    f = lambda shape: jax.ShapeDtypeStruct(shape, F32)
    b = lambda shape: jax.ShapeDtypeStruct(shape, BF16)
    tr_spec = pl.BlockSpec((1, N_KV, HEAD_DIM, tm), lambda i: (i // per, 0, 0, i % per))
    kv_spec = pl.BlockSpec((tm, KV_W), lambda i: (i, 0))
    return pl.pallas_call(
        _pre_prompt_kernel,
        grid=(t // tm,),
        in_specs=_pre_in_specs(tm, 1, per, wn, wt),
        out_specs=[tr_spec] * 6 + [
            kv_spec, kv_spec,
            pl.BlockSpec((tm, S5_W), lambda i: (i, 0)),
            pl.BlockSpec((N_KV, tm, HEAD_DIM), lambda i: (0, i, 0)),
            pl.BlockSpec((N_KV, tm, HEAD_DIM), lambda i: (0, i, 0)),
            pl.BlockSpec((Q_W, tm), lambda i: (0, i)),
            pl.BlockSpec((N_KV, tm // VT, HEAD_DIM, VT), lambda i: (0, i, 0, 0)),
            pl.BlockSpec((N_KV, tm // QT, HEAD_DIM, QT), lambda i: (0, i, 0, 0)),
            pl.BlockSpec((N_KV, tm // QT, HEAD_DIM, QT), lambda i: (0, i, 0, 0)),
            pl.BlockSpec((N_KV, 16, tm), lambda i: (0, 0, i))],
        out_shape=[f((batch, N_KV, HEAD_DIM, seq))] * 6 + [
            f((t, KV_W)), f((t, KV_W)),
            f((t, S5_W)), b((N_KV, t, HEAD_DIM)), b((N_KV, t, HEAD_DIM)), b((Q_W, t)),
            b((N_KV, t // VT, HEAD_DIM, VT)), b((N_KV, t // QT, HEAD_DIM, QT)),
            b((N_KV, t // QT, HEAD_DIM, QT)), f((N_KV, 16, t))],
        compiler_params=_cp("arbitrary"),
        name="pre_mix_prompt",
    )(x, sc, sh, g_pre.reshape(1, D_MODEL), wn, wt)


def _pre_mix_sample(x, sc, sh, g_pre, wn, wt):
    t = x.shape[0]
    f = lambda shape: jax.ShapeDtypeStruct(shape, F32)
    kv_spec = pl.BlockSpec((t, KV_W), lambda i: (i, 0))
    return pl.pallas_call(
        _pre_sample_kernel,
        grid=(1,),
        in_specs=_pre_in_specs(t, t, 1, wn, wt),
        out_specs=[kv_spec] * 6 + [pl.BlockSpec((t, S5_W), lambda i: (i, 0)),
                                   pl.BlockSpec((Q_W, t), lambda i: (0, i)),
                                   pl.BlockSpec((N_KV, 16, t), lambda i: (0, 0, i))],
        out_shape=[f((t, KV_W))] * 6 + [f((t, S5_W)), jax.ShapeDtypeStruct((Q_W, t), BF16),
                                        f((N_KV, 16, t))],
        compiler_params=_cp("arbitrary"),
        name="pre_mix_sample",
    )(x, sc, sh, g_pre.reshape(1, D_MODEL), wn, wt)


def _cmp_kernel(x_ref, pe_ref, w1_ref, w2_ref, o_ref):
    xb = (x_ref[...] + pe_ref[...]).astype(BF16)
    hid = _silu(_dot(xb, w1_ref[...]))
    o_ref[...] = _dot(hid.astype(BF16), w2_ref[...])


def _cmp_weights(pe, w1, w2):
    eye = jnp.eye(N_KV, dtype=F32)
    w1b = jnp.einsum('jde,gh->jgdhe', w1, eye).reshape(CMP_BLOCK * KV_W, KV_W).astype(BF16)
    w2b = jnp.einsum('ef,gh->gehf', w2, eye).reshape(KV_W, KV_W).astype(BF16)
    peb = jnp.broadcast_to(pe[:, None, :], (CMP_BLOCK, N_KV, HEAD_DIM)).reshape(1, CMP_BLOCK * KV_W)
    return peb, w1b, w2b


def _compress(x, peb, w1b, w2b):
    r, k = x.shape
    tr = min(256, r)
    return pl.pallas_call(
        _cmp_kernel,
        grid=(r // tr,),
        in_specs=[pl.BlockSpec((tr, k), lambda i: (i, 0)),
                  pl.BlockSpec((1, k), lambda i: (0, 0)),
                  pl.BlockSpec((k, KV_W), lambda i: (0, 0)),
                  pl.BlockSpec((KV_W, KV_W), lambda i: (0, 0))],
        out_specs=pl.BlockSpec((tr, KV_W), lambda i: (i, 0)),
        out_shape=jax.ShapeDtypeStruct((r, KV_W), F32),
        compiler_params=_cp("arbitrary"),
        name="compress",
    )(x, peb, w1b, w2b)


def _cmp_pages_kernel(xk_ref, xv_ref, w1k_ref, w1v_ref, bk_ref, bv_ref, w2k_ref, w2v_ref,
                      ok_ref, ov_ref):
    n_rows = ok_ref.shape[0]
    for x_ref, w1_ref, b_ref, w2_ref, o_ref in ((xk_ref, w1k_ref, bk_ref, w2k_ref, ok_ref),
                                               (xv_ref, w1v_ref, bv_ref, w2v_ref, ov_ref)):
        acc = jnp.zeros((n_rows, CMP_PER_PAGE * HEAD_DIM), F32)
        for dd in range(HEAD_DIM // 2):
            r0 = x_ref[pl.ds(2 * dd, n_rows, stride=HEAD_DIM), :]
            r1 = x_ref[pl.ds(2 * dd + 1, n_rows, stride=HEAD_DIM), :]
            acc = acc + _dot(jnp.concatenate([r0, r1], axis=1).astype(BF16), w1_ref[dd])
        hid = _silu(acc + b_ref[...])
        o_ref[...] = _dot(hid.astype(BF16), w2_ref[...])


def _cmp_pages_weights(pe, w1, w2):
    eye = jnp.eye(CMP_PER_PAGE, dtype=F32)
    w1t = jnp.einsum('jde,mn->dmjne', w1, eye).reshape(HEAD_DIM // 2, 2 * PAGE, CMP_PER_PAGE * HEAD_DIM)
    bias = jnp.einsum('jd,jde->e', pe, w1, precision=HIGHEST)
    bias = jnp.tile(bias, CMP_PER_PAGE).reshape(1, CMP_PER_PAGE * HEAD_DIM)
    w2t = jnp.einsum('ef,mn->menf', w2, eye).reshape(CMP_PER_PAGE * HEAD_DIM, CMP_PER_PAGE * HEAD_DIM)
    return w1t.astype(BF16), bias, w2t.astype(BF16)


def _compress_pages(xk, xv, wk, wv, n_seq, n_pages):
    rows_in = n_pages * N_KV * HEAD_DIM
    rows_out = n_pages * N_KV
    wcols = CMP_PER_PAGE * HEAD_DIM
    full = lambda a: pl.BlockSpec(a.shape, lambda i: (0,) * a.ndim)
    x_spec = pl.BlockSpec((rows_in, PAGE), lambda i: (i, 0))
    o_spec = pl.BlockSpec((rows_out, wcols), lambda i: (i, 0))
    o_shape = jax.ShapeDtypeStruct((n_seq * rows_out, wcols), F32)
    return pl.pallas_call(
        _cmp_pages_kernel,
        grid=(n_seq,),
        in_specs=[x_spec, x_spec, full(wk[0]), full(wv[0]), full(wk[1]), full(wv[1]),
                  full(wk[2]), full(wv[2])],
        out_specs=[o_spec, o_spec],
        out_shape=[o_shape, o_shape],
        compiler_params=_cp("arbitrary"),
        name="compress_pages",
    )(xk, xv, wk[0], wv[0], wk[1], wv[1], wk[2], wv[2])


def _attn_kernel(q_ref, g_ref, kc_ref, vct_ref, ks_ref, vst_ref, vsn_ref, kw_ref, vwt_ref,
                 ctab_ref, wtab_ref, o_ref, mb_ref, mbf_ref, s_sc, p_sc, *, n_cmp):
    i = pl.program_id(2)
    lanes = GQ * QT
    qt = q_ref[...]
    q4 = jnp.concatenate([qt[hh * HEAD_DIM:(hh + 1) * HEAD_DIM] for hh in range(GQ)], axis=1)
    gates = g_ref[0]
    grow = lambda j: jnp.concatenate([gates[j * GQ + hh:j * GQ + hh + 1] for hh in range(GQ)], axis=1)

    def online(carry, s, v_t):
        m, l, acc = carry
        m_new = jnp.maximum(m, jnp.max(s, axis=0, keepdims=True))
        alpha = jnp.exp(m - m_new)
        p = jnp.exp(s - m_new)
        l = alpha * l + jnp.sum(p, axis=0, keepdims=True)
        return m_new, l, alpha * acc + _dot(v_t, p.astype(BF16))

    init = (jnp.full((1, lanes), M_INIT, F32), jnp.zeros((1, lanes), F32),
            jnp.zeros((HEAD_DIM, lanes), F32))
    finish = lambda carry: carry[2] * (1.0 / jnp.maximum(carry[1], 1e-30))
    block_rows = lambda ref, first, n, extra=0.0: jnp.concatenate(
        [jnp.broadcast_to(ref[pl.ds(first + c, 1), :] + extra, (SEL_BLOCK, lanes)) for c in range(n)],
        axis=0)

    def near_tiles(carry, k_ref, vt_ref, deltas, live, masked):
        ks, vs, bs = [], [], []
        for dl in deltas:
            kt = i - dl
            ktc = jnp.maximum(kt, 0)
            ks.append(k_ref[0, pl.ds(pl.multiple_of(ktc * QT, QT), QT), :])
            vs.append(vt_ref[0, ktc])
            dead = jnp.where(live(dl, kt), 0.0, NEG)
            if masked:
                bs.append(wtab_ref[0, dl] + block_rows(mb_ref, (QT // SEL_BLOCK) * ktc, QT // SEL_BLOCK, dead))
            else:
                bs.append(wtab_ref[0, dl] + dead)
        s = _dot(jnp.concatenate(ks, axis=0), q4) + jnp.concatenate(bs, axis=0)
        return online(carry, s, jnp.concatenate(vs, axis=1))

    o_w = finish(near_tiles(init, kw_ref, vwt_ref, list(range(N_DELTA - 1, -1, -1)),
                            lambda dl, kt: kt >= 0, False))

    half = n_cmp // 2
    sc = _dot(kc_ref[0, 0], q4)
    rho = lax.broadcasted_iota(I32, (n_cmp, 16), 0)
    col = lax.broadcasted_iota(I32, (n_cmp, 16), 1)
    blk_n = 2 * jnp.where(rho >= half, rho - half, rho) + jnp.where(rho >= half, 1, 0)
    rel = blk_n - (4 * i - 4)
    want = jnp.where(rel < 0, 8, jnp.where(rel > 7, 9, rel))
    place = jnp.where(col == want, 1.0, 0.0).astype(BF16)
    sc = sc + _dot(place, ctab_ref[0, 0]) + _dot(place, ctab_ref[0, 1])
    mc = jnp.maximum(jnp.max(sc, axis=0, keepdims=True), M_INIT)
    ec = jnp.exp(sc - mc)
    pc = ec * (1.0 / jnp.maximum(jnp.sum(ec, axis=0, keepdims=True), 1e-30))
    o_cw = grow(0) * _dot(vct_ref[0, 0], pc.astype(BF16)) + grow(2) * o_w

    ps = pc[:half] + pc[half:]
    imp = ps[:, 0:QT]
    for hh in range(1, GQ):
        imp = imp + ps[:, hh * QT:(hh + 1) * QT]
    n_blk = half
    blk = lax.broadcasted_iota(I32, (n_blk, QT), 0)
    tq = i * QT + lax.broadcasted_iota(I32, (n_blk, QT), 1)
    cur = lax.shift_right_logical(tq, 6)
    forced = (blk == 0) | (blk == cur) | (blk == cur - 1)
    score = jnp.where(blk * SEL_BLOCK <= tq, jnp.where(forced, 1e4, imp), -jnp.inf)
    blkf = blk.astype(F32)

    def pick(_, carry):
        work, mb = carry
        mx = jnp.max(work, axis=0, keepdims=True)
        idx = jnp.min(jnp.where(work == mx, blkf, 1e9), axis=0, keepdims=True)
        hit = blkf == idx
        return jnp.where(hit, -jnp.inf, work), jnp.where(hit, 0.0, mb)

    _, mb = lax.fori_loop(0, min(N_SEL, n_blk), pick, (score, jnp.full((n_blk, QT), NEG, F32)))
    for hh in range(GQ):
        mb_ref[:, hh * QT:(hh + 1) * QT] = mb
    mbf_ref[...] = mb_ref[...] + wtab_ref[0, 2, 0:1, :]

    n_far = jnp.maximum(i - 1, 0) // 2
    last_pair = ks_ref.shape[1] // VT - 1
    s_sc[...] = jnp.full(s_sc.shape, NEG, F32)
    p_sc[...] = jnp.zeros(p_sc.shape, BF16)

    def far_step(t, carry):
        m, l, acc, alpha_prev = carry
        pv = _dot(vst_ref[0, jnp.clip(t - 2, 0, last_pair)], p_sc[...])
        s = s_sc[...]
        m_new = jnp.maximum(m, jnp.max(s, axis=0, keepdims=True))
        alpha = jnp.exp(m - m_new)
        p = jnp.exp(s - m_new)
        l = alpha * l + jnp.sum(p, axis=0, keepdims=True)
        ta = jnp.minimum(t, last_pair)
        k_t = ks_ref[0, pl.ds(pl.multiple_of(ta * VT, VT), VT), :]
        s_next = _dot(k_t, q4) + block_rows(mbf_ref, (VT // SEL_BLOCK) * ta, VT // SEL_BLOCK,
                                            jnp.where(t < n_far, 0.0, NEG))
        p_sc[...] = p.astype(BF16)
        s_sc[...] = s_next
        return m_new, l, alpha_prev * acc + pv, alpha

    m, l, acc, _ = lax.fori_loop(0, n_far + 2, far_step, init + (jnp.ones((1, lanes), F32),))

    sel_live = lambda dl, kt: (kt >= 0) & ((dl < 2) | (i % 2 == 0))
    o_s = finish(near_tiles((m, l, acc), ks_ref, vsn_ref, [2, 1, 0], sel_live, True))
    o_t = o_cw + grow(1) * o_s
    o_hd = jnp.concatenate([o_t[:, hh * QT:(hh + 1) * QT] for hh in range(GQ)], axis=0)
    o_ref[...] = o_hd.T.astype(BF16)


def _bias_tables(rel_bias):
    span = N_DELTA * QT
    bt = _bucket_table(WINDOW)
    period = span + QT
    by_dist = jnp.concatenate([rel_bias[bt], jnp.full((period - WINDOW, N_HEADS), NEG, F32)], axis=0)
    toep = jnp.tile(by_dist.T, (1, QT))[:, :QT * (period - 1)].reshape(N_HEADS, QT, period - 1)
    tab = toep[:, :, :span].reshape(N_KV, GQ, QT, N_DELTA, QT)
    wtab = tab.transpose(0, 3, 2, 1, 4).reshape(N_KV, N_DELTA, QT, GQ * QT)
    wtab = jnp.concatenate([wtab, jnp.full((N_KV, 1, QT, GQ * QT), NEG, F32)], axis=1)
    ql = np.arange(QT)[None, :]
    r = np.arange(8)[:, None]
    d = ql + (4 * CMP_BLOCK - CMP_BLOCK + 1) - CMP_BLOCK * r
    near = jnp.where((d >= 0)[..., None], rel_bias[bt[np.clip(d, 0, len(bt) - 1)]], NEG)
    far = jnp.broadcast_to(rel_bias[NUM_BUCKETS - 1][None, None, :], (1, QT, N_HEADS))
    rows = jnp.concatenate([near, far, jnp.full((1, QT, N_HEADS), NEG, F32),
                            jnp.zeros((6, QT, N_HEADS), F32)], axis=0)
    ctab = rows.reshape(16, QT, N_KV, GQ).transpose(2, 0, 3, 1).reshape(N_KV, 16, GQ * QT)
    hi = ctab.astype(BF16)
    lo = (ctab - hi.astype(F32)).astype(BF16)
    return wtab.astype(F32), jnp.stack([hi, lo], axis=1)


def _attn_prompt(qt, gt, kcmp, vcmpt, ksb, vst, vsn, kwb, vwt, wtab, ctab, batch, seq):
    nq = seq // QT
    n_cmp = kcmp.shape[2]
    lanes = GQ * QT
    t = batch * seq
    return pl.pallas_call(
        functools.partial(_attn_kernel, n_cmp=n_cmp),
        grid=(batch, N_KV, nq),
        in_specs=[
            pl.BlockSpec((GQ * HEAD_DIM, QT), lambda b, g, i: (g, b * nq + i)),
            pl.BlockSpec((1, 16, QT), lambda b, g, i: (g, 0, b * nq + i)),
            pl.BlockSpec((1, 1, n_cmp, HEAD_DIM), lambda b, g, i: (b, g, 0, 0)),
            pl.BlockSpec((1, 1, HEAD_DIM, n_cmp), lambda b, g, i: (b, g, 0, 0)),
            pl.BlockSpec((1, seq, HEAD_DIM), lambda b, g, i: (g, b, 0)),
            pl.BlockSpec((1, seq // VT, HEAD_DIM, VT), lambda b, g, i: (g, b, 0, 0)),
            pl.BlockSpec((1, nq, HEAD_DIM, QT), lambda b, g, i: (g, b, 0, 0)),
            pl.BlockSpec((1, seq, HEAD_DIM), lambda b, g, i: (g, b, 0)),
            pl.BlockSpec((1, nq, HEAD_DIM, QT), lambda b, g, i: (g, b, 0, 0)),
            pl.BlockSpec((1, 2, 16, lanes), lambda b, g, i: (g, 0, 0, 0)),
            pl.BlockSpec((1, N_DELTA + 1, QT, lanes), lambda b, g, i: (g, 0, 0, 0)),
        ],
        out_specs=pl.BlockSpec((QT, GQ * HEAD_DIM), lambda b, g, i: (b * nq + i, g)),
        out_shape=jax.ShapeDtypeStruct((t, Q_W), BF16),
        scratch_shapes=[pltpu.VMEM((n_cmp // 2, lanes), F32), pltpu.VMEM((n_cmp // 2, lanes), F32),
                        pltpu.VMEM((VT, lanes), F32), pltpu.VMEM((VT, lanes), BF16)],
        compiler_params=_cp("arbitrary", "arbitrary", "arbitrary"),
        name="nsa_prompt",
    )(qt, gt, kcmp, vcmpt, ksb, vst, vsn, kwb, vwt, ctab, wtab)


def _s5_kernel(u_ref, mt_ref, sbr_ref, sbi_ref, ccr_ref, cci_ref, lb_ref, h0r_ref, h0i_ref,
               y_ref, hr_ref, hi_ref, sr_sc, si_sc, pr_sc, pi_sc, *, n_chunks, rb):
    u = u_ref[0]
    sr_sc[...] = _dot(u, sbr_ref[0])
    si_sc[...] = _dot(u, sbi_ref[0])
    lr = lb_ref[0, 0:1, :]
    li = lb_ref[0, 1:2, :]

    def step(c, carry):
        hr, hi = carry
        rows = pl.ds(pl.multiple_of(c * rb, rb), rb)
        pr_sc[rows, :] = hr
        pi_sc[rows, :] = hi
        return (lr * hr - li * hi + sr_sc[rows, :], lr * hi + li * hr + si_sc[rows, :])

    hr, hi = lax.fori_loop(0, n_chunks, step, (h0r_ref[0], h0i_ref[0]))
    hr_ref[0] = hr
    hi_ref[0] = hi
    y_ref[0] = (_dot(u, mt_ref[0]) + _dot(pr_sc[...].astype(BF16), ccr_ref[0])
                + _dot(pi_sc[...].astype(BF16), cci_ref[0]))


def _s5_prep(lam_re, lam_im, log_dt, b_re, b_im, c_re, c_im, d_skip, chunk):
    lam = lax.complex(lam_re, lam_im)
    z = lam * jnp.exp(log_dt)[:, None]
    lbar = jnp.exp(z)
    bbar = ((lbar - 1.0) / lam)[:, :, None] * lax.complex(b_re, b_im)
    c = lax.complex(c_re, c_im)
    pw = jnp.exp(z[None] * jnp.arange(chunk + 1, dtype=F32)[:, None, None])
    kern = jnp.einsum('gap,jgp,gpb->jgab', c, pw[:chunk], bbar, precision=HIGHEST).real
    kern = kern.at[0].add(jax.vmap(jnp.diag)(d_skip))
    lag = np.arange(chunk)[None, :] - np.arange(chunk)[:, None]
    kt = kern.transpose(1, 0, 3, 2)
    m5 = jnp.where((lag >= 0)[None, :, :, None, None], kt[:, np.clip(lag, 0, None)], 0.0)
    mt = m5.transpose(0, 1, 3, 2, 4).reshape(S5_G, chunk * S5_H, chunk * S5_H)
    sb = jnp.einsum('lgp,gph->glhp', pw[chunk - 1 - np.arange(chunk)], bbar).reshape(
        S5_G, chunk * S5_H, S5_P)
    cc = jnp.einsum('ghp,lgp->gplh', c, pw[1:]).reshape(S5_G, S5_P, chunk * S5_H)
    lb = jnp.stack([pw[chunk].real, pw[chunk].imag], axis=1)
    return (mt.astype(BF16), sb.real.astype(BF16), sb.imag.astype(BF16),
            cc.real.astype(BF16), (-cc.imag).astype(BF16), lb.astype(F32))


def _s5_call(u, prep, h0r, h0i, n_chunks, rb):
    mt, sbr, sbi, ccr, cci, lb = prep
    g, rows, w = u.shape
    per_g = lambda *shape: pl.BlockSpec((1,) + shape, lambda i: (i,) + (0,) * len(shape))
    return pl.pallas_call(
        functools.partial(_s5_kernel, n_chunks=n_chunks, rb=rb),
        grid=(g,),
        in_specs=[per_g(rows, w), per_g(w, w), per_g(w, S5_P), per_g(w, S5_P),
                  per_g(S5_P, w), per_g(S5_P, w), per_g(2, S5_P), per_g(rb, S5_P), per_g(rb, S5_P)],
        out_specs=[per_g(rows, w), per_g(rb, S5_P), per_g(rb, S5_P)],
        out_shape=[jax.ShapeDtypeStruct((g, rows, w), F32),
                   jax.ShapeDtypeStruct((g, rb, S5_P), F32),
                   jax.ShapeDtypeStruct((g, rb, S5_P), F32)],
        scratch_shapes=[pltpu.VMEM((rows, S5_P), F32)] * 4,
        compiler_params=_cp("arbitrary"),
        name="s5_scan",
    )(u, mt, sbr, sbi, ccr, cci, lb, h0r, h0i)


def _s5_prompt(u, prep, batch, seq):
    nc = seq // S5_CHUNK
    ub = u.reshape(batch, nc, S5_CHUNK, S5_G, S5_H).transpose(3, 1, 0, 2, 4)
    ub = jnp.pad(ub, ((0, 0), (0, 0), (0, S5_ROWS - batch), (0, 0), (0, 0)))
    ub = ub.reshape(S5_G, nc * S5_ROWS, S5_CHUNK * S5_H).astype(BF16)
    zero = jnp.zeros((S5_G, S5_ROWS, S5_P), F32)
    y, hr, hi = _s5_call(ub, prep, zero, zero, nc, S5_ROWS)
    y = y.reshape(S5_G, nc, S5_ROWS, S5_CHUNK, S5_H)[:, :, :batch].transpose(2, 1, 3, 0, 4)
    return (y.reshape(batch * seq, S5_W),
            hr[:, :batch].transpose(1, 0, 2), hi[:, :batch].transpose(1, 0, 2))


def _s5_sample(u, prep, h0r, h0i):
    n = u.shape[0]
    ub = u.reshape(n, S5_G, S5_H).transpose(1, 0, 2).astype(BF16)
    y, hr, hi = _s5_call(ub, prep, h0r.transpose(1, 0, 2), h0i.transpose(1, 0, 2), 1, n)
    return y.transpose(1, 0, 2).reshape(n, S5_W), hr.transpose(1, 0, 2), hi.transpose(1, 0, 2)


def _softmax_lanes(s):
    m = jnp.maximum(jnp.max(s, axis=-1, keepdims=True), M_INIT)
    e = jnp.exp(s - m)
    return e * (1.0 / jnp.maximum(jnp.sum(e, axis=-1, keepdims=True), 1e-30))


def _bqk(q, k):
    return jnp.einsum('bhd,bnd->bhn', q, k, preferred_element_type=F32)


def _bpv(p, v):
    return jnp.einsum('bhn,bnd->bhd', p.astype(BF16), v, preferred_element_type=F32)


def _sattn1_kernel(q_ref, kc_ref, vc_ref, bias_ref, oc_ref, pick_ref, *, n_pick):
    q = q_ref[...]
    p = _softmax_lanes(_bqk(q, kc_ref[...]) + bias_ref[...])
    oc_ref[...] = _bpv(p, vc_ref[...])
    n_cmp = p.shape[-1]
    half = n_cmp // 2
    ps = p[:, 0]
    for hh in range(1, GQ):
        ps = ps + p[:, hh]
    imp = ps[:, :half] + ps[:, half:]
    lane = lax.broadcasted_iota(I32, imp.shape, 1).astype(F32)
    work = jnp.where((lane > 0) & (lane < half - 1), imp, -jnp.inf)
    picks = jnp.zeros(imp.shape, F32)
    for t in range(n_pick):
        mx = jnp.max(work, axis=-1, keepdims=True)
        idx = jnp.min(jnp.where(work == mx, lane, 1e9), axis=-1, keepdims=True)
        work = jnp.where(lane == idx, -jnp.inf, work)
        picks = jnp.where(lane == t, idx, picks)
    pick_ref[...] = picks.astype(I32)


def _bqkt(q, kt):
    return jnp.einsum('bhd,bdn->bhn', q, kt.astype(BF16), preferred_element_type=F32)


def _bpvt(p, vt):
    return jnp.einsum('bhn,bdn->bhd', p.astype(BF16), vt.astype(BF16), preferred_element_type=F32)


def _sattn2_kernel(q_ref, ks_ref, vs_ref, base_ref, corr_ref, flag_ref, half_ref, ex_ref, lh_ref,
                   kw_ref, vw_ref, wb_ref, oc_ref, g_ref, o_ref):
    q = q_ref[...]
    ex = ex_ref[...]
    near = _dot(flag_ref[...], ex)
    live = _dot(half_ref[...], ex) == lh_ref[...]
    bias = jnp.where(live[:, None, :], base_ref[...] + near[:, None, :] * corr_ref[...], NEG)
    p_s = _softmax_lanes(_bqkt(q, ks_ref[...]) + bias)
    o_s = _bpvt(p_s, vs_ref[...])
    p_w = _softmax_lanes(_bqkt(q, kw_ref[...]) + wb_ref[...])
    o_w = _bpvt(p_w, vw_ref[...])
    g = g_ref[...]
    o_ref[...] = g[:, :, 0:1] * oc_ref[...] + g[:, :, 1:2] * o_s + g[:, :, 2:3] * o_w


def _attn_sample(q, gates, kcmp, vcmp, page_table, cache_sk, cache_sv, ks_new, vs_new,
                 win_k, win_v, rel_bias):
    n = q.shape[0]
    n_cmp = kcmp.shape[2]
    past = page_table.shape[1] * PAGE
    n_blk = past // SEL_BLOCK
    n_pick = N_SEL - 3
    nbq = n * N_KV
    nb = 16
    bt = _bucket_table(past + 1)
    qg = q.reshape(nbq, GQ, HEAD_DIM)
    per_row = lambda tab: jnp.tile(tab.reshape(-1, N_KV, GQ).transpose(1, 2, 0), (nb // N_KV, 1, 1))
    blk3 = lambda *s: pl.BlockSpec((nb,) + s, lambda i: (i,) + (0,) * len(s))
    const3 = lambda *s: pl.BlockSpec((nb,) + s, lambda i: (0,) * (len(s) + 1))

    order = np.concatenate([np.arange(0, n_cmp, 2), np.arange(1, n_cmp, 2)])
    d_c = past - (order * CMP_BLOCK + CMP_BLOCK - 1)
    cb = per_row(jnp.where((d_c >= 0)[:, None], rel_bias[bt[np.clip(d_c, 0, None)]], NEG))

    o_c, picks = pl.pallas_call(
        functools.partial(_sattn1_kernel, n_pick=n_pick),
        grid=(nbq // nb,),
        in_specs=[blk3(GQ, HEAD_DIM), blk3(n_cmp, HEAD_DIM), blk3(n_cmp, HEAD_DIM), const3(GQ, n_cmp)],
        out_specs=[blk3(GQ, HEAD_DIM), pl.BlockSpec((nb, n_cmp // 2), lambda i: (i, 0))],
        out_shape=[jax.ShapeDtypeStruct((nbq, GQ, HEAD_DIM), F32),
                   jax.ShapeDtypeStruct((nbq, n_cmp // 2), I32)],
        compiler_params=_cp("arbitrary"),
        name="nsa_sample_cmp",
    )(qg, kcmp.reshape(nbq, n_cmp, HEAD_DIM), vcmp.reshape(nbq, n_cmp, HEAD_DIM), cb)

    picks = picks[:, :n_pick].reshape(n, N_KV, n_pick)
    forced = jnp.broadcast_to(jnp.array([n_blk - 1, 0], I32), (n, N_KV, 2))
    sel = jnp.concatenate([forced, picks], axis=-1)
    n_slot = sel.shape[-1]
    page = jnp.take_along_axis(page_table, (sel // 2).reshape(n, -1), axis=1).reshape(sel.shape)
    head = jnp.broadcast_to(jnp.arange(N_KV, dtype=I32)[None, :, None], sel.shape)
    starts = jnp.stack([page, head], axis=-1)
    dnums = lax.GatherDimensionNumbers(offset_dims=(3, 4), collapsed_slice_dims=(0, 1),
                                       start_index_map=(0, 1))
    n_key = (n_slot + 1) * PAGE

    def gather(cache, new):
        got = lax.gather(cache.transpose(0, 2, 3, 1), starts, dnums,
                         slice_sizes=(1, 1, HEAD_DIM, PAGE),
                         mode=lax.GatherScatterMode.PROMISE_IN_BOUNDS)
        got = got.transpose(0, 1, 3, 2, 4).reshape(n, N_KV, HEAD_DIM, n_slot * PAGE)
        tail = jnp.zeros((n, N_KV, HEAD_DIM, PAGE), F32).at[:, :, :, 0].set(
            new.reshape(n, N_KV, HEAD_DIM))
        return jnp.concatenate([got, tail], axis=3).reshape(nbq, HEAD_DIM, n_key)

    k_sel = gather(cache_sk, ks_new)
    v_sel = gather(cache_sv, vs_new)
    nb2 = 8
    per_row2 = lambda tab: jnp.tile(tab.reshape(-1, N_KV, GQ).transpose(1, 2, 0), (nb2 // N_KV, 1, 1))
    blk2 = lambda *s: pl.BlockSpec((nb2,) + s, lambda i: (i,) + (0,) * len(s))
    const2 = lambda *s: pl.BlockSpec((nb2,) + s, lambda i: (0,) * (len(s) + 1))
    r = np.arange(SEL_BLOCK)
    both = lambda d: np.tile(d, PAGE // SEL_BLOCK)
    d_base = np.concatenate([both(SEL_BLOCK - r), both(past - r)] + [np.full(PAGE, past)] * n_pick
                            + [-np.arange(PAGE)])
    base = per_row2(jnp.where((d_base >= 0)[:, None], rel_bias[bt[np.clip(d_base, 0, past)]], NEG))
    d_near = both(np.clip(2 * SEL_BLOCK - r, 0, past))
    delta = rel_bias[bt[d_near]] - rel_bias[bt[past]][None, :]
    in_pick = np.zeros((n_slot + 1, 1, 1), np.float32)
    in_pick[2:n_slot] = 1.0
    corr = per_row2((in_pick * delta[None]).reshape(n_key, N_HEADS))
    pad_slot = lambda a: jnp.pad(a.astype(F32).reshape(nbq, n_slot), ((0, 0), (0, 1)))
    flag = pad_slot(sel == n_blk - 2)
    half = pad_slot(sel % 2)
    expand = jnp.asarray(np.kron(np.eye(n_slot + 1, dtype=np.float32), np.ones((1, PAGE), np.float32)))
    lane_half = jnp.asarray(((np.arange(n_key) // SEL_BLOCK) % 2).astype(np.float32).reshape(1, n_key))

    n_win = win_k.shape[1]
    d_w = n_win - 1 - np.arange(n_win)
    wb = per_row2(jnp.where((d_w < WINDOW)[:, None], rel_bias[bt[d_w]], NEG))
    wk = win_k.transpose(0, 2, 3, 1).reshape(nbq, HEAD_DIM, n_win)
    wv = win_v.transpose(0, 2, 3, 1).reshape(nbq, HEAD_DIM, n_win)
    slot_spec = pl.BlockSpec((nb2, n_slot + 1), lambda i: (i, 0))
    o = pl.pallas_call(
        _sattn2_kernel,
        grid=(nbq // nb2,),
        in_specs=[blk2(GQ, HEAD_DIM), blk2(HEAD_DIM, n_key), blk2(HEAD_DIM, n_key),
                  const2(GQ, n_key), const2(GQ, n_key), slot_spec, slot_spec,
                  pl.BlockSpec((n_slot + 1, n_key), lambda i: (0, 0)),
                  pl.BlockSpec((1, n_key), lambda i: (0, 0)),
                  blk2(HEAD_DIM, n_win), blk2(HEAD_DIM, n_win), const2(GQ, n_win),
                  blk2(GQ, HEAD_DIM), blk2(GQ, 3)],
        out_specs=blk2(GQ, HEAD_DIM),
        out_shape=jax.ShapeDtypeStruct((nbq, GQ, HEAD_DIM), F32),
        compiler_params=_cp("arbitrary"),
        name="nsa_sample_sel_win",
    )(qg, k_sel, v_sel, base, corr, flag, half, expand, lane_half, wk, wv, wb, o_c,
      gates.reshape(nbq, GQ, 3))
    return o.reshape(n, Q_W).astype(BF16)


def _gelu_tanh(x):
    return 0.5 * x * (1.0 + jnp.tanh(math.sqrt(2.0 / math.pi) * (x + 0.044715 * (x * x * x))))


def _post_kernel(x_ref, o_ref, y5_ref, gm_ref, shf_ref, scf_ref, gpost_ref, gpre_ref,
                 wglu_ref, wout_ref, wr_ref, br_ref, tri_ref, cin_ref,
                 x1_ref, h2_ref, h2p_ref, eidx_ref, wts_ref, pos_ref, cout_ref, cnt_sc):
    i = pl.program_id(0)
    tm = x_ref.shape[0]

    @pl.when(i == 0)
    def _():
        cnt_sc[...] = cin_ref[...]

    g5 = _gelu_tanh(y5_ref[...])
    g5 = g5 * _sigmoid(_dot(g5.astype(BF16), wglu_ref[...]))
    m = _dot(o_ref[...], wout_ref[:Q_W]) + _dot(g5.astype(BF16), wout_ref[Q_W:])
    x1 = x_ref[...] + gm_ref[0] * _rms(m, gpost_ref[...])
    h2 = _rms(x1, gpre_ref[...]) * (1.0 + scf_ref[0]) + shf_ref[0]
    x1_ref[...] = x1
    h2_ref[...] = h2.astype(BF16)
    h2p_ref[...] = _pack_bf16_pairs(h2)

    s_t = _sigmoid(_dot(h2, wr_ref[...], precision=HIGHEST)).T
    s_sel = s_t + br_ref[...]
    eio = lax.broadcasted_iota(I32, (N_EXPERTS, tm), 0).astype(F32)
    first = lambda hit, ids: jnp.min(jnp.where(hit, ids, 1e9), axis=0, keepdims=True)
    gscore = []
    for g in range(N_EGROUPS):
        xg = s_sel[g * EGROUP:(g + 1) * EGROUP]
        ig = lax.broadcasted_iota(I32, (EGROUP, tm), 0).astype(F32) + float(g * EGROUP)
        m1 = jnp.max(xg, axis=0, keepdims=True)
        m2 = jnp.max(jnp.where(ig == first(xg == m1, ig), -jnp.inf, xg), axis=0, keepdims=True)
        gscore.append(m1 + m2)
    gsc = jnp.concatenate(gscore, axis=0)
    gio = lax.broadcasted_iota(I32, (N_EGROUPS, tm), 0).astype(F32)
    gmask = jnp.zeros((N_EGROUPS, tm), F32)
    for _ in range(TOPK_GROUPS):
        hit = gio == first(gsc == jnp.max(gsc, axis=0, keepdims=True), gio)
        gmask = jnp.where(hit, 1.0, gmask)
        gsc = jnp.where(hit, -jnp.inf, gsc)
    work = jnp.concatenate(
        [jnp.where(gmask[g:g + 1] > 0.0, s_sel[g * EGROUP:(g + 1) * EGROUP], -jnp.inf)
         for g in range(N_EGROUPS)], axis=0)
    chosen = jnp.zeros((N_EXPERTS, tm), F32)
    ids, wsel = [], []
    for _ in range(TOP_K):
        ik = first(work == jnp.max(work, axis=0, keepdims=True), eio)
        hit = eio == ik
        ids.append(ik)
        wsel.append(jnp.sum(jnp.where(hit, s_t, 0.0), axis=0, keepdims=True))
        work = jnp.where(hit, -jnp.inf, work)
        chosen = jnp.where(hit, 1.0, chosen)
    rank = cnt_sc[:, 0:1] + _dot(chosen.astype(BF16), tri_ref[...])
    pos = [jnp.sum(jnp.where(eio == ik, rank, 0.0), axis=0, keepdims=True) for ik in ids]
    cnt_sc[...] = cnt_sc[...] + jnp.sum(chosen, axis=1, keepdims=True)
    wsum = wsel[0]
    for k in range(1, TOP_K):
        wsum = wsum + wsel[k]
    eidx_ref[...] = jnp.concatenate(ids, axis=0).astype(I32)
    pos_ref[...] = jnp.concatenate(pos, axis=0).astype(I32)
    wts_ref[...] = jnp.concatenate([w / wsum * ROUTED_SCALE for w in wsel], axis=0)
    cout_ref[...] = cnt_sc[...]


def _post_mix(x, o_att, y5, gm, shf, scf, g_post, g_pre, wglu, wout, w_router, b_router,
              cnt_in, tm, rows_per_mod):
    t = x.shape[0]
    r = gm.shape[1]
    per = rows_per_mod // tm
    mod = pl.BlockSpec((1, r, D_MODEL), lambda i: (i // per, 0, 0))
    full = lambda a: pl.BlockSpec(a.shape, lambda i: (0,) * a.ndim)
    vec = pl.BlockSpec((1, D_MODEL), lambda i: (0, 0))
    tri = jnp.asarray(np.triu(np.ones((tm, tm), np.float32), 1), BF16)
    brb = jnp.broadcast_to(b_router[:, None], (N_EXPERTS, tm)).astype(F32)
    route = pl.BlockSpec((TOP_K, tm), lambda i: (0, i))
    return pl.pallas_call(
        _post_kernel,
        grid=(t // tm,),
        in_specs=[pl.BlockSpec((tm, D_MODEL), lambda i: (i, 0)),
                  pl.BlockSpec((tm, Q_W), lambda i: (i, 0)),
                  pl.BlockSpec((tm, S5_W), lambda i: (i, 0)),
                  mod, mod, mod, vec, vec, full(wglu), full(wout), full(w_router), full(brb),
                  full(tri), full(cnt_in)],
        out_specs=[pl.BlockSpec((tm, D_MODEL), lambda i: (i, 0)),
                   pl.BlockSpec((tm, D_MODEL), lambda i: (i, 0)),
                   pl.BlockSpec((tm, D_MODEL // 2), lambda i: (i, 0)),
                   route, route, route, full(cnt_in)],
        out_shape=[jax.ShapeDtypeStruct((t, D_MODEL), F32),
                   jax.ShapeDtypeStruct((t, D_MODEL), BF16),
                   jax.ShapeDtypeStruct((t, D_MODEL // 2), U32),
                   jax.ShapeDtypeStruct((TOP_K, t), I32),
                   jax.ShapeDtypeStruct((TOP_K, t), F32),
                   jax.ShapeDtypeStruct((TOP_K, t), I32),
                   jax.ShapeDtypeStruct(cnt_in.shape, F32)],
        scratch_shapes=[pltpu.VMEM(cnt_in.shape, F32)],
        compiler_params=_cp("arbitrary"),
        name="post_mix_router",
    )(x, o_att, y5, gm, shf, scf, g_post.reshape(1, D_MODEL), g_pre.reshape(1, D_MODEL),
      wglu, wout, w_router, brb, tri, cnt_in)


def _dest_kernel(e_ref, p_ref, ps_ref, o_ref):
    tm = e_ref.shape[1]
    eio = lax.broadcasted_iota(I32, (N_EXPERTS, tm), 0)
    e = e_ref[...]
    start = ps_ref[...]
    rows = [jnp.sum(jnp.where(eio == e[k:k + 1], start, 0.0), axis=0, keepdims=True)
            for k in range(TOP_K)]
    o_ref[...] = jnp.concatenate(rows, axis=0).astype(I32) + p_ref[...]


def _dest(eidx, pos, pad_start, tm):
    t = eidx.shape[1]
    route = pl.BlockSpec((TOP_K, tm), lambda i: (0, i))
    start = jnp.broadcast_to(pad_start.astype(F32)[:, None], (N_EXPERTS, tm))
    return pl.pallas_call(
        _dest_kernel,
        grid=(t // tm,),
        in_specs=[route, route, pl.BlockSpec((N_EXPERTS, tm), lambda i: (0, 0))],
        out_specs=route,
        out_shape=jax.ShapeDtypeStruct((TOP_K, t), I32),
        compiler_params=_cp("arbitrary"),
        name="moe_dest",
    )(eidx, pos, start)


def _moe_kernel(be_ref, nb_ref, x_ref, wg_ref, wu_ref, wd_ref, y_ref, wg_sc, wu_sc, wd_sc):
    j = pl.program_id(0)

    @pl.when(j < nb_ref[0])
    def _():
        @pl.when((j == 0) | (be_ref[j] != be_ref[jnp.maximum(j - 1, 0)]))
        def _():
            wg_sc[...] = wg_ref[0].astype(BF16)
            wu_sc[...] = wu_ref[0].astype(BF16)
            wd_sc[...] = wd_ref[0].astype(BF16)

        x = jnp.concatenate(_unpack_bf16_pairs(x_ref[...]), axis=1).astype(BF16)
        a = _dot(x, wg_sc[...])
        b = _dot(x, wu_sc[...])
        y = _dot((_silu(a) * b).astype(BF16), wd_sc[...])
        y_ref[...] = _pack_bf16_pairs(y)


def _moe(xs, blk_e, nb_used, w_g, w_u, w_d):
    n_slot = xs.shape[0]
    n_blk = n_slot // MOE_BLK
    last = lambda j, nb: jnp.maximum(jnp.minimum(j, nb[0] - 1), 0)
    row = lambda j, be, nb: (last(j, nb), 0)
    wsel = lambda j, be, nb: (be[last(j, nb)], 0, 0)
    return pl.pallas_call(
        _moe_kernel,
        grid_spec=pltpu.PrefetchScalarGridSpec(
            num_scalar_prefetch=2,
            grid=(n_blk,),
            in_specs=[pl.BlockSpec((MOE_BLK, D_MODEL // 2), row),
                      pl.BlockSpec((1, D_MODEL, D_EXPERT), wsel),
                      pl.BlockSpec((1, D_MODEL, D_EXPERT), wsel),
                      pl.BlockSpec((1, D_EXPERT, D_MODEL), wsel)],
            out_specs=pl.BlockSpec((MOE_BLK, D_MODEL // 2), row),
            scratch_shapes=[pltpu.VMEM((D_MODEL, D_EXPERT), BF16), pltpu.VMEM((D_MODEL, D_EXPERT), BF16),
                            pltpu.VMEM((D_EXPERT, D_MODEL), BF16)]),
        out_shape=jax.ShapeDtypeStruct((n_slot, D_MODEL // 2), U32),
        compiler_params=_cp("arbitrary"),
        name="moe_experts",
    )(blk_e, nb_used, xs, w_g, w_u, w_d)


def _fin_kernel(x1_ref, h2_ref, yg_ref, w_ref, gf_ref, gpost_ref, wsg_ref, wsu_ref, wsd_ref, o_ref):
    w = w_ref[...]
    f_lo, f_hi = None, None
    for k in range(TOP_K):
        lo, hi = _unpack_bf16_pairs(yg_ref[k])
        f_lo = w[:, k:k + 1] * lo if k == 0 else f_lo + w[:, k:k + 1] * lo
        f_hi = w[:, k:k + 1] * hi if k == 0 else f_hi + w[:, k:k + 1] * hi
    f = jnp.concatenate([f_lo, f_hi], axis=1)
    hb = h2_ref[...]
    sh = (_silu(_dot(hb, wsg_ref[...])) * _dot(hb, wsu_ref[...])).astype(BF16)
    f = f + _dot(sh, wsd_ref[...])
    o_ref[...] = x1_ref[...] + gf_ref[0] * _rms(f, gpost_ref[...])


def _final(x1, h2, yg, wts, gf, g_post, wsg, wsu, wsd, tm, rows_per_mod):
    t = x1.shape[0]
    r = gf.shape[1]
    per = rows_per_mod // tm
    full = lambda a: pl.BlockSpec(a.shape, lambda i: (0,) * a.ndim)
    return pl.pallas_call(
        _fin_kernel,
        grid=(t // tm,),
        in_specs=[pl.BlockSpec((tm, D_MODEL), lambda i: (i, 0)),
                  pl.BlockSpec((tm, D_MODEL), lambda i: (i, 0)),
                  pl.BlockSpec((TOP_K, tm, D_MODEL // 2), lambda i: (0, i, 0)),
                  pl.BlockSpec((tm, TOP_K), lambda i: (i, 0)),
                  pl.BlockSpec((1, r, D_MODEL), lambda i: (i // per, 0, 0)),
                  pl.BlockSpec((1, D_MODEL), lambda i: (0, 0)),
                  full(wsg), full(wsu), full(wsd)],
        out_specs=pl.BlockSpec((tm, D_MODEL), lambda i: (i, 0)),
        out_shape=jax.ShapeDtypeStruct((t, D_MODEL), F32),
        compiler_params=_cp("arbitrary"),
        name="moe_combine_final",
    )(x1, h2, yg, wts, gf, g_post.reshape(1, D_MODEL), wsg, wsu, wsd)


def _even_odd(a, axis):
    n = a.shape[axis]
    order = np.concatenate([np.arange(0, n, 2), np.arange(1, n, 2)])
    return jnp.take(a, order, axis=axis)


def _layer(xp, xs, c_prompt, c_sample, page_table, ck_c, cv_c, ck_s, cv_s, cw_k, cw_v, st_re, st_im,
           w_ada, b_ada, g_pre_mix, g_post_mix, g_pre_ffn, g_post_ffn, w_in, pe, wk1, wk2, wv1, wv2,
           rel_bias, lam_re, lam_im, log_dt, b_re, b_im, c_re, c_im, d_skip, w_glu, w_out,
           w_router, b_router, w_eg, w_eu, w_ed, w_sg, w_su, w_sd):
    batch, seq, _ = xp.shape
    n_dec = xs.shape[0]
    tp = batch * seq
    x_p = xp.reshape(tp, D_MODEL)
    x_s = xs.reshape(n_dec, D_MODEL)

    n_c = batch + n_dec
    n_pad = -(-n_c // 8) * 8
    c_all = jnp.pad(jnp.concatenate([c_prompt, c_sample], axis=0), ((0, n_pad - n_c), (0, 0)))
    mod = _adaln(c_all, w_ada, b_ada)
    mod_p = [m.reshape(batch, 1, D_MODEL) for m in jnp.split(mod[:batch], 6, axis=-1)]
    mod_s = [m.reshape(1, n_dec, D_MODEL) for m in jnp.split(mod[batch:n_c], 6, axis=-1)]

    wn, wt = _pre_weights(w_in)
    (kct_p, vct_p, kst_p, vst_p, kwt_p, vwt_p, kc_p, vc_p, u_p, ksb, kwb, qt_p, vst, vsn, vwt, gt_p) = \
        _pre_mix_prompt(x_p, mod_p[1], mod_p[0], g_pre_mix, wn, wt, 512, batch, seq)
    (kc_s, vc_s, ks_s, vs_s, kw_s, vw_s, u_s, qt_s, gt_s) = _pre_mix_sample(
        x_s, mod_s[1], mod_s[0], g_pre_mix, wn, wt)

    n_cmp_p = seq // CMP_BLOCK
    kcmp_p = _compress(kc_p.reshape(batch * n_cmp_p, CMP_BLOCK * KV_W), *_cmp_weights(pe, wk1, wk2))
    vcmp_p = _compress(vc_p.reshape(batch * n_cmp_p, CMP_BLOCK * KV_W), *_cmp_weights(pe, wv1, wv2))
    kcmp_p = _even_odd(kcmp_p.reshape(batch, n_cmp_p, N_KV, HEAD_DIM), 1).transpose(0, 2, 1, 3)
    vcmp_p = _even_odd(vcmp_p.reshape(batch, n_cmp_p, N_KV, HEAD_DIM), 1).transpose(0, 2, 3, 1)
    wtab, ctab = _bias_tables(rel_bias)
    o_p = _attn_prompt(qt_p, gt_p, kcmp_p.astype(BF16), vcmp_p.astype(BF16), ksb, vst, vsn, kwb, vwt,
                       wtab, ctab, batch, seq)

    n_pages = page_table.shape[1]
    n_cmp_s = n_pages * CMP_PER_PAGE
    pages = lambda cache: cache.transpose(0, 2, 3, 1)[page_table].reshape(
        n_dec * n_pages * N_KV * HEAD_DIM, PAGE)
    kcmp_s, vcmp_s = _compress_pages(pages(ck_c), pages(cv_c), _cmp_pages_weights(pe, wk1, wk2),
                                     _cmp_pages_weights(pe, wv1, wv2), n_dec, n_pages)
    by_head = lambda a: _even_odd(
        a.reshape(n_dec, n_pages, N_KV, CMP_PER_PAGE, HEAD_DIM).transpose(0, 2, 1, 3, 4).reshape(
            n_dec, N_KV, n_cmp_s, HEAD_DIM), 2)
    win_k = jnp.concatenate([cw_k[:, 1:], kw_s.reshape(n_dec, 1, N_KV, HEAD_DIM)], axis=1)
    win_v = jnp.concatenate([cw_v[:, 1:], vw_s.reshape(n_dec, 1, N_KV, HEAD_DIM)], axis=1)
    q_s = qt_s.T.reshape(n_dec, N_HEADS, HEAD_DIM)
    gates_s = gt_s[:, :3 * GQ].reshape(N_KV, 3, GQ, n_dec).transpose(3, 0, 2, 1).reshape(n_dec, N_HEADS, 3)
    o_s = _attn_sample(q_s, gates_s, by_head(kcmp_s).astype(BF16), by_head(vcmp_s).astype(BF16),
                       page_table, ck_s, cv_s, ks_s, vs_s, win_k, win_v, rel_bias)

    y5_p, s5r_p, s5i_p = _s5_prompt(
        u_p, _s5_prep(lam_re, lam_im, log_dt, b_re, b_im, c_re, c_im, d_skip, S5_CHUNK), batch, seq)
    y5_s, s5r_s, s5i_s = _s5_sample(
        u_s, _s5_prep(lam_re, lam_im, log_dt, b_re, b_im, c_re, c_im, d_skip, 1), st_re, st_im)

    wglu_b, wout_b = w_glu.astype(BF16), w_out.astype(BF16)
    cnt0 = jnp.zeros((N_EXPERTS, 128), F32)
    tm_q = 256
    x1_p, h2_p, h2p_p, e_p, wt_p, pos_p, cnt1 = _post_mix(
        x_p, o_p, y5_p, mod_p[2], mod_p[3], mod_p[4], g_post_mix, g_pre_ffn, wglu_b, wout_b,
        w_router, b_router, cnt0, tm_q, seq)
    x1_s, h2_s, h2p_s, e_s, wt_s, pos_s, cnt2 = _post_mix(
        x_s, o_s, y5_s, mod_s[2], mod_s[3], mod_s[4], g_post_mix, g_pre_ffn, wglu_b, wout_b,
        w_router, b_router, cnt1, n_dec, n_dec)

    t_all = tp + n_dec
    counts = cnt2[:, 0].astype(I32)
    padded = (counts + MOE_BLK - 1) // MOE_BLK * MOE_BLK
    pad_end = jnp.cumsum(padded)
    dest_p = _dest(e_p, pos_p, pad_end - padded, 1024)
    dest_s = _dest(e_s, pos_s, pad_end - padded, n_dec)
    dest = jnp.concatenate([dest_p, dest_s], axis=1)
    n_blk = -(-(t_all * TOP_K) // MOE_BLK) + N_EXPERTS
    n_slot = n_blk * MOE_BLK
    tok = jnp.broadcast_to(jnp.arange(t_all, dtype=I32)[None], (TOP_K, t_all))
    rows = (jnp.arange(n_slot, dtype=I32) % t_all).at[dest.reshape(-1)].set(
        tok.reshape(-1), unique_indices=True)
    xs_rows = jnp.concatenate([h2p_p, h2p_s], axis=0)[rows]
    blk_e = jnp.minimum(jnp.searchsorted(pad_end, jnp.arange(n_blk, dtype=I32) * MOE_BLK, side='right'),
                        N_EXPERTS - 1).astype(I32)
    nb_used = (pad_end[-1:] // MOE_BLK).astype(I32)
    ys = _moe(xs_rows, blk_e, nb_used, w_eg, w_eu, w_ed)
    yg_p = ys[dest_p]
    yg_s = ys[dest_s]

    wsg, wsu, wsd = w_sg.astype(BF16), w_su.astype(BF16), w_sd.astype(BF16)
    out_p = _final(x1_p, h2_p, yg_p, wt_p.T, mod_p[5], g_post_ffn, wsg, wsu, wsd, tm_q, seq)
    out_s = _final(x1_s, h2_s, yg_s, wt_s.T, mod_s[5], g_post_ffn, wsg, wsu, wsd, n_dec, n_dec)

    n_win = min(WINDOW, seq)
    rows_p = lambda a: a.transpose(0, 3, 1, 2)
    kv5 = lambda a: a.reshape(n_dec, 1, N_KV, HEAD_DIM)
    st_p = (rows_p(kct_p), rows_p(vct_p), rows_p(kst_p), rows_p(vst_p),
            rows_p(kwt_p)[:, seq - n_win:], rows_p(vwt_p)[:, seq - n_win:], s5r_p, s5i_p)
    st_s = (kv5(kc_s), kv5(vc_s), kv5(ks_s), kv5(vs_s), win_k, win_v, s5r_s, s5i_s)
    return out_p.reshape(batch, seq, D_MODEL), out_s.reshape(n_dec, 1, D_MODEL), st_p, st_s


def kernel(x_prompt, x_sample, c_prompt, c_sample, page_table, cache_cmp_k, cache_cmp_v, cache_sel_k, cache_sel_v, cache_win_k, cache_win_v, state_s5_re, state_s5_im, w_ada, b_ada, g_pre_mix, g_post_mix, g_pre_ffn, g_post_ffn, w_in, pe_cmp, w_cmp_k1, w_cmp_k2, w_cmp_v1, w_cmp_v2, rel_bias, lam_re, lam_im, log_dt, b_re, b_im, c_re, c_im, d_skip, w_glu, w_out, w_router, b_router, w_exp_gate, w_exp_up, w_exp_down, w_sh_gate, w_sh_up, w_sh_down):
    depth = w_in.shape[0]
    xp, xs = x_prompt, x_sample
    p_states, s_states = [], []
    for l in range(depth):
        xp, xs, st_p, st_s = _layer(
            xp, xs, c_prompt, c_sample, page_table, cache_cmp_k[l], cache_cmp_v[l], cache_sel_k[l],
            cache_sel_v[l], cache_win_k[l], cache_win_v[l], state_s5_re[l], state_s5_im[l],
            w_ada[l], b_ada[l], g_pre_mix[l], g_post_mix[l], g_pre_ffn[l], g_post_ffn[l], w_in[l],
            pe_cmp[l], w_cmp_k1[l], w_cmp_k2[l], w_cmp_v1[l], w_cmp_v2[l], rel_bias,
            lam_re[l], lam_im[l], log_dt[l], b_re[l], b_im[l], c_re[l], c_im[l], d_skip[l],
            w_glu[l], w_out[l], w_router[l], b_router[l], w_exp_gate[l], w_exp_up[l], w_exp_down[l],
            w_sh_gate[l], w_sh_up[l], w_sh_down[l])
        p_states.append(st_p)
        s_states.append(st_s)
    p_st = tuple(jnp.stack(a) for a in zip(*p_states))
    s_st = tuple(jnp.stack(a) for a in zip(*s_states))
    return (xp, xs) + p_st + s_st
```

```python
import functools
import math

import numpy as np
import jax
import jax.numpy as jnp
from jax import lax
from jax.experimental import pallas as pl
from jax.experimental.pallas import tpu as pltpu

F32 = jnp.float32
BF16 = jnp.bfloat16
I32 = jnp.int32
U32 = jnp.uint32

D_MODEL = 1024
N_HEADS = 8
HEAD_DIM = 64
N_KV = 2
GQ = N_HEADS // N_KV
Q_W = N_HEADS * HEAD_DIM
KV_W = N_KV * HEAD_DIM
S5_W = D_MODEL - Q_W
S5_H = 16
S5_G = S5_W // S5_H
S5_P = 64
GATE_OFF = Q_W + 6 * KV_W
U_OFF = GATE_OFF + 3 * N_HEADS
CMP_BLOCK = 32
SEL_BLOCK = 64
N_SEL = 16
WINDOW = 512
NUM_BUCKETS = 32
REL_MAX_DIST = 128
N_EXPERTS = 256
TOP_K = 8
N_EGROUPS = 8
TOPK_GROUPS = 4
EGROUP = N_EXPERTS // N_EGROUPS
D_EXPERT = 256
ROUTED_SCALE = 2.5
EPS = 1e-6
SCALE = HEAD_DIM ** -0.5
PAGE = 128
CMP_PER_PAGE = PAGE // CMP_BLOCK

QT = 128
VT = 2 * QT
N_DELTA = WINDOW // QT + 1
S5_CHUNK = 32
S5_ROWS = 8
MOE_BLK = 512
NEG = -2e30
M_INIT = -1e30
VMEM_LIMIT = 56 * 1024 * 1024
WT_ROWS = Q_W + 6 * KV_W + 32
HIGHEST = lax.Precision.HIGHEST


def _cp(*sem):
    return pltpu.CompilerParams(dimension_semantics=sem, vmem_limit_bytes=VMEM_LIMIT)


def _dot(a, b, precision=None):
    return jnp.dot(a, b, preferred_element_type=F32, precision=precision)


def _dot_nt(a, b):
    return lax.dot_general(a, b, (((1,), (1,)), ((), ())), preferred_element_type=F32)


def _sigmoid(x):
    return 1.0 / (1.0 + jnp.exp(-x))


def _silu(x):
    return x * _sigmoid(x)


def _rms(x, g):
    return x * lax.rsqrt(jnp.mean(x * x, axis=-1, keepdims=True) + EPS) * g


def _pack_bf16_pairs(x):
    c = x.shape[1] // 2
    bits = pltpu.bitcast(x.astype(BF16).astype(F32), U32)
    return (bits[:, :c] >> 16) | bits[:, c:]


def _unpack_bf16_pairs(xp):
    return pltpu.bitcast(xp << 16, F32), pltpu.bitcast(xp & jnp.uint32(0xFFFF0000), F32)


def _bucket_table(n):
    d = np.arange(n)
    exact = NUM_BUCKETS // 2
    nf = np.maximum(d, 1).astype(np.float32)
    large = exact + (np.log(nf / np.float32(exact)) / np.float32(math.log(REL_MAX_DIST / exact))
                     * np.float32(NUM_BUCKETS - exact)).astype(np.int32)
    return np.where(d < exact, d, np.minimum(large, NUM_BUCKETS - 1)).astype(np.int32)


def _ada_kernel(c_ref, w_ref, b_ref, o_ref):
    a = _silu(c_ref[...]).astype(BF16)
    o_ref[...] = _dot(a, w_ref[...].astype(BF16)) + b_ref[...]


def _adaln(c, w_ada, b_ada):
    n, d = c.shape
    n_out = w_ada.shape[1]
    tn = 1024
    return pl.pallas_call(
        _ada_kernel,
        grid=(n_out // tn,),
        in_specs=[pl.BlockSpec((n, d), lambda j: (0, 0)),
                  pl.BlockSpec((d, tn), lambda j: (0, j)),
                  pl.BlockSpec((1, tn), lambda j: (0, j))],
        out_specs=pl.BlockSpec((n, tn), lambda j: (0, j)),
        out_shape=jax.ShapeDtypeStruct((n, n_out), F32),
        compiler_params=_cp("arbitrary"),
        name="adaln",
    )(c, w_ada, b_ada.reshape(1, n_out))


def _pre_project(x_ref, sc_ref, sh_ref, g_ref, wn_ref, wt_ref):
    h = _rms(x_ref[...], g_ref[...]) * (1.0 + sc_ref[0]) + sh_ref[0]
    hb = h.astype(BF16)
    return _dot(hb, wn_ref[...]), _dot_nt(wt_ref[...], hb)


def _pre_prompt_kernel(x_ref, sc_ref, sh_ref, g_ref, wn_ref, wt_ref,
                       kct_ref, vct_ref, kst_ref, vst32_ref, kwt_ref, vwt32_ref, kc_ref, vc_ref, u_ref,
                       ksb_ref, kwb_ref, qt_ref, vst_ref, vsn_ref, vwt_ref, gt_ref):
    tm = x_ref.shape[0]
    zn, zt = _pre_project(x_ref, sc_ref, sh_ref, g_ref, wn_ref, wt_ref)
    trans = lambda j, g: zt[Q_W + j * KV_W + g * HEAD_DIM:Q_W + j * KV_W + (g + 1) * HEAD_DIM]
    for j, ref in enumerate((kct_ref, vct_ref, kst_ref, vst32_ref, kwt_ref, vwt32_ref)):
        for g in range(N_KV):
            ref[0, g] = trans(j, g)
    kc_ref[...] = zn[:, 0:KV_W]
    vc_ref[...] = zn[:, KV_W:2 * KV_W]
    u_ref[...] = zn[:, 6 * KV_W:]
    for g in range(N_KV):
        ksb_ref[g] = zn[:, 2 * KV_W + g * HEAD_DIM:2 * KV_W + (g + 1) * HEAD_DIM].astype(BF16)
        kwb_ref[g] = zn[:, 4 * KV_W + g * HEAD_DIM:4 * KV_W + (g + 1) * HEAD_DIM].astype(BF16)
        vs_t = trans(3, g).astype(BF16)
        vw_t = trans(5, g).astype(BF16)
        for c in range(tm // VT):
            vst_ref[g, c] = vs_t[:, c * VT:(c + 1) * VT]
        for c in range(tm // QT):
            vsn_ref[g, c] = vs_t[:, c * QT:(c + 1) * QT]
            vwt_ref[g, c] = vw_t[:, c * QT:(c + 1) * QT]
        r2 = Q_W + 6 * KV_W + g * 16
        gt_ref[g] = _sigmoid(zt[r2:r2 + 16])
    qt_ref[...] = zt[:Q_W].astype(BF16)


def _pre_sample_kernel(x_ref, sc_ref, sh_ref, g_ref, wn_ref, wt_ref,
                       kc_ref, vc_ref, ks_ref, vs_ref, kw_ref, vw_ref, u_ref, qt_ref, gt_ref):
    zn, zt = _pre_project(x_ref, sc_ref, sh_ref, g_ref, wn_ref, wt_ref)
    for j, ref in enumerate((kc_ref, vc_ref, ks_ref, vs_ref, kw_ref, vw_ref)):
        ref[...] = zn[:, j * KV_W:(j + 1) * KV_W]
    u_ref[...] = zn[:, 6 * KV_W:]
    qt_ref[...] = zt[:Q_W].astype(BF16)
    for g in range(N_KV):
        r2 = Q_W + 6 * KV_W + g * 16
        gt_ref[g] = _sigmoid(zt[r2:r2 + 16])


def _pre_weights(w_in):
    wn = jnp.concatenate([w_in[:, Q_W:GATE_OFF], w_in[:, U_OFF:]], axis=1).astype(BF16)
    gate_cols = []
    for g in range(N_KV):
        for j in range(3):
            for hh in range(GQ):
                gate_cols.append(GATE_OFF + (g * GQ + hh) * 3 + j)
        gate_cols.extend([GATE_OFF] * 4)
    wt = jnp.concatenate([
        w_in[:, :Q_W] * SCALE,
        w_in[:, Q_W:GATE_OFF],
        w_in[:, np.array(gate_cols)],
    ], axis=1).T.astype(BF16)
    return wn, wt


def _pre_in_specs(tm, r, per, wn, wt):
    return [pl.BlockSpec((tm, D_MODEL), lambda i: (i, 0)),
            pl.BlockSpec((1, r, D_MODEL), lambda i: (i // per, 0, 0)),
            pl.BlockSpec((1, r, D_MODEL), lambda i: (i // per, 0, 0)),
            pl.BlockSpec((1, D_MODEL), lambda i: (0, 0)),
            pl.BlockSpec(wn.shape, lambda i: (0, 0)),
            pl.BlockSpec(wt.shape, lambda i: (0, 0))]


def _pre_mix_prompt(x, sc, sh, g_pre, wn, wt, tm, batch, seq):
    t = x.shape[0]
    per = seq // tm
    f = lambda shape: jax.ShapeDtypeStruct(shape, F32)
    b = lambda shape: jax.ShapeDtypeStruct(shape, BF16)
    tr_spec = pl.BlockSpec((1, N_KV, HEAD_DIM, tm), lambda i: (i // per, 0, 0, i % per))
    kv_spec = pl.BlockSpec((tm, KV_W), lambda i: (i, 0))
    return pl.pallas_call(
        _pre_prompt_kernel,
        grid=(t // tm,),
        in_specs=_pre_in_specs(tm, 1, per, wn, wt),
        out_specs=[tr_spec] * 6 + [
            kv_spec, kv_spec,
            pl.BlockSpec((tm, S5_W), lambda i: (i, 0)),
            pl.BlockSpec((N_KV, tm, HEAD_DIM), lambda i: (0, i, 0)),
            pl.BlockSpec((N_KV, tm, HEAD_DIM), lambda i: (0, i, 0)),
            pl.BlockSpec((Q_W, tm), lambda i: (0, i)),
            pl.BlockSpec((N_KV, tm // VT, HEAD_DIM, VT), lambda i: (0, i, 0, 0)),
            pl.BlockSpec((N_KV, tm // QT, HEAD_DIM, QT), lambda i: (0, i, 0, 0)),
            pl.BlockSpec((N_KV, tm // QT, HEAD_DIM, QT), lambda i: (0, i, 0, 0)),
            pl.BlockSpec((N_KV, 16, tm), lambda i: (0, 0, i))],
        out_shape=[f((batch, N_KV, HEAD_DIM, seq))] * 6 + [
            f((t, KV_W)), f((t, KV_W)),
            f((t, S5_W)), b((N_KV, t, HEAD_DIM)), b((N_KV, t, HEAD_DIM)), b((Q_W, t)),
            b((N_KV, t // VT, HEAD_DIM, VT)), b((N_KV, t // QT, HEAD_DIM, QT)),
            b((N_KV, t // QT, HEAD_DIM, QT)), f((N_KV, 16, t))],
        compiler_params=_cp("arbitrary"),
        name="pre_mix_prompt",
    )(x, sc, sh, g_pre.reshape(1, D_MODEL), wn, wt)


def _pre_mix_sample(x, sc, sh, g_pre, wn, wt):
    t = x.shape[0]
    f = lambda shape: jax.ShapeDtypeStruct(shape, F32)
    kv_spec = pl.BlockSpec((t, KV_W), lambda i: (i, 0))
    return pl.pallas_call(
        _pre_sample_kernel,
        grid=(1,),
        in_specs=_pre_in_specs(t, t, 1, wn, wt),
        out_specs=[kv_spec] * 6 + [pl.BlockSpec((t, S5_W), lambda i: (i, 0)),
                                   pl.BlockSpec((Q_W, t), lambda i: (0, i)),
                                   pl.BlockSpec((N_KV, 16, t), lambda i: (0, 0, i))],
        out_shape=[f((t, KV_W))] * 6 + [f((t, S5_W)), jax.ShapeDtypeStruct((Q_W, t), BF16),
                                        f((N_KV, 16, t))],
        compiler_params=_cp("arbitrary"),
        name="pre_mix_sample",
    )(x, sc, sh, g_pre.reshape(1, D_MODEL), wn, wt)


def _cmp_kernel(x_ref, pe_ref, w1_ref, w2_ref, o_ref):
    xb = (x_ref[...] + pe_ref[...]).astype(BF16)
    hid = _silu(_dot(xb, w1_ref[...]))
    o_ref[...] = _dot(hid.astype(BF16), w2_ref[...])


def _cmp_weights(pe, w1, w2):
    eye = jnp.eye(N_KV, dtype=F32)
    w1b = jnp.einsum('jde,gh->jgdhe', w1, eye).reshape(CMP_BLOCK * KV_W, KV_W).astype(BF16)
    w2b = jnp.einsum('ef,gh->gehf', w2, eye).reshape(KV_W, KV_W).astype(BF16)
    peb = jnp.broadcast_to(pe[:, None, :], (CMP_BLOCK, N_KV, HEAD_DIM)).reshape(1, CMP_BLOCK * KV_W)
    return peb, w1b, w2b


def _compress(x, peb, w1b, w2b):
    r, k = x.shape
    tr = min(256, r)
    return pl.pallas_call(
        _cmp_kernel,
        grid=(r // tr,),
        in_specs=[pl.BlockSpec((tr, k), lambda i: (i, 0)),
                  pl.BlockSpec((1, k), lambda i: (0, 0)),
                  pl.BlockSpec((k, KV_W), lambda i: (0, 0)),
                  pl.BlockSpec((KV_W, KV_W), lambda i: (0, 0))],
        out_specs=pl.BlockSpec((tr, KV_W), lambda i: (i, 0)),
        out_shape=jax.ShapeDtypeStruct((r, KV_W), F32),
        compiler_params=_cp("arbitrary"),
        name="compress",
    )(x, peb, w1b, w2b)


def _cmp_pages_kernel(xk_ref, xv_ref, w1k_ref, w1v_ref, bk_ref, bv_ref, w2k_ref, w2v_ref,
                      ok_ref, ov_ref):
    n_rows = ok_ref.shape[0]
    for x_ref, w1_ref, b_ref, w2_ref, o_ref in ((xk_ref, w1k_ref, bk_ref, w2k_ref, ok_ref),
                                               (xv_ref, w1v_ref, bv_ref, w2v_ref, ov_ref)):
        acc = jnp.zeros((n_rows, CMP_PER_PAGE * HEAD_DIM), F32)
        for dd in range(HEAD_DIM // 2):
            r0 = x_ref[pl.ds(2 * dd, n_rows, stride=HEAD_DIM), :]
            r1 = x_ref[pl.ds(2 * dd + 1, n_rows, stride=HEAD_DIM), :]
            acc = acc + _dot(jnp.concatenate([r0, r1], axis=1).astype(BF16), w1_ref[dd])
        hid = _silu(acc + b_ref[...])
        o_ref[...] = _dot(hid.astype(BF16), w2_ref[...])


def _cmp_pages_weights(pe, w1, w2):
    eye = jnp.eye(CMP_PER_PAGE, dtype=F32)
    w1t = jnp.einsum('jde,mn->dmjne', w1, eye).reshape(HEAD_DIM // 2, 2 * PAGE, CMP_PER_PAGE * HEAD_DIM)
    bias = jnp.einsum('jd,jde->e', pe, w1, precision=HIGHEST)
    bias = jnp.tile(bias, CMP_PER_PAGE).reshape(1, CMP_PER_PAGE * HEAD_DIM)
    w2t = jnp.einsum('ef,mn->menf', w2, eye).reshape(CMP_PER_PAGE * HEAD_DIM, CMP_PER_PAGE * HEAD_DIM)
    return w1t.astype(BF16), bias, w2t.astype(BF16)


def _compress_pages(xk, xv, wk, wv, n_seq, n_pages):
    rows_in = n_pages * N_KV * HEAD_DIM
    rows_out = n_pages * N_KV
    wcols = CMP_PER_PAGE * HEAD_DIM
    full = lambda a: pl.BlockSpec(a.shape, lambda i: (0,) * a.ndim)
    x_spec = pl.BlockSpec((rows_in, PAGE), lambda i: (i, 0))
    o_spec = pl.BlockSpec((rows_out, wcols), lambda i: (i, 0))
    o_shape = jax.ShapeDtypeStruct((n_seq * rows_out, wcols), F32)
    return pl.pallas_call(
        _cmp_pages_kernel,
        grid=(n_seq,),
        in_specs=[x_spec, x_spec, full(wk[0]), full(wv[0]), full(wk[1]), full(wv[1]),
                  full(wk[2]), full(wv[2])],
        out_specs=[o_spec, o_spec],
        out_shape=[o_shape, o_shape],
        compiler_params=_cp("arbitrary"),
        name="compress_pages",
    )(xk, xv, wk[0], wv[0], wk[1], wv[1], wk[2], wv[2])


def _attn_kernel(q_ref, g_ref, kc_ref, vct_ref, ks_ref, vst_ref, vsn_ref, kw_ref, vwt_ref,
                 ctab_ref, wtab_ref, o_ref, mb_ref, mbf_ref, s_sc, p_sc, *, n_cmp):
    i = pl.program_id(2)
    lanes = GQ * QT
    qt = q_ref[...]
    q4 = jnp.concatenate([qt[hh * HEAD_DIM:(hh + 1) * HEAD_DIM] for hh in range(GQ)], axis=1)
    gates = g_ref[0]
    grow = lambda j: jnp.concatenate([gates[j * GQ + hh:j * GQ + hh + 1] for hh in range(GQ)], axis=1)

    def online(carry, s, v_t):
        m, l, acc = carry
        m_new = jnp.maximum(m, jnp.max(s, axis=0, keepdims=True))
        alpha = jnp.exp(m - m_new)
        p = jnp.exp(s - m_new)
        l = alpha * l + jnp.sum(p, axis=0, keepdims=True)
        return m_new, l, alpha * acc + _dot(v_t, p.astype(BF16))

    init = (jnp.full((1, lanes), M_INIT, F32), jnp.zeros((1, lanes), F32),
            jnp.zeros((HEAD_DIM, lanes), F32))
    finish = lambda carry: carry[2] * (1.0 / jnp.maximum(carry[1], 1e-30))
    block_rows = lambda ref, first, n, extra=0.0: jnp.concatenate(
        [jnp.broadcast_to(ref[pl.ds(first + c, 1), :] + extra, (SEL_BLOCK, lanes)) for c in range(n)],
        axis=0)

    def near_tiles(carry, k_ref, vt_ref, deltas, live, masked):
        ks, vs, bs = [], [], []
        for dl in deltas:
            kt = i - dl
            ktc = jnp.maximum(kt, 0)
            ks.append(k_ref[0, pl.ds(pl.multiple_of(ktc * QT, QT), QT), :])
            vs.append(vt_ref[0, ktc])
            dead = jnp.where(live(dl, kt), 0.0, NEG)
            if masked:
                bs.append(wtab_ref[0, dl] + block_rows(mb_ref, (QT // SEL_BLOCK) * ktc, QT // SEL_BLOCK, dead))
            else:
                bs.append(wtab_ref[0, dl] + dead)
        s = _dot(jnp.concatenate(ks, axis=0), q4) + jnp.concatenate(bs, axis=0)
        return online(carry, s, jnp.concatenate(vs, axis=1))

    o_w = finish(near_tiles(init, kw_ref, vwt_ref, list(range(N_DELTA - 1, -1, -1)),
                            lambda dl, kt: kt >= 0, False))

    half = n_cmp // 2
    sc = _dot(kc_ref[0, 0], q4)
    rho = lax.broadcasted_iota(I32, (n_cmp, 16), 0)
    col = lax.broadcasted_iota(I32, (n_cmp, 16), 1)
    blk_n = 2 * jnp.where(rho >= half, rho - half, rho) + jnp.where(rho >= half, 1, 0)
    rel = blk_n - (4 * i - 4)
    want = jnp.where(rel < 0, 8, jnp.where(rel > 7, 9, rel))
    place = jnp.where(col == want, 1.0, 0.0).astype(BF16)
    sc = sc + _dot(place, ctab_ref[0, 0]) + _dot(place, ctab_ref[0, 1])
    mc = jnp.maximum(jnp.max(sc, axis=0, keepdims=True), M_INIT)
    ec = jnp.exp(sc - mc)
    pc = ec * (1.0 / jnp.maximum(jnp.sum(ec, axis=0, keepdims=True), 1e-30))
    o_cw = grow(0) * _dot(vct_ref[0, 0], pc.astype(BF16)) + grow(2) * o_w

    ps = pc[:half] + pc[half:]
    imp = ps[:, 0:QT]
    for hh in range(1, GQ):
        imp = imp + ps[:, hh * QT:(hh + 1) * QT]
    n_blk = half
    blk = lax.broadcasted_iota(I32, (n_blk, QT), 0)
    tq = i * QT + lax.broadcasted_iota(I32, (n_blk, QT), 1)
    cur = lax.shift_right_logical(tq, 6)
    valid = blk * SEL_BLOCK <= tq
    forced = (blk == 0) | (blk == cur) | (blk == cur - 1)
    blkf = blk.astype(F32)

    def pick(_, carry):
        work, mb = carry
        mx = jnp.max(work, axis=0, keepdims=True)
        idx = jnp.min(jnp.where(work == mx, blkf, 1e9), axis=0, keepdims=True)
        hit = blkf == idx
        return jnp.where(hit, -jnp.inf, work), jnp.where(hit, 0.0, mb)

    _, mb = lax.fori_loop(0, min(N_SEL, n_blk) - 3, pick,
                          (jnp.where(valid, jnp.where(forced, -jnp.inf, imp), -jnp.inf),
                           jnp.where(valid, jnp.where(forced, 0.0, NEG), NEG)))
    for hh in range(GQ):
        mb_ref[:, hh * QT:(hh + 1) * QT] = mb
    mbf_ref[...] = mb_ref[...] + wtab_ref[0, 2, 0:1, :]

    n_far = jnp.maximum(i - 1, 0) // 2
    last_pair = ks_ref.shape[1] // VT - 1

    def pair_scores(t):
        ta = jnp.minimum(t, last_pair)
        k_t = ks_ref[0, pl.ds(pl.multiple_of(ta * VT, VT), VT), :]
        return _dot(k_t, q4) + block_rows(mbf_ref, (VT // SEL_BLOCK) * ta, VT // SEL_BLOCK,
                                          jnp.where(t < n_far, 0.0, NEG))

    def far_step(t, par, carry):
        m, l, acc, alpha_prev = carry
        pv = _dot(vst_ref[0, jnp.clip(t - 2, 0, last_pair)], p_sc[par])
        s = s_sc[1 - par]
        m_new = jnp.maximum(m, jnp.max(s, axis=0, keepdims=True))
        alpha = jnp.exp(m - m_new)
        p = jnp.exp(s - m_new)
        l = alpha * l + jnp.sum(p, axis=0, keepdims=True)
        s_sc[par] = pair_scores(t)
        p_sc[1 - par] = p.astype(BF16)
        return m_new, l, alpha_prev * acc + pv, alpha

    s_sc[0] = pair_scores(0)
    p_sc[1] = jnp.zeros((VT, lanes), BF16)
    m, l, acc, _ = lax.fori_loop(
        0, (n_far + 2) // 2,
        lambda u, carry: far_step(2 * u + 2, 0, far_step(2 * u + 1, 1, carry)),
        init + (jnp.ones((1, lanes), F32),))

    sel_live = lambda dl, kt: (kt >= 0) & ((dl < 2) | (i % 2 == 0))
    o_s = finish(near_tiles((m, l, acc), ks_ref, vsn_ref, [2, 1, 0], sel_live, True))
    o_t = o_cw + grow(1) * o_s
    o_hd = jnp.concatenate([o_t[:, hh * QT:(hh + 1) * QT] for hh in range(GQ)], axis=0)
    o_ref[...] = o_hd.T.astype(BF16)


def _bias_tables(rel_bias):
    span = N_DELTA * QT
    bt = _bucket_table(WINDOW)
    period = span + QT
    by_dist = jnp.concatenate([rel_bias[bt], jnp.full((period - WINDOW, N_HEADS), NEG, F32)], axis=0)
    toep = jnp.tile(by_dist.T, (1, QT))[:, :QT * (period - 1)].reshape(N_HEADS, QT, period - 1)
    tab = toep[:, :, :span].reshape(N_KV, GQ, QT, N_DELTA, QT)
    wtab = tab.transpose(0, 3, 2, 1, 4).reshape(N_KV, N_DELTA, QT, GQ * QT)
    wtab = jnp.concatenate([wtab, jnp.full((N_KV, 1, QT, GQ * QT), NEG, F32)], axis=1)
    ql = np.arange(QT)[None, :]
    r = np.arange(8)[:, None]
    d = ql + (4 * CMP_BLOCK - CMP_BLOCK + 1) - CMP_BLOCK * r
    near = jnp.where((d >= 0)[..., None], rel_bias[bt[np.clip(d, 0, len(bt) - 1)]], NEG)
    far = jnp.broadcast_to(rel_bias[NUM_BUCKETS - 1][None, None, :], (1, QT, N_HEADS))
    rows = jnp.concatenate([near, far, jnp.full((1, QT, N_HEADS), NEG, F32),
                            jnp.zeros((6, QT, N_HEADS), F32)], axis=0)
    ctab = rows.reshape(16, QT, N_KV, GQ).transpose(2, 0, 3, 1).reshape(N_KV, 16, GQ * QT)
    hi = ctab.astype(BF16)
    lo = (ctab - hi.astype(F32)).astype(BF16)
    return wtab.astype(F32), jnp.stack([hi, lo], axis=1)


def _attn_prompt(qt, gt, kcmp, vcmpt, ksb, vst, vsn, kwb, vwt, wtab, ctab, batch, seq):
    nq = seq // QT
    n_cmp = kcmp.shape[2]
    lanes = GQ * QT
    t = batch * seq
    return pl.pallas_call(
        functools.partial(_attn_kernel, n_cmp=n_cmp),
        grid=(batch, N_KV, nq),
        in_specs=[
            pl.BlockSpec((GQ * HEAD_DIM, QT), lambda b, g, i: (g, b * nq + i)),
            pl.BlockSpec((1, 16, QT), lambda b, g, i: (g, 0, b * nq + i)),
            pl.BlockSpec((1, 1, n_cmp, HEAD_DIM), lambda b, g, i: (b, g, 0, 0)),
            pl.BlockSpec((1, 1, HEAD_DIM, n_cmp), lambda b, g, i: (b, g, 0, 0)),
            pl.BlockSpec((1, seq, HEAD_DIM), lambda b, g, i: (g, b, 0)),
            pl.BlockSpec((1, seq // VT, HEAD_DIM, VT), lambda b, g, i: (g, b, 0, 0)),
            pl.BlockSpec((1, nq, HEAD_DIM, QT), lambda b, g, i: (g, b, 0, 0)),
            pl.BlockSpec((1, seq, HEAD_DIM), lambda b, g, i: (g, b, 0)),
            pl.BlockSpec((1, nq, HEAD_DIM, QT), lambda b, g, i: (g, b, 0, 0)),
            pl.BlockSpec((1, 2, 16, lanes), lambda b, g, i: (g, 0, 0, 0)),
            pl.BlockSpec((1, N_DELTA + 1, QT, lanes), lambda b, g, i: (g, 0, 0, 0)),
        ],
        out_specs=pl.BlockSpec((QT, GQ * HEAD_DIM), lambda b, g, i: (b * nq + i, g)),
        out_shape=jax.ShapeDtypeStruct((t, Q_W), BF16),
        scratch_shapes=[pltpu.VMEM((n_cmp // 2, lanes), F32), pltpu.VMEM((n_cmp // 2, lanes), F32),
                        pltpu.VMEM((2, VT, lanes), F32), pltpu.VMEM((2, VT, lanes), BF16)],
        compiler_params=_cp("arbitrary", "arbitrary", "arbitrary"),
        name="nsa_prompt",
    )(qt, gt, kcmp, vcmpt, ksb, vst, vsn, kwb, vwt, ctab, wtab)


def _s5_kernel(u_ref, mt_ref, sbr_ref, sbi_ref, ccr_ref, cci_ref, lb_ref, h0r_ref, h0i_ref,
               y_ref, hr_ref, hi_ref, sr_sc, si_sc, pr_sc, pi_sc, *, n_chunks, rb, n_seg):
    u = u_ref[0]
    sr_sc[...] = _dot(u, sbr_ref[0])
    si_sc[...] = _dot(u, sbi_ref[0])
    lr = lb_ref[0, 0:1, :]
    li = lb_ref[0, 1:2, :]

    def step(c, carry):
        hr, hi = carry
        rows = pl.ds(pl.multiple_of(c * rb, rb), rb)
        pr_sc[rows, :] = hr
        pi_sc[rows, :] = hi
        return (lr * hr - li * hi + sr_sc[rows, :], lr * hi + li * hr + si_sc[rows, :])

    h0r, h0i = h0r_ref[0], h0i_ref[0]
    first = (lax.broadcasted_iota(I32, (rb, S5_P), 0) & (n_seg - 1)) == 0
    hr, hi = lax.fori_loop(0, n_chunks, step, (h0r, h0i))
    for _ in range(n_seg - 1):
        hr, hi = lax.fori_loop(
            0, n_chunks, step,
            (jnp.where(first, h0r, pltpu.roll(hr, 1, 0)), jnp.where(first, h0i, pltpu.roll(hi, 1, 0))))
    hr_ref[0] = hr
    hi_ref[0] = hi
    y_ref[0] = (_dot(u, mt_ref[0]) + _dot(pr_sc[...].astype(BF16), ccr_ref[0])
                + _dot(pi_sc[...].astype(BF16), cci_ref[0]))


def _s5_prep(lam_re, lam_im, log_dt, b_re, b_im, c_re, c_im, d_skip, chunk):
    lam = lax.complex(lam_re, lam_im)
    z = lam * jnp.exp(log_dt)[:, None]
    lbar = jnp.exp(z)
    bbar = ((lbar - 1.0) / lam)[:, :, None] * lax.complex(b_re, b_im)
    c = lax.complex(c_re, c_im)
    pw = jnp.exp(z[None] * jnp.arange(chunk + 1, dtype=F32)[:, None, None])
    kern = jnp.einsum('gap,jgp,gpb->jgab', c, pw[:chunk], bbar, precision=HIGHEST).real
    kern = kern.at[0].add(jax.vmap(jnp.diag)(d_skip))
    lag = np.arange(chunk)[None, :] - np.arange(chunk)[:, None]
    kt = kern.transpose(1, 0, 3, 2)
    m5 = jnp.where((lag >= 0)[None, :, :, None, None], kt[:, np.clip(lag, 0, None)], 0.0)
    mt = m5.transpose(0, 1, 3, 2, 4).reshape(S5_G, chunk * S5_H, chunk * S5_H)
    sb = jnp.einsum('lgp,gph->glhp', pw[chunk - 1 - np.arange(chunk)], bbar).reshape(
        S5_G, chunk * S5_H, S5_P)
    cc = jnp.einsum('ghp,lgp->gplh', c, pw[1:]).reshape(S5_G, S5_P, chunk * S5_H)
    lb = jnp.stack([pw[chunk].real, pw[chunk].imag], axis=1)
    return (mt.astype(BF16), sb.real.astype(BF16), sb.imag.astype(BF16),
            cc.real.astype(BF16), (-cc.imag).astype(BF16), lb.astype(F32))


def _s5_call(u, prep, h0r, h0i, n_chunks, rb, n_seg):
    mt, sbr, sbi, ccr, cci, lb = prep
    g, rows, w = u.shape
    per_g = lambda *shape: pl.BlockSpec((1,) + shape, lambda i: (i,) + (0,) * len(shape))
    return pl.pallas_call(
        functools.partial(_s5_kernel, n_chunks=n_chunks, rb=rb, n_seg=n_seg),
        grid=(g,),
        in_specs=[per_g(rows, w), per_g(w, w), per_g(w, S5_P), per_g(w, S5_P),
                  per_g(S5_P, w), per_g(S5_P, w), per_g(2, S5_P), per_g(rb, S5_P), per_g(rb, S5_P)],
        out_specs=[per_g(rows, w), per_g(rb, S5_P), per_g(rb, S5_P)],
        out_shape=[jax.ShapeDtypeStruct((g, rows, w), F32),
                   jax.ShapeDtypeStruct((g, rb, S5_P), F32),
                   jax.ShapeDtypeStruct((g, rb, S5_P), F32)],
        scratch_shapes=[pltpu.VMEM((rows, S5_P), F32)] * 4,
        compiler_params=_cp("arbitrary"),
        name="s5_scan",
    )(u, mt, sbr, sbi, ccr, cci, lb, h0r, h0i)


def _s5_prompt(u, prep, batch, seq):
    assert S5_ROWS % batch == 0
    n_seg = S5_ROWS // batch
    nc = seq // S5_CHUNK // n_seg
    ub = u.reshape(batch, n_seg, nc, S5_CHUNK, S5_G, S5_H).transpose(4, 2, 0, 1, 3, 5)
    ub = ub.reshape(S5_G, nc * S5_ROWS, S5_CHUNK * S5_H).astype(BF16)
    zero = jnp.zeros((S5_G, S5_ROWS, S5_P), F32)
    y, hr, hi = _s5_call(ub, prep, zero, zero, nc, S5_ROWS, n_seg)
    y = y.reshape(S5_G, nc, batch, n_seg, S5_CHUNK, S5_H).transpose(2, 3, 1, 4, 0, 5)
    last = lambda h: h[:, n_seg - 1::n_seg].transpose(1, 0, 2)
    return y.reshape(batch * seq, S5_W), last(hr), last(hi)


def _s5_sample(u, prep, h0r, h0i):
    n = u.shape[0]
    ub = u.reshape(n, S5_G, S5_H).transpose(1, 0, 2).astype(BF16)
    y, hr, hi = _s5_call(ub, prep, h0r.transpose(1, 0, 2), h0i.transpose(1, 0, 2), 1, n, 1)
    return y.transpose(1, 0, 2).reshape(n, S5_W), hr.transpose(1, 0, 2), hi.transpose(1, 0, 2)


def _softmax_lanes(s):
    m = jnp.maximum(jnp.max(s, axis=-1, keepdims=True), M_INIT)
    e = jnp.exp(s - m)
    return e * (1.0 / jnp.maximum(jnp.sum(e, axis=-1, keepdims=True), 1e-30))


def _bqk(q, k):
    return jnp.einsum('bhd,bnd->bhn', q, k, preferred_element_type=F32)


def _bpv(p, v):
    return jnp.einsum('bhn,bnd->bhd', p.astype(BF16), v, preferred_element_type=F32)


def _sattn1_kernel(q_ref, kc_ref, vc_ref, bias_ref, oc_ref, pick_ref, *, n_pick):
    q = q_ref[...]
    p = _softmax_lanes(_bqk(q, kc_ref[...]) + bias_ref[...])
    oc_ref[...] = _bpv(p, vc_ref[...])
    n_cmp = p.shape[-1]
    half = n_cmp // 2
    ps = p[:, 0]
    for hh in range(1, GQ):
        ps = ps + p[:, hh]
    imp = ps[:, :half] + ps[:, half:]
    lane = lax.broadcasted_iota(I32, imp.shape, 1).astype(F32)
    work = jnp.where((lane > 0) & (lane < half - 1), imp, -jnp.inf)
    picks = jnp.zeros(imp.shape, F32)
    for t in range(n_pick):
        mx = jnp.max(work, axis=-1, keepdims=True)
        idx = jnp.min(jnp.where(work == mx, lane, 1e9), axis=-1, keepdims=True)
        work = jnp.where(lane == idx, -jnp.inf, work)
        picks = jnp.where(lane == t, idx, picks)
    pick_ref[...] = picks.astype(I32)


def _bqkt(q, kt):
    return jnp.einsum('bhd,bdn->bhn', q, kt.astype(BF16), preferred_element_type=F32)


def _bpvt(p, vt):
    return jnp.einsum('bhn,bdn->bhd', p.astype(BF16), vt.astype(BF16), preferred_element_type=F32)


def _sattn2_kernel(q_ref, ks_ref, vs_ref, base_ref, corr_ref, flag_ref, half_ref, ex_ref, lh_ref,
                   kw_ref, vw_ref, wb_ref, oc_ref, g_ref, o_ref):
    q = q_ref[...]
    ex = ex_ref[...]
    near = _dot(flag_ref[...], ex)
    live = _dot(half_ref[...], ex) == lh_ref[...]
    bias = jnp.where(live[:, None, :], base_ref[...] + near[:, None, :] * corr_ref[...], NEG)
    p_s = _softmax_lanes(_bqkt(q, ks_ref[...]) + bias)
    o_s = _bpvt(p_s, vs_ref[...])
    p_w = _softmax_lanes(_bqkt(q, kw_ref[...]) + wb_ref[...])
    o_w = _bpvt(p_w, vw_ref[...])
    g = g_ref[...]
    o_ref[...] = g[:, :, 0:1] * oc_ref[...] + g[:, :, 1:2] * o_s + g[:, :, 2:3] * o_w


def _attn_sample(q, gates, kcmp, vcmp, page_table, cache_sk, cache_sv, ks_new, vs_new,
                 win_k, win_v, rel_bias):
    n = q.shape[0]
    n_cmp = kcmp.shape[2]
    past = page_table.shape[1] * PAGE
    n_blk = past // SEL_BLOCK
    n_pick = N_SEL - 3
    nbq = n * N_KV
    nb = 16
    bt = _bucket_table(past + 1)
    qg = q.reshape(nbq, GQ, HEAD_DIM)
    per_row = lambda tab: jnp.tile(tab.reshape(-1, N_KV, GQ).transpose(1, 2, 0), (nb // N_KV, 1, 1))
    blk3 = lambda *s: pl.BlockSpec((nb,) + s, lambda i: (i,) + (0,) * len(s))
    const3 = lambda *s: pl.BlockSpec((nb,) + s, lambda i: (0,) * (len(s) + 1))

    order = np.concatenate([np.arange(0, n_cmp, 2), np.arange(1, n_cmp, 2)])
    d_c = past - (order * CMP_BLOCK + CMP_BLOCK - 1)
    cb = per_row(jnp.where((d_c >= 0)[:, None], rel_bias[bt[np.clip(d_c, 0, None)]], NEG))

    o_c, picks = pl.pallas_call(
        functools.partial(_sattn1_kernel, n_pick=n_pick),
        grid=(nbq // nb,),
        in_specs=[blk3(GQ, HEAD_DIM), blk3(n_cmp, HEAD_DIM), blk3(n_cmp, HEAD_DIM), const3(GQ, n_cmp)],
        out_specs=[blk3(GQ, HEAD_DIM), pl.BlockSpec((nb, n_cmp // 2), lambda i: (i, 0))],
        out_shape=[jax.ShapeDtypeStruct((nbq, GQ, HEAD_DIM), F32),
                   jax.ShapeDtypeStruct((nbq, n_cmp // 2), I32)],
        compiler_params=_cp("arbitrary"),
        name="nsa_sample_cmp",
    )(qg, kcmp.reshape(nbq, n_cmp, HEAD_DIM), vcmp.reshape(nbq, n_cmp, HEAD_DIM), cb)

    picks = picks[:, :n_pick].reshape(n, N_KV, n_pick)
    forced = jnp.broadcast_to(jnp.array([n_blk - 1, 0], I32), (n, N_KV, 2))
    sel = jnp.concatenate([forced, picks], axis=-1)
    n_slot = sel.shape[-1]
    page = jnp.take_along_axis(page_table, (sel // 2).reshape(n, -1), axis=1).reshape(sel.shape)
    head = jnp.broadcast_to(jnp.arange(N_KV, dtype=I32)[None, :, None], sel.shape)
    starts = jnp.stack([page, head], axis=-1)
    dnums = lax.GatherDimensionNumbers(offset_dims=(3, 4), collapsed_slice_dims=(0, 1),
                                       start_index_map=(0, 1))
    n_key = (n_slot + 1) * PAGE

    def gather(cache, new):
        got = lax.gather(cache.transpose(0, 2, 3, 1), starts, dnums,
                         slice_sizes=(1, 1, HEAD_DIM, PAGE),
                         mode=lax.GatherScatterMode.PROMISE_IN_BOUNDS)
        got = got.transpose(0, 1, 3, 2, 4).reshape(n, N_KV, HEAD_DIM, n_slot * PAGE)
        tail = jnp.zeros((n, N_KV, HEAD_DIM, PAGE), F32).at[:, :, :, 0].set(
            new.reshape(n, N_KV, HEAD_DIM))
        return jnp.concatenate([got, tail], axis=3).reshape(nbq, HEAD_DIM, n_key)

    k_sel = gather(cache_sk, ks_new)
    v_sel = gather(cache_sv, vs_new)
    nb2 = 8
    per_row2 = lambda tab: jnp.tile(tab.reshape(-1, N_KV, GQ).transpose(1, 2, 0), (nb2 // N_KV, 1, 1))
    blk2 = lambda *s: pl.BlockSpec((nb2,) + s, lambda i: (i,) + (0,) * len(s))
    const2 = lambda *s: pl.BlockSpec((nb2,) + s, lambda i: (0,) * (len(s) + 1))
    r = np.arange(SEL_BLOCK)
    both = lambda d: np.tile(d, PAGE // SEL_BLOCK)
    d_base = np.concatenate([both(SEL_BLOCK - r), both(past - r)] + [np.full(PAGE, past)] * n_pick
                            + [-np.arange(PAGE)])
    base = per_row2(jnp.where((d_base >= 0)[:, None], rel_bias[bt[np.clip(d_base, 0, past)]], NEG))
    d_near = both(np.clip(2 * SEL_BLOCK - r, 0, past))
    delta = rel_bias[bt[d_near]] - rel_bias[bt[past]][None, :]
    in_pick = np.zeros((n_slot + 1, 1, 1), np.float32)
    in_pick[2:n_slot] = 1.0
    corr = per_row2((in_pick * delta[None]).reshape(n_key, N_HEADS))
    pad_slot = lambda a: jnp.pad(a.astype(F32).reshape(nbq, n_slot), ((0, 0), (0, 1)))
    flag = pad_slot(sel == n_blk - 2)
    half = pad_slot(sel % 2)
    expand = jnp.asarray(np.kron(np.eye(n_slot + 1, dtype=np.float32), np.ones((1, PAGE), np.float32)))
    lane_half = jnp.asarray(((np.arange(n_key) // SEL_BLOCK) % 2).astype(np.float32).reshape(1, n_key))

    n_win = win_k.shape[1]
    d_w = n_win - 1 - np.arange(n_win)
    wb = per_row2(jnp.where((d_w < WINDOW)[:, None], rel_bias[bt[d_w]], NEG))
    wk = win_k.transpose(0, 2, 3, 1).reshape(nbq, HEAD_DIM, n_win)
    wv = win_v.transpose(0, 2, 3, 1).reshape(nbq, HEAD_DIM, n_win)
    slot_spec = pl.BlockSpec((nb2, n_slot + 1), lambda i: (i, 0))
    o = pl.pallas_call(
        _sattn2_kernel,
        grid=(nbq // nb2,),
        in_specs=[blk2(GQ, HEAD_DIM), blk2(HEAD_DIM, n_key), blk2(HEAD_DIM, n_key),
                  const2(GQ, n_key), const2(GQ, n_key), slot_spec, slot_spec,
                  pl.BlockSpec((n_slot + 1, n_key), lambda i: (0, 0)),
                  pl.BlockSpec((1, n_key), lambda i: (0, 0)),
                  blk2(HEAD_DIM, n_win), blk2(HEAD_DIM, n_win), const2(GQ, n_win),
                  blk2(GQ, HEAD_DIM), blk2(GQ, 3)],
        out_specs=blk2(GQ, HEAD_DIM),
        out_shape=jax.ShapeDtypeStruct((nbq, GQ, HEAD_DIM), F32),
        compiler_params=_cp("arbitrary"),
        name="nsa_sample_sel_win",
    )(qg, k_sel, v_sel, base, corr, flag, half, expand, lane_half, wk, wv, wb, o_c,
      gates.reshape(nbq, GQ, 3))
    return o.reshape(n, Q_W).astype(BF16)


def _gelu_tanh(x):
    return 0.5 * x * (1.0 + jnp.tanh(math.sqrt(2.0 / math.pi) * (x + 0.044715 * (x * x * x))))


def _post_kernel(x_ref, o_ref, y5_ref, gm_ref, shf_ref, scf_ref, gpost_ref, gpre_ref,
                 wglu_ref, wout_ref, wr_ref, br_ref, tri_ref, cin_ref,
                 x1_ref, h2_ref, h2p_ref, eidx_ref, wts_ref, pos_ref, cout_ref, cnt_sc):
    i = pl.program_id(0)
    tm = x_ref.shape[0]

    @pl.when(i == 0)
    def _():
        cnt_sc[...] = cin_ref[...]

    g5 = _gelu_tanh(y5_ref[...])
    g5 = g5 * _sigmoid(_dot(g5.astype(BF16), wglu_ref[...]))
    m = _dot(o_ref[...], wout_ref[:Q_W]) + _dot(g5.astype(BF16), wout_ref[Q_W:])
    x1 = x_ref[...] + gm_ref[0] * _rms(m, gpost_ref[...])
    h2 = _rms(x1, gpre_ref[...]) * (1.0 + scf_ref[0]) + shf_ref[0]
    x1_ref[...] = x1
    h2_ref[...] = h2.astype(BF16)
    h2p_ref[...] = _pack_bf16_pairs(h2)

    s_t = _sigmoid(_dot(h2, wr_ref[...], precision=HIGHEST)).T
    s_sel = s_t + br_ref[...]
    eio = lax.broadcasted_iota(I32, (N_EXPERTS, tm), 0).astype(F32)
    first = lambda hit, ids: jnp.min(jnp.where(hit, ids, 1e9), axis=0, keepdims=True)
    gscore = []
    for g in range(N_EGROUPS):
        xg = s_sel[g * EGROUP:(g + 1) * EGROUP]
        ig = lax.broadcasted_iota(I32, (EGROUP, tm), 0).astype(F32) + float(g * EGROUP)
        m1 = jnp.max(xg, axis=0, keepdims=True)
        m2 = jnp.max(jnp.where(ig == first(xg == m1, ig), -jnp.inf, xg), axis=0, keepdims=True)
        gscore.append(m1 + m2)
    gsc = jnp.concatenate(gscore, axis=0)
    gio = lax.broadcasted_iota(I32, (N_EGROUPS, tm), 0).astype(F32)
    gmask = jnp.zeros((N_EGROUPS, tm), F32)
    for _ in range(TOPK_GROUPS):
        hit = gio == first(gsc == jnp.max(gsc, axis=0, keepdims=True), gio)
        gmask = jnp.where(hit, 1.0, gmask)
        gsc = jnp.where(hit, -jnp.inf, gsc)
    work = jnp.concatenate(
        [jnp.where(gmask[g:g + 1] > 0.0, s_sel[g * EGROUP:(g + 1) * EGROUP], -jnp.inf)
         for g in range(N_EGROUPS)], axis=0)
    chosen = jnp.zeros((N_EXPERTS, tm), F32)
    ids, wsel = [], []
    for _ in range(TOP_K):
        ik = first(work == jnp.max(work, axis=0, keepdims=True), eio)
        hit = eio == ik
        ids.append(ik)
        wsel.append(jnp.sum(jnp.where(hit, s_t, 0.0), axis=0, keepdims=True))
        work = jnp.where(hit, -jnp.inf, work)
        chosen = jnp.where(hit, 1.0, chosen)
    rank = cnt_sc[:, 0:1] + _dot(chosen.astype(BF16), tri_ref[...])
    pos = [jnp.sum(jnp.where(eio == ik, rank, 0.0), axis=0, keepdims=True) for ik in ids]
    cnt_sc[...] = cnt_sc[...] + jnp.sum(chosen, axis=1, keepdims=True)
    wsum = wsel[0]
    for k in range(1, TOP_K):
        wsum = wsum + wsel[k]
    eidx_ref[...] = jnp.concatenate(ids, axis=0).astype(I32)
    pos_ref[...] = jnp.concatenate(pos, axis=0).astype(I32)
    wts_ref[...] = jnp.concatenate([w / wsum * ROUTED_SCALE for w in wsel], axis=0)
    cout_ref[...] = cnt_sc[...]


def _post_mix(x, o_att, y5, gm, shf, scf, g_post, g_pre, wglu, wout, w_router, b_router,
              cnt_in, tm, rows_per_mod):
    t = x.shape[0]
    r = gm.shape[1]
    per = rows_per_mod // tm
    mod = pl.BlockSpec((1, r, D_MODEL), lambda i: (i // per, 0, 0))
    full = lambda a: pl.BlockSpec(a.shape, lambda i: (0,) * a.ndim)
    vec = pl.BlockSpec((1, D_MODEL), lambda i: (0, 0))
    tri = jnp.asarray(np.triu(np.ones((tm, tm), np.float32), 1), BF16)
    brb = jnp.broadcast_to(b_router[:, None], (N_EXPERTS, tm)).astype(F32)
    route = pl.BlockSpec((TOP_K, tm), lambda i: (0, i))
    return pl.pallas_call(
        _post_kernel,
        grid=(t // tm,),
        in_specs=[pl.BlockSpec((tm, D_MODEL), lambda i: (i, 0)),
                  pl.BlockSpec((tm, Q_W), lambda i: (i, 0)),
                  pl.BlockSpec((tm, S5_W), lambda i: (i, 0)),
                  mod, mod, mod, vec, vec, full(wglu), full(wout), full(w_router), full(brb),
                  full(tri), full(cnt_in)],
        out_specs=[pl.BlockSpec((tm, D_MODEL), lambda i: (i, 0)),
                   pl.BlockSpec((tm, D_MODEL), lambda i: (i, 0)),
                   pl.BlockSpec((tm, D_MODEL // 2), lambda i: (i, 0)),
                   route, route, route, full(cnt_in)],
        out_shape=[jax.ShapeDtypeStruct((t, D_MODEL), F32),
                   jax.ShapeDtypeStruct((t, D_MODEL), BF16),
                   jax.ShapeDtypeStruct((t, D_MODEL // 2), U32),
                   jax.ShapeDtypeStruct((TOP_K, t), I32),
                   jax.ShapeDtypeStruct((TOP_K, t), F32),
                   jax.ShapeDtypeStruct((TOP_K, t), I32),
                   jax.ShapeDtypeStruct(cnt_in.shape, F32)],
        scratch_shapes=[pltpu.VMEM(cnt_in.shape, F32)],
        compiler_params=_cp("arbitrary"),
        name="post_mix_router",
    )(x, o_att, y5, gm, shf, scf, g_post.reshape(1, D_MODEL), g_pre.reshape(1, D_MODEL),
      wglu, wout, w_router, brb, tri, cnt_in)


def _dest_kernel(e_ref, p_ref, ps_ref, o_ref):
    tm = e_ref.shape[1]
    eio = lax.broadcasted_iota(I32, (N_EXPERTS, tm), 0)
    e = e_ref[...]
    start = ps_ref[...]
    rows = [jnp.sum(jnp.where(eio == e[k:k + 1], start, 0.0), axis=0, keepdims=True)
            for k in range(TOP_K)]
    o_ref[...] = jnp.concatenate(rows, axis=0).astype(I32) + p_ref[...]


def _dest(eidx, pos, pad_start, tm):
    t = eidx.shape[1]
    route = pl.BlockSpec((TOP_K, tm), lambda i: (0, i))
    start = jnp.broadcast_to(pad_start.astype(F32)[:, None], (N_EXPERTS, tm))
    return pl.pallas_call(
        _dest_kernel,
        grid=(t // tm,),
        in_specs=[route, route, pl.BlockSpec((N_EXPERTS, tm), lambda i: (0, 0))],
        out_specs=route,
        out_shape=jax.ShapeDtypeStruct((TOP_K, t), I32),
        compiler_params=_cp("arbitrary"),
        name="moe_dest",
    )(eidx, pos, start)


def _moe_kernel(be_ref, nb_ref, x_ref, wg_ref, wu_ref, wd_ref, y_ref, wg_sc, wu_sc, wd_sc):
    j = pl.program_id(0)

    @pl.when(j < nb_ref[0])
    def _():
        @pl.when((j == 0) | (be_ref[j] != be_ref[jnp.maximum(j - 1, 0)]))
        def _():
            wg_sc[...] = wg_ref[0].astype(BF16)
            wu_sc[...] = wu_ref[0].astype(BF16)
            wd_sc[...] = wd_ref[0].astype(BF16)

        for r0 in range(0, MOE_BLK, MOE_BLK // 2):
            rows = pl.ds(r0, MOE_BLK // 2)
            x = jnp.concatenate(_unpack_bf16_pairs(x_ref[rows, :]), axis=1).astype(BF16)
            a = _dot(x, wg_sc[...])
            b = _dot(x, wu_sc[...])
            y = _dot((_silu(a) * b).astype(BF16), wd_sc[...])
            y_ref[rows, :] = _pack_bf16_pairs(y)


def _moe(xs, blk_e, nb_used, w_g, w_u, w_d):
    n_slot = xs.shape[0]
    n_blk = n_slot // MOE_BLK
    last = lambda j, nb: jnp.maximum(jnp.minimum(j, nb[0] - 1), 0)
    row = lambda j, be, nb: (last(j, nb), 0)
    wsel = lambda j, be, nb: (be[last(j, nb)], 0, 0)
    return pl.pallas_call(
        _moe_kernel,
        grid_spec=pltpu.PrefetchScalarGridSpec(
            num_scalar_prefetch=2,
            grid=(n_blk,),
            in_specs=[pl.BlockSpec((MOE_BLK, D_MODEL // 2), row),
                      pl.BlockSpec((1, D_MODEL, D_EXPERT), wsel),
                      pl.BlockSpec((1, D_MODEL, D_EXPERT), wsel),
                      pl.BlockSpec((1, D_EXPERT, D_MODEL), wsel)],
            out_specs=pl.BlockSpec((MOE_BLK, D_MODEL // 2), row),
            scratch_shapes=[pltpu.VMEM((D_MODEL, D_EXPERT), BF16), pltpu.VMEM((D_MODEL, D_EXPERT), BF16),
                            pltpu.VMEM((D_EXPERT, D_MODEL), BF16)]),
        out_shape=jax.ShapeDtypeStruct((n_slot, D_MODEL // 2), U32),
        compiler_params=_cp("arbitrary"),
        name="moe_experts",
    )(blk_e, nb_used, xs, w_g, w_u, w_d)


def _fin_kernel(x1_ref, h2_ref, yg_ref, w_ref, gf_ref, gpost_ref, wsg_ref, wsu_ref, wsd_ref, o_ref):
    w = w_ref[...]
    f_lo, f_hi = None, None
    for k in range(TOP_K):
        lo, hi = _unpack_bf16_pairs(yg_ref[k])
        f_lo = w[:, k:k + 1] * lo if k == 0 else f_lo + w[:, k:k + 1] * lo
        f_hi = w[:, k:k + 1] * hi if k == 0 else f_hi + w[:, k:k + 1] * hi
    f = jnp.concatenate([f_lo, f_hi], axis=1)
    hb = h2_ref[...]
    sh = (_silu(_dot(hb, wsg_ref[...])) * _dot(hb, wsu_ref[...])).astype(BF16)
    f = f + _dot(sh, wsd_ref[...])
    o_ref[...] = x1_ref[...] + gf_ref[0] * _rms(f, gpost_ref[...])


def _final(x1, h2, yg, wts, gf, g_post, wsg, wsu, wsd, tm, rows_per_mod):
    t = x1.shape[0]
    r = gf.shape[1]
    per = rows_per_mod // tm
    full = lambda a: pl.BlockSpec(a.shape, lambda i: (0,) * a.ndim)
    return pl.pallas_call(
        _fin_kernel,
        grid=(t // tm,),
        in_specs=[pl.BlockSpec((tm, D_MODEL), lambda i: (i, 0)),
                  pl.BlockSpec((tm, D_MODEL), lambda i: (i, 0)),
                  pl.BlockSpec((TOP_K, tm, D_MODEL // 2), lambda i: (0, i, 0)),
                  pl.BlockSpec((tm, TOP_K), lambda i: (i, 0)),
                  pl.BlockSpec((1, r, D_MODEL), lambda i: (i // per, 0, 0)),
                  pl.BlockSpec((1, D_MODEL), lambda i: (0, 0)),
                  full(wsg), full(wsu), full(wsd)],
        out_specs=pl.BlockSpec((tm, D_MODEL), lambda i: (i, 0)),
        out_shape=jax.ShapeDtypeStruct((t, D_MODEL), F32),
        compiler_params=_cp("arbitrary"),
        name="moe_combine_final",
    )(x1, h2, yg, wts, gf, g_post.reshape(1, D_MODEL), wsg, wsu, wsd)


def _even_odd(a, axis):
    n = a.shape[axis]
    order = np.concatenate([np.arange(0, n, 2), np.arange(1, n, 2)])
    return jnp.take(a, order, axis=axis)


def _layer(xp, xs, c_prompt, c_sample, page_table, ck_c, cv_c, ck_s, cv_s, cw_k, cw_v, st_re, st_im,
           w_ada, b_ada, g_pre_mix, g_post_mix, g_pre_ffn, g_post_ffn, w_in, pe, wk1, wk2, wv1, wv2,
           rel_bias, lam_re, lam_im, log_dt, b_re, b_im, c_re, c_im, d_skip, w_glu, w_out,
           w_router, b_router, w_eg, w_eu, w_ed, w_sg, w_su, w_sd):
    batch, seq, _ = xp.shape
    n_dec = xs.shape[0]
    tp = batch * seq
    x_p = xp.reshape(tp, D_MODEL)
    x_s = xs.reshape(n_dec, D_MODEL)

    n_c = batch + n_dec
    n_pad = -(-n_c // 8) * 8
    c_all = jnp.pad(jnp.concatenate([c_prompt, c_sample], axis=0), ((0, n_pad - n_c), (0, 0)))
    mod = _adaln(c_all, w_ada, b_ada)
    mod_p = [m.reshape(batch, 1, D_MODEL) for m in jnp.split(mod[:batch], 6, axis=-1)]
    mod_s = [m.reshape(1, n_dec, D_MODEL) for m in jnp.split(mod[batch:n_c], 6, axis=-1)]

    wn, wt = _pre_weights(w_in)
    (kct_p, vct_p, kst_p, vst_p, kwt_p, vwt_p, kc_p, vc_p, u_p, ksb, kwb, qt_p, vst, vsn, vwt, gt_p) = \
        _pre_mix_prompt(x_p, mod_p[1], mod_p[0], g_pre_mix, wn, wt, 512, batch, seq)
    (kc_s, vc_s, ks_s, vs_s, kw_s, vw_s, u_s, qt_s, gt_s) = _pre_mix_sample(
        x_s, mod_s[1], mod_s[0], g_pre_mix, wn, wt)

    n_cmp_p = seq // CMP_BLOCK
    kcmp_p = _compress(kc_p.reshape(batch * n_cmp_p, CMP_BLOCK * KV_W), *_cmp_weights(pe, wk1, wk2))
    vcmp_p = _compress(vc_p.reshape(batch * n_cmp_p, CMP_BLOCK * KV_W), *_cmp_weights(pe, wv1, wv2))
    kcmp_p = _even_odd(kcmp_p.reshape(batch, n_cmp_p, N_KV, HEAD_DIM), 1).transpose(0, 2, 1, 3)
    vcmp_p = _even_odd(vcmp_p.reshape(batch, n_cmp_p, N_KV, HEAD_DIM), 1).transpose(0, 2, 3, 1)
    wtab, ctab = _bias_tables(rel_bias)
    o_p = _attn_prompt(qt_p, gt_p, kcmp_p.astype(BF16), vcmp_p.astype(BF16), ksb, vst, vsn, kwb, vwt,
                       wtab, ctab, batch, seq)

    n_pages = page_table.shape[1]
    n_cmp_s = n_pages * CMP_PER_PAGE
    pages = lambda cache: cache.transpose(0, 2, 3, 1)[page_table].reshape(
        n_dec * n_pages * N_KV * HEAD_DIM, PAGE)
    kcmp_s, vcmp_s = _compress_pages(pages(ck_c), pages(cv_c), _cmp_pages_weights(pe, wk1, wk2),
                                     _cmp_pages_weights(pe, wv1, wv2), n_dec, n_pages)
    by_head = lambda a: _even_odd(
        a.reshape(n_dec, n_pages, N_KV, CMP_PER_PAGE, HEAD_DIM).transpose(0, 2, 1, 3, 4).reshape(
            n_dec, N_KV, n_cmp_s, HEAD_DIM), 2)
    win_k = jnp.concatenate([cw_k[:, 1:], kw_s.reshape(n_dec, 1, N_KV, HEAD_DIM)], axis=1)
    win_v = jnp.concatenate([cw_v[:, 1:], vw_s.reshape(n_dec, 1, N_KV, HEAD_DIM)], axis=1)
    q_s = qt_s.T.reshape(n_dec, N_HEADS, HEAD_DIM)
    gates_s = gt_s[:, :3 * GQ].reshape(N_KV, 3, GQ, n_dec).transpose(3, 0, 2, 1).reshape(n_dec, N_HEADS, 3)
    o_s = _attn_sample(q_s, gates_s, by_head(kcmp_s).astype(BF16), by_head(vcmp_s).astype(BF16),
                       page_table, ck_s, cv_s, ks_s, vs_s, win_k, win_v, rel_bias)

    y5_p, s5r_p, s5i_p = _s5_prompt(
        u_p, _s5_prep(lam_re, lam_im, log_dt, b_re, b_im, c_re, c_im, d_skip, S5_CHUNK), batch, seq)
    y5_s, s5r_s, s5i_s = _s5_sample(
        u_s, _s5_prep(lam_re, lam_im, log_dt, b_re, b_im, c_re, c_im, d_skip, 1), st_re, st_im)

    wglu_b, wout_b = w_glu.astype(BF16), w_out.astype(BF16)
    cnt0 = jnp.zeros((N_EXPERTS, 128), F32)
    tm_q = 256
    x1_p, h2_p, h2p_p, e_p, wt_p, pos_p, cnt1 = _post_mix(
        x_p, o_p, y5_p, mod_p[2], mod_p[3], mod_p[4], g_post_mix, g_pre_ffn, wglu_b, wout_b,
        w_router, b_router, cnt0, tm_q, seq)
    x1_s, h2_s, h2p_s, e_s, wt_s, pos_s, cnt2 = _post_mix(
        x_s, o_s, y5_s, mod_s[2], mod_s[3], mod_s[4], g_post_mix, g_pre_ffn, wglu_b, wout_b,
        w_router, b_router, cnt1, n_dec, n_dec)

    t_all = tp + n_dec
    counts = cnt2[:, 0].astype(I32)
    padded = (counts + MOE_BLK - 1) // MOE_BLK * MOE_BLK
    pad_end = jnp.cumsum(padded)
    dest_p = _dest(e_p, pos_p, pad_end - padded, 1024)
    dest_s = _dest(e_s, pos_s, pad_end - padded, n_dec)
    dest = jnp.concatenate([dest_p, dest_s], axis=1)
    n_blk = -(-(t_all * TOP_K) // MOE_BLK) + N_EXPERTS
    n_slot = n_blk * MOE_BLK
    tok = jnp.broadcast_to(jnp.arange(t_all, dtype=I32)[None], (TOP_K, t_all))
    rows = (jnp.arange(n_slot, dtype=I32) % t_all).at[dest.reshape(-1)].set(
        tok.reshape(-1), unique_indices=True)
    xs_rows = jnp.concatenate([h2p_p, h2p_s], axis=0)[rows]
    blk_e = jnp.minimum(jnp.searchsorted(pad_end, jnp.arange(n_blk, dtype=I32) * MOE_BLK, side='right'),
                        N_EXPERTS - 1).astype(I32)
    nb_used = (pad_end[-1:] // MOE_BLK).astype(I32)
    ys = _moe(xs_rows, blk_e, nb_used, w_eg, w_eu, w_ed)
    yg_p = ys[dest_p]
    yg_s = ys[dest_s]

    wsg, wsu, wsd = w_sg.astype(BF16), w_su.astype(BF16), w_sd.astype(BF16)
    out_p = _final(x1_p, h2_p, yg_p, wt_p.T, mod_p[5], g_post_ffn, wsg, wsu, wsd, tm_q, seq)
    out_s = _final(x1_s, h2_s, yg_s, wt_s.T, mod_s[5], g_post_ffn, wsg, wsu, wsd, n_dec, n_dec)

    n_win = min(WINDOW, seq)
    rows_p = lambda a: a.transpose(0, 3, 1, 2)
    kv5 = lambda a: a.reshape(n_dec, 1, N_KV, HEAD_DIM)
    st_p = (rows_p(kct_p), rows_p(vct_p), rows_p(kst_p), rows_p(vst_p),
            rows_p(kwt_p)[:, seq - n_win:], rows_p(vwt_p)[:, seq - n_win:], s5r_p, s5i_p)
    st_s = (kv5(kc_s), kv5(vc_s), kv5(ks_s), kv5(vs_s), win_k, win_v, s5r_s, s5i_s)
    return out_p.reshape(batch, seq, D_MODEL), out_s.reshape(n_dec, 1, D_MODEL), st_p, st_s


def kernel(x_prompt, x_sample, c_prompt, c_sample, page_table, cache_cmp_k, cache_cmp_v, cache_sel_k, cache_sel_v, cache_win_k, cache_win_v, state_s5_re, state_s5_im, w_ada, b_ada, g_pre_mix, g_post_mix, g_pre_ffn, g_post_ffn, w_in, pe_cmp, w_cmp_k1, w_cmp_k2, w_cmp_v1, w_cmp_v2, rel_bias, lam_re, lam_im, log_dt, b_re, b_im, c_re, c_im, d_skip, w_glu, w_out, w_router, b_router, w_exp_gate, w_exp_up, w_exp_down, w_sh_gate, w_sh_up, w_sh_down):
    depth = w_in.shape[0]
    xp, xs = x_prompt, x_sample
    p_states, s_states = [], []
    for l in range(depth):
        xp, xs, st_p, st_s = _layer(
            xp, xs, c_prompt, c_sample, page_table, cache_cmp_k[l], cache_cmp_v[l], cache_sel_k[l],
            cache_sel_v[l], cache_win_k[l], cache_win_v[l], state_s5_re[l], state_s5_im[l],
            w_ada[l], b_ada[l], g_pre_mix[l], g_post_mix[l], g_pre_ffn[l], g_post_ffn[l], w_in[l],
            pe_cmp[l], w_cmp_k1[l], w_cmp_k2[l], w_cmp_v1[l], w_cmp_v2[l], rel_bias,
            lam_re[l], lam_im[l], log_dt[l], b_re[l], b_im[l], c_re[l], c_im[l], d_skip[l],
            w_glu[l], w_out[l], w_router[l], b_router[l], w_exp_gate[l], w_exp_up[l], w_exp_down[l],
            w_sh_gate[l], w_sh_up[l], w_sh_down[l])
        p_states.append(st_p)
        s_states.append(st_s)
    p_st = tuple(jnp.stack(a) for a in zip(*p_states))
    s_st = tuple(jnp.stack(a) for a in zip(*s_states))
    return (xp, xs) + p_st + s_st
```

```python
import functools
import math

import numpy as np
import jax
import jax.numpy as jnp
from jax import lax
from jax.experimental import pallas as pl
from jax.experimental.pallas import tpu as pltpu

F32 = jnp.float32
BF16 = jnp.bfloat16
I32 = jnp.int32
U32 = jnp.uint32

D_MODEL = 1024
N_HEADS = 8
HEAD_DIM = 64
N_KV = 2
GQ = N_HEADS // N_KV
Q_W = N_HEADS * HEAD_DIM
KV_W = N_KV * HEAD_DIM
S5_W = D_MODEL - Q_W
S5_H = 16
S5_G = S5_W // S5_H
S5_P = 64
GATE_OFF = Q_W + 6 * KV_W
U_OFF = GATE_OFF + 3 * N_HEADS
CMP_BLOCK = 32
SEL_BLOCK = 64
N_SEL = 16
WINDOW = 512
NUM_BUCKETS = 32
REL_MAX_DIST = 128
N_EXPERTS = 256
TOP_K = 8
N_EGROUPS = 8
TOPK_GROUPS = 4
EGROUP = N_EXPERTS // N_EGROUPS
D_EXPERT = 256
ROUTED_SCALE = 2.5
EPS = 1e-6
SCALE = HEAD_DIM ** -0.5
PAGE = 128
CMP_PER_PAGE = PAGE // CMP_BLOCK

QT = 128
VT = 2 * QT
N_DELTA = WINDOW // QT + 1
V_ROWS = HEAD_DIM + 16
LOG2E = math.log2(math.e)
S5_CHUNK = 32
S5_ROWS = 8
MOE_BLK = 512
NEG = -2e30
M_INIT = -1e30
VMEM_LIMIT = 56 * 1024 * 1024
WT_ROWS = Q_W + 6 * KV_W + 32
HIGHEST = lax.Precision.HIGHEST


def _cp(*sem):
    return pltpu.CompilerParams(dimension_semantics=sem, vmem_limit_bytes=VMEM_LIMIT)


def _dot(a, b, precision=None):
    return jnp.dot(a, b, preferred_element_type=F32, precision=precision)


def _dot_nt(a, b):
    return lax.dot_general(a, b, (((1,), (1,)), ((), ())), preferred_element_type=F32)


def _sigmoid(x):
    return 1.0 / (1.0 + jnp.exp(-x))


def _silu(x):
    return x * _sigmoid(x)


def _rms(x, g):
    return x * lax.rsqrt(jnp.mean(x * x, axis=-1, keepdims=True) + EPS) * g


def _pack_bf16_pairs(x):
    c = x.shape[1] // 2
    bits = pltpu.bitcast(x.astype(BF16).astype(F32), U32)
    return (bits[:, :c] >> 16) | bits[:, c:]


def _unpack_bf16_pairs(xp):
    return pltpu.bitcast(xp << 16, F32), pltpu.bitcast(xp & jnp.uint32(0xFFFF0000), F32)


def _bucket_table(n):
    d = np.arange(n)
    exact = NUM_BUCKETS // 2
    nf = np.maximum(d, 1).astype(np.float32)
    large = exact + (np.log(nf / np.float32(exact)) / np.float32(math.log(REL_MAX_DIST / exact))
                     * np.float32(NUM_BUCKETS - exact)).astype(np.int32)
    return np.where(d < exact, d, np.minimum(large, NUM_BUCKETS - 1)).astype(np.int32)


def _ada_kernel(c_ref, w_ref, b_ref, o_ref):
    a = _silu(c_ref[...]).astype(BF16)
    o_ref[...] = _dot(a, w_ref[...].astype(BF16)) + b_ref[...]


def _adaln(c, w_ada, b_ada):
    n, d = c.shape
    n_out = w_ada.shape[1]
    tn = 1024
    return pl.pallas_call(
        _ada_kernel,
        grid=(n_out // tn,),
        in_specs=[pl.BlockSpec((n, d), lambda j: (0, 0)),
                  pl.BlockSpec((d, tn), lambda j: (0, j)),
                  pl.BlockSpec((1, tn), lambda j: (0, j))],
        out_specs=pl.BlockSpec((n, tn), lambda j: (0, j)),
        out_shape=jax.ShapeDtypeStruct((n, n_out), F32),
        compiler_params=_cp("arbitrary"),
        name="adaln",
    )(c, w_ada, b_ada.reshape(1, n_out))


def _pre_project(x_ref, sc_ref, sh_ref, g_ref, wn_ref, wt_ref):
    h = _rms(x_ref[...], g_ref[...]) * (1.0 + sc_ref[0]) + sh_ref[0]
    hb = h.astype(BF16)
    return _dot(hb, wn_ref[...]), _dot_nt(wt_ref[...], hb)


def _pre_prompt_kernel(x_ref, sc_ref, sh_ref, g_ref, wn_ref, wt_ref,
                       kct_ref, vct_ref, kst_ref, vst32_ref, kwt_ref, vwt32_ref, kc_ref, vc_ref, u_ref,
                       ksb_ref, kwb_ref, qt_ref, vst_ref, vsn_ref, vwt_ref, gt_ref):
    tm = x_ref.shape[0]
    zn, zt = _pre_project(x_ref, sc_ref, sh_ref, g_ref, wn_ref, wt_ref)
    trans = lambda j, g: zt[Q_W + j * KV_W + g * HEAD_DIM:Q_W + j * KV_W + (g + 1) * HEAD_DIM]
    for j, ref in enumerate((kct_ref, vct_ref, kst_ref, vst32_ref, kwt_ref, vwt32_ref)):
        for g in range(N_KV):
            ref[0, g] = trans(j, g)
    kc_ref[...] = zn[:, 0:KV_W]
    vc_ref[...] = zn[:, KV_W:2 * KV_W]
    u_ref[...] = zn[:, 6 * KV_W:]
    for g in range(N_KV):
        ksb_ref[g] = zn[:, 2 * KV_W + g * HEAD_DIM:2 * KV_W + (g + 1) * HEAD_DIM].astype(BF16)
        kwb_ref[g] = zn[:, 4 * KV_W + g * HEAD_DIM:4 * KV_W + (g + 1) * HEAD_DIM].astype(BF16)
        ones = jnp.where(lax.broadcasted_iota(I32, (V_ROWS - HEAD_DIM, tm), 0) == 0, 1.0, 0.0)
        vs_t = jnp.concatenate([trans(3, g), ones], axis=0).astype(BF16)
        vw_t = jnp.concatenate([trans(5, g), ones], axis=0).astype(BF16)
        for c in range(tm // VT):
            vst_ref[g, c] = vs_t[:, c * VT:(c + 1) * VT]
        for c in range(tm // QT):
            vsn_ref[g, c] = vs_t[:, c * QT:(c + 1) * QT]
            vwt_ref[g, c] = vw_t[:, c * QT:(c + 1) * QT]
        r2 = Q_W + 6 * KV_W + g * 16
        gt_ref[g] = _sigmoid(zt[r2:r2 + 16])
    qt_ref[...] = zt[:Q_W].astype(BF16)


def _pre_sample_kernel(x_ref, sc_ref, sh_ref, g_ref, wn_ref, wt_ref,
                       kc_ref, vc_ref, ks_ref, vs_ref, kw_ref, vw_ref, u_ref, qt_ref, gt_ref):
    zn, zt = _pre_project(x_ref, sc_ref, sh_ref, g_ref, wn_ref, wt_ref)
    for j, ref in enumerate((kc_ref, vc_ref, ks_ref, vs_ref, kw_ref, vw_ref)):
        ref[...] = zn[:, j * KV_W:(j + 1) * KV_W]
    u_ref[...] = zn[:, 6 * KV_W:]
    qt_ref[...] = zt[:Q_W].astype(BF16)
    for g in range(N_KV):
        r2 = Q_W + 6 * KV_W + g * 16
        gt_ref[g] = _sigmoid(zt[r2:r2 + 16])


def _pre_weights(w_in):
    wn = jnp.concatenate([w_in[:, Q_W:GATE_OFF], w_in[:, U_OFF:]], axis=1).astype(BF16)
    gate_cols = []
    for g in range(N_KV):
        for j in range(3):
            for hh in range(GQ):
                gate_cols.append(GATE_OFF + (g * GQ + hh) * 3 + j)
        gate_cols.extend([GATE_OFF] * 4)
    wt = jnp.concatenate([
        w_in[:, :Q_W] * (SCALE * LOG2E),
        w_in[:, Q_W:GATE_OFF],
        w_in[:, np.array(gate_cols)],
    ], axis=1).T.astype(BF16)
    return wn, wt


def _pre_in_specs(tm, r, per, wn, wt):
    return [pl.BlockSpec((tm, D_MODEL), lambda i: (i, 0)),
            pl.BlockSpec((1, r, D_MODEL), lambda i: (i // per, 0, 0)),
            pl.BlockSpec((1, r, D_MODEL), lambda i: (i // per, 0, 0)),
            pl.BlockSpec((1, D_MODEL), lambda i: (0, 0)),
            pl.BlockSpec(wn.shape, lambda i: (0, 0)),
            pl.BlockSpec(wt.shape, lambda i: (0, 0))]


def _pre_mix_prompt(x, sc, sh, g_pre, wn, wt, tm, batch, seq):
    t = x.shape[0]
    per = seq // tm
    f = lambda shape: jax.ShapeDtypeStruct(shape, F32)
    b = lambda shape: jax.ShapeDtypeStruct(shape, BF16)
    tr_spec = pl.BlockSpec((1, N_KV, HEAD_DIM, tm), lambda i: (i // per, 0, 0, i % per))
    kv_spec = pl.BlockSpec((tm, KV_W), lambda i: (i, 0))
    return pl.pallas_call(
        _pre_prompt_kernel,
        grid=(t // tm,),
        in_specs=_pre_in_specs(tm, 1, per, wn, wt),
        out_specs=[tr_spec] * 6 + [
            kv_spec, kv_spec,
            pl.BlockSpec((tm, S5_W), lambda i: (i, 0)),
            pl.BlockSpec((N_KV, tm, HEAD_DIM), lambda i: (0, i, 0)),
            pl.BlockSpec((N_KV, tm, HEAD_DIM), lambda i: (0, i, 0)),
            pl.BlockSpec((Q_W, tm), lambda i: (0, i)),
            pl.BlockSpec((N_KV, tm // VT, V_ROWS, VT), lambda i: (0, i, 0, 0)),
            pl.BlockSpec((N_KV, tm // QT, V_ROWS, QT), lambda i: (0, i, 0, 0)),
            pl.BlockSpec((N_KV, tm // QT, V_ROWS, QT), lambda i: (0, i, 0, 0)),
            pl.BlockSpec((N_KV, 16, tm), lambda i: (0, 0, i))],
        out_shape=[f((batch, N_KV, HEAD_DIM, seq))] * 6 + [
            f((t, KV_W)), f((t, KV_W)),
            f((t, S5_W)), b((N_KV, t, HEAD_DIM)), b((N_KV, t, HEAD_DIM)), b((Q_W, t)),
            b((N_KV, t // VT, V_ROWS, VT)), b((N_KV, t // QT, V_ROWS, QT)),
            b((N_KV, t // QT, V_ROWS, QT)), f((N_KV, 16, t))],
        compiler_params=_cp("arbitrary"),
        name="pre_mix_prompt",
    )(x, sc, sh, g_pre.reshape(1, D_MODEL), wn, wt)


def _pre_mix_sample(x, sc, sh, g_pre, wn, wt):
    t = x.shape[0]
    f = lambda shape: jax.ShapeDtypeStruct(shape, F32)
    kv_spec = pl.BlockSpec((t, KV_W), lambda i: (i, 0))
    return pl.pallas_call(
        _pre_sample_kernel,
        grid=(1,),
        in_specs=_pre_in_specs(t, t, 1, wn, wt),
        out_specs=[kv_spec] * 6 + [pl.BlockSpec((t, S5_W), lambda i: (i, 0)),
                                   pl.BlockSpec((Q_W, t), lambda i: (0, i)),
                                   pl.BlockSpec((N_KV, 16, t), lambda i: (0, 0, i))],
        out_shape=[f((t, KV_W))] * 6 + [f((t, S5_W)), jax.ShapeDtypeStruct((Q_W, t), BF16),
                                        f((N_KV, 16, t))],
        compiler_params=_cp("arbitrary"),
        name="pre_mix_sample",
    )(x, sc, sh, g_pre.reshape(1, D_MODEL), wn, wt)


def _cmp_kernel(x_ref, pe_ref, w1_ref, w2_ref, o_ref):
    xb = (x_ref[...] + pe_ref[...]).astype(BF16)
    hid = _silu(_dot(xb, w1_ref[...]))
    o_ref[...] = _dot(hid.astype(BF16), w2_ref[...])


def _cmp_weights(pe, w1, w2):
    eye = jnp.eye(N_KV, dtype=F32)
    w1b = jnp.einsum('jde,gh->jgdhe', w1, eye).reshape(CMP_BLOCK * KV_W, KV_W).astype(BF16)
    w2b = jnp.einsum('ef,gh->gehf', w2, eye).reshape(KV_W, KV_W).astype(BF16)
    peb = jnp.broadcast_to(pe[:, None, :], (CMP_BLOCK, N_KV, HEAD_DIM)).reshape(1, CMP_BLOCK * KV_W)
    return peb, w1b, w2b


def _compress(x, peb, w1b, w2b):
    r, k = x.shape
    tr = min(256, r)
    return pl.pallas_call(
        _cmp_kernel,
        grid=(r // tr,),
        in_specs=[pl.BlockSpec((tr, k), lambda i: (i, 0)),
                  pl.BlockSpec((1, k), lambda i: (0, 0)),
                  pl.BlockSpec((k, KV_W), lambda i: (0, 0)),
                  pl.BlockSpec((KV_W, KV_W), lambda i: (0, 0))],
        out_specs=pl.BlockSpec((tr, KV_W), lambda i: (i, 0)),
        out_shape=jax.ShapeDtypeStruct((r, KV_W), F32),
        compiler_params=_cp("arbitrary"),
        name="compress",
    )(x, peb, w1b, w2b)


def _cmp_pages_kernel(xk_ref, xv_ref, w1k_ref, w1v_ref, bk_ref, bv_ref, w2k_ref, w2v_ref,
                      ok_ref, ov_ref):
    n_rows = ok_ref.shape[0]
    for x_ref, w1_ref, b_ref, w2_ref, o_ref in ((xk_ref, w1k_ref, bk_ref, w2k_ref, ok_ref),
                                               (xv_ref, w1v_ref, bv_ref, w2v_ref, ov_ref)):
        acc = jnp.zeros((n_rows, CMP_PER_PAGE * HEAD_DIM), F32)
        for dd in range(HEAD_DIM // 2):
            r0 = x_ref[pl.ds(2 * dd, n_rows, stride=HEAD_DIM), :]
            r1 = x_ref[pl.ds(2 * dd + 1, n_rows, stride=HEAD_DIM), :]
            acc = acc + _dot(jnp.concatenate([r0, r1], axis=1).astype(BF16), w1_ref[dd])
        hid = _silu(acc + b_ref[...])
        o_ref[...] = _dot(hid.astype(BF16), w2_ref[...])


def _cmp_pages_weights(pe, w1, w2):
    eye = jnp.eye(CMP_PER_PAGE, dtype=F32)
    w1t = jnp.einsum('jde,mn->dmjne', w1, eye).reshape(HEAD_DIM // 2, 2 * PAGE, CMP_PER_PAGE * HEAD_DIM)
    bias = jnp.einsum('jd,jde->e', pe, w1, precision=HIGHEST)
    bias = jnp.tile(bias, CMP_PER_PAGE).reshape(1, CMP_PER_PAGE * HEAD_DIM)
    w2t = jnp.einsum('ef,mn->menf', w2, eye).reshape(CMP_PER_PAGE * HEAD_DIM, CMP_PER_PAGE * HEAD_DIM)
    return w1t.astype(BF16), bias, w2t.astype(BF16)


def _compress_pages(xk, xv, wk, wv, n_seq, n_pages):
    rows_in = n_pages * N_KV * HEAD_DIM
    rows_out = n_pages * N_KV
    wcols = CMP_PER_PAGE * HEAD_DIM
    full = lambda a: pl.BlockSpec(a.shape, lambda i: (0,) * a.ndim)
    x_spec = pl.BlockSpec((rows_in, PAGE), lambda i: (i, 0))
    o_spec = pl.BlockSpec((rows_out, wcols), lambda i: (i, 0))
    o_shape = jax.ShapeDtypeStruct((n_seq * rows_out, wcols), F32)
    return pl.pallas_call(
        _cmp_pages_kernel,
        grid=(n_seq,),
        in_specs=[x_spec, x_spec, full(wk[0]), full(wv[0]), full(wk[1]), full(wv[1]),
                  full(wk[2]), full(wv[2])],
        out_specs=[o_spec, o_spec],
        out_shape=[o_shape, o_shape],
        compiler_params=_cp("arbitrary"),
        name="compress_pages",
    )(xk, xv, wk[0], wv[0], wk[1], wv[1], wk[2], wv[2])


def _attn_kernel(q_ref, g_ref, kc_ref, vct_ref, ks_ref, vst_ref, vsn_ref, kw_ref, vwt_ref,
                 ctab_ref, wtab_ref, o_ref, mb_ref, mbf_ref, s_sc, p_sc, *, n_cmp):
    i = pl.program_id(2)
    lanes = GQ * QT
    qt = q_ref[...]
    q4 = jnp.concatenate([qt[hh * HEAD_DIM:(hh + 1) * HEAD_DIM] for hh in range(GQ)], axis=1)
    gates = g_ref[0]
    grow = lambda j: jnp.concatenate([gates[j * GQ + hh:j * GQ + hh + 1] for hh in range(GQ)], axis=1)

    def online(carry, s, v_t):
        m, acc = carry
        m_new = jnp.maximum(m, jnp.max(s, axis=0, keepdims=True))
        p = jnp.exp2(s - m_new)
        return m_new, jnp.exp2(m - m_new) * acc + _dot(v_t, p.astype(BF16))

    init = (jnp.full((1, lanes), M_INIT, F32), jnp.zeros((V_ROWS, lanes), F32))
    finish = lambda carry: carry[1][:HEAD_DIM] * (1.0 / jnp.maximum(carry[1][HEAD_DIM:HEAD_DIM + 1], 1e-30))
    block_rows = lambda ref, first, n, extra=0.0: jnp.concatenate(
        [jnp.broadcast_to(ref[pl.ds(first + c, 1), :] + extra, (SEL_BLOCK, lanes)) for c in range(n)],
        axis=0)

    def near_tiles(carry, k_ref, vt_ref, deltas, live, masked):
        ks, vs, bs = [], [], []
        for dl in deltas:
            kt = i - dl
            ktc = jnp.maximum(kt, 0)
            ks.append(k_ref[0, pl.ds(pl.multiple_of(ktc * QT, QT), QT), :])
            vs.append(vt_ref[0, ktc])
            bias = wtab_ref[0, jnp.where(live(dl, kt), dl, N_DELTA)]
            if masked:
                bias = bias + block_rows(mb_ref, (QT // SEL_BLOCK) * ktc, QT // SEL_BLOCK)
            bs.append(bias)
        s = _dot(jnp.concatenate(ks, axis=0), q4) + jnp.concatenate(bs, axis=0)
        return online(carry, s, jnp.concatenate(vs, axis=1))

    o_w = finish(near_tiles(init, kw_ref, vwt_ref, list(range(N_DELTA - 1, -1, -1)),
                            lambda dl, kt: kt >= 0, False))

    half = n_cmp // 2
    sc = _dot(kc_ref[0, 0], q4)
    rho = lax.broadcasted_iota(I32, (n_cmp, 16), 0)
    col = lax.broadcasted_iota(I32, (n_cmp, 16), 1)
    blk_n = 2 * jnp.where(rho >= half, rho - half, rho) + jnp.where(rho >= half, 1, 0)
    rel = blk_n - (4 * i - 4)
    want = jnp.where(rel < 0, 8, jnp.where(rel > 7, 9, rel))
    place = jnp.where(col == want, 1.0, 0.0).astype(BF16)
    sc = sc + _dot(place, ctab_ref[0, 0]) + _dot(place, ctab_ref[0, 1])
    mc = jnp.maximum(jnp.max(sc, axis=0, keepdims=True), M_INIT)
    ec = jnp.exp2(sc - mc)
    pc = ec * (1.0 / jnp.maximum(jnp.sum(ec, axis=0, keepdims=True), 1e-30))
    o_cw = grow(0) * _dot(vct_ref[0, 0], pc.astype(BF16)) + grow(2) * o_w

    ps = pc[:half] + pc[half:]
    imp = ps[:, 0:QT]
    for hh in range(1, GQ):
        imp = imp + ps[:, hh * QT:(hh + 1) * QT]
    n_blk = half
    blk = lax.broadcasted_iota(I32, (n_blk, QT), 0)
    tq = i * QT + lax.broadcasted_iota(I32, (n_blk, QT), 1)
    cur = lax.shift_right_logical(tq, 6)
    valid = blk * SEL_BLOCK <= tq
    forced = (blk == 0) | (blk == cur) | (blk == cur - 1)
    blkf = blk.astype(F32)

    def pick(_, carry):
        work, mb = carry
        mx = jnp.max(work, axis=0, keepdims=True)
        idx = jnp.min(jnp.where(work == mx, blkf, 1e9), axis=0, keepdims=True)
        hit = blkf == idx
        return jnp.where(hit, -jnp.inf, work), jnp.where(hit, 0.0, mb)

    _, mb = lax.fori_loop(0, min(N_SEL, n_blk) - 3, pick,
                          (jnp.where(valid, jnp.where(forced, -jnp.inf, imp), -jnp.inf),
                           jnp.where(valid, jnp.where(forced, 0.0, NEG), NEG)))
    for hh in range(GQ):
        mb_ref[:, hh * QT:(hh + 1) * QT] = mb
    mbf_ref[...] = mb_ref[...] + wtab_ref[0, 2, 0:1, :]

    n_far = jnp.maximum(i - 1, 0) // 2
    last_pair = ks_ref.shape[1] // VT - 1

    def pair_scores(t):
        ta = jnp.minimum(t, last_pair)
        k_t = ks_ref[0, pl.ds(pl.multiple_of(ta * VT, VT), VT), :]
        return _dot(k_t, q4) + block_rows(mbf_ref, (VT // SEL_BLOCK) * ta, VT // SEL_BLOCK,
                                          jnp.where(t < n_far, 0.0, NEG))

    def far_step(t, par, carry):
        m, acc, alpha_prev = carry
        pv = _dot(vst_ref[0, jnp.clip(t - 2, 0, last_pair)], p_sc[par])
        s = s_sc[1 - par]
        m_new = jnp.maximum(m, jnp.max(s, axis=0, keepdims=True))
        s_sc[par] = pair_scores(t)
        p_sc[1 - par] = jnp.exp2(s - m_new).astype(BF16)
        return m_new, alpha_prev * acc + pv, jnp.exp2(m - m_new)

    s_sc[0] = pair_scores(0)
    p_sc[1] = jnp.zeros((VT, lanes), BF16)
    m, acc, _ = lax.fori_loop(
        0, (n_far + 2) // 2,
        lambda u, carry: far_step(2 * u + 2, 0, far_step(2 * u + 1, 1, carry)),
        init + (jnp.ones((1, lanes), F32),))

    sel_live = lambda dl, kt: (kt >= 0) & ((dl < 2) | (i % 2 == 0))
    o_s = finish(near_tiles((m, acc), ks_ref, vsn_ref, [2, 1, 0], sel_live, True))
    o_t = o_cw + grow(1) * o_s
    o_hd = jnp.concatenate([o_t[:, hh * QT:(hh + 1) * QT] for hh in range(GQ)], axis=0)
    o_ref[...] = o_hd.T.astype(BF16)


def _bias_tables(rel_bias):
    span = N_DELTA * QT
    bt = _bucket_table(WINDOW)
    period = span + QT
    by_dist = jnp.concatenate([rel_bias[bt], jnp.full((period - WINDOW, N_HEADS), NEG, F32)], axis=0)
    toep = jnp.tile(by_dist.T, (1, QT))[:, :QT * (period - 1)].reshape(N_HEADS, QT, period - 1)
    tab = toep[:, :, :span].reshape(N_KV, GQ, QT, N_DELTA, QT)
    wtab = tab.transpose(0, 3, 2, 1, 4).reshape(N_KV, N_DELTA, QT, GQ * QT)
    wtab = jnp.concatenate([wtab, jnp.full((N_KV, 1, QT, GQ * QT), NEG, F32)], axis=1)
    ql = np.arange(QT)[None, :]
    r = np.arange(8)[:, None]
    d = ql + (4 * CMP_BLOCK - CMP_BLOCK + 1) - CMP_BLOCK * r
    near = jnp.where((d >= 0)[..., None], rel_bias[bt[np.clip(d, 0, len(bt) - 1)]], NEG)
    far = jnp.broadcast_to(rel_bias[NUM_BUCKETS - 1][None, None, :], (1, QT, N_HEADS))
    rows = jnp.concatenate([near, far, jnp.full((1, QT, N_HEADS), NEG, F32),
                            jnp.zeros((6, QT, N_HEADS), F32)], axis=0)
    ctab = rows.reshape(16, QT, N_KV, GQ).transpose(2, 0, 3, 1).reshape(N_KV, 16, GQ * QT)
    hi = ctab.astype(BF16)
    lo = (ctab - hi.astype(F32)).astype(BF16)
    return wtab.astype(F32), jnp.stack([hi, lo], axis=1)


def _attn_prompt(qt, gt, kcmp, vcmpt, ksb, vst, vsn, kwb, vwt, wtab, ctab, batch, seq):
    nq = seq // QT
    n_cmp = kcmp.shape[2]
    lanes = GQ * QT
    t = batch * seq
    return pl.pallas_call(
        functools.partial(_attn_kernel, n_cmp=n_cmp),
        grid=(batch, N_KV, nq),
        in_specs=[
            pl.BlockSpec((GQ * HEAD_DIM, QT), lambda b, g, i: (g, b * nq + i)),
            pl.BlockSpec((1, 16, QT), lambda b, g, i: (g, 0, b * nq + i)),
            pl.BlockSpec((1, 1, n_cmp, HEAD_DIM), lambda b, g, i: (b, g, 0, 0)),
            pl.BlockSpec((1, 1, HEAD_DIM, n_cmp), lambda b, g, i: (b, g, 0, 0)),
            pl.BlockSpec((1, seq, HEAD_DIM), lambda b, g, i: (g, b, 0)),
            pl.BlockSpec((1, seq // VT, V_ROWS, VT), lambda b, g, i: (g, b, 0, 0)),
            pl.BlockSpec((1, nq, V_ROWS, QT), lambda b, g, i: (g, b, 0, 0)),
            pl.BlockSpec((1, seq, HEAD_DIM), lambda b, g, i: (g, b, 0)),
            pl.BlockSpec((1, nq, V_ROWS, QT), lambda b, g, i: (g, b, 0, 0)),
            pl.BlockSpec((1, 2, 16, lanes), lambda b, g, i: (g, 0, 0, 0)),
            pl.BlockSpec((1, N_DELTA + 1, QT, lanes), lambda b, g, i: (g, 0, 0, 0)),
        ],
        out_specs=pl.BlockSpec((QT, GQ * HEAD_DIM), lambda b, g, i: (b * nq + i, g)),
        out_shape=jax.ShapeDtypeStruct((t, Q_W), BF16),
        scratch_shapes=[pltpu.VMEM((n_cmp // 2, lanes), F32), pltpu.VMEM((n_cmp // 2, lanes), F32),
                        pltpu.VMEM((2, VT, lanes), F32), pltpu.VMEM((2, VT, lanes), BF16)],
        compiler_params=_cp("arbitrary", "arbitrary", "arbitrary"),
        name="nsa_prompt",
    )(qt, gt, kcmp, vcmpt, ksb, vst, vsn, kwb, vwt, ctab, wtab)


def _s5_kernel(u_ref, mt_ref, sbr_ref, sbi_ref, ccr_ref, cci_ref, lb_ref, h0r_ref, h0i_ref,
               y_ref, hr_ref, hi_ref, sr_sc, si_sc, pr_sc, pi_sc, *, n_chunks, rb, n_seg):
    u = u_ref[0]
    sr_sc[...] = _dot(u, sbr_ref[0])
    si_sc[...] = _dot(u, sbi_ref[0])
    lr = lb_ref[0, 0:1, :]
    li = lb_ref[0, 1:2, :]

    def step(c, carry):
        hr, hi = carry
        rows = pl.ds(pl.multiple_of(c * rb, rb), rb)
        pr_sc[rows, :] = hr
        pi_sc[rows, :] = hi
        return (lr * hr - li * hi + sr_sc[rows, :], lr * hi + li * hr + si_sc[rows, :])

    h0r, h0i = h0r_ref[0], h0i_ref[0]
    first = (lax.broadcasted_iota(I32, (rb, S5_P), 0) & (n_seg - 1)) == 0
    hr, hi = lax.fori_loop(0, n_chunks, step, (h0r, h0i))
    for _ in range(n_seg - 1):
        hr, hi = lax.fori_loop(
            0, n_chunks, step,
            (jnp.where(first, h0r, pltpu.roll(hr, 1, 0)), jnp.where(first, h0i, pltpu.roll(hi, 1, 0))))
    hr_ref[0] = hr
    hi_ref[0] = hi
    y_ref[0] = (_dot(u, mt_ref[0]) + _dot(pr_sc[...].astype(BF16), ccr_ref[0])
                + _dot(pi_sc[...].astype(BF16), cci_ref[0]))


def _s5_prep(lam_re, lam_im, log_dt, b_re, b_im, c_re, c_im, d_skip, chunk):
    lam = lax.complex(lam_re, lam_im)
    z = lam * jnp.exp(log_dt)[:, None]
    lbar = jnp.exp(z)
    bbar = ((lbar - 1.0) / lam)[:, :, None] * lax.complex(b_re, b_im)
    c = lax.complex(c_re, c_im)
    pw = jnp.exp(z[None] * jnp.arange(chunk + 1, dtype=F32)[:, None, None])
    kern = jnp.einsum('gap,jgp,gpb->jgab', c, pw[:chunk], bbar, precision=HIGHEST).real
    kern = kern.at[0].add(jax.vmap(jnp.diag)(d_skip))
    lag = np.arange(chunk)[None, :] - np.arange(chunk)[:, None]
    kt = kern.transpose(1, 0, 3, 2)
    m5 = jnp.where((lag >= 0)[None, :, :, None, None], kt[:, np.clip(lag, 0, None)], 0.0)
    mt = m5.transpose(0, 1, 3, 2, 4).reshape(S5_G, chunk * S5_H, chunk * S5_H)
    sb = jnp.einsum('lgp,gph->glhp', pw[chunk - 1 - np.arange(chunk)], bbar).reshape(
        S5_G, chunk * S5_H, S5_P)
    cc = jnp.einsum('ghp,lgp->gplh', c, pw[1:]).reshape(S5_G, S5_P, chunk * S5_H)
    lb = jnp.stack([pw[chunk].real, pw[chunk].imag], axis=1)
    return (mt.astype(BF16), sb.real.astype(BF16), sb.imag.astype(BF16),
            cc.real.astype(BF16), (-cc.imag).astype(BF16), lb.astype(F32))


def _s5_call(u, prep, h0r, h0i, n_chunks, rb, n_seg):
    mt, sbr, sbi, ccr, cci, lb = prep
    g, rows, w = u.shape
    per_g = lambda *shape: pl.BlockSpec((1,) + shape, lambda i: (i,) + (0,) * len(shape))
    return pl.pallas_call(
        functools.partial(_s5_kernel, n_chunks=n_chunks, rb=rb, n_seg=n_seg),
        grid=(g,),
        in_specs=[per_g(rows, w), per_g(w, w), per_g(w, S5_P), per_g(w, S5_P),
                  per_g(S5_P, w), per_g(S5_P, w), per_g(2, S5_P), per_g(rb, S5_P), per_g(rb, S5_P)],
        out_specs=[per_g(rows, w), per_g(rb, S5_P), per_g(rb, S5_P)],
        out_shape=[jax.ShapeDtypeStruct((g, rows, w), F32),
                   jax.ShapeDtypeStruct((g, rb, S5_P), F32),
                   jax.ShapeDtypeStruct((g, rb, S5_P), F32)],
        scratch_shapes=[pltpu.VMEM((rows, S5_P), F32)] * 4,
        compiler_params=_cp("arbitrary"),
        name="s5_scan",
    )(u, mt, sbr, sbi, ccr, cci, lb, h0r, h0i)


def _s5_prompt(u, prep, batch, seq):
    assert S5_ROWS % batch == 0
    n_seg = S5_ROWS // batch
    nc = seq // S5_CHUNK // n_seg
    ub = u.reshape(batch, n_seg, nc, S5_CHUNK, S5_G, S5_H).transpose(4, 2, 0, 1, 3, 5)
    ub = ub.reshape(S5_G, nc * S5_ROWS, S5_CHUNK * S5_H).astype(BF16)
    zero = jnp.zeros((S5_G, S5_ROWS, S5_P), F32)
    y, hr, hi = _s5_call(ub, prep, zero, zero, nc, S5_ROWS, n_seg)
    y = y.reshape(S5_G, nc, batch, n_seg, S5_CHUNK, S5_H).transpose(2, 3, 1, 4, 0, 5)
    last = lambda h: h[:, n_seg - 1::n_seg].transpose(1, 0, 2)
    return y.reshape(batch * seq, S5_W), last(hr), last(hi)


def _s5_sample(u, prep, h0r, h0i):
    n = u.shape[0]
    ub = u.reshape(n, S5_G, S5_H).transpose(1, 0, 2).astype(BF16)
    y, hr, hi = _s5_call(ub, prep, h0r.transpose(1, 0, 2), h0i.transpose(1, 0, 2), 1, n, 1)
    return y.transpose(1, 0, 2).reshape(n, S5_W), hr.transpose(1, 0, 2), hi.transpose(1, 0, 2)


def _softmax_lanes(s):
    m = jnp.maximum(jnp.max(s, axis=-1, keepdims=True), M_INIT)
    e = jnp.exp2(s - m)
    return e * (1.0 / jnp.maximum(jnp.sum(e, axis=-1, keepdims=True), 1e-30))


def _bqk(q, k):
    return jnp.einsum('bhd,bnd->bhn', q, k, preferred_element_type=F32)


def _bpv(p, v):
    return jnp.einsum('bhn,bnd->bhd', p.astype(BF16), v, preferred_element_type=F32)


def _sattn1_kernel(q_ref, kc_ref, vc_ref, bias_ref, oc_ref, pick_ref, *, n_pick):
    q = q_ref[...]
    p = _softmax_lanes(_bqk(q, kc_ref[...]) + bias_ref[...])
    oc_ref[...] = _bpv(p, vc_ref[...])
    n_cmp = p.shape[-1]
    half = n_cmp // 2
    ps = p[:, 0]
    for hh in range(1, GQ):
        ps = ps + p[:, hh]
    imp = ps[:, :half] + ps[:, half:]
    lane = lax.broadcasted_iota(I32, imp.shape, 1).astype(F32)
    work = jnp.where((lane > 0) & (lane < half - 1), imp, -jnp.inf)
    picks = jnp.zeros(imp.shape, F32)
    for t in range(n_pick):
        mx = jnp.max(work, axis=-1, keepdims=True)
        idx = jnp.min(jnp.where(work == mx, lane, 1e9), axis=-1, keepdims=True)
        work = jnp.where(lane == idx, -jnp.inf, work)
        picks = jnp.where(lane == t, idx, picks)
    pick_ref[...] = picks.astype(I32)


def _bqkt(q, kt):
    return jnp.einsum('bhd,bdn->bhn', q, kt.astype(BF16), preferred_element_type=F32)


def _bpvt(p, vt):
    return jnp.einsum('bhn,bdn->bhd', p.astype(BF16), vt.astype(BF16), preferred_element_type=F32)


def _sattn2_kernel(q_ref, ks_ref, vs_ref, kt_ref, vt_ref, base_ref, corr_ref, flag_ref, half_ref,
                   ex_ref, lh_ref, kw_ref, vw_ref, wb_ref, oc_ref, g_ref, o_ref):
    q = q_ref[...]
    ex = ex_ref[...]
    slabs = lambda ref, tail: jnp.concatenate(
        [ref[:, s] for s in range(ref.shape[1])] + [tail[...]], axis=-1)
    near = _dot(flag_ref[...], ex)
    live = _dot(half_ref[...], ex) == lh_ref[...]
    bias = jnp.where(live[:, None, :], base_ref[...] + near[:, None, :] * corr_ref[...], NEG)
    p_s = _softmax_lanes(_bqkt(q, slabs(ks_ref, kt_ref)) + bias)
    o_s = _bpvt(p_s, slabs(vs_ref, vt_ref))
    p_w = _softmax_lanes(_bqkt(q, kw_ref[...]) + wb_ref[...])
    o_w = _bpvt(p_w, vw_ref[...])
    g = g_ref[...]
    o_ref[...] = g[:, :, 0:1] * oc_ref[...] + g[:, :, 1:2] * o_s + g[:, :, 2:3] * o_w


def _attn_sample(q, gates, kcmp, vcmp, page_table, cache_sk, cache_sv, ks_new, vs_new,
                 win_k, win_v, rel_bias):
    n = q.shape[0]
    n_cmp = kcmp.shape[2]
    past = page_table.shape[1] * PAGE
    n_blk = past // SEL_BLOCK
    n_pick = N_SEL - 3
    nbq = n * N_KV
    nb = 16
    bt = _bucket_table(past + 1)
    qg = q.reshape(nbq, GQ, HEAD_DIM)
    per_row = lambda tab: jnp.tile(tab.reshape(-1, N_KV, GQ).transpose(1, 2, 0), (nb // N_KV, 1, 1))
    blk3 = lambda *s: pl.BlockSpec((nb,) + s, lambda i: (i,) + (0,) * len(s))
    const3 = lambda *s: pl.BlockSpec((nb,) + s, lambda i: (0,) * (len(s) + 1))

    order = np.concatenate([np.arange(0, n_cmp, 2), np.arange(1, n_cmp, 2)])
    d_c = past - (order * CMP_BLOCK + CMP_BLOCK - 1)
    cb = per_row(jnp.where((d_c >= 0)[:, None], rel_bias[bt[np.clip(d_c, 0, None)]], NEG))

    o_c, picks = pl.pallas_call(
        functools.partial(_sattn1_kernel, n_pick=n_pick),
        grid=(nbq // nb,),
        in_specs=[blk3(GQ, HEAD_DIM), blk3(n_cmp, HEAD_DIM), blk3(n_cmp, HEAD_DIM), const3(GQ, n_cmp)],
        out_specs=[blk3(GQ, HEAD_DIM), pl.BlockSpec((nb, n_cmp // 2), lambda i: (i, 0))],
        out_shape=[jax.ShapeDtypeStruct((nbq, GQ, HEAD_DIM), F32),
                   jax.ShapeDtypeStruct((nbq, n_cmp // 2), I32)],
        compiler_params=_cp("arbitrary"),
        name="nsa_sample_cmp",
    )(qg, kcmp.reshape(nbq, n_cmp, HEAD_DIM), vcmp.reshape(nbq, n_cmp, HEAD_DIM), cb)

    picks = picks[:, :n_pick].reshape(n, N_KV, n_pick)
    forced = jnp.broadcast_to(jnp.array([n_blk - 1, 0], I32), (n, N_KV, 2))
    sel = jnp.concatenate([forced, picks], axis=-1)
    n_slot = sel.shape[-1]
    page = jnp.take_along_axis(page_table, (sel // 2).reshape(n, -1), axis=1).reshape(sel.shape)
    head = jnp.broadcast_to(jnp.arange(N_KV, dtype=I32)[None, :, None], sel.shape)
    starts = jnp.stack([page, head], axis=-1)
    dnums = lax.GatherDimensionNumbers(offset_dims=(3, 4), collapsed_slice_dims=(0, 1),
                                       start_index_map=(0, 1))
    n_key = (n_slot + 1) * PAGE

    def gather(cache, new):
        got = lax.gather(cache.transpose(0, 2, 3, 1), starts, dnums,
                         slice_sizes=(1, 1, HEAD_DIM, PAGE),
                         mode=lax.GatherScatterMode.PROMISE_IN_BOUNDS)
        tail = jnp.zeros((nbq, HEAD_DIM, PAGE), F32).at[:, :, 0].set(new.reshape(nbq, HEAD_DIM))
        return got.reshape(nbq, n_slot, HEAD_DIM, PAGE), tail

    k_sel, k_tail = gather(cache_sk, ks_new)
    v_sel, v_tail = gather(cache_sv, vs_new)
    nb2 = 8
    per_row2 = lambda tab: jnp.tile(tab.reshape(-1, N_KV, GQ).transpose(1, 2, 0), (nb2 // N_KV, 1, 1))
    blk2 = lambda *s: pl.BlockSpec((nb2,) + s, lambda i: (i,) + (0,) * len(s))
    const2 = lambda *s: pl.BlockSpec((nb2,) + s, lambda i: (0,) * (len(s) + 1))
    r = np.arange(SEL_BLOCK)
    both = lambda d: np.tile(d, PAGE // SEL_BLOCK)
    d_base = np.concatenate([both(SEL_BLOCK - r), both(past - r)] + [np.full(PAGE, past)] * n_pick
                            + [-np.arange(PAGE)])
    base = per_row2(jnp.where((d_base >= 0)[:, None], rel_bias[bt[np.clip(d_base, 0, past)]], NEG))
    d_near = both(np.clip(2 * SEL_BLOCK - r, 0, past))
    delta = rel_bias[bt[d_near]] - rel_bias[bt[past]][None, :]
    in_pick = np.zeros((n_slot + 1, 1, 1), np.float32)
    in_pick[2:n_slot] = 1.0
    corr = per_row2((in_pick * delta[None]).reshape(n_key, N_HEADS))
    pad_slot = lambda a: jnp.pad(a.astype(F32).reshape(nbq, n_slot), ((0, 0), (0, 1)))
    flag = pad_slot(sel == n_blk - 2)
    half = pad_slot(sel % 2)
    expand = jnp.asarray(np.kron(np.eye(n_slot + 1, dtype=np.float32), np.ones((1, PAGE), np.float32)))
    lane_half = jnp.asarray(((np.arange(n_key) // SEL_BLOCK) % 2).astype(np.float32).reshape(1, n_key))

    n_win = win_k.shape[1]
    d_w = n_win - 1 - np.arange(n_win)
    wb = per_row2(jnp.where((d_w < WINDOW)[:, None], rel_bias[bt[d_w]], NEG))
    wk = win_k.transpose(0, 2, 3, 1).reshape(nbq, HEAD_DIM, n_win)
    wv = win_v.transpose(0, 2, 3, 1).reshape(nbq, HEAD_DIM, n_win)
    slot_spec = pl.BlockSpec((nb2, n_slot + 1), lambda i: (i, 0))
    o = pl.pallas_call(
        _sattn2_kernel,
        grid=(nbq // nb2,),
        in_specs=[blk2(GQ, HEAD_DIM), blk2(n_slot, HEAD_DIM, PAGE), blk2(n_slot, HEAD_DIM, PAGE),
                  blk2(HEAD_DIM, PAGE), blk2(HEAD_DIM, PAGE),
                  const2(GQ, n_key), const2(GQ, n_key), slot_spec, slot_spec,
                  pl.BlockSpec((n_slot + 1, n_key), lambda i: (0, 0)),
                  pl.BlockSpec((1, n_key), lambda i: (0, 0)),
                  blk2(HEAD_DIM, n_win), blk2(HEAD_DIM, n_win), const2(GQ, n_win),
                  blk2(GQ, HEAD_DIM), blk2(GQ, 3)],
        out_specs=blk2(GQ, HEAD_DIM),
        out_shape=jax.ShapeDtypeStruct((nbq, GQ, HEAD_DIM), F32),
        compiler_params=_cp("arbitrary"),
        name="nsa_sample_sel_win",
    )(qg, k_sel, v_sel, k_tail, v_tail, base, corr, flag, half, expand, lane_half, wk, wv, wb, o_c,
      gates.reshape(nbq, GQ, 3))
    return o.reshape(n, Q_W).astype(BF16)


def _gelu_tanh(x):
    return 0.5 * x * (1.0 + jnp.tanh(math.sqrt(2.0 / math.pi) * (x + 0.044715 * (x * x * x))))


def _post_kernel(x_ref, o_ref, y5_ref, gm_ref, shf_ref, scf_ref, gpost_ref, gpre_ref,
                 wglu_ref, wout_ref, wr_ref, br_ref, tri_ref, cin_ref,
                 x1_ref, h2_ref, h2p_ref, eidx_ref, wts_ref, pos_ref, cout_ref, cnt_sc):
    i = pl.program_id(0)
    tm = x_ref.shape[0]

    @pl.when(i == 0)
    def _():
        cnt_sc[...] = cin_ref[...]

    g5 = _gelu_tanh(y5_ref[...])
    g5 = g5 * _sigmoid(_dot(g5.astype(BF16), wglu_ref[...]))
    m = _dot(o_ref[...], wout_ref[:Q_W]) + _dot(g5.astype(BF16), wout_ref[Q_W:])
    x1 = x_ref[...] + gm_ref[0] * _rms(m, gpost_ref[...])
    h2 = _rms(x1, gpre_ref[...]) * (1.0 + scf_ref[0]) + shf_ref[0]
    x1_ref[...] = x1
    h2_ref[...] = h2.astype(BF16)
    h2p_ref[...] = _pack_bf16_pairs(h2)

    h_hi = h2.astype(BF16)
    h_lo = (h2 - h_hi.astype(F32)).astype(BF16)
    logits = _dot(h_hi, wr_ref[0]) + (_dot(h_lo, wr_ref[0]) + _dot(h_hi, wr_ref[1]))
    s_t = _sigmoid(logits).T
    s_sel = s_t + br_ref[...]
    eio = lax.broadcasted_iota(I32, (N_EXPERTS, tm), 0).astype(F32)
    first = lambda hit, ids: jnp.min(jnp.where(hit, ids, 1e9), axis=0, keepdims=True)
    gscore = []
    for g in range(N_EGROUPS):
        xg = s_sel[g * EGROUP:(g + 1) * EGROUP]
        ig = lax.broadcasted_iota(I32, (EGROUP, tm), 0).astype(F32) + float(g * EGROUP)
        m1 = jnp.max(xg, axis=0, keepdims=True)
        m2 = jnp.max(jnp.where(ig == first(xg == m1, ig), -jnp.inf, xg), axis=0, keepdims=True)
        gscore.append(m1 + m2)
    gsc = jnp.concatenate(gscore, axis=0)
    gio = lax.broadcasted_iota(I32, (N_EGROUPS, tm), 0).astype(F32)
    gmask = jnp.zeros((N_EGROUPS, tm), F32)
    for _ in range(TOPK_GROUPS):
        hit = gio == first(gsc == jnp.max(gsc, axis=0, keepdims=True), gio)
        gmask = jnp.where(hit, 1.0, gmask)
        gsc = jnp.where(hit, -jnp.inf, gsc)
    work = jnp.concatenate(
        [jnp.where(gmask[g:g + 1] > 0.0, s_sel[g * EGROUP:(g + 1) * EGROUP], -jnp.inf)
         for g in range(N_EGROUPS)], axis=0)
    chosen = jnp.zeros((N_EXPERTS, tm), F32)
    ids, wsel = [], []
    for _ in range(TOP_K):
        ik = first(work == jnp.max(work, axis=0, keepdims=True), eio)
        hit = eio == ik
        ids.append(ik)
        wsel.append(jnp.sum(jnp.where(hit, s_t, 0.0), axis=0, keepdims=True))
        work = jnp.where(hit, -jnp.inf, work)
        chosen = jnp.where(hit, 1.0, chosen)
    rank = cnt_sc[:, 0:1] + _dot(chosen.astype(BF16), tri_ref[...])
    pos = [jnp.sum(jnp.where(eio == ik, rank, 0.0), axis=0, keepdims=True) for ik in ids]
    cnt_sc[...] = cnt_sc[...] + jnp.sum(chosen, axis=1, keepdims=True)
    wsum = wsel[0]
    for k in range(1, TOP_K):
        wsum = wsum + wsel[k]
    eidx_ref[...] = jnp.concatenate(ids, axis=0).astype(I32)
    pos_ref[...] = jnp.concatenate(pos, axis=0).astype(I32)
    wts_ref[...] = jnp.concatenate([w / wsum * ROUTED_SCALE for w in wsel], axis=0)
    cout_ref[...] = cnt_sc[...]


def _post_mix(x, o_att, y5, gm, shf, scf, g_post, g_pre, wglu, wout, w_router, b_router,
              cnt_in, tm, rows_per_mod):
    t = x.shape[0]
    r = gm.shape[1]
    per = rows_per_mod // tm
    mod = pl.BlockSpec((1, r, D_MODEL), lambda i: (i // per, 0, 0))
    full = lambda a: pl.BlockSpec(a.shape, lambda i: (0,) * a.ndim)
    vec = pl.BlockSpec((1, D_MODEL), lambda i: (0, 0))
    tri = jnp.asarray(np.triu(np.ones((tm, tm), np.float32), 1), BF16)
    brb = jnp.broadcast_to(b_router[:, None], (N_EXPERTS, tm)).astype(F32)
    wr_hi = w_router.astype(BF16)
    w_router = jnp.stack([wr_hi, (w_router - wr_hi.astype(F32)).astype(BF16)])
    route = pl.BlockSpec((TOP_K, tm), lambda i: (0, i))
    return pl.pallas_call(
        _post_kernel,
        grid=(t // tm,),
        in_specs=[pl.BlockSpec((tm, D_MODEL), lambda i: (i, 0)),
                  pl.BlockSpec((tm, Q_W), lambda i: (i, 0)),
                  pl.BlockSpec((tm, S5_W), lambda i: (i, 0)),
                  mod, mod, mod, vec, vec, full(wglu), full(wout), full(w_router), full(brb),
                  full(tri), full(cnt_in)],
        out_specs=[pl.BlockSpec((tm, D_MODEL), lambda i: (i, 0)),
                   pl.BlockSpec((tm, D_MODEL), lambda i: (i, 0)),
                   pl.BlockSpec((tm, D_MODEL // 2), lambda i: (i, 0)),
                   route, route, route, full(cnt_in)],
        out_shape=[jax.ShapeDtypeStruct((t, D_MODEL), F32),
                   jax.ShapeDtypeStruct((t, D_MODEL), BF16),
                   jax.ShapeDtypeStruct((t, D_MODEL // 2), U32),
                   jax.ShapeDtypeStruct((TOP_K, t), I32),
                   jax.ShapeDtypeStruct((TOP_K, t), F32),
                   jax.ShapeDtypeStruct((TOP_K, t), I32),
                   jax.ShapeDtypeStruct(cnt_in.shape, F32)],
        scratch_shapes=[pltpu.VMEM(cnt_in.shape, F32)],
        compiler_params=_cp("arbitrary"),
        name="post_mix_router",
    )(x, o_att, y5, gm, shf, scf, g_post.reshape(1, D_MODEL), g_pre.reshape(1, D_MODEL),
      wglu, wout, w_router, brb, tri, cnt_in)


def _dest_kernel(e_ref, p_ref, ps_ref, o_ref):
    tm = e_ref.shape[1]
    eio = lax.broadcasted_iota(I32, (N_EXPERTS, tm), 0)
    e = e_ref[...]
    start = ps_ref[...]
    rows = [jnp.sum(jnp.where(eio == e[k:k + 1], start, 0.0), axis=0, keepdims=True)
            for k in range(TOP_K)]
    o_ref[...] = jnp.concatenate(rows, axis=0).astype(I32) + p_ref[...]


def _dest(eidx, pos, pad_start, tm):
    t = eidx.shape[1]
    route = pl.BlockSpec((TOP_K, tm), lambda i: (0, i))
    start = jnp.broadcast_to(pad_start.astype(F32)[:, None], (N_EXPERTS, tm))
    return pl.pallas_call(
        _dest_kernel,
        grid=(t // tm,),
        in_specs=[route, route, pl.BlockSpec((N_EXPERTS, tm), lambda i: (0, 0))],
        out_specs=route,
        out_shape=jax.ShapeDtypeStruct((TOP_K, t), I32),
        compiler_params=_cp("arbitrary"),
        name="moe_dest",
    )(eidx, pos, start)


def _moe_kernel(be_ref, nb_ref, x_ref, wg_ref, wu_ref, wd_ref, y_ref, wg_sc, wu_sc, wd_sc):
    j = pl.program_id(0)

    @pl.when(j < nb_ref[0])
    def _():
        @pl.when((j == 0) | (be_ref[j] != be_ref[jnp.maximum(j - 1, 0)]))
        def _():
            wg_sc[...] = wg_ref[0].astype(BF16)
            wu_sc[...] = wu_ref[0].astype(BF16)
            wd_sc[...] = wd_ref[0].astype(BF16)

        for r0 in range(0, MOE_BLK, MOE_BLK // 2):
            rows = pl.ds(r0, MOE_BLK // 2)
            x = jnp.concatenate(_unpack_bf16_pairs(x_ref[rows, :]), axis=1).astype(BF16)
            a = _dot(x, wg_sc[...])
            b = _dot(x, wu_sc[...])
            y = _dot((_silu(a) * b).astype(BF16), wd_sc[...])
            y_ref[rows, :] = _pack_bf16_pairs(y)


def _moe(xs, blk_e, nb_used, w_g, w_u, w_d):
    n_slot = xs.shape[0]
    n_blk = n_slot // MOE_BLK
    last = lambda j, nb: jnp.maximum(jnp.minimum(j, nb[0] - 1), 0)
    row = lambda j, be, nb: (last(j, nb), 0)
    wsel = lambda j, be, nb: (be[last(j, nb)], 0, 0)
    return pl.pallas_call(
        _moe_kernel,
        grid_spec=pltpu.PrefetchScalarGridSpec(
            num_scalar_prefetch=2,
            grid=(n_blk,),
            in_specs=[pl.BlockSpec((MOE_BLK, D_MODEL // 2), row),
                      pl.BlockSpec((1, D_MODEL, D_EXPERT), wsel),
                      pl.BlockSpec((1, D_MODEL, D_EXPERT), wsel),
                      pl.BlockSpec((1, D_EXPERT, D_MODEL), wsel)],
            out_specs=pl.BlockSpec((MOE_BLK, D_MODEL // 2), row),
            scratch_shapes=[pltpu.VMEM((D_MODEL, D_EXPERT), BF16), pltpu.VMEM((D_MODEL, D_EXPERT), BF16),
                            pltpu.VMEM((D_EXPERT, D_MODEL), BF16)]),
        out_shape=jax.ShapeDtypeStruct((n_slot, D_MODEL // 2), U32),
        compiler_params=_cp("arbitrary"),
        name="moe_experts",
    )(blk_e, nb_used, xs, w_g, w_u, w_d)


def _fin_kernel(x1_ref, h2_ref, yg_ref, w_ref, gf_ref, gpost_ref, wsg_ref, wsu_ref, wsd_ref, o_ref):
    w = w_ref[...]
    f_lo, f_hi = None, None
    for k in range(TOP_K):
        lo, hi = _unpack_bf16_pairs(yg_ref[k])
        f_lo = w[:, k:k + 1] * lo if k == 0 else f_lo + w[:, k:k + 1] * lo
        f_hi = w[:, k:k + 1] * hi if k == 0 else f_hi + w[:, k:k + 1] * hi
    f = jnp.concatenate([f_lo, f_hi], axis=1)
    hb = h2_ref[...]
    sh = (_silu(_dot(hb, wsg_ref[...])) * _dot(hb, wsu_ref[...])).astype(BF16)
    f = f + _dot(sh, wsd_ref[...])
    o_ref[...] = x1_ref[...] + gf_ref[0] * _rms(f, gpost_ref[...])


def _final(x1, h2, yg, wts, gf, g_post, wsg, wsu, wsd, tm, rows_per_mod):
    t = x1.shape[0]
    r = gf.shape[1]
    per = rows_per_mod // tm
    full = lambda a: pl.BlockSpec(a.shape, lambda i: (0,) * a.ndim)
    return pl.pallas_call(
        _fin_kernel,
        grid=(t // tm,),
        in_specs=[pl.BlockSpec((tm, D_MODEL), lambda i: (i, 0)),
                  pl.BlockSpec((tm, D_MODEL), lambda i: (i, 0)),
                  pl.BlockSpec((TOP_K, tm, D_MODEL // 2), lambda i: (0, i, 0)),
                  pl.BlockSpec((tm, TOP_K), lambda i: (i, 0)),
                  pl.BlockSpec((1, r, D_MODEL), lambda i: (i // per, 0, 0)),
                  pl.BlockSpec((1, D_MODEL), lambda i: (0, 0)),
                  full(wsg), full(wsu), full(wsd)],
        out_specs=pl.BlockSpec((tm, D_MODEL), lambda i: (i, 0)),
        out_shape=jax.ShapeDtypeStruct((t, D_MODEL), F32),
        compiler_params=_cp("arbitrary"),
        name="moe_combine_final",
    )(x1, h2, yg, wts, gf, g_post.reshape(1, D_MODEL), wsg, wsu, wsd)


def _even_odd(a, axis):
    n = a.shape[axis]
    order = np.concatenate([np.arange(0, n, 2), np.arange(1, n, 2)])
    return jnp.take(a, order, axis=axis)


def _layer(xp, xs, c_prompt, c_sample, page_table, ck_c, cv_c, ck_s, cv_s, cw_k, cw_v, st_re, st_im,
           w_ada, b_ada, g_pre_mix, g_post_mix, g_pre_ffn, g_post_ffn, w_in, pe, wk1, wk2, wv1, wv2,
           rel_bias, lam_re, lam_im, log_dt, b_re, b_im, c_re, c_im, d_skip, w_glu, w_out,
           w_router, b_router, w_eg, w_eu, w_ed, w_sg, w_su, w_sd):
    batch, seq, _ = xp.shape
    n_dec = xs.shape[0]
    tp = batch * seq
    x_p = xp.reshape(tp, D_MODEL)
    x_s = xs.reshape(n_dec, D_MODEL)

    n_c = batch + n_dec
    n_pad = -(-n_c // 8) * 8
    c_all = jnp.pad(jnp.concatenate([c_prompt, c_sample], axis=0), ((0, n_pad - n_c), (0, 0)))
    mod = _adaln(c_all, w_ada, b_ada)
    mod_p = [m.reshape(batch, 1, D_MODEL) for m in jnp.split(mod[:batch], 6, axis=-1)]
    mod_s = [m.reshape(1, n_dec, D_MODEL) for m in jnp.split(mod[batch:n_c], 6, axis=-1)]

    wn, wt = _pre_weights(w_in)
    (kct_p, vct_p, kst_p, vst_p, kwt_p, vwt_p, kc_p, vc_p, u_p, ksb, kwb, qt_p, vst, vsn, vwt, gt_p) = \
        _pre_mix_prompt(x_p, mod_p[1], mod_p[0], g_pre_mix, wn, wt, 512, batch, seq)
    (kc_s, vc_s, ks_s, vs_s, kw_s, vw_s, u_s, qt_s, gt_s) = _pre_mix_sample(
        x_s, mod_s[1], mod_s[0], g_pre_mix, wn, wt)

    n_cmp_p = seq // CMP_BLOCK
    kcmp_p = _compress(kc_p.reshape(batch * n_cmp_p, CMP_BLOCK * KV_W), *_cmp_weights(pe, wk1, wk2))
    vcmp_p = _compress(vc_p.reshape(batch * n_cmp_p, CMP_BLOCK * KV_W), *_cmp_weights(pe, wv1, wv2))
    kcmp_p = _even_odd(kcmp_p.reshape(batch, n_cmp_p, N_KV, HEAD_DIM), 1).transpose(0, 2, 1, 3)
    vcmp_p = _even_odd(vcmp_p.reshape(batch, n_cmp_p, N_KV, HEAD_DIM), 1).transpose(0, 2, 3, 1)
    bias_log2 = rel_bias * LOG2E
    wtab, ctab = _bias_tables(bias_log2)
    o_p = _attn_prompt(qt_p, gt_p, kcmp_p.astype(BF16), vcmp_p.astype(BF16), ksb, vst, vsn, kwb, vwt,
                       wtab, ctab, batch, seq)

    n_pages = page_table.shape[1]
    n_cmp_s = n_pages * CMP_PER_PAGE
    pages = lambda cache: cache.transpose(0, 2, 3, 1)[page_table].reshape(
        n_dec * n_pages * N_KV * HEAD_DIM, PAGE)
    kcmp_s, vcmp_s = _compress_pages(pages(ck_c), pages(cv_c), _cmp_pages_weights(pe, wk1, wk2),
                                     _cmp_pages_weights(pe, wv1, wv2), n_dec, n_pages)
    by_head = lambda a: _even_odd(
        a.reshape(n_dec, n_pages, N_KV, CMP_PER_PAGE, HEAD_DIM).transpose(0, 2, 1, 3, 4).reshape(
            n_dec, N_KV, n_cmp_s, HEAD_DIM), 2)
    win_k = jnp.concatenate([cw_k[:, 1:], kw_s.reshape(n_dec, 1, N_KV, HEAD_DIM)], axis=1)
    win_v = jnp.concatenate([cw_v[:, 1:], vw_s.reshape(n_dec, 1, N_KV, HEAD_DIM)], axis=1)
    q_s = qt_s.T.reshape(n_dec, N_HEADS, HEAD_DIM)
    gates_s = gt_s[:, :3 * GQ].reshape(N_KV, 3, GQ, n_dec).transpose(3, 0, 2, 1).reshape(n_dec, N_HEADS, 3)
    o_s = _attn_sample(q_s, gates_s, by_head(kcmp_s).astype(BF16), by_head(vcmp_s).astype(BF16),
                       page_table, ck_s, cv_s, ks_s, vs_s, win_k, win_v, bias_log2)

    y5_p, s5r_p, s5i_p = _s5_prompt(
        u_p, _s5_prep(lam_re, lam_im, log_dt, b_re, b_im, c_re, c_im, d_skip, S5_CHUNK), batch, seq)
    y5_s, s5r_s, s5i_s = _s5_sample(
        u_s, _s5_prep(lam_re, lam_im, log_dt, b_re, b_im, c_re, c_im, d_skip, 1), st_re, st_im)

    wglu_b, wout_b = w_glu.astype(BF16), w_out.astype(BF16)
    cnt0 = jnp.zeros((N_EXPERTS, 128), F32)
    tm_q = 256
    x1_p, h2_p, h2p_p, e_p, wt_p, pos_p, cnt1 = _post_mix(
        x_p, o_p, y5_p, mod_p[2], mod_p[3], mod_p[4], g_post_mix, g_pre_ffn, wglu_b, wout_b,
        w_router, b_router, cnt0, tm_q, seq)
    x1_s, h2_s, h2p_s, e_s, wt_s, pos_s, cnt2 = _post_mix(
        x_s, o_s, y5_s, mod_s[2], mod_s[3], mod_s[4], g_post_mix, g_pre_ffn, wglu_b, wout_b,
        w_router, b_router, cnt1, n_dec, n_dec)

    t_all = tp + n_dec
    counts = cnt2[:, 0].astype(I32)
    padded = (counts + MOE_BLK - 1) // MOE_BLK * MOE_BLK
    pad_end = jnp.cumsum(padded)
    dest_p = _dest(e_p, pos_p, pad_end - padded, 1024)
    dest_s = _dest(e_s, pos_s, pad_end - padded, n_dec)
    dest = jnp.concatenate([dest_p, dest_s], axis=1)
    n_blk = -(-(t_all * TOP_K) // MOE_BLK) + N_EXPERTS
    n_slot = n_blk * MOE_BLK
    tok = jnp.broadcast_to(jnp.arange(t_all, dtype=I32)[None], (TOP_K, t_all))
    rows = (jnp.arange(n_slot, dtype=I32) % t_all).at[dest.reshape(-1)].set(
        tok.reshape(-1), unique_indices=True)
    xs_rows = jnp.concatenate([h2p_p, h2p_s], axis=0)[rows]
    blk_e = jnp.minimum(jnp.searchsorted(pad_end, jnp.arange(n_blk, dtype=I32) * MOE_BLK, side='right'),
                        N_EXPERTS - 1).astype(I32)
    nb_used = (pad_end[-1:] // MOE_BLK).astype(I32)
    ys = _moe(xs_rows, blk_e, nb_used, w_eg, w_eu, w_ed)
    yg_p = ys[dest_p]
    yg_s = ys[dest_s]

    wsg, wsu, wsd = w_sg.astype(BF16), w_su.astype(BF16), w_sd.astype(BF16)
    out_p = _final(x1_p, h2_p, yg_p, wt_p.T, mod_p[5], g_post_ffn, wsg, wsu, wsd, tm_q, seq)
    out_s = _final(x1_s, h2_s, yg_s, wt_s.T, mod_s[5], g_post_ffn, wsg, wsu, wsd, n_dec, n_dec)

    n_win = min(WINDOW, seq)
    rows_p = lambda a: a.transpose(0, 3, 1, 2)
    kv5 = lambda a: a.reshape(n_dec, 1, N_KV, HEAD_DIM)
    st_p = (rows_p(kct_p), rows_p(vct_p), rows_p(kst_p), rows_p(vst_p),
            rows_p(kwt_p)[:, seq - n_win:], rows_p(vwt_p)[:, seq - n_win:], s5r_p, s5i_p)
    st_s = (kv5(kc_s), kv5(vc_s), kv5(ks_s), kv5(vs_s), win_k, win_v, s5r_s, s5i_s)
    return out_p.reshape(batch, seq, D_MODEL), out_s.reshape(n_dec, 1, D_MODEL), st_p, st_s


def kernel(x_prompt, x_sample, c_prompt, c_sample, page_table, cache_cmp_k, cache_cmp_v, cache_sel_k, cache_sel_v, cache_win_k, cache_win_v, state_s5_re, state_s5_im, w_ada, b_ada, g_pre_mix, g_post_mix, g_pre_ffn, g_post_ffn, w_in, pe_cmp, w_cmp_k1, w_cmp_k2, w_cmp_v1, w_cmp_v2, rel_bias, lam_re, lam_im, log_dt, b_re, b_im, c_re, c_im, d_skip, w_glu, w_out, w_router, b_router, w_exp_gate, w_exp_up, w_exp_down, w_sh_gate, w_sh_up, w_sh_down):
    depth = w_in.shape[0]
    xp, xs = x_prompt, x_sample
    p_states, s_states = [], []
    for l in range(depth):
        xp, xs, st_p, st_s = _layer(
            xp, xs, c_prompt, c_sample, page_table, cache_cmp_k[l], cache_cmp_v[l], cache_sel_k[l],
            cache_sel_v[l], cache_win_k[l], cache_win_v[l], state_s5_re[l], state_s5_im[l],
            w_ada[l], b_ada[l], g_pre_mix[l], g_post_mix[l], g_pre_ffn[l], g_post_ffn[l], w_in[l],
            pe_cmp[l], w_cmp_k1[l], w_cmp_k2[l], w_cmp_v1[l], w_cmp_v2[l], rel_bias,
            lam_re[l], lam_im[l], log_dt[l], b_re[l], b_im[l], c_re[l], c_im[l], d_skip[l],
            w_glu[l], w_out[l], w_router[l], b_router[l], w_exp_gate[l], w_exp_up[l], w_exp_down[l],
            w_sh_gate[l], w_sh_up[l], w_sh_down[l])
        p_states.append(st_p)
        s_states.append(st_s)
    p_st = tuple(jnp.stack(a) for a in zip(*p_states))
    s_st = tuple(jnp.stack(a) for a in zip(*s_states))
    return (xp, xs) + p_st + s_st
```

```python
import functools
import math

import numpy as np
import jax
import jax.numpy as jnp
from jax import lax
from jax.experimental import pallas as pl
from jax.experimental.pallas import tpu as pltpu

F32 = jnp.float32
BF16 = jnp.bfloat16
I32 = jnp.int32
U32 = jnp.uint32

D_MODEL = 1024
N_HEADS = 8
HEAD_DIM = 64
N_KV = 2
GQ = N_HEADS // N_KV
Q_W = N_HEADS * HEAD_DIM
KV_W = N_KV * HEAD_DIM
S5_W = D_MODEL - Q_W
S5_H = 16
S5_G = S5_W // S5_H
S5_P = 64
GATE_OFF = Q_W + 6 * KV_W
U_OFF = GATE_OFF + 3 * N_HEADS
CMP_BLOCK = 32
SEL_BLOCK = 64
N_SEL = 16
WINDOW = 512
NUM_BUCKETS = 32
REL_MAX_DIST = 128
N_EXPERTS = 256
TOP_K = 8
N_EGROUPS = 8
TOPK_GROUPS = 4
EGROUP = N_EXPERTS // N_EGROUPS
D_EXPERT = 256
ROUTED_SCALE = 2.5
EPS = 1e-6
SCALE = HEAD_DIM ** -0.5
PAGE = 128
CMP_PER_PAGE = PAGE // CMP_BLOCK

QT = 128
VT = 2 * QT
N_DELTA = WINDOW // QT + 1
V_ROWS = HEAD_DIM + 16
LOG2E = math.log2(math.e)
S5_CHUNK = 32
S5_ROWS = 8
MOE_BLK = 512
NEG = -2e30
M_INIT = -1e30
VMEM_LIMIT = 56 * 1024 * 1024
WT_ROWS = Q_W + 6 * KV_W + 32
HIGHEST = lax.Precision.HIGHEST


def _cp(*sem):
    return pltpu.CompilerParams(dimension_semantics=sem, vmem_limit_bytes=VMEM_LIMIT)


def _dot(a, b, precision=None):
    return jnp.dot(a, b, preferred_element_type=F32, precision=precision)


def _dot_nt(a, b):
    return lax.dot_general(a, b, (((1,), (1,)), ((), ())), preferred_element_type=F32)


def _sigmoid(x):
    return 1.0 / (1.0 + jnp.exp(-x))


def _silu(x):
    return x * _sigmoid(x)


def _rms(x, g):
    return x * lax.rsqrt(jnp.mean(x * x, axis=-1, keepdims=True) + EPS) * g


def _pack_bf16_pairs(x):
    c = x.shape[1] // 2
    bits = pltpu.bitcast(x.astype(BF16).astype(F32), U32)
    return (bits[:, :c] >> 16) | bits[:, c:]


def _unpack_bf16_pairs(xp):
    return pltpu.bitcast(xp << 16, F32), pltpu.bitcast(xp & jnp.uint32(0xFFFF0000), F32)


def _bucket_table(n):
    d = np.arange(n)
    exact = NUM_BUCKETS // 2
    nf = np.maximum(d, 1).astype(np.float32)
    large = exact + (np.log(nf / np.float32(exact)) / np.float32(math.log(REL_MAX_DIST / exact))
                     * np.float32(NUM_BUCKETS - exact)).astype(np.int32)
    return np.where(d < exact, d, np.minimum(large, NUM_BUCKETS - 1)).astype(np.int32)


def _ada_kernel(c_ref, w_ref, b_ref, o_ref):
    a = _silu(c_ref[...]).astype(BF16)
    o_ref[...] = _dot(a, w_ref[...].astype(BF16)) + b_ref[...]


def _adaln(c, w_ada, b_ada):
    n, d = c.shape
    n_out = w_ada.shape[1]
    tn = 1024
    return pl.pallas_call(
        _ada_kernel,
        grid=(n_out // tn,),
        in_specs=[pl.BlockSpec((n, d), lambda j: (0, 0)),
                  pl.BlockSpec((d, tn), lambda j: (0, j)),
                  pl.BlockSpec((1, tn), lambda j: (0, j))],
        out_specs=pl.BlockSpec((n, tn), lambda j: (0, j)),
        out_shape=jax.ShapeDtypeStruct((n, n_out), F32),
        compiler_params=_cp("arbitrary"),
        name="adaln",
    )(c, w_ada, b_ada.reshape(1, n_out))


def _pre_project(x_ref, sc_ref, sh_ref, g_ref, wn_ref, wt_ref):
    h = _rms(x_ref[...], g_ref[...]) * (1.0 + sc_ref[0]) + sh_ref[0]
    hb = h.astype(BF16)
    return _dot(hb, wn_ref[...]), _dot_nt(wt_ref[...], hb)


def _pre_prompt_kernel(x_ref, sc_ref, sh_ref, g_ref, wn_ref, wt_ref,
                       kct_ref, vct_ref, kst_ref, vst32_ref, kwt_ref, vwt32_ref, kc_ref, vc_ref, u_ref,
                       ksb_ref, kwb_ref, qt_ref, vst_ref, vsn_ref, vwt_ref, gt_ref):
    tm = x_ref.shape[0]
    zn, zt = _pre_project(x_ref, sc_ref, sh_ref, g_ref, wn_ref, wt_ref)
    trans = lambda j, g: zt[Q_W + j * KV_W + g * HEAD_DIM:Q_W + j * KV_W + (g + 1) * HEAD_DIM]
    for j, ref in enumerate((kct_ref, vct_ref, kst_ref, vst32_ref, kwt_ref, vwt32_ref)):
        for g in range(N_KV):
            ref[0, g] = trans(j, g)
    kc_ref[...] = zn[:, 0:KV_W]
    vc_ref[...] = zn[:, KV_W:2 * KV_W]
    u_ref[...] = zn[:, 6 * KV_W:]
    for g in range(N_KV):
        ksb_ref[g] = zn[:, 2 * KV_W + g * HEAD_DIM:2 * KV_W + (g + 1) * HEAD_DIM].astype(BF16)
        kwb_ref[g] = zn[:, 4 * KV_W + g * HEAD_DIM:4 * KV_W + (g + 1) * HEAD_DIM].astype(BF16)
        ones = jnp.where(lax.broadcasted_iota(I32, (V_ROWS - HEAD_DIM, tm), 0) == 0, 1.0, 0.0)
        vs_t = jnp.concatenate([trans(3, g), ones], axis=0).astype(BF16)
        vw_t = jnp.concatenate([trans(5, g), ones], axis=0).astype(BF16)
        for c in range(tm // VT):
            vst_ref[g, c] = vs_t[:, c * VT:(c + 1) * VT]
        for c in range(tm // QT):
            vsn_ref[g, c] = vs_t[:, c * QT:(c + 1) * QT]
            vwt_ref[g, c] = vw_t[:, c * QT:(c + 1) * QT]
        r2 = Q_W + 6 * KV_W + g * 16
        gt_ref[g] = _sigmoid(zt[r2:r2 + 16])
    qt_ref[...] = zt[:Q_W].astype(BF16)


def _pre_sample_kernel(x_ref, sc_ref, sh_ref, g_ref, wn_ref, wt_ref,
                       kc_ref, vc_ref, ks_ref, vs_ref, kw_ref, vw_ref, u_ref, qt_ref, gt_ref):
    zn, zt = _pre_project(x_ref, sc_ref, sh_ref, g_ref, wn_ref, wt_ref)
    for j, ref in enumerate((kc_ref, vc_ref, ks_ref, vs_ref, kw_ref, vw_ref)):
        ref[...] = zn[:, j * KV_W:(j + 1) * KV_W]
    u_ref[...] = zn[:, 6 * KV_W:]
    qt_ref[...] = zt[:Q_W].astype(BF16)
    for g in range(N_KV):
        r2 = Q_W + 6 * KV_W + g * 16
        gt_ref[g] = _sigmoid(zt[r2:r2 + 16])


def _pre_weights(w_in):
    wn = jnp.concatenate([w_in[:, Q_W:GATE_OFF], w_in[:, U_OFF:]], axis=1).astype(BF16)
    gate_cols = []
    for g in range(N_KV):
        for j in range(3):
            for hh in range(GQ):
                gate_cols.append(GATE_OFF + (g * GQ + hh) * 3 + j)
        gate_cols.extend([GATE_OFF] * 4)
    wt = jnp.concatenate([
        w_in[:, :Q_W] * (SCALE * LOG2E),
        w_in[:, Q_W:GATE_OFF],
        w_in[:, np.array(gate_cols)],
    ], axis=1).T.astype(BF16)
    return wn, wt


def _pre_in_specs(tm, r, per, wn, wt):
    return [pl.BlockSpec((tm, D_MODEL), lambda i: (i, 0)),
            pl.BlockSpec((1, r, D_MODEL), lambda i: (i // per, 0, 0)),
            pl.BlockSpec((1, r, D_MODEL), lambda i: (i // per, 0, 0)),
            pl.BlockSpec((1, D_MODEL), lambda i: (0, 0)),
            pl.BlockSpec(wn.shape, lambda i: (0, 0)),
            pl.BlockSpec(wt.shape, lambda i: (0, 0))]


def _pre_mix_prompt(x, sc, sh, g_pre, wn, wt, tm, batch, seq):
    t = x.shape[0]
    per = seq // tm
    f = lambda shape: jax.ShapeDtypeStruct(shape, F32)
    b = lambda shape: jax.ShapeDtypeStruct(shape, BF16)
    tr_spec = pl.BlockSpec((1, N_KV, HEAD_DIM, tm), lambda i: (i // per, 0, 0, i % per))
    kv_spec = pl.BlockSpec((tm, KV_W), lambda i: (i, 0))
    return pl.pallas_call(
        _pre_prompt_kernel,
        grid=(t // tm,),
        in_specs=_pre_in_specs(tm, 1, per, wn, wt),
        out_specs=[tr_spec] * 6 + [
            kv_spec, kv_spec,
            pl.BlockSpec((tm, S5_W), lambda i: (i, 0)),
            pl.BlockSpec((N_KV, tm, HEAD_DIM), lambda i: (0, i, 0)),
            pl.BlockSpec((N_KV, tm, HEAD_DIM), lambda i: (0, i, 0)),
            pl.BlockSpec((Q_W, tm), lambda i: (0, i)),
            pl.BlockSpec((N_KV, tm // VT, V_ROWS, VT), lambda i: (0, i, 0, 0)),
            pl.BlockSpec((N_KV, tm // QT, V_ROWS, QT), lambda i: (0, i, 0, 0)),
            pl.BlockSpec((N_KV, tm // QT, V_ROWS, QT), lambda i: (0, i, 0, 0)),
            pl.BlockSpec((N_KV, 16, tm), lambda i: (0, 0, i))],
        out_shape=[f((batch, N_KV, HEAD_DIM, seq))] * 6 + [
            f((t, KV_W)), f((t, KV_W)),
            f((t, S5_W)), b((N_KV, t, HEAD_DIM)), b((N_KV, t, HEAD_DIM)), b((Q_W, t)),
            b((N_KV, t // VT, V_ROWS, VT)), b((N_KV, t // QT, V_ROWS, QT)),
            b((N_KV, t // QT, V_ROWS, QT)), f((N_KV, 16, t))],
        compiler_params=_cp("arbitrary"),
        name="pre_mix_prompt",
    )(x, sc, sh, g_pre.reshape(1, D_MODEL), wn, wt)


def _pre_mix_sample(x, sc, sh, g_pre, wn, wt):
    t = x.shape[0]
    f = lambda shape: jax.ShapeDtypeStruct(shape, F32)
    kv_spec = pl.BlockSpec((t, KV_W), lambda i: (i, 0))
    return pl.pallas_call(
        _pre_sample_kernel,
        grid=(1,),
        in_specs=_pre_in_specs(t, t, 1, wn, wt),
        out_specs=[kv_spec] * 6 + [pl.BlockSpec((t, S5_W), lambda i: (i, 0)),
                                   pl.BlockSpec((Q_W, t), lambda i: (0, i)),
                                   pl.BlockSpec((N_KV, 16, t), lambda i: (0, 0, i))],
        out_shape=[f((t, KV_W))] * 6 + [f((t, S5_W)), jax.ShapeDtypeStruct((Q_W, t), BF16),
                                        f((N_KV, 16, t))],
        compiler_params=_cp("arbitrary"),
        name="pre_mix_sample",
    )(x, sc, sh, g_pre.reshape(1, D_MODEL), wn, wt)


def _cmp_kernel(x_ref, pe_ref, w1_ref, w2_ref, o_ref):
    xb = (x_ref[...] + pe_ref[...]).astype(BF16)
    hid = _silu(_dot(xb, w1_ref[...]))
    o_ref[...] = _dot(hid.astype(BF16), w2_ref[...])


def _cmp_weights(pe, w1, w2):
    eye = jnp.eye(N_KV, dtype=F32)
    w1b = jnp.einsum('jde,gh->jgdhe', w1, eye).reshape(CMP_BLOCK * KV_W, KV_W).astype(BF16)
    w2b = jnp.einsum('ef,gh->gehf', w2, eye).reshape(KV_W, KV_W).astype(BF16)
    peb = jnp.broadcast_to(pe[:, None, :], (CMP_BLOCK, N_KV, HEAD_DIM)).reshape(1, CMP_BLOCK * KV_W)
    return peb, w1b, w2b


def _compress(x, peb, w1b, w2b):
    r, k = x.shape
    tr = min(256, r)
    return pl.pallas_call(
        _cmp_kernel,
        grid=(r // tr,),
        in_specs=[pl.BlockSpec((tr, k), lambda i: (i, 0)),
                  pl.BlockSpec((1, k), lambda i: (0, 0)),
                  pl.BlockSpec((k, KV_W), lambda i: (0, 0)),
                  pl.BlockSpec((KV_W, KV_W), lambda i: (0, 0))],
        out_specs=pl.BlockSpec((tr, KV_W), lambda i: (i, 0)),
        out_shape=jax.ShapeDtypeStruct((r, KV_W), F32),
        compiler_params=_cp("arbitrary"),
        name="compress",
    )(x, peb, w1b, w2b)


def _cmp_pages_kernel(pt_ref, ck_hbm, cv_hbm, w1k_ref, w1v_ref, bk_ref, bv_ref, w2k_ref, w2v_ref,
                      ok_ref, ov_ref, kbuf, vbuf, sem, *, n_pages):
    i = pl.program_id(0)
    n_rows = ok_ref.shape[0]

    def page_copies(seq, slot):
        copies = []
        for p in range(n_pages):
            page = pt_ref[seq * n_pages + p]
            rows = pl.ds(p * KV_W, KV_W)
            copies.append(pltpu.make_async_copy(ck_hbm.at[page], kbuf.at[slot, rows], sem.at[0, slot]))
            copies.append(pltpu.make_async_copy(cv_hbm.at[page], vbuf.at[slot, rows], sem.at[1, slot]))
        return copies

    @pl.when(i == 0)
    def _():
        for c in page_copies(0, 0):
            c.start()

    @pl.when(i + 1 < pl.num_programs(0))
    def _():
        for c in page_copies(i + 1, (i + 1) % 2):
            c.start()

    slot = i % 2
    for c in page_copies(i, slot):
        c.wait()
    for x_ref, w1_ref, b_ref, w2_ref, o_ref in ((kbuf.at[slot], w1k_ref, bk_ref, w2k_ref, ok_ref),
                                               (vbuf.at[slot], w1v_ref, bv_ref, w2v_ref, ov_ref)):
        acc = jnp.zeros((n_rows, CMP_PER_PAGE * HEAD_DIM), F32)
        for dd in range(HEAD_DIM // 2):
            r0 = x_ref[pl.ds(2 * dd, n_rows, stride=HEAD_DIM), :]
            r1 = x_ref[pl.ds(2 * dd + 1, n_rows, stride=HEAD_DIM), :]
            acc = acc + _dot(jnp.concatenate([r0, r1], axis=1).astype(BF16), w1_ref[dd])
        hid = _silu(acc + b_ref[...])
        o_ref[...] = _dot(hid.astype(BF16), w2_ref[...])


def _cmp_pages_weights(pe, w1, w2):
    eye = jnp.eye(CMP_PER_PAGE, dtype=F32)
    w1t = jnp.einsum('jde,mn->dmjne', w1, eye).reshape(HEAD_DIM // 2, 2 * PAGE, CMP_PER_PAGE * HEAD_DIM)
    bias = jnp.einsum('jd,jde->e', pe, w1, precision=HIGHEST)
    bias = jnp.tile(bias, CMP_PER_PAGE).reshape(1, CMP_PER_PAGE * HEAD_DIM)
    w2t = jnp.einsum('ef,mn->menf', w2, eye).reshape(CMP_PER_PAGE * HEAD_DIM, CMP_PER_PAGE * HEAD_DIM)
    return w1t.astype(BF16), bias, w2t.astype(BF16)


def _compress_pages(page_table, ck, cv, wk, wv):
    n_seq, n_pages = page_table.shape
    rows_in = n_pages * KV_W
    rows_out = n_pages * N_KV
    wcols = CMP_PER_PAGE * HEAD_DIM
    full = lambda a: pl.BlockSpec(a.shape, lambda i, pt: (0,) * a.ndim)
    hbm = pl.BlockSpec(memory_space=pl.ANY)
    o_spec = pl.BlockSpec((rows_out, wcols), lambda i, pt: (i, 0))
    o_shape = jax.ShapeDtypeStruct((n_seq * rows_out, wcols), F32)
    return pl.pallas_call(
        functools.partial(_cmp_pages_kernel, n_pages=n_pages),
        grid_spec=pltpu.PrefetchScalarGridSpec(
            num_scalar_prefetch=1,
            grid=(n_seq,),
            in_specs=[hbm, hbm, full(wk[0]), full(wv[0]), full(wk[1]), full(wv[1]),
                      full(wk[2]), full(wv[2])],
            out_specs=[o_spec, o_spec],
            scratch_shapes=[pltpu.VMEM((2, rows_in, PAGE), F32), pltpu.VMEM((2, rows_in, PAGE), F32),
                            pltpu.SemaphoreType.DMA((2, 2))]),
        out_shape=[o_shape, o_shape],
        compiler_params=_cp("arbitrary"),
        name="compress_pages",
    )(page_table.reshape(-1), ck, cv, wk[0], wv[0], wk[1], wv[1], wk[2], wv[2])


def _attn_kernel(q_ref, g_ref, kc_ref, vct_ref, ks_ref, vst_ref, vsn_ref, kw_ref, vwt_ref,
                 ctab_ref, wtab_ref, o_ref, mb_ref, mbf_ref, s_sc, p_sc, *, n_cmp):
    i = pl.program_id(2)
    lanes = GQ * QT
    qt = q_ref[...]
    q4 = jnp.concatenate([qt[hh * HEAD_DIM:(hh + 1) * HEAD_DIM] for hh in range(GQ)], axis=1)
    gates = g_ref[0]
    grow = lambda j: jnp.concatenate([gates[j * GQ + hh:j * GQ + hh + 1] for hh in range(GQ)], axis=1)

    def online(carry, s, v_t):
        m, acc = carry
        m_new = jnp.maximum(m, jnp.max(s, axis=0, keepdims=True))
        p = jnp.exp2(s - m_new)
        return m_new, jnp.exp2(m - m_new) * acc + _dot(v_t, p.astype(BF16))

    init = (jnp.full((1, lanes), M_INIT, F32), jnp.zeros((V_ROWS, lanes), F32))
    finish = lambda carry: carry[1][:HEAD_DIM] * (1.0 / jnp.maximum(carry[1][HEAD_DIM:HEAD_DIM + 1], 1e-30))
    block_rows = lambda ref, first, n, extra=0.0: jnp.concatenate(
        [jnp.broadcast_to(ref[pl.ds(first + c, 1), :] + extra, (SEL_BLOCK, lanes)) for c in range(n)],
        axis=0)

    def near_tiles(carry, k_ref, vt_ref, deltas, live, masked):
        ks, vs, bs = [], [], []
        for dl in deltas:
            kt = i - dl
            ktc = jnp.maximum(kt, 0)
            ks.append(k_ref[0, pl.ds(pl.multiple_of(ktc * QT, QT), QT), :])
            vs.append(vt_ref[0, ktc])
            bias = wtab_ref[0, jnp.where(live(dl, kt), dl, N_DELTA)]
            if masked:
                bias = bias + block_rows(mb_ref, (QT // SEL_BLOCK) * ktc, QT // SEL_BLOCK)
            bs.append(bias)
        s = _dot(jnp.concatenate(ks, axis=0), q4) + jnp.concatenate(bs, axis=0)
        return online(carry, s, jnp.concatenate(vs, axis=1))

    o_w = finish(near_tiles(init, kw_ref, vwt_ref, list(range(N_DELTA - 1, -1, -1)),
                            lambda dl, kt: kt >= 0, False))

    half = n_cmp // 2
    sc = _dot(kc_ref[0, 0], q4)
    rho = lax.broadcasted_iota(I32, (n_cmp, 16), 0)
    col = lax.broadcasted_iota(I32, (n_cmp, 16), 1)
    blk_n = 2 * jnp.where(rho >= half, rho - half, rho) + jnp.where(rho >= half, 1, 0)
    rel = blk_n - (4 * i - 4)
    want = jnp.where(rel < 0, 8, jnp.where(rel > 7, 9, rel))
    place = jnp.where(col == want, 1.0, 0.0).astype(BF16)
    sc = sc + _dot(place, ctab_ref[0, 0]) + _dot(place, ctab_ref[0, 1])
    mc = jnp.maximum(jnp.max(sc, axis=0, keepdims=True), M_INIT)
    ec = jnp.exp2(sc - mc)
    pc = ec * (1.0 / jnp.maximum(jnp.sum(ec, axis=0, keepdims=True), 1e-30))
    o_cw = grow(0) * _dot(vct_ref[0, 0], pc.astype(BF16)) + grow(2) * o_w

    ps = pc[:half] + pc[half:]
    imp = ps[:, 0:QT]
    for hh in range(1, GQ):
        imp = imp + ps[:, hh * QT:(hh + 1) * QT]
    n_blk = half
    blk = lax.broadcasted_iota(I32, (n_blk, QT), 0)
    tq = i * QT + lax.broadcasted_iota(I32, (n_blk, QT), 1)
    cur = lax.shift_right_logical(tq, 6)
    valid = blk * SEL_BLOCK <= tq
    forced = (blk == 0) | (blk == cur) | (blk == cur - 1)
    blkf = blk.astype(F32)

    def pick(_, carry):
        work, mb = carry
        mx = jnp.max(work, axis=0, keepdims=True)
        idx = jnp.min(jnp.where(work == mx, blkf, 1e9), axis=0, keepdims=True)
        hit = blkf == idx
        return jnp.where(hit, -jnp.inf, work), jnp.where(hit, 0.0, mb)

    _, mb = lax.fori_loop(0, min(N_SEL, n_blk) - 3, pick,
                          (jnp.where(valid, jnp.where(forced, -jnp.inf, imp), -jnp.inf),
                           jnp.where(valid, jnp.where(forced, 0.0, NEG), NEG)))
    for hh in range(GQ):
        mb_ref[:, hh * QT:(hh + 1) * QT] = mb
    mbf_ref[...] = mb_ref[...] + wtab_ref[0, 2, 0:1, :]

    n_far = jnp.maximum(i - 1, 0) // 2
    last_pair = ks_ref.shape[1] // VT - 1

    def pair_scores(t):
        ta = jnp.minimum(t, last_pair)
        k_t = ks_ref[0, pl.ds(pl.multiple_of(ta * VT, VT), VT), :]
        return _dot(k_t, q4) + block_rows(mbf_ref, (VT // SEL_BLOCK) * ta, VT // SEL_BLOCK,
                                          jnp.where(t < n_far, 0.0, NEG))

    def far_step(t, par, carry):
        m, acc, alpha_prev = carry
        pv = _dot(vst_ref[0, jnp.clip(t - 2, 0, last_pair)], p_sc[par])
        s = s_sc[1 - par]
        m_new = jnp.maximum(m, jnp.max(s, axis=0, keepdims=True))
        s_sc[par] = pair_scores(t)
        p_sc[1 - par] = jnp.exp2(s - m_new).astype(BF16)
        return m_new, alpha_prev * acc + pv, jnp.exp2(m - m_new)

    s_sc[0] = pair_scores(0)
    p_sc[1] = jnp.zeros((VT, lanes), BF16)
    m, acc, _ = lax.fori_loop(
        0, (n_far + 2) // 2,
        lambda u, carry: far_step(2 * u + 2, 0, far_step(2 * u + 1, 1, carry)),
        init + (jnp.ones((1, lanes), F32),))

    sel_live = lambda dl, kt: (kt >= 0) & ((dl < 2) | (i % 2 == 0))
    o_s = finish(near_tiles((m, acc), ks_ref, vsn_ref, [2, 1, 0], sel_live, True))
    o_t = o_cw + grow(1) * o_s
    o_hd = jnp.concatenate([o_t[:, hh * QT:(hh + 1) * QT] for hh in range(GQ)], axis=0)
    o_ref[...] = o_hd.T.astype(BF16)


def _bias_tables(rel_bias):
    span = N_DELTA * QT
    bt = _bucket_table(WINDOW)
    period = span + QT
    by_dist = jnp.concatenate([rel_bias[bt], jnp.full((period - WINDOW, N_HEADS), NEG, F32)], axis=0)
    toep = jnp.tile(by_dist.T, (1, QT))[:, :QT * (period - 1)].reshape(N_HEADS, QT, period - 1)
    tab = toep[:, :, :span].reshape(N_KV, GQ, QT, N_DELTA, QT)
    wtab = tab.transpose(0, 3, 2, 1, 4).reshape(N_KV, N_DELTA, QT, GQ * QT)
    wtab = jnp.concatenate([wtab, jnp.full((N_KV, 1, QT, GQ * QT), NEG, F32)], axis=1)
    ql = np.arange(QT)[None, :]
    r = np.arange(8)[:, None]
    d = ql + (4 * CMP_BLOCK - CMP_BLOCK + 1) - CMP_BLOCK * r
    near = jnp.where((d >= 0)[..., None], rel_bias[bt[np.clip(d, 0, len(bt) - 1)]], NEG)
    far = jnp.broadcast_to(rel_bias[NUM_BUCKETS - 1][None, None, :], (1, QT, N_HEADS))
    rows = jnp.concatenate([near, far, jnp.full((1, QT, N_HEADS), NEG, F32),
                            jnp.zeros((6, QT, N_HEADS), F32)], axis=0)
    ctab = rows.reshape(16, QT, N_KV, GQ).transpose(2, 0, 3, 1).reshape(N_KV, 16, GQ * QT)
    hi = ctab.astype(BF16)
    lo = (ctab - hi.astype(F32)).astype(BF16)
    return wtab.astype(F32), jnp.stack([hi, lo], axis=1)


def _attn_prompt(qt, gt, kcmp, vcmpt, ksb, vst, vsn, kwb, vwt, wtab, ctab, batch, seq):
    nq = seq // QT
    n_cmp = kcmp.shape[2]
    lanes = GQ * QT
    t = batch * seq
    return pl.pallas_call(
        functools.partial(_attn_kernel, n_cmp=n_cmp),
        grid=(batch, N_KV, nq),
        in_specs=[
            pl.BlockSpec((GQ * HEAD_DIM, QT), lambda b, g, i: (g, b * nq + i)),
            pl.BlockSpec((1, 16, QT), lambda b, g, i: (g, 0, b * nq + i)),
            pl.BlockSpec((1, 1, n_cmp, HEAD_DIM), lambda b, g, i: (b, g, 0, 0)),
            pl.BlockSpec((1, 1, HEAD_DIM, n_cmp), lambda b, g, i: (b, g, 0, 0)),
            pl.BlockSpec((1, seq, HEAD_DIM), lambda b, g, i: (g, b, 0)),
            pl.BlockSpec((1, seq // VT, V_ROWS, VT), lambda b, g, i: (g, b, 0, 0)),
            pl.BlockSpec((1, nq, V_ROWS, QT), lambda b, g, i: (g, b, 0, 0)),
            pl.BlockSpec((1, seq, HEAD_DIM), lambda b, g, i: (g, b, 0)),
            pl.BlockSpec((1, nq, V_ROWS, QT), lambda b, g, i: (g, b, 0, 0)),
            pl.BlockSpec((1, 2, 16, lanes), lambda b, g, i: (g, 0, 0, 0)),
            pl.BlockSpec((1, N_DELTA + 1, QT, lanes), lambda b, g, i: (g, 0, 0, 0)),
        ],
        out_specs=pl.BlockSpec((QT, GQ * HEAD_DIM), lambda b, g, i: (b * nq + i, g)),
        out_shape=jax.ShapeDtypeStruct((t, Q_W), BF16),
        scratch_shapes=[pltpu.VMEM((n_cmp // 2, lanes), F32), pltpu.VMEM((n_cmp // 2, lanes), F32),
                        pltpu.VMEM((2, VT, lanes), F32), pltpu.VMEM((2, VT, lanes), BF16)],
        compiler_params=_cp("arbitrary", "arbitrary", "arbitrary"),
        name="nsa_prompt",
    )(qt, gt, kcmp, vcmpt, ksb, vst, vsn, kwb, vwt, ctab, wtab)


def _s5_kernel(u_ref, mt_ref, sbr_ref, sbi_ref, ccr_ref, cci_ref, lb_ref, h0r_ref, h0i_ref,
               y_ref, hr_ref, hi_ref, sr_sc, si_sc, pr_sc, pi_sc, *, n_chunks, rb, n_seg):
    u = u_ref[0]
    sr_sc[...] = _dot(u, sbr_ref[0])
    si_sc[...] = _dot(u, sbi_ref[0])
    lr = lb_ref[0, 0:1, :]
    li = lb_ref[0, 1:2, :]

    def step(c, carry):
        hr, hi = carry
        rows = pl.ds(pl.multiple_of(c * rb, rb), rb)
        pr_sc[rows, :] = hr
        pi_sc[rows, :] = hi
        return (lr * hr - li * hi + sr_sc[rows, :], lr * hi + li * hr + si_sc[rows, :])

    h0r, h0i = h0r_ref[0], h0i_ref[0]
    first = (lax.broadcasted_iota(I32, (rb, S5_P), 0) & (n_seg - 1)) == 0
    hr, hi = lax.fori_loop(0, n_chunks, step, (h0r, h0i))
    for _ in range(n_seg - 1):
        hr, hi = lax.fori_loop(
            0, n_chunks, step,
            (jnp.where(first, h0r, pltpu.roll(hr, 1, 0)), jnp.where(first, h0i, pltpu.roll(hi, 1, 0))))
    hr_ref[0] = hr
    hi_ref[0] = hi
    y_ref[0] = (_dot(u, mt_ref[0]) + _dot(pr_sc[...].astype(BF16), ccr_ref[0])
                + _dot(pi_sc[...].astype(BF16), cci_ref[0]))


def _s5_prep(lam_re, lam_im, log_dt, b_re, b_im, c_re, c_im, d_skip, chunk):
    lam = lax.complex(lam_re, lam_im)
    z = lam * jnp.exp(log_dt)[:, None]
    lbar = jnp.exp(z)
    bbar = ((lbar - 1.0) / lam)[:, :, None] * lax.complex(b_re, b_im)
    c = lax.complex(c_re, c_im)
    pw = jnp.exp(z[None] * jnp.arange(chunk + 1, dtype=F32)[:, None, None])
    kern = jnp.einsum('gap,jgp,gpb->jgab', c, pw[:chunk], bbar, precision=HIGHEST).real
    kern = kern.at[0].add(jax.vmap(jnp.diag)(d_skip))
    lag = np.arange(chunk)[None, :] - np.arange(chunk)[:, None]
    kt = kern.transpose(1, 0, 3, 2)
    m5 = jnp.where((lag >= 0)[None, :, :, None, None], kt[:, np.clip(lag, 0, None)], 0.0)
    mt = m5.transpose(0, 1, 3, 2, 4).reshape(S5_G, chunk * S5_H, chunk * S5_H)
    sb = jnp.einsum('lgp,gph->glhp', pw[chunk - 1 - np.arange(chunk)], bbar).reshape(
        S5_G, chunk * S5_H, S5_P)
    cc = jnp.einsum('ghp,lgp->gplh', c, pw[1:]).reshape(S5_G, S5_P, chunk * S5_H)
    lb = jnp.stack([pw[chunk].real, pw[chunk].imag], axis=1)
    return (mt.astype(BF16), sb.real.astype(BF16), sb.imag.astype(BF16),
            cc.real.astype(BF16), (-cc.imag).astype(BF16), lb.astype(F32))


def _s5_call(u, prep, h0r, h0i, n_chunks, rb, n_seg):
    mt, sbr, sbi, ccr, cci, lb = prep
    g, rows, w = u.shape
    per_g = lambda *shape: pl.BlockSpec((1,) + shape, lambda i: (i,) + (0,) * len(shape))
    return pl.pallas_call(
        functools.partial(_s5_kernel, n_chunks=n_chunks, rb=rb, n_seg=n_seg),
        grid=(g,),
        in_specs=[per_g(rows, w), per_g(w, w), per_g(w, S5_P), per_g(w, S5_P),
                  per_g(S5_P, w), per_g(S5_P, w), per_g(2, S5_P), per_g(rb, S5_P), per_g(rb, S5_P)],
        out_specs=[per_g(rows, w), per_g(rb, S5_P), per_g(rb, S5_P)],
        out_shape=[jax.ShapeDtypeStruct((g, rows, w), F32),
                   jax.ShapeDtypeStruct((g, rb, S5_P), F32),
                   jax.ShapeDtypeStruct((g, rb, S5_P), F32)],
        scratch_shapes=[pltpu.VMEM((rows, S5_P), F32)] * 4,
        compiler_params=_cp("arbitrary"),
        name="s5_scan",
    )(u, mt, sbr, sbi, ccr, cci, lb, h0r, h0i)


def _s5_prompt(u, prep, batch, seq):
    assert S5_ROWS % batch == 0
    n_seg = S5_ROWS // batch
    nc = seq // S5_CHUNK // n_seg
    ub = u.reshape(batch, n_seg, nc, S5_CHUNK, S5_G, S5_H).transpose(4, 2, 0, 1, 3, 5)
    ub = ub.reshape(S5_G, nc * S5_ROWS, S5_CHUNK * S5_H).astype(BF16)
    zero = jnp.zeros((S5_G, S5_ROWS, S5_P), F32)
    y, hr, hi = _s5_call(ub, prep, zero, zero, nc, S5_ROWS, n_seg)
    y = y.reshape(S5_G, nc, batch, n_seg, S5_CHUNK, S5_H).transpose(2, 3, 1, 4, 0, 5)
    last = lambda h: h[:, n_seg - 1::n_seg].transpose(1, 0, 2)
    return y.reshape(batch * seq, S5_W), last(hr), last(hi)


def _s5_sample(u, prep, h0r, h0i):
    n = u.shape[0]
    ub = u.reshape(n, S5_G, S5_H).transpose(1, 0, 2).astype(BF16)
    y, hr, hi = _s5_call(ub, prep, h0r.transpose(1, 0, 2), h0i.transpose(1, 0, 2), 1, n, 1)
    return y.transpose(1, 0, 2).reshape(n, S5_W), hr.transpose(1, 0, 2), hi.transpose(1, 0, 2)


def _softmax_lanes(s):
    m = jnp.maximum(jnp.max(s, axis=-1, keepdims=True), M_INIT)
    e = jnp.exp2(s - m)
    return e * (1.0 / jnp.maximum(jnp.sum(e, axis=-1, keepdims=True), 1e-30))


def _bqk(q, k):
    return jnp.einsum('bhd,bnd->bhn', q, k, preferred_element_type=F32)


def _bpv(p, v):
    return jnp.einsum('bhn,bnd->bhd', p.astype(BF16), v, preferred_element_type=F32)


def _sattn1_kernel(q_ref, kc_ref, vc_ref, bias_ref, oc_ref, pick_ref, *, n_pick):
    q = q_ref[...]
    p = _softmax_lanes(_bqk(q, kc_ref[...]) + bias_ref[...])
    oc_ref[...] = _bpv(p, vc_ref[...])
    n_cmp = p.shape[-1]
    half = n_cmp // 2
    ps = p[:, 0]
    for hh in range(1, GQ):
        ps = ps + p[:, hh]
    imp = ps[:, :half] + ps[:, half:]
    lane = lax.broadcasted_iota(I32, imp.shape, 1).astype(F32)
    work = jnp.where((lane > 0) & (lane < half - 1), imp, -jnp.inf)
    picks = jnp.zeros(imp.shape, F32)
    for t in range(n_pick):
        mx = jnp.max(work, axis=-1, keepdims=True)
        idx = jnp.min(jnp.where(work == mx, lane, 1e9), axis=-1, keepdims=True)
        work = jnp.where(lane == idx, -jnp.inf, work)
        picks = jnp.where(lane == t, idx, picks)
    pick_ref[...] = picks.astype(I32)


def _bqkt(q, kt):
    return jnp.einsum('bhd,bdn->bhn', q, kt.astype(BF16), preferred_element_type=F32)


def _bpvt(p, vt):
    return jnp.einsum('bhn,bdn->bhd', p.astype(BF16), vt.astype(BF16), preferred_element_type=F32)


def _sattn2_kernel(q_ref, ks_ref, vs_ref, kt_ref, vt_ref, base_ref, corr_ref, flag_ref, half_ref,
                   ex_ref, lh_ref, kw_ref, vw_ref, wb_ref, oc_ref, g_ref, o_ref):
    q = q_ref[...]
    ex = ex_ref[...]
    slabs = lambda ref, tail: jnp.concatenate(
        [ref[:, s] for s in range(ref.shape[1])] + [tail[...]], axis=-1)
    near = _dot(flag_ref[...], ex)
    live = _dot(half_ref[...], ex) == lh_ref[...]
    bias = jnp.where(live[:, None, :], base_ref[...] + near[:, None, :] * corr_ref[...], NEG)
    p_s = _softmax_lanes(_bqkt(q, slabs(ks_ref, kt_ref)) + bias)
    o_s = _bpvt(p_s, slabs(vs_ref, vt_ref))
    p_w = _softmax_lanes(_bqkt(q, kw_ref[...]) + wb_ref[...])
    o_w = _bpvt(p_w, vw_ref[...])
    g = g_ref[...]
    o_ref[...] = g[:, :, 0:1] * oc_ref[...] + g[:, :, 1:2] * o_s + g[:, :, 2:3] * o_w


def _attn_sample(q, gates, kcmp, vcmp, page_table, cache_sk, cache_sv, ks_new, vs_new,
                 win_k, win_v, rel_bias):
    n = q.shape[0]
    n_cmp = kcmp.shape[2]
    past = page_table.shape[1] * PAGE
    n_blk = past // SEL_BLOCK
    n_pick = N_SEL - 3
    nbq = n * N_KV
    nb = 16
    bt = _bucket_table(past + 1)
    qg = q.reshape(nbq, GQ, HEAD_DIM)
    per_row = lambda tab: jnp.tile(tab.reshape(-1, N_KV, GQ).transpose(1, 2, 0), (nb // N_KV, 1, 1))
    blk3 = lambda *s: pl.BlockSpec((nb,) + s, lambda i: (i,) + (0,) * len(s))
    const3 = lambda *s: pl.BlockSpec((nb,) + s, lambda i: (0,) * (len(s) + 1))

    order = np.concatenate([np.arange(0, n_cmp, 2), np.arange(1, n_cmp, 2)])
    d_c = past - (order * CMP_BLOCK + CMP_BLOCK - 1)
    cb = per_row(jnp.where((d_c >= 0)[:, None], rel_bias[bt[np.clip(d_c, 0, None)]], NEG))

    o_c, picks = pl.pallas_call(
        functools.partial(_sattn1_kernel, n_pick=n_pick),
        grid=(nbq // nb,),
        in_specs=[blk3(GQ, HEAD_DIM), blk3(n_cmp, HEAD_DIM), blk3(n_cmp, HEAD_DIM), const3(GQ, n_cmp)],
        out_specs=[blk3(GQ, HEAD_DIM), pl.BlockSpec((nb, n_cmp // 2), lambda i: (i, 0))],
        out_shape=[jax.ShapeDtypeStruct((nbq, GQ, HEAD_DIM), F32),
                   jax.ShapeDtypeStruct((nbq, n_cmp // 2), I32)],
        compiler_params=_cp("arbitrary"),
        name="nsa_sample_cmp",
    )(qg, kcmp.reshape(nbq, n_cmp, HEAD_DIM), vcmp.reshape(nbq, n_cmp, HEAD_DIM), cb)

    picks = picks[:, :n_pick].reshape(n, N_KV, n_pick)
    forced = jnp.broadcast_to(jnp.array([n_blk - 1, 0], I32), (n, N_KV, 2))
    sel = jnp.concatenate([forced, picks], axis=-1)
    n_slot = sel.shape[-1]
    page = jnp.take_along_axis(page_table, (sel // 2).reshape(n, -1), axis=1).reshape(sel.shape)
    head = jnp.broadcast_to(jnp.arange(N_KV, dtype=I32)[None, :, None], sel.shape)
    starts = jnp.stack([page, head], axis=-1)
    dnums = lax.GatherDimensionNumbers(offset_dims=(3, 4), collapsed_slice_dims=(0, 1),
                                       start_index_map=(0, 1))
    n_key = (n_slot + 1) * PAGE

    def gather(cache, new):
        got = lax.gather(cache.transpose(0, 2, 3, 1), starts, dnums,
                         slice_sizes=(1, 1, HEAD_DIM, PAGE),
                         mode=lax.GatherScatterMode.PROMISE_IN_BOUNDS)
        tail = jnp.zeros((nbq, HEAD_DIM, PAGE), F32).at[:, :, 0].set(new.reshape(nbq, HEAD_DIM))
        return got.reshape(nbq, n_slot, HEAD_DIM, PAGE), tail

    k_sel, k_tail = gather(cache_sk, ks_new)
    v_sel, v_tail = gather(cache_sv, vs_new)
    nb2 = 8
    per_row2 = lambda tab: jnp.tile(tab.reshape(-1, N_KV, GQ).transpose(1, 2, 0), (nb2 // N_KV, 1, 1))
    blk2 = lambda *s: pl.BlockSpec((nb2,) + s, lambda i: (i,) + (0,) * len(s))
    const2 = lambda *s: pl.BlockSpec((nb2,) + s, lambda i: (0,) * (len(s) + 1))
    r = np.arange(SEL_BLOCK)
    both = lambda d: np.tile(d, PAGE // SEL_BLOCK)
    d_base = np.concatenate([both(SEL_BLOCK - r), both(past - r)] + [np.full(PAGE, past)] * n_pick
                            + [-np.arange(PAGE)])
    base = per_row2(jnp.where((d_base >= 0)[:, None], rel_bias[bt[np.clip(d_base, 0, past)]], NEG))
    d_near = both(np.clip(2 * SEL_BLOCK - r, 0, past))
    delta = rel_bias[bt[d_near]] - rel_bias[bt[past]][None, :]
    in_pick = np.zeros((n_slot + 1, 1, 1), np.float32)
    in_pick[2:n_slot] = 1.0
    corr = per_row2((in_pick * delta[None]).reshape(n_key, N_HEADS))
    pad_slot = lambda a: jnp.pad(a.astype(F32).reshape(nbq, n_slot), ((0, 0), (0, 1)))
    flag = pad_slot(sel == n_blk - 2)
    half = pad_slot(sel % 2)
    expand = jnp.asarray(np.kron(np.eye(n_slot + 1, dtype=np.float32), np.ones((1, PAGE), np.float32)))
    lane_half = jnp.asarray(((np.arange(n_key) // SEL_BLOCK) % 2).astype(np.float32).reshape(1, n_key))

    n_win = win_k.shape[1]
    d_w = n_win - 1 - np.arange(n_win)
    wb = per_row2(jnp.where((d_w < WINDOW)[:, None], rel_bias[bt[d_w]], NEG))
    wk = win_k.transpose(0, 2, 3, 1).reshape(nbq, HEAD_DIM, n_win)
    wv = win_v.transpose(0, 2, 3, 1).reshape(nbq, HEAD_DIM, n_win)
    slot_spec = pl.BlockSpec((nb2, n_slot + 1), lambda i: (i, 0))
    o = pl.pallas_call(
        _sattn2_kernel,
        grid=(nbq // nb2,),
        in_specs=[blk2(GQ, HEAD_DIM), blk2(n_slot, HEAD_DIM, PAGE), blk2(n_slot, HEAD_DIM, PAGE),
                  blk2(HEAD_DIM, PAGE), blk2(HEAD_DIM, PAGE),
                  const2(GQ, n_key), const2(GQ, n_key), slot_spec, slot_spec,
                  pl.BlockSpec((n_slot + 1, n_key), lambda i: (0, 0)),
                  pl.BlockSpec((1, n_key), lambda i: (0, 0)),
                  blk2(HEAD_DIM, n_win), blk2(HEAD_DIM, n_win), const2(GQ, n_win),
                  blk2(GQ, HEAD_DIM), blk2(GQ, 3)],
        out_specs=blk2(GQ, HEAD_DIM),
        out_shape=jax.ShapeDtypeStruct((nbq, GQ, HEAD_DIM), F32),
        compiler_params=_cp("arbitrary"),
        name="nsa_sample_sel_win",
    )(qg, k_sel, v_sel, k_tail, v_tail, base, corr, flag, half, expand, lane_half, wk, wv, wb, o_c,
      gates.reshape(nbq, GQ, 3))
    return o.reshape(n, Q_W).astype(BF16)


def _gelu_tanh(x):
    return 0.5 * x * (1.0 + jnp.tanh(math.sqrt(2.0 / math.pi) * (x + 0.044715 * (x * x * x))))


def _post_kernel(x_ref, o_ref, y5_ref, gm_ref, shf_ref, scf_ref, gpost_ref, gpre_ref,
                 wglu_ref, wout_ref, wr_ref, br_ref, tri_ref, cin_ref,
                 x1_ref, h2_ref, h2p_ref, eidx_ref, wts_ref, pos_ref, cout_ref, cnt_sc):
    i = pl.program_id(0)
    tm = x_ref.shape[0]

    @pl.when(i == 0)
    def _():
        cnt_sc[...] = cin_ref[...]

    g5 = _gelu_tanh(y5_ref[...])
    g5 = g5 * _sigmoid(_dot(g5.astype(BF16), wglu_ref[...]))
    m = _dot(o_ref[...], wout_ref[:Q_W]) + _dot(g5.astype(BF16), wout_ref[Q_W:])
    x1 = x_ref[...] + gm_ref[0] * _rms(m, gpost_ref[...])
    h2 = _rms(x1, gpre_ref[...]) * (1.0 + scf_ref[0]) + shf_ref[0]
    x1_ref[...] = x1
    h2_ref[...] = h2.astype(BF16)
    h2p_ref[...] = _pack_bf16_pairs(h2)

    h_hi = h2.astype(BF16)
    h_lo = (h2 - h_hi.astype(F32)).astype(BF16)
    logits = _dot(h_hi, wr_ref[0]) + (_dot(h_lo, wr_ref[0]) + _dot(h_hi, wr_ref[1]))
    s_t = _sigmoid(logits).T
    s_sel = s_t + br_ref[...]
    eio = lax.broadcasted_iota(I32, (N_EXPERTS, tm), 0).astype(F32)
    first = lambda hit, ids: jnp.min(jnp.where(hit, ids, 1e9), axis=0, keepdims=True)
    gscore = []
    for g in range(N_EGROUPS):
        xg = s_sel[g * EGROUP:(g + 1) * EGROUP]
        ig = lax.broadcasted_iota(I32, (EGROUP, tm), 0).astype(F32) + float(g * EGROUP)
        m1 = jnp.max(xg, axis=0, keepdims=True)
        m2 = jnp.max(jnp.where(ig == first(xg == m1, ig), -jnp.inf, xg), axis=0, keepdims=True)
        gscore.append(m1 + m2)
    gsc = jnp.concatenate(gscore, axis=0)
    gio = lax.broadcasted_iota(I32, (N_EGROUPS, tm), 0).astype(F32)
    gmask = jnp.zeros((N_EGROUPS, tm), F32)
    for _ in range(TOPK_GROUPS):
        hit = gio == first(gsc == jnp.max(gsc, axis=0, keepdims=True), gio)
        gmask = jnp.where(hit, 1.0, gmask)
        gsc = jnp.where(hit, -jnp.inf, gsc)
    work = jnp.concatenate(
        [jnp.where(gmask[g:g + 1] > 0.0, s_sel[g * EGROUP:(g + 1) * EGROUP], -jnp.inf)
         for g in range(N_EGROUPS)], axis=0)
    chosen = jnp.zeros((N_EXPERTS, tm), F32)
    ids, wsel = [], []
    for _ in range(TOP_K):
        ik = first(work == jnp.max(work, axis=0, keepdims=True), eio)
        hit = eio == ik
        ids.append(ik)
        wsel.append(jnp.sum(jnp.where(hit, s_t, 0.0), axis=0, keepdims=True))
        work = jnp.where(hit, -jnp.inf, work)
        chosen = jnp.where(hit, 1.0, chosen)
    rank = cnt_sc[:, 0:1] + _dot(chosen.astype(BF16), tri_ref[...])
    pos = [jnp.sum(jnp.where(eio == ik, rank, 0.0), axis=0, keepdims=True) for ik in ids]
    cnt_sc[...] = cnt_sc[...] + jnp.sum(chosen, axis=1, keepdims=True)
    wsum = wsel[0]
    for k in range(1, TOP_K):
        wsum = wsum + wsel[k]
    eidx_ref[...] = jnp.concatenate(ids, axis=0).astype(I32)
    pos_ref[...] = jnp.concatenate(pos, axis=0).astype(I32)
    wts_ref[...] = jnp.concatenate([w / wsum * ROUTED_SCALE for w in wsel], axis=0)
    cout_ref[...] = cnt_sc[...]


def _post_mix(x, o_att, y5, gm, shf, scf, g_post, g_pre, wglu, wout, w_router, b_router,
              cnt_in, tm, rows_per_mod):
    t = x.shape[0]
    r = gm.shape[1]
    per = rows_per_mod // tm
    mod = pl.BlockSpec((1, r, D_MODEL), lambda i: (i // per, 0, 0))
    full = lambda a: pl.BlockSpec(a.shape, lambda i: (0,) * a.ndim)
    vec = pl.BlockSpec((1, D_MODEL), lambda i: (0, 0))
    tri = jnp.asarray(np.triu(np.ones((tm, tm), np.float32), 1), BF16)
    brb = jnp.broadcast_to(b_router[:, None], (N_EXPERTS, tm)).astype(F32)
    wr_hi = w_router.astype(BF16)
    w_router = jnp.stack([wr_hi, (w_router - wr_hi.astype(F32)).astype(BF16)])
    route = pl.BlockSpec((TOP_K, tm), lambda i: (0, i))
    return pl.pallas_call(
        _post_kernel,
        grid=(t // tm,),
        in_specs=[pl.BlockSpec((tm, D_MODEL), lambda i: (i, 0)),
                  pl.BlockSpec((tm, Q_W), lambda i: (i, 0)),
                  pl.BlockSpec((tm, S5_W), lambda i: (i, 0)),
                  mod, mod, mod, vec, vec, full(wglu), full(wout), full(w_router), full(brb),
                  full(tri), full(cnt_in)],
        out_specs=[pl.BlockSpec((tm, D_MODEL), lambda i: (i, 0)),
                   pl.BlockSpec((tm, D_MODEL), lambda i: (i, 0)),
                   pl.BlockSpec((tm, D_MODEL // 2), lambda i: (i, 0)),
                   route, route, route, full(cnt_in)],
        out_shape=[jax.ShapeDtypeStruct((t, D_MODEL), F32),
                   jax.ShapeDtypeStruct((t, D_MODEL), BF16),
                   jax.ShapeDtypeStruct((t, D_MODEL // 2), U32),
                   jax.ShapeDtypeStruct((TOP_K, t), I32),
                   jax.ShapeDtypeStruct((TOP_K, t), F32),
                   jax.ShapeDtypeStruct((TOP_K, t), I32),
                   jax.ShapeDtypeStruct(cnt_in.shape, F32)],
        scratch_shapes=[pltpu.VMEM(cnt_in.shape, F32)],
        compiler_params=_cp("arbitrary"),
        name="post_mix_router",
    )(x, o_att, y5, gm, shf, scf, g_post.reshape(1, D_MODEL), g_pre.reshape(1, D_MODEL),
      wglu, wout, w_router, brb, tri, cnt_in)


def _dest_kernel(e_ref, p_ref, ps_ref, o_ref):
    tm = e_ref.shape[1]
    eio = lax.broadcasted_iota(I32, (N_EXPERTS, tm), 0)
    e = e_ref[...]
    start = ps_ref[...]
    rows = [jnp.sum(jnp.where(eio == e[k:k + 1], start, 0.0), axis=0, keepdims=True)
            for k in range(TOP_K)]
    o_ref[...] = jnp.concatenate(rows, axis=0).astype(I32) + p_ref[...]


def _dest(eidx, pos, pad_start, tm):
    t = eidx.shape[1]
    route = pl.BlockSpec((TOP_K, tm), lambda i: (0, i))
    start = jnp.broadcast_to(pad_start.astype(F32)[:, None], (N_EXPERTS, tm))
    return pl.pallas_call(
        _dest_kernel,
        grid=(t // tm,),
        in_specs=[route, route, pl.BlockSpec((N_EXPERTS, tm), lambda i: (0, 0))],
        out_specs=route,
        out_shape=jax.ShapeDtypeStruct((TOP_K, t), I32),
        compiler_params=_cp("arbitrary"),
        name="moe_dest",
    )(eidx, pos, start)


def _moe_kernel(be_ref, nb_ref, x_ref, wg_ref, wu_ref, wd_ref, y_ref, wg_sc, wu_sc, wd_sc):
    j = pl.program_id(0)

    @pl.when(j < nb_ref[0])
    def _():
        @pl.when((j == 0) | (be_ref[j] != be_ref[jnp.maximum(j - 1, 0)]))
        def _():
            wg_sc[...] = wg_ref[0].astype(BF16)
            wu_sc[...] = wu_ref[0].astype(BF16)
            wd_sc[...] = wd_ref[0].astype(BF16)

        for r0 in range(0, MOE_BLK, MOE_BLK // 2):
            rows = pl.ds(r0, MOE_BLK // 2)
            x = jnp.concatenate(_unpack_bf16_pairs(x_ref[rows, :]), axis=1).astype(BF16)
            a = _dot(x, wg_sc[...])
            b = _dot(x, wu_sc[...])
            y = _dot((_silu(a) * b).astype(BF16), wd_sc[...])
            y_ref[rows, :] = _pack_bf16_pairs(y)

    @pl.when(j >= nb_ref[0])
    def _():
        y_ref[...] = jnp.zeros(y_ref.shape, U32)


def _moe(xs, blk_e, nb_used, w_g, w_u, w_d):
    n_slot = xs.shape[0]
    n_blk = n_slot // MOE_BLK
    last = lambda j, nb: jnp.maximum(jnp.minimum(j, nb[0] - 1), 0)
    row = lambda j, be, nb: (last(j, nb), 0)
    wsel = lambda j, be, nb: (be[last(j, nb)], 0, 0)
    return pl.pallas_call(
        _moe_kernel,
        grid_spec=pltpu.PrefetchScalarGridSpec(
            num_scalar_prefetch=2,
            grid=(n_blk,),
            in_specs=[pl.BlockSpec((MOE_BLK, D_MODEL // 2), row),
                      pl.BlockSpec((1, D_MODEL, D_EXPERT), wsel),
                      pl.BlockSpec((1, D_MODEL, D_EXPERT), wsel),
                      pl.BlockSpec((1, D_EXPERT, D_MODEL), wsel)],
            out_specs=pl.BlockSpec((MOE_BLK, D_MODEL // 2), lambda j, be, nb: (j, 0)),
            scratch_shapes=[pltpu.VMEM((D_MODEL, D_EXPERT), BF16), pltpu.VMEM((D_MODEL, D_EXPERT), BF16),
                            pltpu.VMEM((D_EXPERT, D_MODEL), BF16)]),
        out_shape=jax.ShapeDtypeStruct((n_slot, D_MODEL // 2), U32),
        compiler_params=_cp("arbitrary"),
        name="moe_experts",
    )(blk_e, nb_used, xs, w_g, w_u, w_d)


def _fin_kernel(x1_ref, h2_ref, yg_ref, w_ref, gf_ref, gpost_ref, wsg_ref, wsu_ref, wsd_ref, o_ref):
    w = w_ref[...]
    f_lo, f_hi = None, None
    for k in range(TOP_K):
        lo, hi = _unpack_bf16_pairs(yg_ref[k])
        f_lo = w[:, k:k + 1] * lo if k == 0 else f_lo + w[:, k:k + 1] * lo
        f_hi = w[:, k:k + 1] * hi if k == 0 else f_hi + w[:, k:k + 1] * hi
    f = jnp.concatenate([f_lo, f_hi], axis=1)
    hb = h2_ref[...]
    sh = (_silu(_dot(hb, wsg_ref[...])) * _dot(hb, wsu_ref[...])).astype(BF16)
    f = f + _dot(sh, wsd_ref[...])
    o_ref[...] = x1_ref[...] + gf_ref[0] * _rms(f, gpost_ref[...])


def _final(x1, h2, yg, wts, gf, g_post, wsg, wsu, wsd, tm, rows_per_mod):
    t = x1.shape[0]
    r = gf.shape[1]
    per = rows_per_mod // tm
    full = lambda a: pl.BlockSpec(a.shape, lambda i: (0,) * a.ndim)
    return pl.pallas_call(
        _fin_kernel,
        grid=(t // tm,),
        in_specs=[pl.BlockSpec((tm, D_MODEL), lambda i: (i, 0)),
                  pl.BlockSpec((tm, D_MODEL), lambda i: (i, 0)),
                  pl.BlockSpec((TOP_K, tm, D_MODEL // 2), lambda i: (0, i, 0)),
                  pl.BlockSpec((tm, TOP_K), lambda i: (i, 0)),
                  pl.BlockSpec((1, r, D_MODEL), lambda i: (i // per, 0, 0)),
                  pl.BlockSpec((1, D_MODEL), lambda i: (0, 0)),
                  full(wsg), full(wsu), full(wsd)],
        out_specs=pl.BlockSpec((tm, D_MODEL), lambda i: (i, 0)),
        out_shape=jax.ShapeDtypeStruct((t, D_MODEL), F32),
        compiler_params=_cp("arbitrary"),
        name="moe_combine_final",
    )(x1, h2, yg, wts, gf, g_post.reshape(1, D_MODEL), wsg, wsu, wsd)


def _even_odd(a, axis):
    n = a.shape[axis]
    order = np.concatenate([np.arange(0, n, 2), np.arange(1, n, 2)])
    return jnp.take(a, order, axis=axis)


def _layer(xp, xs, c_prompt, c_sample, page_table, ck_c, cv_c, ck_s, cv_s, cw_k, cw_v, st_re, st_im,
           w_ada, b_ada, g_pre_mix, g_post_mix, g_pre_ffn, g_post_ffn, w_in, pe, wk1, wk2, wv1, wv2,
           rel_bias, lam_re, lam_im, log_dt, b_re, b_im, c_re, c_im, d_skip, w_glu, w_out,
           w_router, b_router, w_eg, w_eu, w_ed, w_sg, w_su, w_sd):
    batch, seq, _ = xp.shape
    n_dec = xs.shape[0]
    tp = batch * seq
    x_p = xp.reshape(tp, D_MODEL)
    x_s = xs.reshape(n_dec, D_MODEL)

    n_c = batch + n_dec
    n_pad = -(-n_c // 8) * 8
    c_all = jnp.pad(jnp.concatenate([c_prompt, c_sample], axis=0), ((0, n_pad - n_c), (0, 0)))
    mod = _adaln(c_all, w_ada, b_ada)
    mod_p = [m.reshape(batch, 1, D_MODEL) for m in jnp.split(mod[:batch], 6, axis=-1)]
    mod_s = [m.reshape(1, n_dec, D_MODEL) for m in jnp.split(mod[batch:n_c], 6, axis=-1)]

    wn, wt = _pre_weights(w_in)
    (kct_p, vct_p, kst_p, vst_p, kwt_p, vwt_p, kc_p, vc_p, u_p, ksb, kwb, qt_p, vst, vsn, vwt, gt_p) = \
        _pre_mix_prompt(x_p, mod_p[1], mod_p[0], g_pre_mix, wn, wt, 512, batch, seq)
    (kc_s, vc_s, ks_s, vs_s, kw_s, vw_s, u_s, qt_s, gt_s) = _pre_mix_sample(
        x_s, mod_s[1], mod_s[0], g_pre_mix, wn, wt)

    n_cmp_p = seq // CMP_BLOCK
    kcmp_p = _compress(kc_p.reshape(batch * n_cmp_p, CMP_BLOCK * KV_W), *_cmp_weights(pe, wk1, wk2))
    vcmp_p = _compress(vc_p.reshape(batch * n_cmp_p, CMP_BLOCK * KV_W), *_cmp_weights(pe, wv1, wv2))
    kcmp_p = _even_odd(kcmp_p.reshape(batch, n_cmp_p, N_KV, HEAD_DIM), 1).transpose(0, 2, 1, 3)
    vcmp_p = _even_odd(vcmp_p.reshape(batch, n_cmp_p, N_KV, HEAD_DIM), 1).transpose(0, 2, 3, 1)
    bias_log2 = rel_bias * LOG2E
    wtab, ctab = _bias_tables(bias_log2)
    o_p = _attn_prompt(qt_p, gt_p, kcmp_p.astype(BF16), vcmp_p.astype(BF16), ksb, vst, vsn, kwb, vwt,
                       wtab, ctab, batch, seq)

    n_pages = page_table.shape[1]
    n_cmp_s = n_pages * CMP_PER_PAGE
    pool = lambda cache: cache.transpose(0, 2, 3, 1).reshape(cache.shape[0], KV_W, PAGE)
    kcmp_s, vcmp_s = _compress_pages(page_table, pool(ck_c), pool(cv_c),
                                     _cmp_pages_weights(pe, wk1, wk2), _cmp_pages_weights(pe, wv1, wv2))
    by_head = lambda a: _even_odd(
        a.reshape(n_dec, n_pages, N_KV, CMP_PER_PAGE, HEAD_DIM).transpose(0, 2, 1, 3, 4).reshape(
            n_dec, N_KV, n_cmp_s, HEAD_DIM), 2)
    win_k = jnp.concatenate([cw_k[:, 1:], kw_s.reshape(n_dec, 1, N_KV, HEAD_DIM)], axis=1)
    win_v = jnp.concatenate([cw_v[:, 1:], vw_s.reshape(n_dec, 1, N_KV, HEAD_DIM)], axis=1)
    q_s = qt_s.T.reshape(n_dec, N_HEADS, HEAD_DIM)
    gates_s = gt_s[:, :3 * GQ].reshape(N_KV, 3, GQ, n_dec).transpose(3, 0, 2, 1).reshape(n_dec, N_HEADS, 3)
    o_s = _attn_sample(q_s, gates_s, by_head(kcmp_s).astype(BF16), by_head(vcmp_s).astype(BF16),
                       page_table, ck_s, cv_s, ks_s, vs_s, win_k, win_v, bias_log2)

    y5_p, s5r_p, s5i_p = _s5_prompt(
        u_p, _s5_prep(lam_re, lam_im, log_dt, b_re, b_im, c_re, c_im, d_skip, S5_CHUNK), batch, seq)
    y5_s, s5r_s, s5i_s = _s5_sample(
        u_s, _s5_prep(lam_re, lam_im, log_dt, b_re, b_im, c_re, c_im, d_skip, 1), st_re, st_im)

    wglu_b, wout_b = w_glu.astype(BF16), w_out.astype(BF16)
    cnt0 = jnp.zeros((N_EXPERTS, 128), F32)
    tm_q = 256
    x1_p, h2_p, h2p_p, e_p, wt_p, pos_p, cnt1 = _post_mix(
        x_p, o_p, y5_p, mod_p[2], mod_p[3], mod_p[4], g_post_mix, g_pre_ffn, wglu_b, wout_b,
        w_router, b_router, cnt0, tm_q, seq)
    x1_s, h2_s, h2p_s, e_s, wt_s, pos_s, cnt2 = _post_mix(
        x_s, o_s, y5_s, mod_s[2], mod_s[3], mod_s[4], g_post_mix, g_pre_ffn, wglu_b, wout_b,
        w_router, b_router, cnt1, n_dec, n_dec)

    t_all = tp + n_dec
    counts = cnt2[:, 0].astype(I32)
    padded = (counts + MOE_BLK - 1) // MOE_BLK * MOE_BLK
    pad_end = jnp.cumsum(padded)
    dest_p = _dest(e_p, pos_p, pad_end - padded, 1024)
    dest_s = _dest(e_s, pos_s, pad_end - padded, n_dec)
    dest = jnp.concatenate([dest_p, dest_s], axis=1)
    n_blk = -(-(t_all * TOP_K) // MOE_BLK) + N_EXPERTS
    n_slot = n_blk * MOE_BLK
    tok = jnp.broadcast_to(jnp.arange(t_all, dtype=I32)[None], (TOP_K, t_all))
    rows = (jnp.arange(n_slot, dtype=I32) % t_all).at[dest.reshape(-1)].set(
        tok.reshape(-1), unique_indices=True)
    xs_rows = jnp.concatenate([h2p_p, h2p_s], axis=0)[rows]
    blk_e = jnp.minimum(jnp.searchsorted(pad_end, jnp.arange(n_blk, dtype=I32) * MOE_BLK, side='right'),
                        N_EXPERTS - 1).astype(I32)
    nb_used = (pad_end[-1:] // MOE_BLK).astype(I32)
    ys = _moe(xs_rows, blk_e, nb_used, w_eg, w_eu, w_ed)
    yg_p = ys[dest_p]
    yg_s = ys[dest_s]

    wsg, wsu, wsd = w_sg.astype(BF16), w_su.astype(BF16), w_sd.astype(BF16)
    out_p = _final(x1_p, h2_p, yg_p, wt_p.T, mod_p[5], g_post_ffn, wsg, wsu, wsd, tm_q, seq)
    out_s = _final(x1_s, h2_s, yg_s, wt_s.T, mod_s[5], g_post_ffn, wsg, wsu, wsd, n_dec, n_dec)

    n_win = min(WINDOW, seq)
    rows_p = lambda a: a.transpose(0, 3, 1, 2)
    kv5 = lambda a: a.reshape(n_dec, 1, N_KV, HEAD_DIM)
    st_p = (rows_p(kct_p), rows_p(vct_p), rows_p(kst_p), rows_p(vst_p),
            rows_p(kwt_p)[:, seq - n_win:], rows_p(vwt_p)[:, seq - n_win:], s5r_p, s5i_p)
    st_s = (kv5(kc_s), kv5(vc_s), kv5(ks_s), kv5(vs_s), win_k, win_v, s5r_s, s5i_s)
    return out_p.reshape(batch, seq, D_MODEL), out_s.reshape(n_dec, 1, D_MODEL), st_p, st_s


def kernel(x_prompt, x_sample, c_prompt, c_sample, page_table, cache_cmp_k, cache_cmp_v, cache_sel_k, cache_sel_v, cache_win_k, cache_win_v, state_s5_re, state_s5_im, w_ada, b_ada, g_pre_mix, g_post_mix, g_pre_ffn, g_post_ffn, w_in, pe_cmp, w_cmp_k1, w_cmp_k2, w_cmp_v1, w_cmp_v2, rel_bias, lam_re, lam_im, log_dt, b_re, b_im, c_re, c_im, d_skip, w_glu, w_out, w_router, b_router, w_exp_gate, w_exp_up, w_exp_down, w_sh_gate, w_sh_up, w_sh_down):
    depth = w_in.shape[0]
    xp, xs = x_prompt, x_sample
    p_states, s_states = [], []
    for l in range(depth):
        xp, xs, st_p, st_s = _layer(
            xp, xs, c_prompt, c_sample, page_table, cache_cmp_k[l], cache_cmp_v[l], cache_sel_k[l],
            cache_sel_v[l], cache_win_k[l], cache_win_v[l], state_s5_re[l], state_s5_im[l],
            w_ada[l], b_ada[l], g_pre_mix[l], g_post_mix[l], g_pre_ffn[l], g_post_ffn[l], w_in[l],
            pe_cmp[l], w_cmp_k1[l], w_cmp_k2[l], w_cmp_v1[l], w_cmp_v2[l], rel_bias,
            lam_re[l], lam_im[l], log_dt[l], b_re[l], b_im[l], c_re[l], c_im[l], d_skip[l],
            w_glu[l], w_out[l], w_router[l], b_router[l], w_exp_gate[l], w_exp_up[l], w_exp_down[l],
            w_sh_gate[l], w_sh_up[l], w_sh_down[l])
        p_states.append(st_p)
        s_states.append(st_s)
    p_st = tuple(jnp.stack(a) for a in zip(*p_states))
    s_st = tuple(jnp.stack(a) for a in zip(*s_states))
    return (xp, xs) + p_st + s_st
```

```python
import functools
import math

import numpy as np
import jax
import jax.numpy as jnp
from jax import lax
from jax.experimental import pallas as pl
from jax.experimental.pallas import tpu as pltpu

F32 = jnp.float32
BF16 = jnp.bfloat16
I32 = jnp.int32
U32 = jnp.uint32

D_MODEL = 1024
N_HEADS = 8
HEAD_DIM = 64
N_KV = 2
GQ = N_HEADS // N_KV
Q_W = N_HEADS * HEAD_DIM
KV_W = N_KV * HEAD_DIM
S5_W = D_MODEL - Q_W
S5_H = 16
S5_G = S5_W // S5_H
S5_P = 64
GATE_OFF = Q_W + 6 * KV_W
U_OFF = GATE_OFF + 3 * N_HEADS
CMP_BLOCK = 32
SEL_BLOCK = 64
N_SEL = 16
WINDOW = 512
NUM_BUCKETS = 32
REL_MAX_DIST = 128
N_EXPERTS = 256
TOP_K = 8
N_EGROUPS = 8
TOPK_GROUPS = 4
EGROUP = N_EXPERTS // N_EGROUPS
D_EXPERT = 256
ROUTED_SCALE = 2.5
EPS = 1e-6
SCALE = HEAD_DIM ** -0.5
PAGE = 128
CMP_PER_PAGE = PAGE // CMP_BLOCK

QT = 128
VT = 2 * QT
N_DELTA = WINDOW // QT + 1
V_ROWS = HEAD_DIM + 16
LOG2E = math.log2(math.e)
S5_CHUNK = 32
S5_ROWS = 8
MOE_BLK = 512
NEG = -2e30
M_INIT = -1e30
VMEM_LIMIT = 56 * 1024 * 1024
WT_ROWS = Q_W + 6 * KV_W + 32
HIGHEST = lax.Precision.HIGHEST


def _cp(*sem):
    return pltpu.CompilerParams(dimension_semantics=sem, vmem_limit_bytes=VMEM_LIMIT)


def _dot(a, b, precision=None):
    return jnp.dot(a, b, preferred_element_type=F32, precision=precision)


def _dot_nt(a, b):
    return lax.dot_general(a, b, (((1,), (1,)), ((), ())), preferred_element_type=F32)


def _sigmoid(x):
    return 1.0 / (1.0 + jnp.exp(-x))


def _silu(x):
    return x * _sigmoid(x)


def _rms(x, g):
    return x * lax.rsqrt(jnp.mean(x * x, axis=-1, keepdims=True) + EPS) * g


def _pack_bf16_pairs(x):
    c = x.shape[1] // 2
    bits = pltpu.bitcast(x.astype(BF16).astype(F32), U32)
    return (bits[:, :c] >> 16) | bits[:, c:]


def _unpack_bf16_pairs(xp):
    return pltpu.bitcast(xp << 16, F32), pltpu.bitcast(xp & jnp.uint32(0xFFFF0000), F32)


def _bucket_table(n):
    d = np.arange(n)
    exact = NUM_BUCKETS // 2
    nf = np.maximum(d, 1).astype(np.float32)
    large = exact + (np.log(nf / np.float32(exact)) / np.float32(math.log(REL_MAX_DIST / exact))
                     * np.float32(NUM_BUCKETS - exact)).astype(np.int32)
    return np.where(d < exact, d, np.minimum(large, NUM_BUCKETS - 1)).astype(np.int32)


def _ada_kernel(c_ref, w_ref, b_ref, o_ref):
    a = _silu(c_ref[...]).astype(BF16)
    o_ref[...] = _dot(a, w_ref[...].astype(BF16)) + b_ref[...]


def _adaln(c, w_ada, b_ada):
    n, d = c.shape
    n_out = w_ada.shape[1]
    tn = 1024
    return pl.pallas_call(
        _ada_kernel,
        grid=(n_out // tn,),
        in_specs=[pl.BlockSpec((n, d), lambda j: (0, 0)),
                  pl.BlockSpec((d, tn), lambda j: (0, j)),
                  pl.BlockSpec((1, tn), lambda j: (0, j))],
        out_specs=pl.BlockSpec((n, tn), lambda j: (0, j)),
        out_shape=jax.ShapeDtypeStruct((n, n_out), F32),
        compiler_params=_cp("arbitrary"),
        name="adaln",
    )(c, w_ada, b_ada.reshape(1, n_out))


def _pre_project(x_ref, sc_ref, sh_ref, g_ref, wn_ref, wt_ref):
    h = _rms(x_ref[...], g_ref[...]) * (1.0 + sc_ref[0]) + sh_ref[0]
    hb = h.astype(BF16)
    return _dot(hb, wn_ref[...]), _dot_nt(wt_ref[...], hb)


def _pre_prompt_kernel(x_ref, sc_ref, sh_ref, g_ref, wn_ref, wt_ref,
                       kct_ref, vct_ref, kst_ref, vst32_ref, kwt_ref, vwt32_ref, kc_ref, vc_ref, u_ref,
                       ksb_ref, kwb_ref, qt_ref, vst_ref, vsn_ref, vwt_ref, gt_ref, kv_sc):
    tm = x_ref.shape[0]
    zn, zt = _pre_project(x_ref, sc_ref, sh_ref, g_ref, wn_ref, wt_ref)
    trans = lambda j, g: zt[Q_W + j * KV_W + g * HEAD_DIM:Q_W + j * KV_W + (g + 1) * HEAD_DIM]
    for j, ref in enumerate((kct_ref, vct_ref, kst_ref, vst32_ref, kwt_ref, vwt32_ref)):
        for g in range(N_KV):
            ref[0, g] = trans(j, g)
    kv_sc[0] = zn[:, 0:KV_W]
    kv_sc[1] = zn[:, KV_W:2 * KV_W]
    for j in range(CMP_BLOCK):
        kc_ref[:, j * KV_W:(j + 1) * KV_W] = kv_sc[0, pl.ds(j, tm // CMP_BLOCK, stride=CMP_BLOCK), :]
        vc_ref[:, j * KV_W:(j + 1) * KV_W] = kv_sc[1, pl.ds(j, tm // CMP_BLOCK, stride=CMP_BLOCK), :]
    u_ref[...] = zn[:, 6 * KV_W:]
    for g in range(N_KV):
        ksb_ref[g] = zn[:, 2 * KV_W + g * HEAD_DIM:2 * KV_W + (g + 1) * HEAD_DIM].astype(BF16)
        kwb_ref[g] = zn[:, 4 * KV_W + g * HEAD_DIM:4 * KV_W + (g + 1) * HEAD_DIM].astype(BF16)
        ones = jnp.where(lax.broadcasted_iota(I32, (V_ROWS - HEAD_DIM, tm), 0) == 0, 1.0, 0.0)
        vs_t = jnp.concatenate([trans(3, g), ones], axis=0).astype(BF16)
        vw_t = jnp.concatenate([trans(5, g), ones], axis=0).astype(BF16)
        for c in range(tm // VT):
            vst_ref[g, c] = vs_t[:, c * VT:(c + 1) * VT]
        for c in range(tm // QT):
            vsn_ref[g, c] = vs_t[:, c * QT:(c + 1) * QT]
            vwt_ref[g, c] = vw_t[:, c * QT:(c + 1) * QT]
        r2 = Q_W + 6 * KV_W + g * 16
        gt_ref[g] = _sigmoid(zt[r2:r2 + 16])
    qt_ref[...] = zt[:Q_W].astype(BF16)


def _pre_sample_kernel(x_ref, sc_ref, sh_ref, g_ref, wn_ref, wt_ref,
                       kc_ref, vc_ref, ks_ref, vs_ref, kw_ref, vw_ref, u_ref, qt_ref, gt_ref):
    zn, zt = _pre_project(x_ref, sc_ref, sh_ref, g_ref, wn_ref, wt_ref)
    for j, ref in enumerate((kc_ref, vc_ref, ks_ref, vs_ref, kw_ref, vw_ref)):
        ref[...] = zn[:, j * KV_W:(j + 1) * KV_W]
    u_ref[...] = zn[:, 6 * KV_W:]
    qt_ref[...] = zt[:Q_W].astype(BF16)
    for g in range(N_KV):
        r2 = Q_W + 6 * KV_W + g * 16
        gt_ref[g] = _sigmoid(zt[r2:r2 + 16])


def _pre_weights(w_in):
    wn = jnp.concatenate([w_in[:, Q_W:GATE_OFF], w_in[:, U_OFF:]], axis=1).astype(BF16)
    gate_cols = []
    for g in range(N_KV):
        for j in range(3):
            for hh in range(GQ):
                gate_cols.append(GATE_OFF + (g * GQ + hh) * 3 + j)
        gate_cols.extend([GATE_OFF] * 4)
    wt = jnp.concatenate([
        w_in[:, :Q_W] * (SCALE * LOG2E),
        w_in[:, Q_W:GATE_OFF],
        w_in[:, np.array(gate_cols)],
    ], axis=1).T.astype(BF16)
    return wn, wt


def _pre_in_specs(tm, r, per, wn, wt):
    return [pl.BlockSpec((tm, D_MODEL), lambda i: (i, 0)),
            pl.BlockSpec((1, r, D_MODEL), lambda i: (i // per, 0, 0)),
            pl.BlockSpec((1, r, D_MODEL), lambda i: (i // per, 0, 0)),
            pl.BlockSpec((1, D_MODEL), lambda i: (0, 0)),
            pl.BlockSpec(wn.shape, lambda i: (0, 0)),
            pl.BlockSpec(wt.shape, lambda i: (0, 0))]


def _pre_mix_prompt(x, sc, sh, g_pre, wn, wt, tm, batch, seq):
    t = x.shape[0]
    per = seq // tm
    f = lambda shape: jax.ShapeDtypeStruct(shape, F32)
    b = lambda shape: jax.ShapeDtypeStruct(shape, BF16)
    tr_spec = pl.BlockSpec((1, N_KV, HEAD_DIM, tm), lambda i: (i // per, 0, 0, i % per))
    kv_spec = pl.BlockSpec((tm // CMP_BLOCK, CMP_BLOCK * KV_W), lambda i: (i, 0))
    return pl.pallas_call(
        _pre_prompt_kernel,
        grid=(t // tm,),
        in_specs=_pre_in_specs(tm, 1, per, wn, wt),
        out_specs=[tr_spec] * 6 + [
            kv_spec, kv_spec,
            pl.BlockSpec((tm, S5_W), lambda i: (i, 0)),
            pl.BlockSpec((N_KV, tm, HEAD_DIM), lambda i: (0, i, 0)),
            pl.BlockSpec((N_KV, tm, HEAD_DIM), lambda i: (0, i, 0)),
            pl.BlockSpec((Q_W, tm), lambda i: (0, i)),
            pl.BlockSpec((N_KV, tm // VT, V_ROWS, VT), lambda i: (0, i, 0, 0)),
            pl.BlockSpec((N_KV, tm // QT, V_ROWS, QT), lambda i: (0, i, 0, 0)),
            pl.BlockSpec((N_KV, tm // QT, V_ROWS, QT), lambda i: (0, i, 0, 0)),
            pl.BlockSpec((N_KV, 16, tm), lambda i: (0, 0, i))],
        out_shape=[f((batch, N_KV, HEAD_DIM, seq))] * 6 + [
            f((t // CMP_BLOCK, CMP_BLOCK * KV_W)), f((t // CMP_BLOCK, CMP_BLOCK * KV_W)),
            f((t, S5_W)), b((N_KV, t, HEAD_DIM)), b((N_KV, t, HEAD_DIM)), b((Q_W, t)),
            b((N_KV, t // VT, V_ROWS, VT)), b((N_KV, t // QT, V_ROWS, QT)),
            b((N_KV, t // QT, V_ROWS, QT)), f((N_KV, 16, t))],
        scratch_shapes=[pltpu.VMEM((2, tm, KV_W), F32)],
        compiler_params=_cp("arbitrary"),
        name="pre_mix_prompt",
    )(x, sc, sh, g_pre.reshape(1, D_MODEL), wn, wt)


def _pre_mix_sample(x, sc, sh, g_pre, wn, wt):
    t = x.shape[0]
    f = lambda shape: jax.ShapeDtypeStruct(shape, F32)
    kv_spec = pl.BlockSpec((t, KV_W), lambda i: (i, 0))
    return pl.pallas_call(
        _pre_sample_kernel,
        grid=(1,),
        in_specs=_pre_in_specs(t, t, 1, wn, wt),
        out_specs=[kv_spec] * 6 + [pl.BlockSpec((t, S5_W), lambda i: (i, 0)),
                                   pl.BlockSpec((Q_W, t), lambda i: (0, i)),
                                   pl.BlockSpec((N_KV, 16, t), lambda i: (0, 0, i))],
        out_shape=[f((t, KV_W))] * 6 + [f((t, S5_W)), jax.ShapeDtypeStruct((Q_W, t), BF16),
                                        f((N_KV, 16, t))],
        compiler_params=_cp("arbitrary"),
        name="pre_mix_sample",
    )(x, sc, sh, g_pre.reshape(1, D_MODEL), wn, wt)


def _cmp_kernel(x_ref, pe_ref, w1_ref, w2_ref, o_ref):
    xb = (x_ref[...] + pe_ref[...]).astype(BF16)
    hid = _silu(_dot(xb, w1_ref[...]))
    o_ref[...] = _dot(hid.astype(BF16), w2_ref[...])


def _cmp_weights(pe, w1, w2):
    eye = jnp.eye(N_KV, dtype=F32)
    w1b = jnp.einsum('jde,gh->jgdhe', w1, eye).reshape(CMP_BLOCK * KV_W, KV_W).astype(BF16)
    w2b = jnp.einsum('ef,gh->gehf', w2, eye).reshape(KV_W, KV_W).astype(BF16)
    peb = jnp.broadcast_to(pe[:, None, :], (CMP_BLOCK, N_KV, HEAD_DIM)).reshape(1, CMP_BLOCK * KV_W)
    return peb, w1b, w2b


def _compress(x, peb, w1b, w2b):
    r, k = x.shape
    tr = min(256, r)
    return pl.pallas_call(
        _cmp_kernel,
        grid=(r // tr,),
        in_specs=[pl.BlockSpec((tr, k), lambda i: (i, 0)),
                  pl.BlockSpec((1, k), lambda i: (0, 0)),
                  pl.BlockSpec((k, KV_W), lambda i: (0, 0)),
                  pl.BlockSpec((KV_W, KV_W), lambda i: (0, 0))],
        out_specs=pl.BlockSpec((tr, KV_W), lambda i: (i, 0)),
        out_shape=jax.ShapeDtypeStruct((r, KV_W), F32),
        compiler_params=_cp("arbitrary"),
        name="compress",
    )(x, peb, w1b, w2b)


def _cmp_pages_kernel(pt_ref, ck_hbm, cv_hbm, w1k_ref, w1v_ref, bk_ref, bv_ref, w2k_ref, w2v_ref,
                      ok_ref, ov_ref, kbuf, vbuf, sem, *, n_pages):
    i = pl.program_id(0)
    n_rows = ok_ref.shape[0]

    def page_copies(seq, slot):
        copies = []
        for p in range(n_pages):
            page = pt_ref[seq * n_pages + p]
            rows = pl.ds(p * KV_W, KV_W)
            copies.append(pltpu.make_async_copy(ck_hbm.at[page], kbuf.at[slot, rows], sem.at[0, slot]))
            copies.append(pltpu.make_async_copy(cv_hbm.at[page], vbuf.at[slot, rows], sem.at[1, slot]))
        return copies

    @pl.when(i == 0)
    def _():
        for c in page_copies(0, 0):
            c.start()

    @pl.when(i + 1 < pl.num_programs(0))
    def _():
        for c in page_copies(i + 1, (i + 1) % 2):
            c.start()

    slot = i % 2
    for c in page_copies(i, slot):
        c.wait()
    for x_ref, w1_ref, b_ref, w2_ref, o_ref in ((kbuf.at[slot], w1k_ref, bk_ref, w2k_ref, ok_ref),
                                               (vbuf.at[slot], w1v_ref, bv_ref, w2v_ref, ov_ref)):
        acc = jnp.zeros((n_rows, CMP_PER_PAGE * HEAD_DIM), F32)
        for dd in range(HEAD_DIM // 2):
            r0 = x_ref[pl.ds(2 * dd, n_rows, stride=HEAD_DIM), :]
            r1 = x_ref[pl.ds(2 * dd + 1, n_rows, stride=HEAD_DIM), :]
            acc = acc + _dot(jnp.concatenate([r0, r1], axis=1).astype(BF16), w1_ref[dd])
        hid = _silu(acc + b_ref[...])
        o_ref[...] = _dot(hid.astype(BF16), w2_ref[...])


def _cmp_pages_weights(pe, w1, w2):
    eye = jnp.eye(CMP_PER_PAGE, dtype=F32)
    w1t = jnp.einsum('jde,mn->dmjne', w1, eye).reshape(HEAD_DIM // 2, 2 * PAGE, CMP_PER_PAGE * HEAD_DIM)
    bias = jnp.einsum('jd,jde->e', pe, w1, precision=HIGHEST)
    bias = jnp.tile(bias, CMP_PER_PAGE).reshape(1, CMP_PER_PAGE * HEAD_DIM)
    w2t = jnp.einsum('ef,mn->menf', w2, eye).reshape(CMP_PER_PAGE * HEAD_DIM, CMP_PER_PAGE * HEAD_DIM)
    return w1t.astype(BF16), bias, w2t.astype(BF16)


def _compress_pages(page_table, ck, cv, wk, wv):
    n_seq, n_pages = page_table.shape
    rows_in = n_pages * KV_W
    rows_out = n_pages * N_KV
    wcols = CMP_PER_PAGE * HEAD_DIM
    full = lambda a: pl.BlockSpec(a.shape, lambda i, pt: (0,) * a.ndim)
    hbm = pl.BlockSpec(memory_space=pl.ANY)
    o_spec = pl.BlockSpec((rows_out, wcols), lambda i, pt: (i, 0))
    o_shape = jax.ShapeDtypeStruct((n_seq * rows_out, wcols), F32)
    return pl.pallas_call(
        functools.partial(_cmp_pages_kernel, n_pages=n_pages),
        grid_spec=pltpu.PrefetchScalarGridSpec(
            num_scalar_prefetch=1,
            grid=(n_seq,),
            in_specs=[hbm, hbm, full(wk[0]), full(wv[0]), full(wk[1]), full(wv[1]),
                      full(wk[2]), full(wv[2])],
            out_specs=[o_spec, o_spec],
            scratch_shapes=[pltpu.VMEM((2, rows_in, PAGE), F32), pltpu.VMEM((2, rows_in, PAGE), F32),
                            pltpu.SemaphoreType.DMA((2, 2))]),
        out_shape=[o_shape, o_shape],
        compiler_params=_cp("arbitrary"),
        name="compress_pages",
    )(page_table.reshape(-1), ck, cv, wk[0], wv[0], wk[1], wv[1], wk[2], wv[2])


def _attn_kernel(q_ref, g_ref, kc_ref, vct_ref, ks_ref, vst_ref, vsn_ref, kw_ref, vwt_ref,
                 ctab_ref, wtab_ref, o_ref, mb_ref, mbf_ref, s_sc, p_sc, *, n_cmp):
    i = pl.program_id(2)
    lanes = GQ * QT
    qt = q_ref[...]
    q4 = jnp.concatenate([qt[hh * HEAD_DIM:(hh + 1) * HEAD_DIM] for hh in range(GQ)], axis=1)
    gates = g_ref[0]
    grow = lambda j: jnp.concatenate([gates[j * GQ + hh:j * GQ + hh + 1] for hh in range(GQ)], axis=1)

    def online(carry, s, v_t):
        m, acc = carry
        m_new = jnp.maximum(m, jnp.max(s, axis=0, keepdims=True))
        p = jnp.exp2(s - m_new)
        return m_new, jnp.exp2(m - m_new) * acc + _dot(v_t, p.astype(BF16))

    init = (jnp.full((1, lanes), M_INIT, F32), jnp.zeros((V_ROWS, lanes), F32))
    finish = lambda carry: carry[1][:HEAD_DIM] * (1.0 / jnp.maximum(carry[1][HEAD_DIM:HEAD_DIM + 1], 1e-30))
    block_rows = lambda ref, first, n, extra=0.0: jnp.concatenate(
        [jnp.broadcast_to(ref[pl.ds(first + c, 1), :] + extra, (SEL_BLOCK, lanes)) for c in range(n)],
        axis=0)

    def near_tiles(carry, k_ref, vt_ref, deltas, live, masked):
        ks, vs, bs = [], [], []
        for dl in deltas:
            kt = i - dl
            ktc = jnp.maximum(kt, 0)
            ks.append(k_ref[0, pl.ds(pl.multiple_of(ktc * QT, QT), QT), :])
            vs.append(vt_ref[0, ktc])
            bias = wtab_ref[0, jnp.where(live(dl, kt), dl, N_DELTA)]
            if masked:
                bias = bias + block_rows(mb_ref, (QT // SEL_BLOCK) * ktc, QT // SEL_BLOCK)
            bs.append(bias)
        s = _dot(jnp.concatenate(ks, axis=0), q4) + jnp.concatenate(bs, axis=0)
        return online(carry, s, jnp.concatenate(vs, axis=1))

    o_w = finish(near_tiles(init, kw_ref, vwt_ref, list(range(N_DELTA - 1, -1, -1)),
                            lambda dl, kt: kt >= 0, False))

    half = n_cmp // 2
    sc = _dot(kc_ref[0, 0], q4)
    rho = lax.broadcasted_iota(I32, (n_cmp, 16), 0)
    col = lax.broadcasted_iota(I32, (n_cmp, 16), 1)
    blk_n = 2 * jnp.where(rho >= half, rho - half, rho) + jnp.where(rho >= half, 1, 0)
    rel = blk_n - (4 * i - 4)
    want = jnp.where(rel < 0, 8, jnp.where(rel > 7, 9, rel))
    place = jnp.where(col == want, 1.0, 0.0).astype(BF16)
    sc = sc + _dot(place, ctab_ref[0, 0]) + _dot(place, ctab_ref[0, 1])
    mc = jnp.maximum(jnp.max(sc, axis=0, keepdims=True), M_INIT)
    ec = jnp.exp2(sc - mc)
    pc = ec * (1.0 / jnp.maximum(jnp.sum(ec, axis=0, keepdims=True), 1e-30))
    o_cw = grow(0) * _dot(vct_ref[0, 0], pc.astype(BF16)) + grow(2) * o_w

    ps = pc[:half] + pc[half:]
    imp = ps[:, 0:QT]
    for hh in range(1, GQ):
        imp = imp + ps[:, hh * QT:(hh + 1) * QT]
    n_blk = half
    blk = lax.broadcasted_iota(I32, (n_blk, QT), 0)
    tq = i * QT + lax.broadcasted_iota(I32, (n_blk, QT), 1)
    cur = lax.shift_right_logical(tq, 6)
    valid = blk * SEL_BLOCK <= tq
    forced = (blk == 0) | (blk == cur) | (blk == cur - 1)
    blkf = blk.astype(F32)

    def pick(_, carry):
        work, mb = carry
        mx = jnp.max(work, axis=0, keepdims=True)
        idx = jnp.min(jnp.where(work == mx, blkf, 1e9), axis=0, keepdims=True)
        hit = blkf == idx
        return jnp.where(hit, -jnp.inf, work), jnp.where(hit, 0.0, mb)

    _, mb = lax.fori_loop(0, min(N_SEL, n_blk) - 3, pick,
                          (jnp.where(valid, jnp.where(forced, -jnp.inf, imp), -jnp.inf),
                           jnp.where(valid, jnp.where(forced, 0.0, NEG), NEG)))
    for hh in range(GQ):
        mb_ref[:, hh * QT:(hh + 1) * QT] = mb
    mbf_ref[...] = mb_ref[...] + wtab_ref[0, 2, 0:1, :]

    n_far = jnp.maximum(i - 1, 0) // 2
    last_pair = ks_ref.shape[1] // VT - 1

    def pair_scores(t):
        ta = jnp.minimum(t, last_pair)
        k_t = ks_ref[0, pl.ds(pl.multiple_of(ta * VT, VT), VT), :]
        return _dot(k_t, q4) + block_rows(mbf_ref, (VT // SEL_BLOCK) * ta, VT // SEL_BLOCK,
                                          jnp.where(t < n_far, 0.0, NEG))

    def far_step(t, par, carry):
        m, acc, alpha_prev = carry
        pv = _dot(vst_ref[0, jnp.clip(t - 2, 0, last_pair)], p_sc[par])
        s = s_sc[1 - par]
        m_new = jnp.maximum(m, jnp.max(s, axis=0, keepdims=True))
        s_sc[par] = pair_scores(t)
        p_sc[1 - par] = jnp.exp2(s - m_new).astype(BF16)
        return m_new, alpha_prev * acc + pv, jnp.exp2(m - m_new)

    s_sc[0] = pair_scores(0)
    p_sc[1] = jnp.zeros((VT, lanes), BF16)
    m, acc, _ = lax.fori_loop(
        0, (n_far + 2) // 2,
        lambda u, carry: far_step(2 * u + 2, 0, far_step(2 * u + 1, 1, carry)),
        init + (jnp.ones((1, lanes), F32),))

    sel_live = lambda dl, kt: (kt >= 0) & ((dl < 2) | (i % 2 == 0))
    o_s = finish(near_tiles((m, acc), ks_ref, vsn_ref, [2, 1, 0], sel_live, True))
    o_t = o_cw + grow(1) * o_s
    o_hd = jnp.concatenate([o_t[:, hh * QT:(hh + 1) * QT] for hh in range(GQ)], axis=0)
    o_ref[...] = o_hd.T.astype(BF16)


def _bias_tables(rel_bias):
    span = N_DELTA * QT
    bt = _bucket_table(WINDOW)
    period = span + QT
    by_dist = jnp.concatenate([rel_bias[bt], jnp.full((period - WINDOW, N_HEADS), NEG, F32)], axis=0)
    toep = jnp.tile(by_dist.T, (1, QT))[:, :QT * (period - 1)].reshape(N_HEADS, QT, period - 1)
    tab = toep[:, :, :span].reshape(N_KV, GQ, QT, N_DELTA, QT)
    wtab = tab.transpose(0, 3, 2, 1, 4).reshape(N_KV, N_DELTA, QT, GQ * QT)
    wtab = jnp.concatenate([wtab, jnp.full((N_KV, 1, QT, GQ * QT), NEG, F32)], axis=1)
    ql = np.arange(QT)[None, :]
    r = np.arange(8)[:, None]
    d = ql + (4 * CMP_BLOCK - CMP_BLOCK + 1) - CMP_BLOCK * r
    near = jnp.where((d >= 0)[..., None], rel_bias[bt[np.clip(d, 0, len(bt) - 1)]], NEG)
    far = jnp.broadcast_to(rel_bias[NUM_BUCKETS - 1][None, None, :], (1, QT, N_HEADS))
    rows = jnp.concatenate([near, far, jnp.full((1, QT, N_HEADS), NEG, F32),
                            jnp.zeros((6, QT, N_HEADS), F32)], axis=0)
    ctab = rows.reshape(16, QT, N_KV, GQ).transpose(2, 0, 3, 1).reshape(N_KV, 16, GQ * QT)
    hi = ctab.astype(BF16)
    lo = (ctab - hi.astype(F32)).astype(BF16)
    return wtab.astype(F32), jnp.stack([hi, lo], axis=1)


def _attn_prompt(qt, gt, kcmp, vcmpt, ksb, vst, vsn, kwb, vwt, wtab, ctab, batch, seq):
    nq = seq // QT
    n_cmp = kcmp.shape[2]
    lanes = GQ * QT
    t = batch * seq
    return pl.pallas_call(
        functools.partial(_attn_kernel, n_cmp=n_cmp),
        grid=(batch, N_KV, nq),
        in_specs=[
            pl.BlockSpec((GQ * HEAD_DIM, QT), lambda b, g, i: (g, b * nq + i)),
            pl.BlockSpec((1, 16, QT), lambda b, g, i: (g, 0, b * nq + i)),
            pl.BlockSpec((1, 1, n_cmp, HEAD_DIM), lambda b, g, i: (b, g, 0, 0)),
            pl.BlockSpec((1, 1, HEAD_DIM, n_cmp), lambda b, g, i: (b, g, 0, 0)),
            pl.BlockSpec((1, seq, HEAD_DIM), lambda b, g, i: (g, b, 0)),
            pl.BlockSpec((1, seq // VT, V_ROWS, VT), lambda b, g, i: (g, b, 0, 0)),
            pl.BlockSpec((1, nq, V_ROWS, QT), lambda b, g, i: (g, b, 0, 0)),
            pl.BlockSpec((1, seq, HEAD_DIM), lambda b, g, i: (g, b, 0)),
            pl.BlockSpec((1, nq, V_ROWS, QT), lambda b, g, i: (g, b, 0, 0)),
            pl.BlockSpec((1, 2, 16, lanes), lambda b, g, i: (g, 0, 0, 0)),
            pl.BlockSpec((1, N_DELTA + 1, QT, lanes), lambda b, g, i: (g, 0, 0, 0)),
        ],
        out_specs=pl.BlockSpec((QT, GQ * HEAD_DIM), lambda b, g, i: (b * nq + i, g)),
        out_shape=jax.ShapeDtypeStruct((t, Q_W), BF16),
        scratch_shapes=[pltpu.VMEM((n_cmp // 2, lanes), F32), pltpu.VMEM((n_cmp // 2, lanes), F32),
                        pltpu.VMEM((2, VT, lanes), F32), pltpu.VMEM((2, VT, lanes), BF16)],
        compiler_params=_cp("arbitrary", "arbitrary", "arbitrary"),
        name="nsa_prompt",
    )(qt, gt, kcmp, vcmpt, ksb, vst, vsn, kwb, vwt, ctab, wtab)


def _s5_kernel(u_ref, mt_ref, sbr_ref, sbi_ref, ccr_ref, cci_ref, lb_ref, h0r_ref, h0i_ref,
               y_ref, hr_ref, hi_ref, sr_sc, si_sc, pr_sc, pi_sc, *, n_chunks, rb, n_seg):
    u = u_ref[0]
    sr_sc[...] = _dot(u, sbr_ref[0])
    si_sc[...] = _dot(u, sbi_ref[0])
    lr = lb_ref[0, 0:1, :]
    li = lb_ref[0, 1:2, :]

    def step(c, carry):
        hr, hi = carry
        rows = pl.ds(pl.multiple_of(c * rb, rb), rb)
        pr_sc[rows, :] = hr
        pi_sc[rows, :] = hi
        return (lr * hr - li * hi + sr_sc[rows, :], lr * hi + li * hr + si_sc[rows, :])

    h0r, h0i = h0r_ref[0], h0i_ref[0]
    first = (lax.broadcasted_iota(I32, (rb, S5_P), 0) & (n_seg - 1)) == 0
    hr, hi = lax.fori_loop(0, n_chunks, step, (h0r, h0i))
    for _ in range(n_seg - 1):
        hr, hi = lax.fori_loop(
            0, n_chunks, step,
            (jnp.where(first, h0r, pltpu.roll(hr, 1, 0)), jnp.where(first, h0i, pltpu.roll(hi, 1, 0))))
    hr_ref[0] = hr
    hi_ref[0] = hi
    y_ref[0] = (_dot(u, mt_ref[0]) + _dot(pr_sc[...].astype(BF16), ccr_ref[0])
                + _dot(pi_sc[...].astype(BF16), cci_ref[0]))


def _s5_prep(lam_re, lam_im, log_dt, b_re, b_im, c_re, c_im, d_skip, chunk):
    lam = lax.complex(lam_re, lam_im)
    z = lam * jnp.exp(log_dt)[:, None]
    lbar = jnp.exp(z)
    bbar = ((lbar - 1.0) / lam)[:, :, None] * lax.complex(b_re, b_im)
    c = lax.complex(c_re, c_im)
    pw = jnp.exp(z[None] * jnp.arange(chunk + 1, dtype=F32)[:, None, None])
    kern = jnp.einsum('gap,jgp,gpb->jgab', c, pw[:chunk], bbar, precision=HIGHEST).real
    kern = kern.at[0].add(jax.vmap(jnp.diag)(d_skip))
    lag = np.arange(chunk)[None, :] - np.arange(chunk)[:, None]
    kt = kern.transpose(1, 0, 3, 2)
    m5 = jnp.where((lag >= 0)[None, :, :, None, None], kt[:, np.clip(lag, 0, None)], 0.0)
    mt = m5.transpose(0, 1, 3, 2, 4).reshape(S5_G, chunk * S5_H, chunk * S5_H)
    sb = jnp.einsum('lgp,gph->glhp', pw[chunk - 1 - np.arange(chunk)], bbar).reshape(
        S5_G, chunk * S5_H, S5_P)
    cc = jnp.einsum('ghp,lgp->gplh', c, pw[1:]).reshape(S5_G, S5_P, chunk * S5_H)
    lb = jnp.stack([pw[chunk].real, pw[chunk].imag], axis=1)
    return (mt.astype(BF16), sb.real.astype(BF16), sb.imag.astype(BF16),
            cc.real.astype(BF16), (-cc.imag).astype(BF16), lb.astype(F32))


def _s5_call(u, prep, h0r, h0i, n_chunks, rb, n_seg):
    mt, sbr, sbi, ccr, cci, lb = prep
    g, rows, w = u.shape
    per_g = lambda *shape: pl.BlockSpec((1,) + shape, lambda i: (i,) + (0,) * len(shape))
    return pl.pallas_call(
        functools.partial(_s5_kernel, n_chunks=n_chunks, rb=rb, n_seg=n_seg),
        grid=(g,),
        in_specs=[per_g(rows, w), per_g(w, w), per_g(w, S5_P), per_g(w, S5_P),
                  per_g(S5_P, w), per_g(S5_P, w), per_g(2, S5_P), per_g(rb, S5_P), per_g(rb, S5_P)],
        out_specs=[per_g(rows, w), per_g(rb, S5_P), per_g(rb, S5_P)],
        out_shape=[jax.ShapeDtypeStruct((g, rows, w), F32),
                   jax.ShapeDtypeStruct((g, rb, S5_P), F32),
                   jax.ShapeDtypeStruct((g, rb, S5_P), F32)],
        scratch_shapes=[pltpu.VMEM((rows, S5_P), F32)] * 4,
        compiler_params=_cp("arbitrary"),
        name="s5_scan",
    )(u, mt, sbr, sbi, ccr, cci, lb, h0r, h0i)


def _s5_prompt(u, prep, batch, seq):
    assert S5_ROWS % batch == 0
    n_seg = S5_ROWS // batch
    nc = seq // S5_CHUNK // n_seg
    ub = u.reshape(batch, n_seg, nc, S5_CHUNK, S5_G, S5_H).transpose(4, 2, 0, 1, 3, 5)
    ub = ub.reshape(S5_G, nc * S5_ROWS, S5_CHUNK * S5_H).astype(BF16)
    zero = jnp.zeros((S5_G, S5_ROWS, S5_P), F32)
    y, hr, hi = _s5_call(ub, prep, zero, zero, nc, S5_ROWS, n_seg)
    y = y.reshape(S5_G, nc, batch, n_seg, S5_CHUNK, S5_H).transpose(2, 3, 1, 4, 0, 5)
    last = lambda h: h[:, n_seg - 1::n_seg].transpose(1, 0, 2)
    return y.reshape(batch * seq, S5_W), last(hr), last(hi)


def _s5_sample(u, prep, h0r, h0i):
    n = u.shape[0]
    ub = u.reshape(n, S5_G, S5_H).transpose(1, 0, 2).astype(BF16)
    y, hr, hi = _s5_call(ub, prep, h0r.transpose(1, 0, 2), h0i.transpose(1, 0, 2), 1, n, 1)
    return y.transpose(1, 0, 2).reshape(n, S5_W), hr.transpose(1, 0, 2), hi.transpose(1, 0, 2)


def _softmax_lanes(s):
    m = jnp.maximum(jnp.max(s, axis=-1, keepdims=True), M_INIT)
    e = jnp.exp2(s - m)
    return e * (1.0 / jnp.maximum(jnp.sum(e, axis=-1, keepdims=True), 1e-30))


def _bqk(q, k):
    return jnp.einsum('bhd,bnd->bhn', q, k, preferred_element_type=F32)


def _bpv(p, v):
    return jnp.einsum('bhn,bnd->bhd', p.astype(BF16), v, preferred_element_type=F32)


def _sattn1_kernel(q_ref, kc_ref, vc_ref, bias_ref, oc_ref, pick_ref, *, n_pick):
    q = q_ref[...]
    p = _softmax_lanes(_bqk(q, kc_ref[...]) + bias_ref[...])
    oc_ref[...] = _bpv(p, vc_ref[...])
    n_cmp = p.shape[-1]
    half = n_cmp // 2
    ps = p[:, 0]
    for hh in range(1, GQ):
        ps = ps + p[:, hh]
    imp = ps[:, :half] + ps[:, half:]
    lane = lax.broadcasted_iota(I32, imp.shape, 1).astype(F32)
    work = jnp.where((lane > 0) & (lane < half - 1), imp, -jnp.inf)
    picks = jnp.zeros(imp.shape, F32)
    for t in range(n_pick):
        mx = jnp.max(work, axis=-1, keepdims=True)
        idx = jnp.min(jnp.where(work == mx, lane, 1e9), axis=-1, keepdims=True)
        work = jnp.where(lane == idx, -jnp.inf, work)
        picks = jnp.where(lane == t, idx, picks)
    pick_ref[...] = picks.astype(I32)


def _bqkt(q, kt):
    return jnp.einsum('bhd,bdn->bhn', q, kt.astype(BF16), preferred_element_type=F32)


def _bpvt(p, vt):
    return jnp.einsum('bhn,bdn->bhd', p.astype(BF16), vt.astype(BF16), preferred_element_type=F32)


def _sattn2_kernel(q_ref, ks_ref, vs_ref, kt_ref, vt_ref, base_ref, corr_ref, flag_ref, half_ref,
                   ex_ref, lh_ref, kw_ref, vw_ref, wb_ref, oc_ref, g_ref, o_ref):
    q = q_ref[...]
    ex = ex_ref[...]
    slabs = lambda ref, tail: jnp.concatenate(
        [ref[:, s] for s in range(ref.shape[1])] + [tail[...]], axis=-1)
    near = _dot(flag_ref[...], ex)
    live = _dot(half_ref[...], ex) == lh_ref[...]
    bias = jnp.where(live[:, None, :], base_ref[...] + near[:, None, :] * corr_ref[...], NEG)
    p_s = _softmax_lanes(_bqkt(q, slabs(ks_ref, kt_ref)) + bias)
    o_s = _bpvt(p_s, slabs(vs_ref, vt_ref))
    p_w = _softmax_lanes(_bqkt(q, kw_ref[...]) + wb_ref[...])
    o_w = _bpvt(p_w, vw_ref[...])
    g = g_ref[...]
    o_ref[...] = g[:, :, 0:1] * oc_ref[...] + g[:, :, 1:2] * o_s + g[:, :, 2:3] * o_w


def _attn_sample(q, gates, kcmp, vcmp, page_table, cache_sk, cache_sv, ks_new, vs_new,
                 win_k, win_v, rel_bias):
    n = q.shape[0]
    n_cmp = kcmp.shape[2]
    past = page_table.shape[1] * PAGE
    n_blk = past // SEL_BLOCK
    n_pick = N_SEL - 3
    nbq = n * N_KV
    nb = 16
    bt = _bucket_table(past + 1)
    qg = q.reshape(nbq, GQ, HEAD_DIM)
    per_row = lambda tab: jnp.tile(tab.reshape(-1, N_KV, GQ).transpose(1, 2, 0), (nb // N_KV, 1, 1))
    blk3 = lambda *s: pl.BlockSpec((nb,) + s, lambda i: (i,) + (0,) * len(s))
    const3 = lambda *s: pl.BlockSpec((nb,) + s, lambda i: (0,) * (len(s) + 1))

    order = np.concatenate([np.arange(0, n_cmp, 2), np.arange(1, n_cmp, 2)])
    d_c = past - (order * CMP_BLOCK + CMP_BLOCK - 1)
    cb = per_row(jnp.where((d_c >= 0)[:, None], rel_bias[bt[np.clip(d_c, 0, None)]], NEG))

    o_c, picks = pl.pallas_call(
        functools.partial(_sattn1_kernel, n_pick=n_pick),
        grid=(nbq // nb,),
        in_specs=[blk3(GQ, HEAD_DIM), blk3(n_cmp, HEAD_DIM), blk3(n_cmp, HEAD_DIM), const3(GQ, n_cmp)],
        out_specs=[blk3(GQ, HEAD_DIM), pl.BlockSpec((nb, n_cmp // 2), lambda i: (i, 0))],
        out_shape=[jax.ShapeDtypeStruct((nbq, GQ, HEAD_DIM), F32),
                   jax.ShapeDtypeStruct((nbq, n_cmp // 2), I32)],
        compiler_params=_cp("arbitrary"),
        name="nsa_sample_cmp",
    )(qg, kcmp.reshape(nbq, n_cmp, HEAD_DIM), vcmp.reshape(nbq, n_cmp, HEAD_DIM), cb)

    picks = picks[:, :n_pick].reshape(n, N_KV, n_pick)
    forced = jnp.broadcast_to(jnp.array([n_blk - 1, 0], I32), (n, N_KV, 2))
    sel = jnp.concatenate([forced, picks], axis=-1)
    n_slot = sel.shape[-1]
    page = jnp.take_along_axis(page_table, (sel // 2).reshape(n, -1), axis=1).reshape(sel.shape)
    head = jnp.broadcast_to(jnp.arange(N_KV, dtype=I32)[None, :, None], sel.shape)
    starts = jnp.stack([page, head], axis=-1)
    dnums = lax.GatherDimensionNumbers(offset_dims=(3, 4), collapsed_slice_dims=(0, 1),
                                       start_index_map=(0, 1))
    n_key = (n_slot + 1) * PAGE

    def gather(cache, new):
        got = lax.gather(cache.transpose(0, 2, 3, 1), starts, dnums,
                         slice_sizes=(1, 1, HEAD_DIM, PAGE),
                         mode=lax.GatherScatterMode.PROMISE_IN_BOUNDS)
        tail = jnp.zeros((nbq, HEAD_DIM, PAGE), F32).at[:, :, 0].set(new.reshape(nbq, HEAD_DIM))
        return got.reshape(nbq, n_slot, HEAD_DIM, PAGE), tail

    k_sel, k_tail = gather(cache_sk, ks_new)
    v_sel, v_tail = gather(cache_sv, vs_new)
    nb2 = 8
    per_row2 = lambda tab: jnp.tile(tab.reshape(-1, N_KV, GQ).transpose(1, 2, 0), (nb2 // N_KV, 1, 1))
    blk2 = lambda *s: pl.BlockSpec((nb2,) + s, lambda i: (i,) + (0,) * len(s))
    const2 = lambda *s: pl.BlockSpec((nb2,) + s, lambda i: (0,) * (len(s) + 1))
    r = np.arange(SEL_BLOCK)
    both = lambda d: np.tile(d, PAGE // SEL_BLOCK)
    d_base = np.concatenate([both(SEL_BLOCK - r), both(past - r)] + [np.full(PAGE, past)] * n_pick
                            + [-np.arange(PAGE)])
    base = per_row2(jnp.where((d_base >= 0)[:, None], rel_bias[bt[np.clip(d_base, 0, past)]], NEG))
    d_near = both(np.clip(2 * SEL_BLOCK - r, 0, past))
    delta = rel_bias[bt[d_near]] - rel_bias[bt[past]][None, :]
    in_pick = np.zeros((n_slot + 1, 1, 1), np.float32)
    in_pick[2:n_slot] = 1.0
    corr = per_row2((in_pick * delta[None]).reshape(n_key, N_HEADS))
    pad_slot = lambda a: jnp.pad(a.astype(F32).reshape(nbq, n_slot), ((0, 0), (0, 1)))
    flag = pad_slot(sel == n_blk - 2)
    half = pad_slot(sel % 2)
    expand = jnp.asarray(np.kron(np.eye(n_slot + 1, dtype=np.float32), np.ones((1, PAGE), np.float32)))
    lane_half = jnp.asarray(((np.arange(n_key) // SEL_BLOCK) % 2).astype(np.float32).reshape(1, n_key))

    n_win = win_k.shape[1]
    d_w = n_win - 1 - np.arange(n_win)
    wb = per_row2(jnp.where((d_w < WINDOW)[:, None], rel_bias[bt[d_w]], NEG))
    wk = win_k.transpose(0, 2, 3, 1).reshape(nbq, HEAD_DIM, n_win)
    wv = win_v.transpose(0, 2, 3, 1).reshape(nbq, HEAD_DIM, n_win)
    slot_spec = pl.BlockSpec((nb2, n_slot + 1), lambda i: (i, 0))
    o = pl.pallas_call(
        _sattn2_kernel,
        grid=(nbq // nb2,),
        in_specs=[blk2(GQ, HEAD_DIM), blk2(n_slot, HEAD_DIM, PAGE), blk2(n_slot, HEAD_DIM, PAGE),
                  blk2(HEAD_DIM, PAGE), blk2(HEAD_DIM, PAGE),
                  const2(GQ, n_key), const2(GQ, n_key), slot_spec, slot_spec,
                  pl.BlockSpec((n_slot + 1, n_key), lambda i: (0, 0)),
                  pl.BlockSpec((1, n_key), lambda i: (0, 0)),
                  blk2(HEAD_DIM, n_win), blk2(HEAD_DIM, n_win), const2(GQ, n_win),
                  blk2(GQ, HEAD_DIM), blk2(GQ, 3)],
        out_specs=blk2(GQ, HEAD_DIM),
        out_shape=jax.ShapeDtypeStruct((nbq, GQ, HEAD_DIM), F32),
        compiler_params=_cp("arbitrary"),
        name="nsa_sample_sel_win",
    )(qg, k_sel, v_sel, k_tail, v_tail, base, corr, flag, half, expand, lane_half, wk, wv, wb, o_c,
      gates.reshape(nbq, GQ, 3))
    return o.reshape(n, Q_W).astype(BF16)


def _gelu_tanh(x):
    return 0.5 * x * (1.0 + jnp.tanh(math.sqrt(2.0 / math.pi) * (x + 0.044715 * (x * x * x))))


def _post_kernel(x_ref, o_ref, y5_ref, gm_ref, shf_ref, scf_ref, gpost_ref, gpre_ref,
                 wglu_ref, wout_ref, wr_ref, br_ref, tri_ref, cin_ref,
                 x1_ref, h2_ref, h2p_ref, eidx_ref, wts_ref, pos_ref, cout_ref, cnt_sc):
    i = pl.program_id(0)
    tm = x_ref.shape[0]

    @pl.when(i == 0)
    def _():
        cnt_sc[...] = cin_ref[...]

    g5 = _gelu_tanh(y5_ref[...])
    g5 = g5 * _sigmoid(_dot(g5.astype(BF16), wglu_ref[...]))
    m = _dot(o_ref[...], wout_ref[:Q_W]) + _dot(g5.astype(BF16), wout_ref[Q_W:])
    x1 = x_ref[...] + gm_ref[0] * _rms(m, gpost_ref[...])
    h2 = _rms(x1, gpre_ref[...]) * (1.0 + scf_ref[0]) + shf_ref[0]
    x1_ref[...] = x1
    h2_ref[...] = h2.astype(BF16)
    h2p_ref[...] = _pack_bf16_pairs(h2)

    h_hi = h2.astype(BF16)
    h_lo = (h2 - h_hi.astype(F32)).astype(BF16)
    logits = _dot(h_hi, wr_ref[0]) + (_dot(h_lo, wr_ref[0]) + _dot(h_hi, wr_ref[1]))
    s_t = _sigmoid(logits).T
    s_sel = s_t + br_ref[...]
    eio = lax.broadcasted_iota(I32, (N_EXPERTS, tm), 0).astype(F32)
    first = lambda hit, ids: jnp.min(jnp.where(hit, ids, 1e9), axis=0, keepdims=True)
    gscore = []
    for g in range(N_EGROUPS):
        xg = s_sel[g * EGROUP:(g + 1) * EGROUP]
        ig = lax.broadcasted_iota(I32, (EGROUP, tm), 0).astype(F32) + float(g * EGROUP)
        m1 = jnp.max(xg, axis=0, keepdims=True)
        m2 = jnp.max(jnp.where(ig == first(xg == m1, ig), -jnp.inf, xg), axis=0, keepdims=True)
        gscore.append(m1 + m2)
    gsc = jnp.concatenate(gscore, axis=0)
    gio = lax.broadcasted_iota(I32, (N_EGROUPS, tm), 0).astype(F32)
    gmask = jnp.zeros((N_EGROUPS, tm), F32)
    for _ in range(TOPK_GROUPS):
        hit = gio == first(gsc == jnp.max(gsc, axis=0, keepdims=True), gio)
        gmask = jnp.where(hit, 1.0, gmask)
        gsc = jnp.where(hit, -jnp.inf, gsc)
    work = jnp.concatenate(
        [jnp.where(gmask[g:g + 1] > 0.0, s_sel[g * EGROUP:(g + 1) * EGROUP], -jnp.inf)
         for g in range(N_EGROUPS)], axis=0)
    chosen = jnp.zeros((N_EXPERTS, tm), F32)
    ids, wsel = [], []
    for _ in range(TOP_K):
        ik = first(work == jnp.max(work, axis=0, keepdims=True), eio)
        hit = eio == ik
        ids.append(ik)
        wsel.append(jnp.sum(jnp.where(hit, s_t, 0.0), axis=0, keepdims=True))
        work = jnp.where(hit, -jnp.inf, work)
        chosen = jnp.where(hit, 1.0, chosen)
    rank = cnt_sc[:, 0:1] + _dot(chosen.astype(BF16), tri_ref[...])
    pos = [jnp.sum(jnp.where(eio == ik, rank, 0.0), axis=0, keepdims=True) for ik in ids]
    cnt_sc[...] = cnt_sc[...] + jnp.sum(chosen, axis=1, keepdims=True)
    wsum = wsel[0]
    for k in range(1, TOP_K):
        wsum = wsum + wsel[k]
    eidx_ref[...] = jnp.concatenate(ids, axis=0).astype(I32)
    pos_ref[...] = jnp.concatenate(pos, axis=0).astype(I32)
    wts_ref[...] = jnp.concatenate([w / wsum * ROUTED_SCALE for w in wsel], axis=0)
    cout_ref[...] = cnt_sc[...]


def _post_mix(x, o_att, y5, gm, shf, scf, g_post, g_pre, wglu, wout, w_router, b_router,
              cnt_in, tm, rows_per_mod):
    t = x.shape[0]
    r = gm.shape[1]
    per = rows_per_mod // tm
    mod = pl.BlockSpec((1, r, D_MODEL), lambda i: (i // per, 0, 0))
    full = lambda a: pl.BlockSpec(a.shape, lambda i: (0,) * a.ndim)
    vec = pl.BlockSpec((1, D_MODEL), lambda i: (0, 0))
    tri = jnp.asarray(np.triu(np.ones((tm, tm), np.float32), 1), BF16)
    brb = jnp.broadcast_to(b_router[:, None], (N_EXPERTS, tm)).astype(F32)
    wr_hi = w_router.astype(BF16)
    w_router = jnp.stack([wr_hi, (w_router - wr_hi.astype(F32)).astype(BF16)])
    route = pl.BlockSpec((TOP_K, tm), lambda i: (0, i))
    return pl.pallas_call(
        _post_kernel,
        grid=(t // tm,),
        in_specs=[pl.BlockSpec((tm, D_MODEL), lambda i: (i, 0)),
                  pl.BlockSpec((tm, Q_W), lambda i: (i, 0)),
                  pl.BlockSpec((tm, S5_W), lambda i: (i, 0)),
                  mod, mod, mod, vec, vec, full(wglu), full(wout), full(w_router), full(brb),
                  full(tri), full(cnt_in)],
        out_specs=[pl.BlockSpec((tm, D_MODEL), lambda i: (i, 0)),
                   pl.BlockSpec((tm, D_MODEL), lambda i: (i, 0)),
                   pl.BlockSpec((tm, D_MODEL // 2), lambda i: (i, 0)),
                   route, route, route, full(cnt_in)],
        out_shape=[jax.ShapeDtypeStruct((t, D_MODEL), F32),
                   jax.ShapeDtypeStruct((t, D_MODEL), BF16),
                   jax.ShapeDtypeStruct((t, D_MODEL // 2), U32),
                   jax.ShapeDtypeStruct((TOP_K, t), I32),
                   jax.ShapeDtypeStruct((TOP_K, t), F32),
                   jax.ShapeDtypeStruct((TOP_K, t), I32),
                   jax.ShapeDtypeStruct(cnt_in.shape, F32)],
        scratch_shapes=[pltpu.VMEM(cnt_in.shape, F32)],
        compiler_params=_cp("arbitrary"),
        name="post_mix_router",
    )(x, o_att, y5, gm, shf, scf, g_post.reshape(1, D_MODEL), g_pre.reshape(1, D_MODEL),
      wglu, wout, w_router, brb, tri, cnt_in)


def _dest_kernel(e_ref, p_ref, ps_ref, o_ref):
    tm = e_ref.shape[1]
    eio = lax.broadcasted_iota(I32, (N_EXPERTS, tm), 0)
    e = e_ref[...]
    start = ps_ref[...]
    rows = [jnp.sum(jnp.where(eio == e[k:k + 1], start, 0.0), axis=0, keepdims=True)
            for k in range(TOP_K)]
    o_ref[...] = jnp.concatenate(rows, axis=0).astype(I32) + p_ref[...]


def _dest(eidx, pos, pad_start, tm):
    t = eidx.shape[1]
    route = pl.BlockSpec((TOP_K, tm), lambda i: (0, i))
    start = jnp.broadcast_to(pad_start.astype(F32)[:, None], (N_EXPERTS, tm))
    return pl.pallas_call(
        _dest_kernel,
        grid=(t // tm,),
        in_specs=[route, route, pl.BlockSpec((N_EXPERTS, tm), lambda i: (0, 0))],
        out_specs=route,
        out_shape=jax.ShapeDtypeStruct((TOP_K, t), I32),
        compiler_params=_cp("arbitrary"),
        name="moe_dest",
    )(eidx, pos, start)


def _moe_kernel(be_ref, nb_ref, x_ref, wg_ref, wu_ref, wd_ref, y_ref, wg_sc, wu_sc, wd_sc):
    j = pl.program_id(0)

    @pl.when(j < nb_ref[0])
    def _():
        @pl.when((j == 0) | (be_ref[j] != be_ref[jnp.maximum(j - 1, 0)]))
        def _():
            wg_sc[...] = wg_ref[0].astype(BF16)
            wu_sc[...] = wu_ref[0].astype(BF16)
            wd_sc[...] = wd_ref[0].astype(BF16)

        x = jnp.concatenate(_unpack_bf16_pairs(x_ref[...]), axis=1).astype(BF16)
        a = _dot(x, wg_sc[...])
        b = _dot(x, wu_sc[...])
        y = _dot((_silu(a) * b).astype(BF16), wd_sc[...])
        y_ref[...] = _pack_bf16_pairs(y)

    @pl.when(j >= nb_ref[0])
    def _():
        y_ref[...] = jnp.zeros(y_ref.shape, U32)


def _moe(xs, blk_e, nb_used, w_g, w_u, w_d):
    n_slot = xs.shape[0]
    n_blk = n_slot // MOE_BLK
    last = lambda j, nb: jnp.maximum(jnp.minimum(j, nb[0] - 1), 0)
    row = lambda j, be, nb: (last(j, nb), 0)
    wsel = lambda j, be, nb: (be[last(j, nb)], 0, 0)
    return pl.pallas_call(
        _moe_kernel,
        grid_spec=pltpu.PrefetchScalarGridSpec(
            num_scalar_prefetch=2,
            grid=(n_blk,),
            in_specs=[pl.BlockSpec((MOE_BLK, D_MODEL // 2), row),
                      pl.BlockSpec((1, D_MODEL, D_EXPERT), wsel),
                      pl.BlockSpec((1, D_MODEL, D_EXPERT), wsel),
                      pl.BlockSpec((1, D_EXPERT, D_MODEL), wsel)],
            out_specs=pl.BlockSpec((MOE_BLK, D_MODEL // 2), lambda j, be, nb: (j, 0)),
            scratch_shapes=[pltpu.VMEM((D_MODEL, D_EXPERT), BF16), pltpu.VMEM((D_MODEL, D_EXPERT), BF16),
                            pltpu.VMEM((D_EXPERT, D_MODEL), BF16)]),
        out_shape=jax.ShapeDtypeStruct((n_slot, D_MODEL // 2), U32),
        compiler_params=_cp("arbitrary"),
        name="moe_experts",
    )(blk_e, nb_used, xs, w_g, w_u, w_d)


def _fin_kernel(x1_ref, h2_ref, yg_ref, w_ref, gf_ref, gpost_ref, wsg_ref, wsu_ref, wsd_ref, o_ref):
    w = w_ref[...]
    f_lo, f_hi = None, None
    for k in range(TOP_K):
        lo, hi = _unpack_bf16_pairs(yg_ref[k])
        f_lo = w[:, k:k + 1] * lo if k == 0 else f_lo + w[:, k:k + 1] * lo
        f_hi = w[:, k:k + 1] * hi if k == 0 else f_hi + w[:, k:k + 1] * hi
    f = jnp.concatenate([f_lo, f_hi], axis=1)
    hb = h2_ref[...]
    sh = (_silu(_dot(hb, wsg_ref[...])) * _dot(hb, wsu_ref[...])).astype(BF16)
    f = f + _dot(sh, wsd_ref[...])
    o_ref[...] = x1_ref[...] + gf_ref[0] * _rms(f, gpost_ref[...])


def _final(x1, h2, yg, wts, gf, g_post, wsg, wsu, wsd, tm, rows_per_mod):
    t = x1.shape[0]
    r = gf.shape[1]
    per = rows_per_mod // tm
    full = lambda a: pl.BlockSpec(a.shape, lambda i: (0,) * a.ndim)
    return pl.pallas_call(
        _fin_kernel,
        grid=(t // tm,),
        in_specs=[pl.BlockSpec((tm, D_MODEL), lambda i: (i, 0)),
                  pl.BlockSpec((tm, D_MODEL), lambda i: (i, 0)),
                  pl.BlockSpec((TOP_K, tm, D_MODEL // 2), lambda i: (0, i, 0)),
                  pl.BlockSpec((tm, TOP_K), lambda i: (i, 0)),
                  pl.BlockSpec((1, r, D_MODEL), lambda i: (i // per, 0, 0)),
                  pl.BlockSpec((1, D_MODEL), lambda i: (0, 0)),
                  full(wsg), full(wsu), full(wsd)],
        out_specs=pl.BlockSpec((tm, D_MODEL), lambda i: (i, 0)),
        out_shape=jax.ShapeDtypeStruct((t, D_MODEL), F32),
        compiler_params=_cp("arbitrary"),
        name="moe_combine_final",
    )(x1, h2, yg, wts, gf, g_post.reshape(1, D_MODEL), wsg, wsu, wsd)


def _even_odd(a, axis):
    n = a.shape[axis]
    order = np.concatenate([np.arange(0, n, 2), np.arange(1, n, 2)])
    return jnp.take(a, order, axis=axis)


def _layer(xp, xs, c_prompt, c_sample, page_table, ck_c, cv_c, ck_s, cv_s, cw_k, cw_v, st_re, st_im,
           w_ada, b_ada, g_pre_mix, g_post_mix, g_pre_ffn, g_post_ffn, w_in, pe, wk1, wk2, wv1, wv2,
           rel_bias, lam_re, lam_im, log_dt, b_re, b_im, c_re, c_im, d_skip, w_glu, w_out,
           w_router, b_router, w_eg, w_eu, w_ed, w_sg, w_su, w_sd):
    batch, seq, _ = xp.shape
    n_dec = xs.shape[0]
    tp = batch * seq
    x_p = xp.reshape(tp, D_MODEL)
    x_s = xs.reshape(n_dec, D_MODEL)

    n_c = batch + n_dec
    n_pad = -(-n_c // 8) * 8
    c_all = jnp.pad(jnp.concatenate([c_prompt, c_sample], axis=0), ((0, n_pad - n_c), (0, 0)))
    mod = _adaln(c_all, w_ada, b_ada)
    mod_p = [m.reshape(batch, 1, D_MODEL) for m in jnp.split(mod[:batch], 6, axis=-1)]
    mod_s = [m.reshape(1, n_dec, D_MODEL) for m in jnp.split(mod[batch:n_c], 6, axis=-1)]

    wn, wt = _pre_weights(w_in)
    (kct_p, vct_p, kst_p, vst_p, kwt_p, vwt_p, kc_p, vc_p, u_p, ksb, kwb, qt_p, vst, vsn, vwt, gt_p) = \
        _pre_mix_prompt(x_p, mod_p[1], mod_p[0], g_pre_mix, wn, wt, 512, batch, seq)
    (kc_s, vc_s, ks_s, vs_s, kw_s, vw_s, u_s, qt_s, gt_s) = _pre_mix_sample(
        x_s, mod_s[1], mod_s[0], g_pre_mix, wn, wt)

    n_cmp_p = seq // CMP_BLOCK
    kcmp_p = _compress(kc_p, *_cmp_weights(pe, wk1, wk2))
    vcmp_p = _compress(vc_p, *_cmp_weights(pe, wv1, wv2))
    kcmp_p = _even_odd(kcmp_p.reshape(batch, n_cmp_p, N_KV, HEAD_DIM), 1).transpose(0, 2, 1, 3)
    vcmp_p = _even_odd(vcmp_p.reshape(batch, n_cmp_p, N_KV, HEAD_DIM), 1).transpose(0, 2, 3, 1)
    bias_log2 = rel_bias * LOG2E
    wtab, ctab = _bias_tables(bias_log2)
    o_p = _attn_prompt(qt_p, gt_p, kcmp_p.astype(BF16), vcmp_p.astype(BF16), ksb, vst, vsn, kwb, vwt,
                       wtab, ctab, batch, seq)

    n_pages = page_table.shape[1]
    n_cmp_s = n_pages * CMP_PER_PAGE
    pool = lambda cache: cache.transpose(0, 2, 3, 1).reshape(cache.shape[0], KV_W, PAGE)
    kcmp_s, vcmp_s = _compress_pages(page_table, pool(ck_c), pool(cv_c),
                                     _cmp_pages_weights(pe, wk1, wk2), _cmp_pages_weights(pe, wv1, wv2))
    by_head = lambda a: _even_odd(
        a.reshape(n_dec, n_pages, N_KV, CMP_PER_PAGE, HEAD_DIM).transpose(0, 2, 1, 3, 4).reshape(
            n_dec, N_KV, n_cmp_s, HEAD_DIM), 2)
    win_k = jnp.concatenate([cw_k[:, 1:], kw_s.reshape(n_dec, 1, N_KV, HEAD_DIM)], axis=1)
    win_v = jnp.concatenate([cw_v[:, 1:], vw_s.reshape(n_dec, 1, N_KV, HEAD_DIM)], axis=1)
    q_s = qt_s.T.reshape(n_dec, N_HEADS, HEAD_DIM)
    gates_s = gt_s[:, :3 * GQ].reshape(N_KV, 3, GQ, n_dec).transpose(3, 0, 2, 1).reshape(n_dec, N_HEADS, 3)
    o_s = _attn_sample(q_s, gates_s, by_head(kcmp_s).astype(BF16), by_head(vcmp_s).astype(BF16),
                       page_table, ck_s, cv_s, ks_s, vs_s, win_k, win_v, bias_log2)

    y5_p, s5r_p, s5i_p = _s5_prompt(
        u_p, _s5_prep(lam_re, lam_im, log_dt, b_re, b_im, c_re, c_im, d_skip, S5_CHUNK), batch, seq)
    y5_s, s5r_s, s5i_s = _s5_sample(
        u_s, _s5_prep(lam_re, lam_im, log_dt, b_re, b_im, c_re, c_im, d_skip, 1), st_re, st_im)

    wglu_b, wout_b = w_glu.astype(BF16), w_out.astype(BF16)
    cnt0 = jnp.zeros((N_EXPERTS, 128), F32)
    tm_q = 256
    x1_p, h2_p, h2p_p, e_p, wt_p, pos_p, cnt1 = _post_mix(
        x_p, o_p, y5_p, mod_p[2], mod_p[3], mod_p[4], g_post_mix, g_pre_ffn, wglu_b, wout_b,
        w_router, b_router, cnt0, tm_q, seq)
    x1_s, h2_s, h2p_s, e_s, wt_s, pos_s, cnt2 = _post_mix(
        x_s, o_s, y5_s, mod_s[2], mod_s[3], mod_s[4], g_post_mix, g_pre_ffn, wglu_b, wout_b,
        w_router, b_router, cnt1, n_dec, n_dec)

    t_all = tp + n_dec
    counts = cnt2[:, 0].astype(I32)
    padded = (counts + MOE_BLK - 1) // MOE_BLK * MOE_BLK
    pad_end = jnp.cumsum(padded)
    dest_p = _dest(e_p, pos_p, pad_end - padded, 1024)
    dest_s = _dest(e_s, pos_s, pad_end - padded, n_dec)
    dest = jnp.concatenate([dest_p, dest_s], axis=1)
    n_blk = -(-(t_all * TOP_K) // MOE_BLK) + N_EXPERTS
    n_slot = n_blk * MOE_BLK
    tok = jnp.broadcast_to(jnp.arange(t_all, dtype=I32)[None], (TOP_K, t_all))
    rows = (jnp.arange(n_slot, dtype=I32) % t_all).at[dest.reshape(-1)].set(
        tok.reshape(-1), unique_indices=True)
    xs_rows = jnp.concatenate([h2p_p, h2p_s], axis=0)[rows]
    blk_e = jnp.minimum(jnp.searchsorted(pad_end, jnp.arange(n_blk, dtype=I32) * MOE_BLK, side='right'),
                        N_EXPERTS - 1).astype(I32)
    nb_used = (pad_end[-1:] // MOE_BLK).astype(I32)
    ys = _moe(xs_rows, blk_e, nb_used, w_eg, w_eu, w_ed)
    yg_p = ys[dest_p]
    yg_s = ys[dest_s]

    wsg, wsu, wsd = w_sg.astype(BF16), w_su.astype(BF16), w_sd.astype(BF16)
    out_p = _final(x1_p, h2_p, yg_p, wt_p.T, mod_p[5], g_post_ffn, wsg, wsu, wsd, tm_q, seq)
    out_s = _final(x1_s, h2_s, yg_s, wt_s.T, mod_s[5], g_post_ffn, wsg, wsu, wsd, n_dec, n_dec)

    n_win = min(WINDOW, seq)
    rows_p = lambda a: a.transpose(0, 3, 1, 2)
    kv5 = lambda a: a.reshape(n_dec, 1, N_KV, HEAD_DIM)
    st_p = (rows_p(kct_p), rows_p(vct_p), rows_p(kst_p), rows_p(vst_p),
            rows_p(kwt_p)[:, seq - n_win:], rows_p(vwt_p)[:, seq - n_win:], s5r_p, s5i_p)
    st_s = (kv5(kc_s), kv5(vc_s), kv5(ks_s), kv5(vs_s), win_k, win_v, s5r_s, s5i_s)
    return out_p.reshape(batch, seq, D_MODEL), out_s.reshape(n_dec, 1, D_MODEL), st_p, st_s


def kernel(x_prompt, x_sample, c_prompt, c_sample, page_table, cache_cmp_k, cache_cmp_v, cache_sel_k, cache_sel_v, cache_win_k, cache_win_v, state_s5_re, state_s5_im, w_ada, b_ada, g_pre_mix, g_post_mix, g_pre_ffn, g_post_ffn, w_in, pe_cmp, w_cmp_k1, w_cmp_k2, w_cmp_v1, w_cmp_v2, rel_bias, lam_re, lam_im, log_dt, b_re, b_im, c_re, c_im, d_skip, w_glu, w_out, w_router, b_router, w_exp_gate, w_exp_up, w_exp_down, w_sh_gate, w_sh_up, w_sh_down):
    depth = w_in.shape[0]
    xp, xs = x_prompt, x_sample
    p_states, s_states = [], []
    for l in range(depth):
        xp, xs, st_p, st_s = _layer(
            xp, xs, c_prompt, c_sample, page_table, cache_cmp_k[l], cache_cmp_v[l], cache_sel_k[l],
            cache_sel_v[l], cache_win_k[l], cache_win_v[l], state_s5_re[l], state_s5_im[l],
            w_ada[l], b_ada[l], g_pre_mix[l], g_post_mix[l], g_pre_ffn[l], g_post_ffn[l], w_in[l],
            pe_cmp[l], w_cmp_k1[l], w_cmp_k2[l], w_cmp_v1[l], w_cmp_v2[l], rel_bias,
            lam_re[l], lam_im[l], log_dt[l], b_re[l], b_im[l], c_re[l], c_im[l], d_skip[l],
            w_glu[l], w_out[l], w_router[l], b_router[l], w_exp_gate[l], w_exp_up[l], w_exp_down[l],
            w_sh_gate[l], w_sh_up[l], w_sh_down[l])
        p_states.append(st_p)
        s_states.append(st_s)
    p_st = tuple(jnp.stack(a) for a in zip(*p_states))
    s_st = tuple(jnp.stack(a) for a in zip(*s_states))
    return (xp, xs) + p_st + s_st
```

```python
import functools
import math

import numpy as np
import jax
import jax.numpy as jnp
from jax import lax
from jax.experimental import pallas as pl
from jax.experimental.pallas import tpu as pltpu

F32 = jnp.float32
BF16 = jnp.bfloat16
I32 = jnp.int32
U32 = jnp.uint32

D_MODEL = 1024
N_HEADS = 8
HEAD_DIM = 64
N_KV = 2
GQ = N_HEADS // N_KV
Q_W = N_HEADS * HEAD_DIM
KV_W = N_KV * HEAD_DIM
S5_W = D_MODEL - Q_W
S5_H = 16
S5_G = S5_W // S5_H
S5_P = 64
GATE_OFF = Q_W + 6 * KV_W
U_OFF = GATE_OFF + 3 * N_HEADS
CMP_BLOCK = 32
SEL_BLOCK = 64
N_SEL = 16
WINDOW = 512
NUM_BUCKETS = 32
REL_MAX_DIST = 128
N_EXPERTS = 256
TOP_K = 8
N_EGROUPS = 8
TOPK_GROUPS = 4
EGROUP = N_EXPERTS // N_EGROUPS
D_EXPERT = 256
ROUTED_SCALE = 2.5
EPS = 1e-6
SCALE = HEAD_DIM ** -0.5
PAGE = 128
CMP_PER_PAGE = PAGE // CMP_BLOCK

QT = 128
VT = 2 * QT
N_DELTA = WINDOW // QT + 1
V_ROWS = HEAD_DIM + 16
LOG2E = math.log2(math.e)
S5_CHUNK = 32
S5_ROWS = 8
MOE_BLK = 512
NEG = -2e30
M_INIT = -1e30
VMEM_LIMIT = 56 * 1024 * 1024
WT_ROWS = Q_W + 6 * KV_W + 32
HIGHEST = lax.Precision.HIGHEST


def _cp(*sem):
    return pltpu.CompilerParams(dimension_semantics=sem, vmem_limit_bytes=VMEM_LIMIT)


def _dot(a, b, precision=None):
    return jnp.dot(a, b, preferred_element_type=F32, precision=precision)


def _dot_nt(a, b):
    return lax.dot_general(a, b, (((1,), (1,)), ((), ())), preferred_element_type=F32)


def _sigmoid(x):
    return 1.0 / (1.0 + jnp.exp(-x))


def _silu(x):
    return x * _sigmoid(x)


def _rms(x, g):
    return x * lax.rsqrt(jnp.mean(x * x, axis=-1, keepdims=True) + EPS) * g


def _pack_bf16_pairs(x):
    c = x.shape[1] // 2
    bits = pltpu.bitcast(x.astype(BF16).astype(F32), U32)
    return (bits[:, :c] >> 16) | bits[:, c:]


def _unpack_bf16_pairs(xp):
    return pltpu.bitcast(xp << 16, F32), pltpu.bitcast(xp & jnp.uint32(0xFFFF0000), F32)


def _bucket_table(n):
    d = np.arange(n)
    exact = NUM_BUCKETS // 2
    nf = np.maximum(d, 1).astype(np.float32)
    large = exact + (np.log(nf / np.float32(exact)) / np.float32(math.log(REL_MAX_DIST / exact))
                     * np.float32(NUM_BUCKETS - exact)).astype(np.int32)
    return np.where(d < exact, d, np.minimum(large, NUM_BUCKETS - 1)).astype(np.int32)


def _ada_kernel(c_ref, w_ref, b_ref, o_ref):
    a = _silu(c_ref[...]).astype(BF16)
    o_ref[...] = _dot(a, w_ref[...].astype(BF16)) + b_ref[...]


def _adaln(c, w_ada, b_ada):
    n, d = c.shape
    n_out = w_ada.shape[1]
    tn = 1024
    return pl.pallas_call(
        _ada_kernel,
        grid=(n_out // tn,),
        in_specs=[pl.BlockSpec((n, d), lambda j: (0, 0)),
                  pl.BlockSpec((d, tn), lambda j: (0, j)),
                  pl.BlockSpec((1, tn), lambda j: (0, j))],
        out_specs=pl.BlockSpec((n, tn), lambda j: (0, j)),
        out_shape=jax.ShapeDtypeStruct((n, n_out), F32),
        compiler_params=_cp("arbitrary"),
        name="adaln",
    )(c, w_ada, b_ada.reshape(1, n_out))


def _pre_project(x_ref, sc_ref, sh_ref, g_ref, wn_ref, wt_ref):
    h = _rms(x_ref[...], g_ref[...]) * (1.0 + sc_ref[0]) + sh_ref[0]
    hb = h.astype(BF16)
    return _dot(hb, wn_ref[...]), _dot_nt(wt_ref[...], hb)


def _pre_prompt_kernel(x_ref, sc_ref, sh_ref, g_ref, wn_ref, wt_ref,
                       kct_ref, vct_ref, kst_ref, vst32_ref, kwt_ref, vwt32_ref, kc_ref, vc_ref, u_ref,
                       ksb_ref, kwb_ref, qt_ref, vst_ref, vsn_ref, vwt_ref, gt_ref, kv_sc):
    tm = x_ref.shape[0]
    zn, zt = _pre_project(x_ref, sc_ref, sh_ref, g_ref, wn_ref, wt_ref)
    trans = lambda j, g: zt[Q_W + j * KV_W + g * HEAD_DIM:Q_W + j * KV_W + (g + 1) * HEAD_DIM]
    for j, ref in enumerate((kct_ref, vct_ref, kst_ref, vst32_ref, kwt_ref, vwt32_ref)):
        for g in range(N_KV):
            ref[0, g] = trans(j, g)
    kv_sc[0] = zn[:, 0:KV_W]
    kv_sc[1] = zn[:, KV_W:2 * KV_W]
    for j in range(CMP_BLOCK):
        kc_ref[:, j * KV_W:(j + 1) * KV_W] = kv_sc[0, pl.ds(j, tm // CMP_BLOCK, stride=CMP_BLOCK), :]
        vc_ref[:, j * KV_W:(j + 1) * KV_W] = kv_sc[1, pl.ds(j, tm // CMP_BLOCK, stride=CMP_BLOCK), :]
    u_ref[...] = zn[:, 6 * KV_W:]
    for g in range(N_KV):
        ksb_ref[g] = zn[:, 2 * KV_W + g * HEAD_DIM:2 * KV_W + (g + 1) * HEAD_DIM].astype(BF16)
        kwb_ref[g] = zn[:, 4 * KV_W + g * HEAD_DIM:4 * KV_W + (g + 1) * HEAD_DIM].astype(BF16)
        ones = jnp.where(lax.broadcasted_iota(I32, (V_ROWS - HEAD_DIM, tm), 0) == 0, 1.0, 0.0)
        vs_t = jnp.concatenate([trans(3, g), ones], axis=0).astype(BF16)
        vw_t = jnp.concatenate([trans(5, g), ones], axis=0).astype(BF16)
        for c in range(tm // VT):
            vst_ref[g, c] = vs_t[:, c * VT:(c + 1) * VT]
        for c in range(tm // QT):
            vsn_ref[g, c] = vs_t[:, c * QT:(c + 1) * QT]
            vwt_ref[g, c] = vw_t[:, c * QT:(c + 1) * QT]
        r2 = Q_W + 6 * KV_W + g * 16
        gt_ref[g] = _sigmoid(zt[r2:r2 + 16])
    qt_ref[...] = zt[:Q_W].astype(BF16)


def _pre_sample_kernel(x_ref, sc_ref, sh_ref, g_ref, wn_ref, wt_ref,
                       kc_ref, vc_ref, ks_ref, vs_ref, kw_ref, vw_ref, u_ref, qt_ref, gt_ref):
    zn, zt = _pre_project(x_ref, sc_ref, sh_ref, g_ref, wn_ref, wt_ref)
    for j, ref in enumerate((kc_ref, vc_ref, ks_ref, vs_ref, kw_ref, vw_ref)):
        ref[...] = zn[:, j * KV_W:(j + 1) * KV_W]
    u_ref[...] = zn[:, 6 * KV_W:]
    qt_ref[...] = zt[:Q_W].astype(BF16)
    for g in range(N_KV):
        r2 = Q_W + 6 * KV_W + g * 16
        gt_ref[g] = _sigmoid(zt[r2:r2 + 16])


def _pre_weights(w_in):
    wn = jnp.concatenate([w_in[:, Q_W:GATE_OFF], w_in[:, U_OFF:]], axis=1).astype(BF16)
    gate_cols = []
    for g in range(N_KV):
        for j in range(3):
            for hh in range(GQ):
                gate_cols.append(GATE_OFF + (g * GQ + hh) * 3 + j)
        gate_cols.extend([GATE_OFF] * 4)
    wt = jnp.concatenate([
        w_in[:, :Q_W] * (SCALE * LOG2E),
        w_in[:, Q_W:GATE_OFF],
        w_in[:, np.array(gate_cols)],
    ], axis=1).T.astype(BF16)
    return wn, wt


def _pre_in_specs(tm, r, per, wn, wt):
    return [pl.BlockSpec((tm, D_MODEL), lambda i: (i, 0)),
            pl.BlockSpec((1, r, D_MODEL), lambda i: (i // per, 0, 0)),
            pl.BlockSpec((1, r, D_MODEL), lambda i: (i // per, 0, 0)),
            pl.BlockSpec((1, D_MODEL), lambda i: (0, 0)),
            pl.BlockSpec(wn.shape, lambda i: (0, 0)),
            pl.BlockSpec(wt.shape, lambda i: (0, 0))]


def _pre_mix_prompt(x, sc, sh, g_pre, wn, wt, tm, batch, seq):
    t = x.shape[0]
    per = seq // tm
    f = lambda shape: jax.ShapeDtypeStruct(shape, F32)
    b = lambda shape: jax.ShapeDtypeStruct(shape, BF16)
    tr_spec = pl.BlockSpec((1, N_KV, HEAD_DIM, tm), lambda i: (i // per, 0, 0, i % per))
    kv_spec = pl.BlockSpec((tm // CMP_BLOCK, CMP_BLOCK * KV_W), lambda i: (i, 0))
    return pl.pallas_call(
        _pre_prompt_kernel,
        grid=(t // tm,),
        in_specs=_pre_in_specs(tm, 1, per, wn, wt),
        out_specs=[tr_spec] * 6 + [
            kv_spec, kv_spec,
            pl.BlockSpec((tm, S5_W), lambda i: (i, 0)),
            pl.BlockSpec((N_KV, tm, HEAD_DIM), lambda i: (0, i, 0)),
            pl.BlockSpec((N_KV, tm, HEAD_DIM), lambda i: (0, i, 0)),
            pl.BlockSpec((Q_W, tm), lambda i: (0, i)),
            pl.BlockSpec((N_KV, tm // VT, V_ROWS, VT), lambda i: (0, i, 0, 0)),
            pl.BlockSpec((N_KV, tm // QT, V_ROWS, QT), lambda i: (0, i, 0, 0)),
            pl.BlockSpec((N_KV, tm // QT, V_ROWS, QT), lambda i: (0, i, 0, 0)),
            pl.BlockSpec((N_KV, 16, tm), lambda i: (0, 0, i))],
        out_shape=[f((batch, N_KV, HEAD_DIM, seq))] * 6 + [
            f((t // CMP_BLOCK, CMP_BLOCK * KV_W)), f((t // CMP_BLOCK, CMP_BLOCK * KV_W)),
            f((t, S5_W)), b((N_KV, t, HEAD_DIM)), b((N_KV, t, HEAD_DIM)), b((Q_W, t)),
            b((N_KV, t // VT, V_ROWS, VT)), b((N_KV, t // QT, V_ROWS, QT)),
            b((N_KV, t // QT, V_ROWS, QT)), f((N_KV, 16, t))],
        scratch_shapes=[pltpu.VMEM((2, tm, KV_W), F32)],
        compiler_params=_cp("arbitrary"),
        name="pre_mix_prompt",
    )(x, sc, sh, g_pre.reshape(1, D_MODEL), wn, wt)


def _pre_mix_sample(x, sc, sh, g_pre, wn, wt):
    t = x.shape[0]
    f = lambda shape: jax.ShapeDtypeStruct(shape, F32)
    kv_spec = pl.BlockSpec((t, KV_W), lambda i: (i, 0))
    return pl.pallas_call(
        _pre_sample_kernel,
        grid=(1,),
        in_specs=_pre_in_specs(t, t, 1, wn, wt),
        out_specs=[kv_spec] * 6 + [pl.BlockSpec((t, S5_W), lambda i: (i, 0)),
                                   pl.BlockSpec((Q_W, t), lambda i: (0, i)),
                                   pl.BlockSpec((N_KV, 16, t), lambda i: (0, 0, i))],
        out_shape=[f((t, KV_W))] * 6 + [f((t, S5_W)), jax.ShapeDtypeStruct((Q_W, t), BF16),
                                        f((N_KV, 16, t))],
        compiler_params=_cp("arbitrary"),
        name="pre_mix_sample",
    )(x, sc, sh, g_pre.reshape(1, D_MODEL), wn, wt)


def _cmp_kernel(x_ref, pe_ref, w1_ref, w2_ref, o_ref):
    xb = (x_ref[...] + pe_ref[...]).astype(BF16)
    hid = _silu(_dot(xb, w1_ref[...]))
    o_ref[...] = _dot(hid.astype(BF16), w2_ref[...])


def _cmp_weights(pe, w1, w2):
    eye = jnp.eye(N_KV, dtype=F32)
    w1b = jnp.einsum('jde,gh->jgdhe', w1, eye).reshape(CMP_BLOCK * KV_W, KV_W).astype(BF16)
    w2b = jnp.einsum('ef,gh->gehf', w2, eye).reshape(KV_W, KV_W).astype(BF16)
    peb = jnp.broadcast_to(pe[:, None, :], (CMP_BLOCK, N_KV, HEAD_DIM)).reshape(1, CMP_BLOCK * KV_W)
    return peb, w1b, w2b


def _compress(x, peb, w1b, w2b):
    r, k = x.shape
    tr = min(256, r)
    return pl.pallas_call(
        _cmp_kernel,
        grid=(r // tr,),
        in_specs=[pl.BlockSpec((tr, k), lambda i: (i, 0)),
                  pl.BlockSpec((1, k), lambda i: (0, 0)),
                  pl.BlockSpec((k, KV_W), lambda i: (0, 0)),
                  pl.BlockSpec((KV_W, KV_W), lambda i: (0, 0))],
        out_specs=pl.BlockSpec((tr, KV_W), lambda i: (i, 0)),
        out_shape=jax.ShapeDtypeStruct((r, KV_W), F32),
        compiler_params=_cp("arbitrary"),
        name="compress",
    )(x, peb, w1b, w2b)


def _cmp_pages_kernel(pt_ref, ck_hbm, cv_hbm, w1k_ref, w1v_ref, bk_ref, bv_ref, w2k_ref, w2v_ref,
                      ok_ref, ov_ref, kbuf, vbuf, sem, *, n_pages):
    i = pl.program_id(0)
    n_rows = ok_ref.shape[0]

    def page_copies(seq, slot):
        copies = []
        for p in range(n_pages):
            page = pt_ref[seq * n_pages + p]
            rows = pl.ds(p * KV_W, KV_W)
            copies.append(pltpu.make_async_copy(ck_hbm.at[page], kbuf.at[slot, rows], sem.at[0, slot]))
            copies.append(pltpu.make_async_copy(cv_hbm.at[page], vbuf.at[slot, rows], sem.at[1, slot]))
        return copies

    @pl.when(i == 0)
    def _():
        for c in page_copies(0, 0):
            c.start()

    @pl.when(i + 1 < pl.num_programs(0))
    def _():
        for c in page_copies(i + 1, (i + 1) % 2):
            c.start()

    slot = i % 2
    for c in page_copies(i, slot):
        c.wait()
    for x_ref, w1_ref, b_ref, w2_ref, o_ref in ((kbuf.at[slot], w1k_ref, bk_ref, w2k_ref, ok_ref),
                                               (vbuf.at[slot], w1v_ref, bv_ref, w2v_ref, ov_ref)):
        acc = jnp.zeros((n_rows, CMP_PER_PAGE * HEAD_DIM), F32)
        for dd in range(HEAD_DIM // 2):
            r0 = x_ref[pl.ds(2 * dd, n_rows, stride=HEAD_DIM), :]
            r1 = x_ref[pl.ds(2 * dd + 1, n_rows, stride=HEAD_DIM), :]
            acc = acc + _dot(jnp.concatenate([r0, r1], axis=1).astype(BF16), w1_ref[dd])
        hid = _silu(acc + b_ref[...])
        o_ref[...] = _dot(hid.astype(BF16), w2_ref[...])


def _cmp_pages_weights(pe, w1, w2):
    eye = jnp.eye(CMP_PER_PAGE, dtype=F32)
    w1t = jnp.einsum('jde,mn->dmjne', w1, eye).reshape(HEAD_DIM // 2, 2 * PAGE, CMP_PER_PAGE * HEAD_DIM)
    bias = jnp.einsum('jd,jde->e', pe, w1, precision=HIGHEST)
    bias = jnp.tile(bias, CMP_PER_PAGE).reshape(1, CMP_PER_PAGE * HEAD_DIM)
    w2t = jnp.einsum('ef,mn->menf', w2, eye).reshape(CMP_PER_PAGE * HEAD_DIM, CMP_PER_PAGE * HEAD_DIM)
    return w1t.astype(BF16), bias, w2t.astype(BF16)


def _compress_pages(page_table, ck, cv, wk, wv):
    n_seq, n_pages = page_table.shape
    rows_in = n_pages * KV_W
    rows_out = n_pages * N_KV
    wcols = CMP_PER_PAGE * HEAD_DIM
    full = lambda a: pl.BlockSpec(a.shape, lambda i, pt: (0,) * a.ndim)
    hbm = pl.BlockSpec(memory_space=pl.ANY)
    o_spec = pl.BlockSpec((rows_out, wcols), lambda i, pt: (i, 0))
    o_shape = jax.ShapeDtypeStruct((n_seq * rows_out, wcols), F32)
    return pl.pallas_call(
        functools.partial(_cmp_pages_kernel, n_pages=n_pages),
        grid_spec=pltpu.PrefetchScalarGridSpec(
            num_scalar_prefetch=1,
            grid=(n_seq,),
            in_specs=[hbm, hbm, full(wk[0]), full(wv[0]), full(wk[1]), full(wv[1]),
                      full(wk[2]), full(wv[2])],
            out_specs=[o_spec, o_spec],
            scratch_shapes=[pltpu.VMEM((2, rows_in, PAGE), F32), pltpu.VMEM((2, rows_in, PAGE), F32),
                            pltpu.SemaphoreType.DMA((2, 2))]),
        out_shape=[o_shape, o_shape],
        compiler_params=_cp("arbitrary"),
        name="compress_pages",
    )(page_table.reshape(-1), ck, cv, wk[0], wv[0], wk[1], wv[1], wk[2], wv[2])


def _attn_kernel(q_ref, g_ref, kc_ref, vct_ref, ks_ref, vst_ref, vsn_ref, kw_ref, vwt_ref,
                 ctab_ref, wtab_ref, o_ref, mb_ref, mbf_ref, s_sc, p_sc, *, n_cmp):
    i = pl.program_id(2)
    lanes = GQ * QT
    qt = q_ref[...]
    q4 = jnp.concatenate([qt[hh * HEAD_DIM:(hh + 1) * HEAD_DIM] for hh in range(GQ)], axis=1)
    gates = g_ref[0]
    grow = lambda j: jnp.concatenate([gates[j * GQ + hh:j * GQ + hh + 1] for hh in range(GQ)], axis=1)

    def online(carry, s, v_t):
        m, acc = carry
        m_new = jnp.maximum(m, jnp.max(s, axis=0, keepdims=True))
        p = jnp.exp2(s - m_new)
        return m_new, jnp.exp2(m - m_new) * acc + _dot(v_t, p.astype(BF16))

    init = (jnp.full((1, lanes), M_INIT, F32), jnp.zeros((V_ROWS, lanes), F32))
    finish = lambda carry: carry[1][:HEAD_DIM] * (1.0 / jnp.maximum(carry[1][HEAD_DIM:HEAD_DIM + 1], 1e-30))
    block_rows = lambda ref, first, n, extra=0.0: jnp.concatenate(
        [jnp.broadcast_to(ref[pl.ds(first + c, 1), :] + extra, (SEL_BLOCK, lanes)) for c in range(n)],
        axis=0)

    def near_tiles(carry, k_ref, vt_ref, deltas, live, masked):
        ks, vs, bs = [], [], []
        for dl in deltas:
            kt = i - dl
            ktc = jnp.maximum(kt, 0)
            ks.append(k_ref[0, pl.ds(pl.multiple_of(ktc * QT, QT), QT), :])
            vs.append(vt_ref[0, ktc])
            bias = wtab_ref[0, jnp.where(live(dl, kt), dl, N_DELTA)]
            if masked:
                bias = bias + block_rows(mb_ref, (QT // SEL_BLOCK) * ktc, QT // SEL_BLOCK)
            bs.append(bias)
        s = _dot(jnp.concatenate(ks, axis=0), q4) + jnp.concatenate(bs, axis=0)
        return online(carry, s, jnp.concatenate(vs, axis=1))

    o_w = finish(near_tiles(init, kw_ref, vwt_ref, list(range(N_DELTA - 1, -1, -1)),
                            lambda dl, kt: kt >= 0, False))

    half = n_cmp // 2
    sc = _dot(kc_ref[0, 0], q4)
    rho = lax.broadcasted_iota(I32, (n_cmp, 16), 0)
    col = lax.broadcasted_iota(I32, (n_cmp, 16), 1)
    blk_n = 2 * jnp.where(rho >= half, rho - half, rho) + jnp.where(rho >= half, 1, 0)
    rel = blk_n - (4 * i - 4)
    want = jnp.where(rel < 0, 8, jnp.where(rel > 7, 9, rel))
    place = jnp.where(col == want, 1.0, 0.0).astype(BF16)
    sc = sc + _dot(place, ctab_ref[0, 0]) + _dot(place, ctab_ref[0, 1])
    mc = jnp.maximum(jnp.max(sc, axis=0, keepdims=True), M_INIT)
    ec = jnp.exp2(sc - mc)
    pc = ec * (1.0 / jnp.maximum(jnp.sum(ec, axis=0, keepdims=True), 1e-30))
    o_cw = grow(0) * _dot(vct_ref[0, 0], pc.astype(BF16)) + grow(2) * o_w

    ps = pc[:half] + pc[half:]
    imp = ps[:, 0:QT]
    for hh in range(1, GQ):
        imp = imp + ps[:, hh * QT:(hh + 1) * QT]
    n_blk = half
    blk = lax.broadcasted_iota(I32, (n_blk, QT), 0)
    tq = i * QT + lax.broadcasted_iota(I32, (n_blk, QT), 1)
    cur = lax.shift_right_logical(tq, 6)
    valid = blk * SEL_BLOCK <= tq
    forced = (blk == 0) | (blk == cur) | (blk == cur - 1)
    blkf = blk.astype(F32)

    def pick(_, carry):
        work, mb = carry
        mx = jnp.max(work, axis=0, keepdims=True)
        idx = jnp.min(jnp.where(work == mx, blkf, 1e9), axis=0, keepdims=True)
        hit = blkf == idx
        return jnp.where(hit, -jnp.inf, work), jnp.where(hit, 0.0, mb)

    _, mb = lax.fori_loop(0, min(N_SEL, n_blk) - 3, pick,
                          (jnp.where(valid, jnp.where(forced, -jnp.inf, imp), -jnp.inf),
                           jnp.where(valid, jnp.where(forced, 0.0, NEG), NEG)))
    for hh in range(GQ):
        mb_ref[:, hh * QT:(hh + 1) * QT] = mb
    mbf_ref[...] = mb_ref[...] + wtab_ref[0, 2, 0:1, :]

    n_far = jnp.maximum(i - 1, 0) // 2
    last_pair = ks_ref.shape[1] // VT - 1

    def pair_scores(t):
        ta = jnp.minimum(t, last_pair)
        k_t = ks_ref[0, pl.ds(pl.multiple_of(ta * VT, VT), VT), :]
        return _dot(k_t, q4) + block_rows(mbf_ref, (VT // SEL_BLOCK) * ta, VT // SEL_BLOCK,
                                          jnp.where(t < n_far, 0.0, NEG))

    def far_step(t, par, carry):
        m, acc, alpha_prev = carry
        pv = _dot(vst_ref[0, jnp.clip(t - 2, 0, last_pair)], p_sc[par])
        s = s_sc[1 - par]
        m_new = jnp.maximum(m, jnp.max(s, axis=0, keepdims=True))
        s_sc[par] = pair_scores(t)
        p_sc[1 - par] = jnp.exp2(s - m_new).astype(BF16)
        return m_new, alpha_prev * acc + pv, jnp.exp2(m - m_new)

    s_sc[0] = pair_scores(0)
    p_sc[1] = jnp.zeros((VT, lanes), BF16)
    m, acc, _ = lax.fori_loop(
        0, (n_far + 2) // 2,
        lambda u, carry: far_step(2 * u + 2, 0, far_step(2 * u + 1, 1, carry)),
        init + (jnp.ones((1, lanes), F32),))

    sel_live = lambda dl, kt: (kt >= 0) & ((dl < 2) | (i % 2 == 0))
    o_s = finish(near_tiles((m, acc), ks_ref, vsn_ref, [2, 1, 0], sel_live, True))
    o_t = o_cw + grow(1) * o_s
    o_hd = jnp.concatenate([o_t[:, hh * QT:(hh + 1) * QT] for hh in range(GQ)], axis=0)
    o_ref[...] = o_hd.T.astype(BF16)


def _bias_tables(rel_bias):
    span = N_DELTA * QT
    bt = _bucket_table(WINDOW)
    period = span + QT
    by_dist = jnp.concatenate([rel_bias[bt], jnp.full((period - WINDOW, N_HEADS), NEG, F32)], axis=0)
    toep = jnp.tile(by_dist.T, (1, QT))[:, :QT * (period - 1)].reshape(N_HEADS, QT, period - 1)
    tab = toep[:, :, :span].reshape(N_KV, GQ, QT, N_DELTA, QT)
    wtab = tab.transpose(0, 3, 2, 1, 4).reshape(N_KV, N_DELTA, QT, GQ * QT)
    wtab = jnp.concatenate([wtab, jnp.full((N_KV, 1, QT, GQ * QT), NEG, F32)], axis=1)
    ql = np.arange(QT)[None, :]
    r = np.arange(8)[:, None]
    d = ql + (4 * CMP_BLOCK - CMP_BLOCK + 1) - CMP_BLOCK * r
    near = jnp.where((d >= 0)[..., None], rel_bias[bt[np.clip(d, 0, len(bt) - 1)]], NEG)
    far = jnp.broadcast_to(rel_bias[NUM_BUCKETS - 1][None, None, :], (1, QT, N_HEADS))
    rows = jnp.concatenate([near, far, jnp.full((1, QT, N_HEADS), NEG, F32),
                            jnp.zeros((6, QT, N_HEADS), F32)], axis=0)
    ctab = rows.reshape(16, QT, N_KV, GQ).transpose(2, 0, 3, 1).reshape(N_KV, 16, GQ * QT)
    hi = ctab.astype(BF16)
    lo = (ctab - hi.astype(F32)).astype(BF16)
    return wtab.astype(F32), jnp.stack([hi, lo], axis=1)


def _attn_prompt(qt, gt, kcmp, vcmpt, ksb, vst, vsn, kwb, vwt, wtab, ctab, batch, seq):
    nq = seq // QT
    n_cmp = kcmp.shape[2]
    lanes = GQ * QT
    t = batch * seq
    return pl.pallas_call(
        functools.partial(_attn_kernel, n_cmp=n_cmp),
        grid=(batch, N_KV, nq),
        in_specs=[
            pl.BlockSpec((GQ * HEAD_DIM, QT), lambda b, g, i: (g, b * nq + i)),
            pl.BlockSpec((1, 16, QT), lambda b, g, i: (g, 0, b * nq + i)),
            pl.BlockSpec((1, 1, n_cmp, HEAD_DIM), lambda b, g, i: (b, g, 0, 0)),
            pl.BlockSpec((1, 1, HEAD_DIM, n_cmp), lambda b, g, i: (b, g, 0, 0)),
            pl.BlockSpec((1, seq, HEAD_DIM), lambda b, g, i: (g, b, 0)),
            pl.BlockSpec((1, seq // VT, V_ROWS, VT), lambda b, g, i: (g, b, 0, 0)),
            pl.BlockSpec((1, nq, V_ROWS, QT), lambda b, g, i: (g, b, 0, 0)),
            pl.BlockSpec((1, seq, HEAD_DIM), lambda b, g, i: (g, b, 0)),
            pl.BlockSpec((1, nq, V_ROWS, QT), lambda b, g, i: (g, b, 0, 0)),
            pl.BlockSpec((1, 2, 16, lanes), lambda b, g, i: (g, 0, 0, 0)),
            pl.BlockSpec((1, N_DELTA + 1, QT, lanes), lambda b, g, i: (g, 0, 0, 0)),
        ],
        out_specs=pl.BlockSpec((QT, GQ * HEAD_DIM), lambda b, g, i: (b * nq + i, g)),
        out_shape=jax.ShapeDtypeStruct((t, Q_W), BF16),
        scratch_shapes=[pltpu.VMEM((n_cmp // 2, lanes), F32), pltpu.VMEM((n_cmp // 2, lanes), F32),
                        pltpu.VMEM((2, VT, lanes), F32), pltpu.VMEM((2, VT, lanes), BF16)],
        compiler_params=_cp("arbitrary", "arbitrary", "arbitrary"),
        name="nsa_prompt",
    )(qt, gt, kcmp, vcmpt, ksb, vst, vsn, kwb, vwt, ctab, wtab)


def _s5_kernel(u_ref, mt_ref, sbr_ref, sbi_ref, ccr_ref, cci_ref, lb_ref, h0r_ref, h0i_ref,
               y_ref, hr_ref, hi_ref, sr_sc, si_sc, pr_sc, pi_sc, *, n_chunks, rb, n_seg):
    u = u_ref[0]
    sr_sc[...] = _dot(u, sbr_ref[0])
    si_sc[...] = _dot(u, sbi_ref[0])
    lr = lb_ref[0, 0:1, :]
    li = lb_ref[0, 1:2, :]

    def step(c, carry):
        hr, hi = carry
        rows = pl.ds(pl.multiple_of(c * rb, rb), rb)
        pr_sc[rows, :] = hr
        pi_sc[rows, :] = hi
        return (lr * hr - li * hi + sr_sc[rows, :], lr * hi + li * hr + si_sc[rows, :])

    h0r, h0i = h0r_ref[0], h0i_ref[0]
    first = (lax.broadcasted_iota(I32, (rb, S5_P), 0) & (n_seg - 1)) == 0
    hr, hi = lax.fori_loop(0, n_chunks, step, (h0r, h0i))
    for _ in range(n_seg - 1):
        hr, hi = lax.fori_loop(
            0, n_chunks, step,
            (jnp.where(first, h0r, pltpu.roll(hr, 1, 0)), jnp.where(first, h0i, pltpu.roll(hi, 1, 0))))
    hr_ref[0] = hr
    hi_ref[0] = hi
    y_ref[0] = (_dot(u, mt_ref[0]) + _dot(pr_sc[...].astype(BF16), ccr_ref[0])
                + _dot(pi_sc[...].astype(BF16), cci_ref[0]))


def _s5_prep(lam_re, lam_im, log_dt, b_re, b_im, c_re, c_im, d_skip, chunk):
    lam = lax.complex(lam_re, lam_im)
    z = lam * jnp.exp(log_dt)[:, None]
    lbar = jnp.exp(z)
    bbar = ((lbar - 1.0) / lam)[:, :, None] * lax.complex(b_re, b_im)
    c = lax.complex(c_re, c_im)
    pw = jnp.exp(z[None] * jnp.arange(chunk + 1, dtype=F32)[:, None, None])
    kern = jnp.einsum('gap,jgp,gpb->jgab', c, pw[:chunk], bbar, precision=HIGHEST).real
    kern = kern.at[0].add(jax.vmap(jnp.diag)(d_skip))
    lag = np.arange(chunk)[None, :] - np.arange(chunk)[:, None]
    kt = kern.transpose(1, 0, 3, 2)
    m5 = jnp.where((lag >= 0)[None, :, :, None, None], kt[:, np.clip(lag, 0, None)], 0.0)
    mt = m5.transpose(0, 1, 3, 2, 4).reshape(S5_G, chunk * S5_H, chunk * S5_H)
    sb = jnp.einsum('lgp,gph->glhp', pw[chunk - 1 - np.arange(chunk)], bbar).reshape(
        S5_G, chunk * S5_H, S5_P)
    cc = jnp.einsum('ghp,lgp->gplh', c, pw[1:]).reshape(S5_G, S5_P, chunk * S5_H)
    lb = jnp.stack([pw[chunk].real, pw[chunk].imag], axis=1)
    return (mt.astype(BF16), sb.real.astype(BF16), sb.imag.astype(BF16),
            cc.real.astype(BF16), (-cc.imag).astype(BF16), lb.astype(F32))


def _s5_call(u, prep, h0r, h0i, n_chunks, rb, n_seg):
    mt, sbr, sbi, ccr, cci, lb = prep
    g, rows, w = u.shape
    per_g = lambda *shape: pl.BlockSpec((1,) + shape, lambda i: (i,) + (0,) * len(shape))
    return pl.pallas_call(
        functools.partial(_s5_kernel, n_chunks=n_chunks, rb=rb, n_seg=n_seg),
        grid=(g,),
        in_specs=[per_g(rows, w), per_g(w, w), per_g(w, S5_P), per_g(w, S5_P),
                  per_g(S5_P, w), per_g(S5_P, w), per_g(2, S5_P), per_g(rb, S5_P), per_g(rb, S5_P)],
        out_specs=[per_g(rows, w), per_g(rb, S5_P), per_g(rb, S5_P)],
        out_shape=[jax.ShapeDtypeStruct((g, rows, w), F32),
                   jax.ShapeDtypeStruct((g, rb, S5_P), F32),
                   jax.ShapeDtypeStruct((g, rb, S5_P), F32)],
        scratch_shapes=[pltpu.VMEM((rows, S5_P), F32)] * 4,
        compiler_params=_cp("arbitrary"),
        name="s5_scan",
    )(u, mt, sbr, sbi, ccr, cci, lb, h0r, h0i)


def _s5_prompt(u, prep, batch, seq):
    assert S5_ROWS % batch == 0
    n_seg = S5_ROWS // batch
    nc = seq // S5_CHUNK // n_seg
    ub = u.reshape(batch, n_seg, nc, S5_CHUNK, S5_G, S5_H).transpose(4, 2, 0, 1, 3, 5)
    ub = ub.reshape(S5_G, nc * S5_ROWS, S5_CHUNK * S5_H).astype(BF16)
    zero = jnp.zeros((S5_G, S5_ROWS, S5_P), F32)
    y, hr, hi = _s5_call(ub, prep, zero, zero, nc, S5_ROWS, n_seg)
    y = y.reshape(S5_G, nc, batch, n_seg, S5_CHUNK, S5_H).transpose(2, 3, 1, 4, 0, 5)
    last = lambda h: h[:, n_seg - 1::n_seg].transpose(1, 0, 2)
    return y.reshape(batch * seq, S5_W), last(hr), last(hi)


def _s5_sample(u, prep, h0r, h0i):
    n = u.shape[0]
    ub = u.reshape(n, S5_G, S5_H).transpose(1, 0, 2).astype(BF16)
    y, hr, hi = _s5_call(ub, prep, h0r.transpose(1, 0, 2), h0i.transpose(1, 0, 2), 1, n, 1)
    return y.transpose(1, 0, 2).reshape(n, S5_W), hr.transpose(1, 0, 2), hi.transpose(1, 0, 2)


def _softmax_lanes(s):
    m = jnp.maximum(jnp.max(s, axis=-1, keepdims=True), M_INIT)
    e = jnp.exp2(s - m)
    return e * (1.0 / jnp.maximum(jnp.sum(e, axis=-1, keepdims=True), 1e-30))


def _bqk(q, k):
    return jnp.einsum('bhd,bnd->bhn', q, k, preferred_element_type=F32)


def _bpv(p, v):
    return jnp.einsum('bhn,bnd->bhd', p.astype(BF16), v, preferred_element_type=F32)


def _sattn1_kernel(q_ref, kc_ref, vc_ref, bias_ref, oc_ref, pick_ref, *, n_pick):
    q = q_ref[...]
    p = _softmax_lanes(_bqk(q, kc_ref[...]) + bias_ref[...])
    oc_ref[...] = _bpv(p, vc_ref[...])
    n_cmp = p.shape[-1]
    half = n_cmp // 2
    ps = p[:, 0]
    for hh in range(1, GQ):
        ps = ps + p[:, hh]
    imp = ps[:, :half] + ps[:, half:]
    lane = lax.broadcasted_iota(I32, imp.shape, 1).astype(F32)
    work = jnp.where((lane > 0) & (lane < half - 1), imp, -jnp.inf)
    picks = jnp.zeros(imp.shape, F32)
    for t in range(n_pick):
        mx = jnp.max(work, axis=-1, keepdims=True)
        idx = jnp.min(jnp.where(work == mx, lane, 1e9), axis=-1, keepdims=True)
        work = jnp.where(lane == idx, -jnp.inf, work)
        picks = jnp.where(lane == t, idx, picks)
    pick_ref[...] = picks.astype(I32)


def _bqkt(q, kt):
    return jnp.einsum('bhd,bdn->bhn', q, kt.astype(BF16), preferred_element_type=F32)


def _bpvt(p, vt):
    return jnp.einsum('bhn,bdn->bhd', p.astype(BF16), vt.astype(BF16), preferred_element_type=F32)


def _sattn2_kernel(q_ref, ks_ref, vs_ref, kt_ref, vt_ref, base_ref, corr_ref, flag_ref, half_ref,
                   ex_ref, lh_ref, kw_ref, vw_ref, wb_ref, oc_ref, g_ref, o_ref):
    q = q_ref[...]
    ex = ex_ref[...]
    slabs = lambda ref, tail: jnp.concatenate(
        [ref[:, s] for s in range(ref.shape[1])] + [tail[...]], axis=-1)
    near = _dot(flag_ref[...], ex)
    live = _dot(half_ref[...], ex) == lh_ref[...]
    bias = jnp.where(live[:, None, :], base_ref[...] + near[:, None, :] * corr_ref[...], NEG)
    p_s = _softmax_lanes(_bqkt(q, slabs(ks_ref, kt_ref)) + bias)
    o_s = _bpvt(p_s, slabs(vs_ref, vt_ref))
    p_w = _softmax_lanes(_bqkt(q, kw_ref[...]) + wb_ref[...])
    o_w = _bpvt(p_w, vw_ref[...])
    g = g_ref[...]
    o_ref[...] = g[:, :, 0:1] * oc_ref[...] + g[:, :, 1:2] * o_s + g[:, :, 2:3] * o_w


def _attn_sample(q, gates, kcmp, vcmp, page_table, cache_sk, cache_sv, ks_new, vs_new,
                 win_k, win_v, rel_bias):
    n = q.shape[0]
    n_cmp = kcmp.shape[2]
    past = page_table.shape[1] * PAGE
    n_blk = past // SEL_BLOCK
    n_pick = N_SEL - 3
    nbq = n * N_KV
    nb = 16
    bt = _bucket_table(past + 1)
    qg = q.reshape(nbq, GQ, HEAD_DIM)
    per_row = lambda tab: jnp.tile(tab.reshape(-1, N_KV, GQ).transpose(1, 2, 0), (nb // N_KV, 1, 1))
    blk3 = lambda *s: pl.BlockSpec((nb,) + s, lambda i: (i,) + (0,) * len(s))
    const3 = lambda *s: pl.BlockSpec((nb,) + s, lambda i: (0,) * (len(s) + 1))

    order = np.concatenate([np.arange(0, n_cmp, 2), np.arange(1, n_cmp, 2)])
    d_c = past - (order * CMP_BLOCK + CMP_BLOCK - 1)
    cb = per_row(jnp.where((d_c >= 0)[:, None], rel_bias[bt[np.clip(d_c, 0, None)]], NEG))

    o_c, picks = pl.pallas_call(
        functools.partial(_sattn1_kernel, n_pick=n_pick),
        grid=(nbq // nb,),
        in_specs=[blk3(GQ, HEAD_DIM), blk3(n_cmp, HEAD_DIM), blk3(n_cmp, HEAD_DIM), const3(GQ, n_cmp)],
        out_specs=[blk3(GQ, HEAD_DIM), pl.BlockSpec((nb, n_cmp // 2), lambda i: (i, 0))],
        out_shape=[jax.ShapeDtypeStruct((nbq, GQ, HEAD_DIM), F32),
                   jax.ShapeDtypeStruct((nbq, n_cmp // 2), I32)],
        compiler_params=_cp("arbitrary"),
        name="nsa_sample_cmp",
    )(qg, kcmp.reshape(nbq, n_cmp, HEAD_DIM), vcmp.reshape(nbq, n_cmp, HEAD_DIM), cb)

    picks = picks[:, :n_pick].reshape(n, N_KV, n_pick)
    forced = jnp.broadcast_to(jnp.array([n_blk - 1, 0], I32), (n, N_KV, 2))
    sel = jnp.concatenate([forced, picks], axis=-1)
    n_slot = sel.shape[-1]
    page = jnp.take_along_axis(page_table, (sel // 2).reshape(n, -1), axis=1).reshape(sel.shape)
    head = jnp.broadcast_to(jnp.arange(N_KV, dtype=I32)[None, :, None], sel.shape)
    starts = jnp.stack([page, head], axis=-1)
    dnums = lax.GatherDimensionNumbers(offset_dims=(3, 4), collapsed_slice_dims=(0, 1),
                                       start_index_map=(0, 1))
    n_key = (n_slot + 1) * PAGE

    def gather(cache, new):
        got = lax.gather(cache.transpose(0, 2, 3, 1), starts, dnums,
                         slice_sizes=(1, 1, HEAD_DIM, PAGE),
                         mode=lax.GatherScatterMode.PROMISE_IN_BOUNDS)
        tail = jnp.zeros((nbq, HEAD_DIM, PAGE), F32).at[:, :, 0].set(new.reshape(nbq, HEAD_DIM))
        return got.reshape(nbq, n_slot, HEAD_DIM, PAGE), tail

    k_sel, k_tail = gather(cache_sk, ks_new)
    v_sel, v_tail = gather(cache_sv, vs_new)
    nb2 = 8
    per_row2 = lambda tab: jnp.tile(tab.reshape(-1, N_KV, GQ).transpose(1, 2, 0), (nb2 // N_KV, 1, 1))
    blk2 = lambda *s: pl.BlockSpec((nb2,) + s, lambda i: (i,) + (0,) * len(s))
    const2 = lambda *s: pl.BlockSpec((nb2,) + s, lambda i: (0,) * (len(s) + 1))
    r = np.arange(SEL_BLOCK)
    both = lambda d: np.tile(d, PAGE // SEL_BLOCK)
    d_base = np.concatenate([both(SEL_BLOCK - r), both(past - r)] + [np.full(PAGE, past)] * n_pick
                            + [-np.arange(PAGE)])
    base = per_row2(jnp.where((d_base >= 0)[:, None], rel_bias[bt[np.clip(d_base, 0, past)]], NEG))
    d_near = both(np.clip(2 * SEL_BLOCK - r, 0, past))
    delta = rel_bias[bt[d_near]] - rel_bias[bt[past]][None, :]
    in_pick = np.zeros((n_slot + 1, 1, 1), np.float32)
    in_pick[2:n_slot] = 1.0
    corr = per_row2((in_pick * delta[None]).reshape(n_key, N_HEADS))
    pad_slot = lambda a: jnp.pad(a.astype(F32).reshape(nbq, n_slot), ((0, 0), (0, 1)))
    flag = pad_slot(sel == n_blk - 2)
    half = pad_slot(sel % 2)
    expand = jnp.asarray(np.kron(np.eye(n_slot + 1, dtype=np.float32), np.ones((1, PAGE), np.float32)))
    lane_half = jnp.asarray(((np.arange(n_key) // SEL_BLOCK) % 2).astype(np.float32).reshape(1, n_key))

    n_win = win_k.shape[1]
    d_w = n_win - 1 - np.arange(n_win)
    wb = per_row2(jnp.where((d_w < WINDOW)[:, None], rel_bias[bt[d_w]], NEG))
    wk = win_k.transpose(0, 2, 3, 1).reshape(nbq, HEAD_DIM, n_win)
    wv = win_v.transpose(0, 2, 3, 1).reshape(nbq, HEAD_DIM, n_win)
    slot_spec = pl.BlockSpec((nb2, n_slot + 1), lambda i: (i, 0))
    o = pl.pallas_call(
        _sattn2_kernel,
        grid=(nbq // nb2,),
        in_specs=[blk2(GQ, HEAD_DIM), blk2(n_slot, HEAD_DIM, PAGE), blk2(n_slot, HEAD_DIM, PAGE),
                  blk2(HEAD_DIM, PAGE), blk2(HEAD_DIM, PAGE),
                  const2(GQ, n_key), const2(GQ, n_key), slot_spec, slot_spec,
                  pl.BlockSpec((n_slot + 1, n_key), lambda i: (0, 0)),
                  pl.BlockSpec((1, n_key), lambda i: (0, 0)),
                  blk2(HEAD_DIM, n_win), blk2(HEAD_DIM, n_win), const2(GQ, n_win),
                  blk2(GQ, HEAD_DIM), blk2(GQ, 3)],
        out_specs=blk2(GQ, HEAD_DIM),
        out_shape=jax.ShapeDtypeStruct((nbq, GQ, HEAD_DIM), F32),
        compiler_params=_cp("arbitrary"),
        name="nsa_sample_sel_win",
    )(qg, k_sel, v_sel, k_tail, v_tail, base, corr, flag, half, expand, lane_half, wk, wv, wb, o_c,
      gates.reshape(nbq, GQ, 3))
    return o.reshape(n, Q_W).astype(BF16)


def _gelu_tanh(x):
    return 0.5 * x * (1.0 + jnp.tanh(math.sqrt(2.0 / math.pi) * (x + 0.044715 * (x * x * x))))


def _post_kernel(x_ref, o_ref, y5_ref, gm_ref, shf_ref, scf_ref, gpost_ref, gpre_ref,
                 wglu_ref, wout_ref, wr_ref, br_ref, tri_ref, cin_ref,
                 x1_ref, h2_ref, h2p_ref, eidx_ref, wts_ref, pos_ref, cout_ref, cnt_sc):
    i = pl.program_id(0)
    tm = x_ref.shape[0]

    @pl.when(i == 0)
    def _():
        cnt_sc[...] = cin_ref[...]

    g5 = _gelu_tanh(y5_ref[...])
    g5 = g5 * _sigmoid(_dot(g5.astype(BF16), wglu_ref[...]))
    m = _dot(o_ref[...], wout_ref[:Q_W]) + _dot(g5.astype(BF16), wout_ref[Q_W:])
    x1 = x_ref[...] + gm_ref[0] * _rms(m, gpost_ref[...])
    h2 = _rms(x1, gpre_ref[...]) * (1.0 + scf_ref[0]) + shf_ref[0]
    x1_ref[...] = x1
    h2_ref[...] = h2.astype(BF16)
    h2p_ref[...] = _pack_bf16_pairs(h2)

    h_hi = h2.astype(BF16)
    h_lo = (h2 - h_hi.astype(F32)).astype(BF16)
    logits = _dot(h_hi, wr_ref[0]) + (_dot(h_lo, wr_ref[0]) + _dot(h_hi, wr_ref[1]))
    s_t = _sigmoid(logits).T
    s_sel = s_t + br_ref[...]
    eio = lax.broadcasted_iota(I32, (N_EXPERTS, tm), 0).astype(F32)
    first = lambda hit, ids: jnp.min(jnp.where(hit, ids, 1e9), axis=0, keepdims=True)
    gscore = []
    for g in range(N_EGROUPS):
        xg = s_sel[g * EGROUP:(g + 1) * EGROUP]
        ig = lax.broadcasted_iota(I32, (EGROUP, tm), 0).astype(F32) + float(g * EGROUP)
        m1 = jnp.max(xg, axis=0, keepdims=True)
        m2 = jnp.max(jnp.where(ig == first(xg == m1, ig), -jnp.inf, xg), axis=0, keepdims=True)
        gscore.append(m1 + m2)
    gsc = jnp.concatenate(gscore, axis=0)
    gio = lax.broadcasted_iota(I32, (N_EGROUPS, tm), 0).astype(F32)
    gmask = jnp.zeros((N_EGROUPS, tm), F32)
    for _ in range(TOPK_GROUPS):
        hit = gio == first(gsc == jnp.max(gsc, axis=0, keepdims=True), gio)
        gmask = jnp.where(hit, 1.0, gmask)
        gsc = jnp.where(hit, -jnp.inf, gsc)
    work = jnp.concatenate(
        [jnp.where(gmask[g:g + 1] > 0.0, s_sel[g * EGROUP:(g + 1) * EGROUP], -jnp.inf)
         for g in range(N_EGROUPS)], axis=0)
    chosen = jnp.zeros((N_EXPERTS, tm), F32)
    ids, wsel = [], []
    for _ in range(TOP_K):
        ik = first(work == jnp.max(work, axis=0, keepdims=True), eio)
        hit = eio == ik
        ids.append(ik)
        wsel.append(jnp.sum(jnp.where(hit, s_t, 0.0), axis=0, keepdims=True))
        work = jnp.where(hit, -jnp.inf, work)
        chosen = jnp.where(hit, 1.0, chosen)
    rank = cnt_sc[:, 0:1] + _dot(chosen.astype(BF16), tri_ref[...])
    pos = [jnp.sum(jnp.where(eio == ik, rank, 0.0), axis=0, keepdims=True) for ik in ids]
    cnt_sc[...] = cnt_sc[...] + jnp.sum(chosen, axis=1, keepdims=True)
    wsum = wsel[0]
    for k in range(1, TOP_K):
        wsum = wsum + wsel[k]
    eidx_ref[...] = jnp.concatenate(ids, axis=0).astype(I32)
    pos_ref[...] = jnp.concatenate(pos, axis=0).astype(I32)
    wts_ref[...] = jnp.concatenate([w / wsum * ROUTED_SCALE for w in wsel], axis=0)
    cout_ref[...] = cnt_sc[...]


def _post_mix(x, o_att, y5, gm, shf, scf, g_post, g_pre, wglu, wout, w_router, b_router,
              cnt_in, tm, rows_per_mod):
    t = x.shape[0]
    r = gm.shape[1]
    per = rows_per_mod // tm
    mod = pl.BlockSpec((1, r, D_MODEL), lambda i: (i // per, 0, 0))
    full = lambda a: pl.BlockSpec(a.shape, lambda i: (0,) * a.ndim)
    vec = pl.BlockSpec((1, D_MODEL), lambda i: (0, 0))
    tri = jnp.asarray(np.triu(np.ones((tm, tm), np.float32), 1), BF16)
    brb = jnp.broadcast_to(b_router[:, None], (N_EXPERTS, tm)).astype(F32)
    wr_hi = w_router.astype(BF16)
    w_router = jnp.stack([wr_hi, (w_router - wr_hi.astype(F32)).astype(BF16)])
    route = pl.BlockSpec((TOP_K, tm), lambda i: (0, i))
    return pl.pallas_call(
        _post_kernel,
        grid=(t // tm,),
        in_specs=[pl.BlockSpec((tm, D_MODEL), lambda i: (i, 0)),
                  pl.BlockSpec((tm, Q_W), lambda i: (i, 0)),
                  pl.BlockSpec((tm, S5_W), lambda i: (i, 0)),
                  mod, mod, mod, vec, vec, full(wglu), full(wout), full(w_router), full(brb),
                  full(tri), full(cnt_in)],
        out_specs=[pl.BlockSpec((tm, D_MODEL), lambda i: (i, 0)),
                   pl.BlockSpec((tm, D_MODEL), lambda i: (i, 0)),
                   pl.BlockSpec((tm, D_MODEL // 2), lambda i: (i, 0)),
                   route, route, route, full(cnt_in)],
        out_shape=[jax.ShapeDtypeStruct((t, D_MODEL), F32),
                   jax.ShapeDtypeStruct((t, D_MODEL), BF16),
                   jax.ShapeDtypeStruct((t, D_MODEL // 2), U32),
                   jax.ShapeDtypeStruct((TOP_K, t), I32),
                   jax.ShapeDtypeStruct((TOP_K, t), F32),
                   jax.ShapeDtypeStruct((TOP_K, t), I32),
                   jax.ShapeDtypeStruct(cnt_in.shape, F32)],
        scratch_shapes=[pltpu.VMEM(cnt_in.shape, F32)],
        compiler_params=_cp("arbitrary"),
        name="post_mix_router",
    )(x, o_att, y5, gm, shf, scf, g_post.reshape(1, D_MODEL), g_pre.reshape(1, D_MODEL),
      wglu, wout, w_router, brb, tri, cnt_in)


def _dest_kernel(e_ref, p_ref, ps_ref, o_ref):
    tm = e_ref.shape[1]
    eio = lax.broadcasted_iota(I32, (N_EXPERTS, tm), 0)
    e = e_ref[...]
    start = ps_ref[...]
    rows = [jnp.sum(jnp.where(eio == e[k:k + 1], start, 0.0), axis=0, keepdims=True)
            for k in range(TOP_K)]
    o_ref[...] = jnp.concatenate(rows, axis=0).astype(I32) + p_ref[...]


def _dest(eidx, pos, pad_start, tm):
    t = eidx.shape[1]
    route = pl.BlockSpec((TOP_K, tm), lambda i: (0, i))
    start = jnp.broadcast_to(pad_start.astype(F32)[:, None], (N_EXPERTS, tm))
    return pl.pallas_call(
        _dest_kernel,
        grid=(t // tm,),
        in_specs=[route, route, pl.BlockSpec((N_EXPERTS, tm), lambda i: (0, 0))],
        out_specs=route,
        out_shape=jax.ShapeDtypeStruct((TOP_K, t), I32),
        compiler_params=_cp("arbitrary"),
        name="moe_dest",
    )(eidx, pos, start)


def _moe_kernel(be_ref, nb_ref, x_ref, wg_ref, wu_ref, wd_ref, y_ref, wg_sc, wu_sc, wd_sc):
    j = pl.program_id(0)

    @pl.when(j < nb_ref[0])
    def _():
        @pl.when((j == 0) | (be_ref[j] != be_ref[jnp.maximum(j - 1, 0)]))
        def _():
            wg_sc[...] = wg_ref[0].astype(BF16)
            wu_sc[...] = wu_ref[0].astype(BF16)
            wd_sc[...] = wd_ref[0].astype(BF16)

        x = jnp.concatenate(_unpack_bf16_pairs(x_ref[...]), axis=1).astype(BF16)
        a = _dot(x, wg_sc[...])
        b = _dot(x, wu_sc[...])
        y = _dot((_silu(a) * b).astype(BF16), wd_sc[...])
        y_ref[...] = _pack_bf16_pairs(y)

    @pl.when(j >= nb_ref[0])
    def _():
        y_ref[...] = jnp.zeros(y_ref.shape, U32)


def _moe(xs, blk_e, nb_used, w_g, w_u, w_d):
    n_slot = xs.shape[0]
    n_blk = n_slot // MOE_BLK
    last = lambda j, nb: jnp.maximum(jnp.minimum(j, nb[0] - 1), 0)
    row = lambda j, be, nb: (last(j, nb), 0)
    wsel = lambda j, be, nb: (be[last(j, nb)], 0, 0)
    return pl.pallas_call(
        _moe_kernel,
        grid_spec=pltpu.PrefetchScalarGridSpec(
            num_scalar_prefetch=2,
            grid=(n_blk,),
            in_specs=[pl.BlockSpec((MOE_BLK, D_MODEL // 2), row),
                      pl.BlockSpec((1, D_MODEL, D_EXPERT), wsel),
                      pl.BlockSpec((1, D_MODEL, D_EXPERT), wsel),
                      pl.BlockSpec((1, D_EXPERT, D_MODEL), wsel)],
            out_specs=pl.BlockSpec((MOE_BLK, D_MODEL // 2), lambda j, be, nb: (j, 0)),
            scratch_shapes=[pltpu.VMEM((D_MODEL, D_EXPERT), BF16), pltpu.VMEM((D_MODEL, D_EXPERT), BF16),
                            pltpu.VMEM((D_EXPERT, D_MODEL), BF16)]),
        out_shape=jax.ShapeDtypeStruct((n_slot, D_MODEL // 2), U32),
        compiler_params=_cp("arbitrary"),
        name="moe_experts",
    )(blk_e, nb_used, xs, w_g, w_u, w_d)


def _fin_kernel(x1_ref, h2_ref, yg_ref, w_ref, gf_ref, gpost_ref, wsg_ref, wsu_ref, wsd_ref, o_ref):
    w = w_ref[...]
    f_lo, f_hi = None, None
    for k in range(TOP_K):
        lo, hi = _unpack_bf16_pairs(yg_ref[k])
        f_lo = w[:, k:k + 1] * lo if k == 0 else f_lo + w[:, k:k + 1] * lo
        f_hi = w[:, k:k + 1] * hi if k == 0 else f_hi + w[:, k:k + 1] * hi
    f = jnp.concatenate([f_lo, f_hi], axis=1)
    hb = h2_ref[...]
    sh = (_silu(_dot(hb, wsg_ref[...])) * _dot(hb, wsu_ref[...])).astype(BF16)
    f = f + _dot(sh, wsd_ref[...])
    o_ref[...] = x1_ref[...] + gf_ref[0] * _rms(f, gpost_ref[...])


def _final(x1, h2, yg, wts, gf, g_post, wsg, wsu, wsd, tm, rows_per_mod):
    t = x1.shape[0]
    r = gf.shape[1]
    per = rows_per_mod // tm
    full = lambda a: pl.BlockSpec(a.shape, lambda i: (0,) * a.ndim)
    return pl.pallas_call(
        _fin_kernel,
        grid=(t // tm,),
        in_specs=[pl.BlockSpec((tm, D_MODEL), lambda i: (i, 0)),
                  pl.BlockSpec((tm, D_MODEL), lambda i: (i, 0)),
                  pl.BlockSpec((TOP_K, tm, D_MODEL // 2), lambda i: (0, i, 0)),
                  pl.BlockSpec((tm, TOP_K), lambda i: (i, 0)),
                  pl.BlockSpec((1, r, D_MODEL), lambda i: (i // per, 0, 0)),
                  pl.BlockSpec((1, D_MODEL), lambda i: (0, 0)),
                  full(wsg), full(wsu), full(wsd)],
        out_specs=pl.BlockSpec((tm, D_MODEL), lambda i: (i, 0)),
        out_shape=jax.ShapeDtypeStruct((t, D_MODEL), F32),
        compiler_params=_cp("arbitrary"),
        name="moe_combine_final",
    )(x1, h2, yg, wts, gf, g_post.reshape(1, D_MODEL), wsg, wsu, wsd)


def _even_odd(a, axis):
    n = a.shape[axis]
    order = np.concatenate([np.arange(0, n, 2), np.arange(1, n, 2)])
    return jnp.take(a, order, axis=axis)


def _layer(xp, xs, c_prompt, c_sample, page_table, ck_c, cv_c, ck_s, cv_s, cw_k, cw_v, st_re, st_im,
           w_ada, b_ada, g_pre_mix, g_post_mix, g_pre_ffn, g_post_ffn, w_in, pe, wk1, wk2, wv1, wv2,
           rel_bias, lam_re, lam_im, log_dt, b_re, b_im, c_re, c_im, d_skip, w_glu, w_out,
           w_router, b_router, w_eg, w_eu, w_ed, w_sg, w_su, w_sd):
    batch, seq, _ = xp.shape
    n_dec = xs.shape[0]
    tp = batch * seq
    x_p = xp.reshape(tp, D_MODEL)
    x_s = xs.reshape(n_dec, D_MODEL)

    n_c = batch + n_dec
    n_pad = -(-n_c // 8) * 8
    c_all = jnp.pad(jnp.concatenate([c_prompt, c_sample], axis=0), ((0, n_pad - n_c), (0, 0)))
    mod = _adaln(c_all, w_ada, b_ada)
    mod_p = [m.reshape(batch, 1, D_MODEL) for m in jnp.split(mod[:batch], 6, axis=-1)]
    mod_s = [m.reshape(1, n_dec, D_MODEL) for m in jnp.split(mod[batch:n_c], 6, axis=-1)]

    wn, wt = _pre_weights(w_in)
    (kct_p, vct_p, kst_p, vst_p, kwt_p, vwt_p, kc_p, vc_p, u_p, ksb, kwb, qt_p, vst, vsn, vwt, gt_p) = \
        _pre_mix_prompt(x_p, mod_p[1], mod_p[0], g_pre_mix, wn, wt, 512, batch, seq)
    (kc_s, vc_s, ks_s, vs_s, kw_s, vw_s, u_s, qt_s, gt_s) = _pre_mix_sample(
        x_s, mod_s[1], mod_s[0], g_pre_mix, wn, wt)

    n_cmp_p = seq // CMP_BLOCK
    kcmp_p = _compress(kc_p, *_cmp_weights(pe, wk1, wk2))
    vcmp_p = _compress(vc_p, *_cmp_weights(pe, wv1, wv2))
    kcmp_p = _even_odd(kcmp_p.reshape(batch, n_cmp_p, N_KV, HEAD_DIM), 1).transpose(0, 2, 1, 3)
    vcmp_p = _even_odd(vcmp_p.reshape(batch, n_cmp_p, N_KV, HEAD_DIM), 1).transpose(0, 2, 3, 1)
    bias_log2 = rel_bias * LOG2E
    wtab, ctab = _bias_tables(bias_log2)
    o_p = _attn_prompt(qt_p, gt_p, kcmp_p.astype(BF16), vcmp_p.astype(BF16), ksb, vst, vsn, kwb, vwt,
                       wtab, ctab, batch, seq)

    n_pages = page_table.shape[1]
    n_cmp_s = n_pages * CMP_PER_PAGE
    pool = lambda cache: cache.transpose(0, 2, 3, 1).reshape(cache.shape[0], KV_W, PAGE)
    kcmp_s, vcmp_s = _compress_pages(page_table, pool(ck_c), pool(cv_c),
                                     _cmp_pages_weights(pe, wk1, wk2), _cmp_pages_weights(pe, wv1, wv2))
    by_head = lambda a: _even_odd(
        a.reshape(n_dec, n_pages, N_KV, CMP_PER_PAGE, HEAD_DIM).transpose(0, 2, 1, 3, 4).reshape(
            n_dec, N_KV, n_cmp_s, HEAD_DIM), 2)
    win_k = jnp.concatenate([cw_k[:, 1:], kw_s.reshape(n_dec, 1, N_KV, HEAD_DIM)], axis=1)
    win_v = jnp.concatenate([cw_v[:, 1:], vw_s.reshape(n_dec, 1, N_KV, HEAD_DIM)], axis=1)
    q_s = qt_s.T.reshape(n_dec, N_HEADS, HEAD_DIM)
    gates_s = gt_s[:, :3 * GQ].reshape(N_KV, 3, GQ, n_dec).transpose(3, 0, 2, 1).reshape(n_dec, N_HEADS, 3)
    o_s = _attn_sample(q_s, gates_s, by_head(kcmp_s).astype(BF16), by_head(vcmp_s).astype(BF16),
                       page_table, ck_s, cv_s, ks_s, vs_s, win_k, win_v, bias_log2)

    y5_p, s5r_p, s5i_p = _s5_prompt(
        u_p, _s5_prep(lam_re, lam_im, log_dt, b_re, b_im, c_re, c_im, d_skip, S5_CHUNK), batch, seq)
    y5_s, s5r_s, s5i_s = _s5_sample(
        u_s, _s5_prep(lam_re, lam_im, log_dt, b_re, b_im, c_re, c_im, d_skip, 1), st_re, st_im)

    wglu_b, wout_b = w_glu.astype(BF16), w_out.astype(BF16)
    cnt0 = jnp.zeros((N_EXPERTS, 128), F32)
    tm_q = 512
    x1_p, h2_p, h2p_p, e_p, wt_p, pos_p, cnt1 = _post_mix(
        x_p, o_p, y5_p, mod_p[2], mod_p[3], mod_p[4], g_post_mix, g_pre_ffn, wglu_b, wout_b,
        w_router, b_router, cnt0, tm_q, seq)
    x1_s, h2_s, h2p_s, e_s, wt_s, pos_s, cnt2 = _post_mix(
        x_s, o_s, y5_s, mod_s[2], mod_s[3], mod_s[4], g_post_mix, g_pre_ffn, wglu_b, wout_b,
        w_router, b_router, cnt1, n_dec, n_dec)

    t_all = tp + n_dec
    counts = cnt2[:, 0].astype(I32)
    padded = (counts + MOE_BLK - 1) // MOE_BLK * MOE_BLK
    pad_end = jnp.cumsum(padded)
    dest_p = _dest(e_p, pos_p, pad_end - padded, 1024)
    dest_s = _dest(e_s, pos_s, pad_end - padded, n_dec)
    dest = jnp.concatenate([dest_p, dest_s], axis=1)
    n_blk = -(-(t_all * TOP_K) // MOE_BLK) + N_EXPERTS
    n_slot = n_blk * MOE_BLK
    tok = jnp.broadcast_to(jnp.arange(t_all, dtype=I32)[None], (TOP_K, t_all))
    rows = (jnp.arange(n_slot, dtype=I32) % t_all).at[dest.reshape(-1)].set(
        tok.reshape(-1), unique_indices=True)
    xs_rows = jnp.concatenate([h2p_p, h2p_s], axis=0)[rows]
    blk_e = jnp.minimum(jnp.searchsorted(pad_end, jnp.arange(n_blk, dtype=I32) * MOE_BLK, side='right'),
                        N_EXPERTS - 1).astype(I32)
    nb_used = (pad_end[-1:] // MOE_BLK).astype(I32)
    ys = _moe(xs_rows, blk_e, nb_used, w_eg, w_eu, w_ed)
    yg_p = ys[dest_p]
    yg_s = ys[dest_s]

    wsg, wsu, wsd = w_sg.astype(BF16), w_su.astype(BF16), w_sd.astype(BF16)
    out_p = _final(x1_p, h2_p, yg_p, wt_p.T, mod_p[5], g_post_ffn, wsg, wsu, wsd, tm_q, seq)
    out_s = _final(x1_s, h2_s, yg_s, wt_s.T, mod_s[5], g_post_ffn, wsg, wsu, wsd, n_dec, n_dec)

    n_win = min(WINDOW, seq)
    rows_p = lambda a: a.transpose(0, 3, 1, 2)
    kv5 = lambda a: a.reshape(n_dec, 1, N_KV, HEAD_DIM)
    st_p = (rows_p(kct_p), rows_p(vct_p), rows_p(kst_p), rows_p(vst_p),
            rows_p(kwt_p)[:, seq - n_win:], rows_p(vwt_p)[:, seq - n_win:], s5r_p, s5i_p)
    st_s = (kv5(kc_s), kv5(vc_s), kv5(ks_s), kv5(vs_s), win_k, win_v, s5r_s, s5i_s)
    return out_p.reshape(batch, seq, D_MODEL), out_s.reshape(n_dec, 1, D_MODEL), st_p, st_s


def kernel(x_prompt, x_sample, c_prompt, c_sample, page_table, cache_cmp_k, cache_cmp_v, cache_sel_k, cache_sel_v, cache_win_k, cache_win_v, state_s5_re, state_s5_im, w_ada, b_ada, g_pre_mix, g_post_mix, g_pre_ffn, g_post_ffn, w_in, pe_cmp, w_cmp_k1, w_cmp_k2, w_cmp_v1, w_cmp_v2, rel_bias, lam_re, lam_im, log_dt, b_re, b_im, c_re, c_im, d_skip, w_glu, w_out, w_router, b_router, w_exp_gate, w_exp_up, w_exp_down, w_sh_gate, w_sh_up, w_sh_down):
    depth = w_in.shape[0]
    xp, xs = x_prompt, x_sample
    p_states, s_states = [], []
    for l in range(depth):
        xp, xs, st_p, st_s = _layer(
            xp, xs, c_prompt, c_sample, page_table, cache_cmp_k[l], cache_cmp_v[l], cache_sel_k[l],
            cache_sel_v[l], cache_win_k[l], cache_win_v[l], state_s5_re[l], state_s5_im[l],
            w_ada[l], b_ada[l], g_pre_mix[l], g_post_mix[l], g_pre_ffn[l], g_post_ffn[l], w_in[l],
            pe_cmp[l], w_cmp_k1[l], w_cmp_k2[l], w_cmp_v1[l], w_cmp_v2[l], rel_bias,
            lam_re[l], lam_im[l], log_dt[l], b_re[l], b_im[l], c_re[l], c_im[l], d_skip[l],
            w_glu[l], w_out[l], w_router[l], b_router[l], w_exp_gate[l], w_exp_up[l], w_exp_down[l],
            w_sh_gate[l], w_sh_up[l], w_sh_down[l])
        p_states.append(st_p)
        s_states.append(st_s)
    p_st = tuple(jnp.stack(a) for a in zip(*p_states))
    s_st = tuple(jnp.stack(a) for a in zip(*s_states))
    return (xp, xs) + p_st + s_st
```
